```python
import math
import jax
import jax.numpy as jnp
from jax import lax
import numpy as np

D_MODEL = 1024
BATCH = 8
SEQ = 2048
DEPTH = 2
DEC_BATCH = 128
DEC_SEQ = 8
PAST_LEN = 16384
PAGE_SIZE = 128

GROUP_W = D_MODEL // 4
D_FF = 4 * D_MODEL
N_MOD = 6
EPS = 1e-6

SSD_HEADDIM = 64
SSD_HEADS = GROUP_W // SSD_HEADDIM
SSD_NGROUPS = 2
SSD_STATE = 64
SSD_CONV = 4
SSD_CHUNK = 64
SSD_BC = SSD_NGROUPS * SSD_STATE
SSD_CONV_CH = GROUP_W + 2 * SSD_BC
SSD_PROJ = GROUP_W + SSD_CONV_CH + SSD_HEADS

RWKV_HEADDIM = 64
RWKV_HEADS = GROUP_W // RWKV_HEADDIM
RWKV_DECAY_LORA = 32
RWKV_AAA_LORA = 32
RWKV_GATE_LORA = 64
RWKV_LN_EPS = 64e-5
RWKV_PROJ = 3 * GROUP_W + RWKV_DECAY_LORA + RWKV_AAA_LORA + RWKV_GATE_LORA

GLA_HEADS = 4
GLA_DK = GROUP_W // (2 * GLA_HEADS)
GLA_DV = GROUP_W // GLA_HEADS
GLA_GATE_LORA = 16
GLA_TAU = 16.0
GLA_CHUNK = 32
GLA_PROJ = 2 * GLA_HEADS * GLA_DK + GROUP_W + GLA_GATE_LORA + GROUP_W

S5_GROUP = 16
S5_GROUPS = GROUP_W // S5_GROUP
S5_STATE = 64
S5_PROJ = GROUP_W

PROJ_W = SSD_PROJ + RWKV_PROJ + GLA_PROJ + S5_PROJ
PROJ_SPLITS = (SSD_PROJ, SSD_PROJ + RWKV_PROJ, SSD_PROJ + RWKV_PROJ + GLA_PROJ)

STATE_NAMES = ('ssd', 'ssd_conv', 'rwkv', 'rwkv_shift', 'gla', 's5_re', 's5_im')

kernel_name = 'hybrid_ssd_rwkv7_gla_s5_step'


def rmsnorm(x, g):
    xf = x.astype(jnp.float32)
    y = xf * lax.rsqrt(jnp.mean(xf * xf, axis=-1, keepdims=True) + EPS)
    return (y * g.astype(jnp.float32)).astype(x.dtype)


def head_rmsnorm(y, g):
    y = y * lax.rsqrt(jnp.mean(y * y, axis=-1, keepdims=True) + EPS)
    return y.reshape(y.shape[0], y.shape[1], -1) * g.astype(jnp.float32)


def to_chunks(t, cs):
    b, l = t.shape[:2]
    return jnp.moveaxis(t.reshape(b, l // cs, cs, *t.shape[2:]), 1, 0)


def from_chunks(t):
    nc, b, cs = t.shape[:3]
    return jnp.moveaxis(t, 0, 1).reshape(b, nc * cs, *t.shape[3:])


def causal_dwconv(x, buf, w, b):
    L = x.shape[1]
    K = w.shape[0]
    xp = jnp.concatenate([buf, x], axis=1)
    y = b + xp[:, 0:L] * w[0]
    for j in range(1, K):
        y = y + xp[:, j:j + L] * w[j]
    return y, xp[:, L:]


def ssd_mixer(p, conv_buf, s0, conv_w, conv_b, a_log, dt_bias, d_skip, norm_g):
    bsz, L, _ = p.shape
    f32 = jnp.float32
    pf = p.astype(f32)
    z = pf[..., :GROUP_W]
    xbc = pf[..., GROUP_W:GROUP_W + SSD_CONV_CH]
    dt_raw = pf[..., GROUP_W + SSD_CONV_CH:]
    xbc, new_buf = causal_dwconv(xbc, conv_buf.astype(f32), conv_w.astype(f32), conv_b.astype(f32))
    xbc = jax.nn.silu(xbc)
    xs = xbc[..., :GROUP_W].reshape(bsz, L, SSD_HEADS, SSD_HEADDIM)
    bm = xbc[..., GROUP_W:GROUP_W + SSD_BC].reshape(bsz, L, SSD_NGROUPS, SSD_STATE)
    cm = xbc[..., GROUP_W + SSD_BC:].reshape(bsz, L, SSD_NGROUPS, SSD_STATE)
    rep = SSD_HEADS // SSD_NGROUPS
    bh = jnp.repeat(bm, rep, axis=2)
    ch = jnp.repeat(cm, rep, axis=2)
    dt = jax.nn.softplus(dt_raw + dt_bias)
    A = -jnp.exp(a_log.astype(f32))
    cs = math.gcd(L, SSD_CHUNK)
    mask = jnp.tril(jnp.ones((cs, cs), dtype=bool))

    def step(S, inp):
        xc, dtc, bc, cc = inp
        cum = jnp.cumsum(dtc * A, axis=1)
        seg = cum[:, :, None, :] - cum[:, None, :, :]
        decay = jnp.exp(jnp.where(mask[None, :, :, None], seg, -jnp.inf))
        scores = jnp.einsum('bihn,bjhn->bijh', cc, bc) * decay * dtc[:, None, :, :]
        y = jnp.einsum('bijh,bjhp->bihp', scores, xc)
        y = y + jnp.einsum('bihn,bhpn->bihp', cc, S) * jnp.exp(cum)[..., None]
        last = cum[:, -1]
        wts = jnp.exp(last[:, None, :] - cum) * dtc
        S = S * jnp.exp(last)[:, :, None, None] + jnp.einsum('bjh,bjhp,bjhn->bhpn', wts, xc, bc)
        return S, y

    s_fin, y = lax.scan(step, s0.astype(f32),
                        (to_chunks(xs, cs), to_chunks(dt, cs), to_chunks(bh, cs), to_chunks(ch, cs)))
    y = from_chunks(y) + d_skip.astype(f32)[:, None] * xs
    y = y * jax.nn.silu(z).reshape(bsz, L, SSD_HEADS, SSD_HEADDIM)
    return head_rmsnorm(y, norm_g), new_buf, s_fin


def rwkv_mixer(p, prev, s0, mu, w0, w2, a0, a2, g2, k_k, k_a, r_k, lnx_w, lnx_b):
    bsz, L, _ = p.shape
    H, N, G = RWKV_HEADS, RWKV_HEADDIM, GROUP_W
    f32 = jnp.float32
    pf = p.astype(f32)
    shifted = jnp.concatenate([prev.astype(f32)[:, None], pf[:, :-1]], axis=1)
    pm = pf + (shifted - pf) * mu
    o1 = 3 * G + RWKV_DECAY_LORA
    o2 = o1 + RWKV_AAA_LORA
    r, k, v = pm[..., :G], pm[..., G:2 * G], pm[..., 2 * G:3 * G]
    w = -jax.nn.softplus(-(w0 + jnp.tanh(pm[..., 3 * G:o1]) @ w2)) - 0.5
    decay = jnp.exp(-jnp.exp(w))
    a = jax.nn.sigmoid(a0 + pm[..., o1:o2] @ a2)
    g = jax.nn.sigmoid(pm[..., o2:]) @ g2
    kk = (k * k_k).reshape(bsz, L, H, N)
    kk = kk / jnp.maximum(jnp.linalg.norm(kk, axis=-1, keepdims=True), 1e-12)
    k = k * (1.0 + (a - 1.0) * k_a)
    hd = lambda t: t.reshape(bsz, L, H, N)
    r, k, v, decay, a = hd(r), hd(k), hd(v), hd(decay), hd(a)

    def step(S, inp):
        r_t, w_t, k_t, v_t, kk_t, a_t = inp
        sk = jnp.einsum('bhvk,bhk->bhv', S, kk_t)
        S = (S * w_t[:, :, None, :] - sk[..., None] * (kk_t * a_t)[:, :, None, :]
             + v_t[..., None] * k_t[:, :, None, :])
        return S, jnp.einsum('bhvk,bhk->bhv', S, r_t)

    tm = lambda t: jnp.moveaxis(t, 1, 0)
    s_fin, y = lax.scan(step, s0.astype(f32), (tm(r), tm(decay), tm(k), tm(v), tm(kk), tm(a)))
    y = tm(y)
    mean = jnp.mean(y, axis=-1, keepdims=True)
    var = jnp.mean(jnp.square(y - mean), axis=-1, keepdims=True)
    y = ((y - mean) * lax.rsqrt(var + RWKV_LN_EPS)).reshape(bsz, L, G) * lnx_w + lnx_b
    bonus = jnp.sum(r * k * r_k, axis=-1, keepdims=True) * v
    y = (y + bonus.reshape(bsz, L, G)) * g
    return y, p[:, -1], s_fin


def gla_mixer(p, s0, wg2, bg, norm_g):
    bsz, L, _ = p.shape
    f32 = jnp.float32
    pf = p.astype(f32)
    qk = GLA_HEADS * GLA_DK
    ov = 2 * qk + GROUP_W
    q = pf[..., :qk].reshape(bsz, L, GLA_HEADS, GLA_DK) * (GLA_DK ** -0.5)
    k = pf[..., qk:2 * qk].reshape(bsz, L, GLA_HEADS, GLA_DK)
    v = pf[..., 2 * qk:ov].reshape(bsz, L, GLA_HEADS, GLA_DV)
    dg = pf[..., ov:ov + GLA_GATE_LORA]
    og = pf[..., ov + GLA_GATE_LORA:]
    lg = (jax.nn.log_sigmoid(dg @ wg2 + bg) / GLA_TAU).reshape(bsz, L, GLA_HEADS, GLA_DK)
    cs = math.gcd(L, GLA_CHUNK)
    mask = jnp.tril(jnp.ones((cs, cs), dtype=bool))

    def step(S, inp):
        qc, kc, vc, gc = inp
        b = jnp.cumsum(gc, axis=1)
        qd = qc * jnp.exp(b)
        kd = kc * jnp.exp(-b)
        att = jnp.where(mask, jnp.einsum('bihk,bjhk->bhij', qd, kd), 0.0)
        y = jnp.einsum('bhij,bjhv->bihv', att, vc) + jnp.einsum('bihk,bhkv->bihv', qd, S)
        bl = b[:, -1]
        S = S * jnp.exp(bl)[..., None] + jnp.einsum('bjhk,bjhv->bhkv', kc * jnp.exp(bl[:, None] - b), vc)
        return S, y

    s_fin, y = lax.scan(step, s0.astype(f32),
                        (to_chunks(q, cs), to_chunks(k, cs), to_chunks(v, cs), to_chunks(lg, cs)))
    y = head_rmsnorm(from_chunks(y), norm_g) * jax.nn.silu(og)
    return y, s_fin


def _complex_affine_combine(e1, e2):
    a1r, a1i, b1r, b1i = e1
    a2r, a2i, b2r, b2i = e2
    return (a2r * a1r - a2i * a1i,
            a2r * a1i + a2i * a1r,
            a2r * b1r - a2i * b1i + b2r,
            a2r * b1i + a2i * b1r + b2i)


def s5_mixer(u, h0_re, h0_im, lam_re, lam_im, log_step, b_re, b_im, c_re, c_im, d_skip, w_glu, b_glu):
    bsz, L, _ = u.shape
    f32 = jnp.float32
    uf = u.astype(f32)
    ug = uf.reshape(bsz, L, S5_GROUPS, S5_GROUP)
    lr = lam_re.astype(f32)
    li = lam_im.astype(f32)
    dt = jnp.exp(log_step.astype(f32))[:, None]
    mag = jnp.exp(lr * dt)
    ar = mag * jnp.cos(li * dt)
    ai = mag * jnp.sin(li * dt)
    den = lr * lr + li * li
    fr = ((ar - 1.0) * lr + ai * li) / den
    fi = (ai * lr - (ar - 1.0) * li) / den
    bb_re = fr[..., None] * b_re - fi[..., None] * b_im
    bb_im = fr[..., None] * b_im + fi[..., None] * b_re
    bu_re = jnp.einsum('gni,blgi->blgn', bb_re, ug)
    bu_im = jnp.einsum('gni,blgi->blgn', bb_im, ug)
    h0r = h0_re.astype(f32)
    h0i = h0_im.astype(f32)
    bu_re = bu_re.at[:, 0].add(ar * h0r - ai * h0i)
    bu_im = bu_im.at[:, 0].add(ar * h0i + ai * h0r)
    shp = bu_re.shape
    _, _, h_re, h_im = lax.associative_scan(
        _complex_affine_combine,
        (jnp.broadcast_to(ar, shp), jnp.broadcast_to(ai, shp), bu_re, bu_im), axis=1)
    y = jnp.einsum('gon,blgn->blgo', c_re, h_re) - jnp.einsum('gon,blgn->blgo', c_im, h_im)
    y = y.reshape(bsz, L, GROUP_W) + d_skip * uf
    y = jax.nn.gelu(y)
    y = y * jax.nn.sigmoid(y @ w_glu + b_glu)
    return y, h_re[:, -1], h_im[:, -1]


def run_group(x, c, st, W):
    bsz = x.shape[0]
    out = {n: [] for n in STATE_NAMES}
    for l in range(DEPTH):
        mod = (jax.nn.silu(c) @ W['w_ada'][l] + W['b_ada'][l]).reshape(bsz, N_MOD, 1, D_MODEL)
        sh_m, sc_m, gt_m, sh_f, sc_f, gt_f = (mod[:, i] for i in range(N_MOD))
        h = rmsnorm(x, W['g_mix_pre'][l]) * (1.0 + sc_m) + sh_m
        p = h @ W['w_in'][l]
        p_ssd, p_rwkv, p_gla, p_s5 = jnp.split(p, PROJ_SPLITS, axis=-1)
        y_ssd, conv_new, ssd_new = ssd_mixer(
            p_ssd, st['ssd_conv'][l], st['ssd'][l], W['ssd_conv_w'][l], W['ssd_conv_b'][l],
            W['ssd_a_log'][l], W['ssd_dt_bias'][l], W['ssd_d'][l], W['ssd_norm_g'][l])
        y_rwkv, shift_new, rwkv_new = rwkv_mixer(
            p_rwkv, st['rwkv_shift'][l], st['rwkv'][l], W['rwkv_mu'][l], W['rwkv_w0'][l],
            W['rwkv_w2'][l], W['rwkv_a0'][l], W['rwkv_a2'][l], W['rwkv_g2'][l], W['rwkv_k_k'][l],
            W['rwkv_k_a'][l], W['rwkv_r_k'][l], W['rwkv_lnx_w'][l], W['rwkv_lnx_b'][l])
        y_gla, gla_new = gla_mixer(p_gla, st['gla'][l], W['gla_wg2'][l], W['gla_bg'][l], W['gla_norm_g'][l])
        y_s5, s5r_new, s5i_new = s5_mixer(
            p_s5, st['s5_re'][l], st['s5_im'][l], W['s5_lam_re'][l], W['s5_lam_im'][l],
            W['s5_log_step'][l], W['s5_b_re'][l], W['s5_b_im'][l], W['s5_c_re'][l], W['s5_c_im'][l],
            W['s5_d'][l], W['s5_w_glu'][l], W['s5_b_glu'][l])
        mix = jnp.concatenate([y_ssd, y_rwkv, y_gla, y_s5], axis=-1).astype(x.dtype) @ W['w_out'][l]
        x = x + gt_m * rmsnorm(mix, W['g_mix_post'][l])
        h = rmsnorm(x, W['g_ffn_pre'][l]) * (1.0 + sc_f) + sh_f
        f = jnp.square(jax.nn.relu(h @ W['mlp_w1'][l])) @ W['mlp_w2'][l]
        x = x + gt_f * rmsnorm(f, W['g_ffn_post'][l])
        new = dict(ssd=ssd_new, ssd_conv=conv_new, rwkv=rwkv_new, rwkv_shift=shift_new,
                   gla=gla_new, s5_re=s5r_new, s5_im=s5i_new)
        for n in STATE_NAMES:
            out[n].append(new[n].astype(st[n].dtype))
    return x, {n: jnp.stack(out[n]) for n in STATE_NAMES}


def zero_states(bsz, dtype):
    shapes = dict(
        ssd=(DEPTH, bsz, SSD_HEADS, SSD_HEADDIM, SSD_STATE),
        ssd_conv=(DEPTH, bsz, SSD_CONV - 1, SSD_CONV_CH),
        rwkv=(DEPTH, bsz, RWKV_HEADS, RWKV_HEADDIM, RWKV_HEADDIM),
        rwkv_shift=(DEPTH, bsz, RWKV_PROJ),
        gla=(DEPTH, bsz, GLA_HEADS, GLA_DK, GLA_DV),
        s5_re=(DEPTH, bsz, S5_GROUPS, S5_STATE),
        s5_im=(DEPTH, bsz, S5_GROUPS, S5_STATE))
    return {n: jnp.zeros(s, dtype) for n, s in shapes.items()}


def setup_inputs(seed: int = 0) -> dict:
    key = jax.random.key(seed)
    ks = iter(jax.random.split(key, 80))
    f32 = jnp.float32

    def nrm(shape, scale):
        return scale * jax.random.normal(next(ks), shape, f32)

    def unif(shape, lo, hi):
        return jax.random.uniform(next(ks), shape, f32, lo, hi)

    L = DEPTH
    G = GROUP_W
    x_prompt = nrm((BATCH, SEQ, D_MODEL), 1.0)
    x_sample = nrm((DEC_BATCH, DEC_SEQ, D_MODEL), 1.0)
    c_prompt = nrm((BATCH, D_MODEL), 1.0)
    c_sample = nrm((DEC_BATCH, D_MODEL), 1.0)
    state_ssd = nrm((L, DEC_BATCH, SSD_HEADS, SSD_HEADDIM, SSD_STATE), 0.5)
    state_ssd_conv = nrm((L, DEC_BATCH, SSD_CONV - 1, SSD_CONV_CH), 1.0)
    state_rwkv = nrm((L, DEC_BATCH, RWKV_HEADS, RWKV_HEADDIM, RWKV_HEADDIM), 0.5)
    state_rwkv_shift = nrm((L, DEC_BATCH, RWKV_PROJ), 1.0)
    state_gla = nrm((L, DEC_BATCH, GLA_HEADS, GLA_DK, GLA_DV), 0.5)
    state_s5_re = nrm((L, DEC_BATCH, S5_GROUPS, S5_STATE), 0.5)
    state_s5_im = nrm((L, DEC_BATCH, S5_GROUPS, S5_STATE), 0.5)

    w_ada = nrm((L, D_MODEL, N_MOD * D_MODEL), 0.5 * D_MODEL ** -0.5)
    b_ada = nrm((L, N_MOD * D_MODEL), 0.02)
    g_mix_pre = 1.0 + nrm((L, D_MODEL), 0.02)
    g_mix_post = 1.0 + nrm((L, D_MODEL), 0.02)
    g_ffn_pre = 1.0 + nrm((L, D_MODEL), 0.02)
    g_ffn_post = 1.0 + nrm((L, D_MODEL), 0.02)
    w_in = nrm((L, D_MODEL, PROJ_W), D_MODEL ** -0.5)
    w_out = nrm((L, D_MODEL, D_MODEL), D_MODEL ** -0.5)

    ssd_conv_w = nrm((L, SSD_CONV, SSD_CONV_CH), SSD_CONV ** -0.5)
    ssd_conv_b = nrm((L, SSD_CONV_CH), 0.02)
    ssd_a_log = jnp.log(unif((L, SSD_HEADS), 1.0, 16.0))
    dt0 = jnp.exp(unif((L, SSD_HEADS), math.log(1e-3), math.log(1e-1)))
    ssd_dt_bias = dt0 + jnp.log(-jnp.expm1(-dt0))
    ssd_d = 1.0 + nrm((L, SSD_HEADS), 0.1)
    ssd_norm_g = 1.0 + nrm((L, G), 0.02)

    rwkv_mu = unif((L, RWKV_PROJ), 0.0, 1.0)
    rwkv_w0 = unif((L, G), -6.0, -1.0)
    rwkv_w2 = nrm((L, RWKV_DECAY_LORA, G), 0.5 * RWKV_DECAY_LORA ** -0.5)
    rwkv_a0 = nrm((L, G), 0.1)
    rwkv_a2 = nrm((L, RWKV_AAA_LORA, G), 0.5 * RWKV_AAA_LORA ** -0.5)
    rwkv_g2 = nrm((L, RWKV_GATE_LORA, G), RWKV_GATE_LORA ** -0.5)
    rwkv_k_k = 0.85 + nrm((L, G), 0.02)
    rwkv_k_a = 1.0 + nrm((L, G), 0.02)
    rwkv_r_k = nrm((L, RWKV_HEADS, RWKV_HEADDIM), 0.1)
    rwkv_lnx_w = 1.0 + nrm((L, G), 0.02)
    rwkv_lnx_b = nrm((L, G), 0.02)

    gla_wg2 = nrm((L, GLA_GATE_LORA, GLA_HEADS * GLA_DK), GLA_GATE_LORA ** -0.5)
    gla_bg = nrm((L, GLA_HEADS * GLA_DK), 0.1)
    gla_norm_g = 1.0 + nrm((L, G), 0.02)

    s5_lam_re = -0.5 + nrm((L, S5_GROUPS, S5_STATE), 0.01)
    s5_lam_im = jnp.pi * jnp.arange(S5_STATE, dtype=f32) + nrm((L, S5_GROUPS, S5_STATE), 0.01)
    s5_log_step = unif((L, S5_GROUPS), math.log(1e-3), math.log(1e-1))
    s5_b_re = nrm((L, S5_GROUPS, S5_STATE, S5_GROUP), (2 * S5_GROUP) ** -0.5)
    s5_b_im = nrm((L, S5_GROUPS, S5_STATE, S5_GROUP), (2 * S5_GROUP) ** -0.5)
    s5_c_re = nrm((L, S5_GROUPS, S5_GROUP, S5_STATE), (2 * S5_STATE) ** -0.5)
    s5_c_im = nrm((L, S5_GROUPS, S5_GROUP, S5_STATE), (2 * S5_STATE) ** -0.5)
    s5_d = nrm((L, G), 0.5)
    s5_w_glu = nrm((L, G, G), G ** -0.5)
    s5_b_glu = nrm((L, G), 0.02)

    mlp_w1 = nrm((L, D_MODEL, D_FF), D_MODEL ** -0.5)
    mlp_w2 = nrm((L, D_FF, D_MODEL), D_FF ** -0.5)

    return {
        'x_prompt': x_prompt, 'x_sample': x_sample, 'c_prompt': c_prompt, 'c_sample': c_sample,
        'state_ssd': state_ssd, 'state_ssd_conv': state_ssd_conv, 'state_rwkv': state_rwkv,
        'state_rwkv_shift': state_rwkv_shift, 'state_gla': state_gla,
        'state_s5_re': state_s5_re, 'state_s5_im': state_s5_im,
        'w_ada': w_ada, 'b_ada': b_ada, 'g_mix_pre': g_mix_pre, 'g_mix_post': g_mix_post,
        'g_ffn_pre': g_ffn_pre, 'g_ffn_post': g_ffn_post, 'w_in': w_in, 'w_out': w_out,
        'ssd_conv_w': ssd_conv_w, 'ssd_conv_b': ssd_conv_b, 'ssd_a_log': ssd_a_log,
        'ssd_dt_bias': ssd_dt_bias, 'ssd_d': ssd_d, 'ssd_norm_g': ssd_norm_g,
        'rwkv_mu': rwkv_mu, 'rwkv_w0': rwkv_w0, 'rwkv_w2': rwkv_w2, 'rwkv_a0': rwkv_a0,
        'rwkv_a2': rwkv_a2, 'rwkv_g2': rwkv_g2, 'rwkv_k_k': rwkv_k_k, 'rwkv_k_a': rwkv_k_a,
        'rwkv_r_k': rwkv_r_k, 'rwkv_lnx_w': rwkv_lnx_w, 'rwkv_lnx_b': rwkv_lnx_b,
        'gla_wg2': gla_wg2, 'gla_bg': gla_bg, 'gla_norm_g': gla_norm_g,
        's5_lam_re': s5_lam_re, 's5_lam_im': s5_lam_im, 's5_log_step': s5_log_step,
        's5_b_re': s5_b_re, 's5_b_im': s5_b_im, 's5_c_re': s5_c_re, 's5_c_im': s5_c_im,
        's5_d': s5_d, 's5_w_glu': s5_w_glu, 's5_b_glu': s5_b_glu,
        'mlp_w1': mlp_w1, 'mlp_w2': mlp_w2,
    }


def reference(x_prompt, x_sample, c_prompt, c_sample,
              state_ssd, state_ssd_conv, state_rwkv, state_rwkv_shift, state_gla,
              state_s5_re, state_s5_im,
              w_ada, b_ada, g_mix_pre, g_mix_post, g_ffn_pre, g_ffn_post, w_in, w_out,
              ssd_conv_w, ssd_conv_b, ssd_a_log, ssd_dt_bias, ssd_d, ssd_norm_g,
              rwkv_mu, rwkv_w0, rwkv_w2, rwkv_a0, rwkv_a2, rwkv_g2, rwkv_k_k, rwkv_k_a,
              rwkv_r_k, rwkv_lnx_w, rwkv_lnx_b,
              gla_wg2, gla_bg, gla_norm_g,
              s5_lam_re, s5_lam_im, s5_log_step, s5_b_re, s5_b_im, s5_c_re, s5_c_im,
              s5_d, s5_w_glu, s5_b_glu,
              mlp_w1, mlp_w2):
    W = dict(w_ada=w_ada, b_ada=b_ada, g_mix_pre=g_mix_pre, g_mix_post=g_mix_post,
             g_ffn_pre=g_ffn_pre, g_ffn_post=g_ffn_post, w_in=w_in, w_out=w_out,
             ssd_conv_w=ssd_conv_w, ssd_conv_b=ssd_conv_b, ssd_a_log=ssd_a_log,
             ssd_dt_bias=ssd_dt_bias, ssd_d=ssd_d, ssd_norm_g=ssd_norm_g,
             rwkv_mu=rwkv_mu, rwkv_w0=rwkv_w0, rwkv_w2=rwkv_w2, rwkv_a0=rwkv_a0, rwkv_a2=rwkv_a2,
             rwkv_g2=rwkv_g2, rwkv_k_k=rwkv_k_k, rwkv_k_a=rwkv_k_a, rwkv_r_k=rwkv_r_k,
             rwkv_lnx_w=rwkv_lnx_w, rwkv_lnx_b=rwkv_lnx_b,
             gla_wg2=gla_wg2, gla_bg=gla_bg, gla_norm_g=gla_norm_g,
             s5_lam_re=s5_lam_re, s5_lam_im=s5_lam_im, s5_log_step=s5_log_step,
             s5_b_re=s5_b_re, s5_b_im=s5_b_im, s5_c_re=s5_c_re, s5_c_im=s5_c_im,
             s5_d=s5_d, s5_w_glu=s5_w_glu, s5_b_glu=s5_b_glu,
             mlp_w1=mlp_w1, mlp_w2=mlp_w2)
    st_sample = dict(ssd=state_ssd, ssd_conv=state_ssd_conv, rwkv=state_rwkv,
                     rwkv_shift=state_rwkv_shift, gla=state_gla,
                     s5_re=state_s5_re, s5_im=state_s5_im)
    st_prompt = zero_states(x_prompt.shape[0], x_prompt.dtype)
    y_prompt, sp = run_group(x_prompt, c_prompt, st_prompt, W)
    y_sample, ss = run_group(x_sample, c_sample, st_sample, W)
    return (y_prompt, y_sample,
            sp['ssd'], ss['ssd'], sp['ssd_conv'], ss['ssd_conv'],
            sp['rwkv'], ss['rwkv'], sp['rwkv_shift'], ss['rwkv_shift'],
            sp['gla'], ss['gla'], sp['s5_re'], ss['s5_re'], sp['s5_im'], ss['s5_im'])
```

```python
import functools
import math

import jax
import jax.numpy as jnp
from jax import lax
from jax.experimental import pallas as pl
from jax.experimental.pallas import tpu as pltpu

F32 = jnp.float32
BF16 = jnp.bfloat16

D_MODEL = 1024
DEPTH = 2
GROUP_W = 256
D_FF = 4096
N_MOD = 6
EPS = 1e-6

SSD_HEADS = 4
SSD_HEADDIM = 64
SSD_STATE = 64
SSD_NGROUPS = 2
SSD_BC = SSD_NGROUPS * SSD_STATE
SSD_CONV = 4
SSD_CONV_CH = GROUP_W + 2 * SSD_BC
SSD_PROJ = GROUP_W + SSD_CONV_CH + SSD_HEADS
SSD_PW = GROUP_W + SSD_CONV_CH + GROUP_W

RWKV_HEADS = 4
RWKV_HEADDIM = 64
RWKV_LORA = 128
RWKV_PROJ = 3 * GROUP_W + RWKV_LORA
RWKV_LN_EPS = 64e-5

GLA_HEADS = 4
GLA_DK = 32
GLA_DV = 64
GLA_GATE_LORA = 16
GLA_TAU = 16.0
GLA_QK = GLA_HEADS * GLA_DK
GLA_PROJ = 2 * GLA_QK + GROUP_W + GLA_GATE_LORA + GROUP_W
GLA_PW = 2 * GLA_QK + GROUP_W + 128 + GROUP_W

S5_GROUP = 16
S5_GROUPS = 16
S5_STATE = 64
S5_CH = S5_GROUPS * S5_STATE
S5_PROJ = GROUP_W

HIST = 8
CHUNK = 64
GLA_CS = 32
VMEM_LIMIT = 56 * 1024 * 1024


def _dot(a, b, dims=(((1,), (0,)), ((), ()))):
    return lax.dot_general(a.astype(BF16), b.astype(BF16), dims, preferred_element_type=F32)


_NT = (((1,), (1,)), ((), ()))
_TN = (((0,), (0,)), ((), ()))
_NN = (((1,), (0,)), ((), ()))


def _split2(a):
    hi = a.astype(BF16)
    lo = (a - hi.astype(F32)).astype(BF16)
    return hi, lo


def _dot3(a, b, dims=_NN):
    ah, al = _split2(a)
    bh, bl = _split2(b)
    d = lambda x, y: lax.dot_general(x, y, dims, preferred_element_type=F32)
    return d(ah, bh) + (d(ah, bl) + d(al, bh))


def _dot_exact_lhs(m, x, dims=_NN):
    mb = m.astype(BF16)
    x1 = x.astype(BF16)
    r1 = x - x1.astype(F32)
    x2 = r1.astype(BF16)
    x3 = (r1 - x2.astype(F32)).astype(BF16)
    d = lambda y: lax.dot_general(mb, y, dims, preferred_element_type=F32)
    return d(x1) + (d(x2) + d(x3))


def _segsum(x, width):
    n = x.shape[-1]
    r = lax.broadcasted_iota(jnp.int32, (n, n), 0) // width
    c = lax.broadcasted_iota(jnp.int32, (n, n), 1) // width
    j = jnp.where(r == c, 1.0, 0.0).astype(BF16)
    x1 = x.astype(BF16)
    r1 = x - x1.astype(F32)
    x2 = r1.astype(BF16)
    x3 = (r1 - x2.astype(F32)).astype(BF16)
    d = lambda y: lax.dot_general(y, j, _NN, preferred_element_type=F32)
    return d(x1) + (d(x2) + d(x3))


def _sigmoid(x):
    return 1.0 / (1.0 + jnp.exp(-x))


def _silu(x):
    return x * _sigmoid(x)


def _softplus(x):
    return jnp.maximum(x, 0.0) + jnp.log1p(jnp.exp(-jnp.abs(x)))


def _rms(x, g):
    return x * lax.rsqrt(jnp.mean(x * x, axis=-1, keepdims=True) + EPS) * g


def _chunk_masks(rows, lt):
    r = lax.broadcasted_iota(jnp.int32, (rows, rows), 0)
    c = lax.broadcasted_iota(jnp.int32, (rows, rows), 1)
    same = (r // lt) == (c // lt)
    return same & (c <= r), same & (c < r)


def _bcast_last(x, nb, lt):
    c = x.shape[-1]
    last = x.reshape(nb, lt, c)[:, lt - 1:lt, :]
    return last, jnp.broadcast_to(last, (nb, lt, c)).reshape(nb * lt, c)


def _ada_kernel(c_ref, w_ref, b_ref, o_ref):
    c = c_ref[...]
    o_ref[0] = _dot(_silu(c), w_ref[0]) + b_ref[0]


def _ada(c_all, w_ada, b_ada):
    rows = c_all.shape[0]
    n = N_MOD * D_MODEL
    tn = 1536
    return pl.pallas_call(
        _ada_kernel,
        out_shape=jax.ShapeDtypeStruct((DEPTH, rows, n), F32),
        grid=(DEPTH, n // tn),
        in_specs=[
            pl.BlockSpec((rows, D_MODEL), lambda l, j: (0, 0)),
            pl.BlockSpec((1, D_MODEL, tn), lambda l, j: (l, 0, j)),
            pl.BlockSpec((1, 1, tn), lambda l, j: (l, 0, j)),
        ],
        out_specs=pl.BlockSpec((1, rows, tn), lambda l, j: (l, 0, j)),
        compiler_params=pltpu.CompilerParams(
            dimension_semantics=("parallel", "parallel"), vmem_limit_bytes=VMEM_LIMIT),
        name="ada",
    )(c_all, w_ada, b_ada.reshape(DEPTH, 1, n))


def _in_kernel(x_ref, mod_ref, g_ref, wssd_ref, wrwkv_ref, wgla_ref, ws5_ref,
               ossd_ref, orwkv_ref, ogla_ref, os5_ref):
    nb, lt, d = x_ref.shape
    m = mod_ref[...]
    h = _rms(x_ref[...], g_ref[...]) * (1.0 + m[:, 1:2, :]) + m[:, 0:1, :]
    hb = h.reshape(nb * lt, d).astype(BF16)
    for w_ref, o_ref in ((wssd_ref, ossd_ref), (wrwkv_ref, orwkv_ref),
                         (wgla_ref, ogla_ref), (ws5_ref, os5_ref)):
        o_ref[...] = jnp.dot(hb, w_ref[...], preferred_element_type=F32).reshape(o_ref.shape)


def _in_proj(x, mod, g, wts, nb, lt):
    bsz, seq, d = x.shape
    widths = (SSD_PW, RWKV_PROJ, GLA_PW, S5_PROJ)
    const2 = lambda i, j: (0, 0)
    return pl.pallas_call(
        _in_kernel,
        out_shape=tuple(jax.ShapeDtypeStruct((bsz, seq, w), F32) for w in widths),
        grid=(bsz // nb, seq // lt),
        in_specs=[
            pl.BlockSpec((nb, lt, d), lambda i, j: (i, j, 0)),
            pl.BlockSpec((nb, N_MOD, d), lambda i, j: (i, 0, 0)),
            pl.BlockSpec((1, 1, d), lambda i, j: (0, 0, 0)),
        ] + [pl.BlockSpec((d, w), const2) for w in widths],
        out_specs=tuple(pl.BlockSpec((nb, lt, w), lambda i, j: (i, j, 0)) for w in widths),
        compiler_params=pltpu.CompilerParams(
            dimension_semantics=("parallel", "parallel"), vmem_limit_bytes=VMEM_LIMIT),
        name="in_proj",
    )(x, mod, g, *wts)


def _out_mlp_kernel(x_ref, y0_ref, y1_ref, y2_ref, y3_ref, mod_ref, gpost_ref, gpre_ref, gfpost_ref,
                    wout_ref, w1_ref, w2_ref, o_ref):
    nb, lt, d = x_ref.shape
    rows = nb * lt
    m = mod_ref[...]
    mix = None
    for i, y_ref in enumerate((y0_ref, y1_ref, y2_ref, y3_ref)):
        t = jnp.dot(y_ref[...].reshape(rows, GROUP_W).astype(BF16), wout_ref[i],
                    preferred_element_type=F32)
        mix = t if mix is None else mix + t
    x1 = x_ref[...] + m[:, 2:3, :] * _rms(mix, gpost_ref[0]).reshape(nb, lt, d)
    h = _rms(x1, gpre_ref[...]) * (1.0 + m[:, 4:5, :]) + m[:, 3:4, :]
    hb = h.reshape(rows, d).astype(BF16)
    acc = None
    step = 1024
    for j in range(D_FF // step):
        a = jnp.dot(hb, w1_ref[:, j * step:(j + 1) * step], preferred_element_type=F32)
        a = jnp.square(jnp.maximum(a, 0.0)).astype(BF16)
        t = jnp.dot(a, w2_ref[j * step:(j + 1) * step, :], preferred_element_type=F32)
        acc = t if acc is None else acc + t
    o_ref[...] = x1 + m[:, 5:6, :] * _rms(acc, gfpost_ref[0]).reshape(nb, lt, d)


def _out_mlp(x, ys, mod, gpost, gpre, gfpost, wout, w1, w2, nb, lt):
    bsz, seq, d = x.shape
    tok = lambda i, j: (i, j, 0)
    c3 = lambda i, j: (0, 0, 0)
    c2 = lambda i, j: (0, 0)
    one = pl.Buffered(1)
    return pl.pallas_call(
        _out_mlp_kernel,
        out_shape=jax.ShapeDtypeStruct((bsz, seq, d), F32),
        grid=(bsz // nb, seq // lt),
        in_specs=[pl.BlockSpec((nb, lt, d), tok)]
        + [pl.BlockSpec((nb, lt, GROUP_W), tok)] * 4
        + [pl.BlockSpec((nb, N_MOD, d), lambda i, j: (i, 0, 0))]
        + [pl.BlockSpec((1, 1, d), c3)] * 3
        + [pl.BlockSpec((4, GROUP_W, d), c3, pipeline_mode=one),
           pl.BlockSpec((d, D_FF), c2, pipeline_mode=one),
           pl.BlockSpec((D_FF, d), c2, pipeline_mode=one)],
        out_specs=pl.BlockSpec((nb, lt, d), tok),
        compiler_params=pltpu.CompilerParams(
            dimension_semantics=("parallel", "parallel"), vmem_limit_bytes=VMEM_LIMIT),
        name="out_mlp",
    )(x, *ys, mod, gpost, gpre, gfpost, wout, w1, w2)


def _ssd_kernel(p_ref, hist_ref, s0_ref, cw_ref, cb_ref, alog_ref, dtb_ref, dsk_ref, ng_ref,
                y_ref, hout_ref, sout_ref, ext_ref, yscr_ref, *, nb, lt):
    rows = nb * lt
    hd = SSD_HEADDIM

    @pl.when(pl.program_id(1) == 0)
    def _():
        ext_ref[:, 0:HIST, :] = hist_ref[...]
        sout_ref[...] = s0_ref[...]

    p = p_ref[...]
    ext_ref[:, HIST:HIST + lt, :] = p[:, :, GROUP_W:GROUP_W + SSD_CONV_CH]
    conv = cb_ref[...]
    for j in range(SSD_CONV):
        o = HIST - (SSD_CONV - 1) + j
        conv = conv + ext_ref[:, o:o + lt, :] * cw_ref[j:j + 1, :]
    tail = ext_ref[:, lt:lt + HIST, :]
    ext_ref[:, 0:HIST, :] = tail
    hout_ref[...] = tail

    xbc = _silu(conv).reshape(rows, SSD_CONV_CH)
    xs = xbc[:, 0:GROUP_W]
    bm = xbc[:, GROUP_W:GROUP_W + SSD_BC]
    cm = xbc[:, GROUP_W + SSD_BC:]
    z = p[:, :, 0:GROUP_W].reshape(rows, GROUP_W)
    dt = _softplus(p[:, :, GROUP_W + SSD_CONV_CH:].reshape(rows, GROUP_W) + dtb_ref[...])
    a = dt * (-jnp.exp(alog_ref[...]))
    incl, _ = _chunk_masks(rows, lt)
    cum = _dot_exact_lhs(jnp.where(incl, 1.0, 0.0), a)
    last, last_b = _bcast_last(cum, nb, lt)
    xdt = xs * dt
    xw = xdt * jnp.exp(last_b - cum)
    ecum = jnp.exp(cum)
    elast = jnp.exp(last)
    ones_row = jnp.full((rows, hd), 1.0 / hd, F32)

    for g in range(SSD_NGROUPS):
        cg = cm[:, g * SSD_STATE:(g + 1) * SSD_STATE]
        bg = bm[:, g * SSD_STATE:(g + 1) * SSD_STATE]
        cb = _dot(cg, bg, _NT)
        for h in range(g * 2, g * 2 + 2):
            sl = slice(h * hd, (h + 1) * hd)
            ccol = cum[:, sl]
            crow = _dot_exact_lhs(ones_row, ccol, _NT)
            decay = jnp.exp(jnp.where(incl, ccol - crow, -jnp.inf))
            yh = _dot(cb * decay, xdt[:, sl])
            inter = []
            for i in range(nb):
                rs = slice(i * lt, (i + 1) * lt)
                s_old = sout_ref[i, h]
                inter.append(_dot(cg[rs], s_old, _NT))
                sout_ref[i, h] = s_old * elast[i][:, sl] + _dot(xw[rs, sl], bg[rs], _TN)
            inter = inter[0] if nb == 1 else jnp.concatenate(inter, axis=0)
            yscr_ref[:, sl] = yh + inter * ecum[:, sl]

    y = yscr_ref[...] + dsk_ref[...] * xs
    y = y * _silu(z)
    y = y * lax.rsqrt(_segsum(y * y, hd) * (1.0 / hd) + EPS) * ng_ref[...]
    y_ref[...] = y.reshape(nb, lt, GROUP_W)


def _ssd(p, hist, s0, prm, nb, lt):
    bsz, seq, _ = p.shape
    assert nb * lt == CHUNK and CHUNK == SSD_HEADDIM
    row = lambda i, c: (0, 0)
    kern = functools.partial(_ssd_kernel, nb=nb, lt=lt)
    return pl.pallas_call(
        kern,
        out_shape=(jax.ShapeDtypeStruct((bsz, seq, GROUP_W), F32),
                   jax.ShapeDtypeStruct((bsz, HIST, SSD_CONV_CH), F32),
                   jax.ShapeDtypeStruct(s0.shape, F32)),
        grid=(bsz // nb, seq // lt),
        in_specs=[
            pl.BlockSpec((nb, lt, SSD_PW), lambda i, c: (i, c, 0)),
            pl.BlockSpec((nb, HIST, SSD_CONV_CH), lambda i, c: (i, 0, 0)),
            pl.BlockSpec((nb,) + s0.shape[1:], lambda i, c: (i, 0, 0, 0)),
            pl.BlockSpec((SSD_CONV, SSD_CONV_CH), row),
            pl.BlockSpec((1, SSD_CONV_CH), row),
        ] + [pl.BlockSpec((1, GROUP_W), row)] * 4,
        out_specs=(pl.BlockSpec((nb, lt, GROUP_W), lambda i, c: (i, c, 0)),
                   pl.BlockSpec((nb, HIST, SSD_CONV_CH), lambda i, c: (i, 0, 0)),
                   pl.BlockSpec((nb,) + s0.shape[1:], lambda i, c: (i, 0, 0, 0))),
        scratch_shapes=[pltpu.VMEM((nb, HIST + lt, SSD_CONV_CH), F32),
                        pltpu.VMEM((nb * lt, GROUP_W), F32)],
        compiler_params=pltpu.CompilerParams(
            dimension_semantics=("parallel", "arbitrary"), vmem_limit_bytes=VMEM_LIMIT),
        name="ssd",
    )(p, hist, s0, *prm)


def _rwkv_kernel(p_ref, sh0_ref, s0_ref, mu_ref, wl_ref, w0_ref, a0_ref, kk_ref, ka_ref, rk_ref,
                 lnw_ref, lnb_ref, y_ref, shout_ref, sout_ref, ext_ref, yscr_ref, *, nb, lt):
    rows = nb * lt
    hd = RWKV_HEADDIM
    gw = GROUP_W

    @pl.when(pl.program_id(1) == 0)
    def _():
        ext_ref[:, HIST - 1:HIST, :] = sh0_ref[...]
        sout_ref[...] = s0_ref[...]

    p = p_ref[...]
    ext_ref[:, HIST:HIST + lt, :] = p
    prev = ext_ref[:, HIST - 1:HIST - 1 + lt, :]
    pm = (p + (prev - p) * mu_ref[...]).reshape(rows, RWKV_PROJ)
    last_p = p[:, lt - 1:lt, :]
    ext_ref[:, HIST - 1:HIST, :] = last_p
    shout_ref[...] = last_p

    r = pm[:, 0:gw]
    k = pm[:, gw:2 * gw]
    v = pm[:, 2 * gw:3 * gw]
    lo = pm[:, 3 * gw:]
    lane = lax.broadcasted_iota(jnp.int32, (rows, RWKV_LORA), 1)
    t = jnp.where(lane < 32, jnp.tanh(lo), jnp.where(lane < 64, lo, _sigmoid(lo)))
    lora = _dot(t, wl_ref[...])
    wlog = -_softplus(-(w0_ref[...] + lora[:, 0:gw])) - 0.5
    lw = -jnp.exp(wlog)
    a = _sigmoid(a0_ref[...] + lora[:, gw:2 * gw])
    gate = lora[:, 2 * gw:]
    kk = k * kk_ref[...]
    kk = kk / jnp.maximum(jnp.sqrt(_segsum(kk * kk, hd)), 1e-12)
    k2 = k * (1.0 + (a - 1.0) * ka_ref[...])
    beta = kk * a

    incl, strict = _chunk_masks(rows, lt)
    gcum = _dot_exact_lhs(jnp.where(incl, 1.0, 0.0), lw)
    glast, glast_b = _bcast_last(gcum, nb, lt)
    eg = jnp.exp(gcum)
    einv = jnp.exp(-gcum)
    rt = r * eg
    kt = kk * jnp.exp(gcum - lw)
    k2t = k2 * einv
    bt = beta * einv
    etail = jnp.exp(glast_b - gcum)
    k2g = k2 * etail
    bg = beta * etail
    elast = jnp.exp(glast)

    ri = lax.broadcasted_iota(jnp.int32, (rows, rows), 0)
    ci = lax.broadcasted_iota(jnp.int32, (rows, rows), 1)
    eye = jnp.where(ri == ci, 1.0, 0.0)

    for h in range(RWKV_HEADS):
        sl = slice(h * hd, (h + 1) * hd)
        a1 = jnp.where(strict, _dot3(kt[:, sl], k2t[:, sl], _NT), 0.0)
        a2 = jnp.where(strict, _dot3(kt[:, sl], bt[:, sl], _NT), 0.0)
        l1 = jnp.where(incl, _dot3(rt[:, sl], k2t[:, sl], _NT), 0.0)
        l2 = jnp.where(incl, _dot3(rt[:, sl], bt[:, sl], _NT), 0.0)
        x = eye
        s = 1
        while s < lt:
            off = ((ri // s) % 2 == 1) & ((ci // s) == (ri // s) - 1)
            x = x - _dot3(_dot3(x, jnp.where(off, a2, 0.0)), x)
            s *= 2
        vh = v[:, sl]
        p0, q0 = [], []
        for i in range(nb):
            rs = slice(i * lt, (i + 1) * lt)
            s_old = sout_ref[i, h]
            p0.append(_dot3(kt[rs, sl], s_old, _NT))
            q0.append(_dot3(rt[rs, sl], s_old, _NT))
        p0 = p0[0] if nb == 1 else jnp.concatenate(p0, axis=0)
        q0 = q0[0] if nb == 1 else jnp.concatenate(q0, axis=0)
        u = _dot3(x, p0 + _dot3(a1, vh))
        yscr_ref[:, sl] = q0 + _dot3(l1, vh) - _dot3(l2, u)
        for i in range(nb):
            rs = slice(i * lt, (i + 1) * lt)
            s_old = sout_ref[i, h]
            sout_ref[i, h] = (s_old * elast[i][:, sl] + _dot3(vh[rs], k2g[rs, sl], _TN)
                              - _dot3(u[rs], bg[rs, sl], _TN))

    y = yscr_ref[...]
    mean = _segsum(y, hd) * (1.0 / hd)
    dlt = y - mean
    var = _segsum(dlt * dlt, hd) * (1.0 / hd)
    yn = dlt * lax.rsqrt(var + RWKV_LN_EPS) * lnw_ref[...] + lnb_ref[...]
    bonus = _segsum(r * k2 * rk_ref[...], hd) * v
    y_ref[...] = ((yn + bonus) * gate).reshape(nb, lt, gw)


def _rwkv(p, sh0, s0, prm, nb, lt):
    bsz, seq, _ = p.shape
    assert nb * lt == CHUNK
    row = lambda i, c: (0, 0)
    kern = functools.partial(_rwkv_kernel, nb=nb, lt=lt)
    return pl.pallas_call(
        kern,
        out_shape=(jax.ShapeDtypeStruct((bsz, seq, GROUP_W), F32),
                   jax.ShapeDtypeStruct((bsz, 1, RWKV_PROJ), F32),
                   jax.ShapeDtypeStruct(s0.shape, F32)),
        grid=(bsz // nb, seq // lt),
        in_specs=[
            pl.BlockSpec((nb, lt, RWKV_PROJ), lambda i, c: (i, c, 0)),
            pl.BlockSpec((nb, 1, RWKV_PROJ), lambda i, c: (i, 0, 0)),
            pl.BlockSpec((nb,) + s0.shape[1:], lambda i, c: (i, 0, 0, 0)),
            pl.BlockSpec((1, RWKV_PROJ), row),
            pl.BlockSpec((RWKV_LORA, 3 * GROUP_W), row),
        ] + [pl.BlockSpec((1, GROUP_W), row)] * 7,
        out_specs=(pl.BlockSpec((nb, lt, GROUP_W), lambda i, c: (i, c, 0)),
                   pl.BlockSpec((nb, 1, RWKV_PROJ), lambda i, c: (i, 0, 0)),
                   pl.BlockSpec((nb,) + s0.shape[1:], lambda i, c: (i, 0, 0, 0))),
        scratch_shapes=[pltpu.VMEM((nb, HIST + lt, RWKV_PROJ), F32),
                        pltpu.VMEM((nb * lt, GROUP_W), F32)],
        compiler_params=pltpu.CompilerParams(
            dimension_semantics=("parallel", "arbitrary"), vmem_limit_bytes=VMEM_LIMIT),
        name="rwkv",
    )(p, sh0, s0, *prm)


def _gla_kernel(p_ref, s0_ref, wg2_ref, bg_ref, ng_ref, y_ref, sout_ref, yscr_ref, *, nb, lt, cs):
    rows = nb * lt
    dk, dv = GLA_DK, GLA_DV
    qk = GLA_QK
    units = rows // cs
    per_group = GLA_CS // cs

    @pl.when(pl.program_id(1) == 0)
    def _():
        sout_ref[...] = s0_ref[...]

    p = p_ref[...].reshape(rows, GLA_PW)
    q = p[:, 0:qk] * (dk ** -0.5)
    k = p[:, qk:2 * qk]
    v = p[:, 2 * qk:2 * qk + GROUP_W]
    dg = p[:, 2 * qk + GROUP_W:2 * qk + GROUP_W + 128]
    og = p[:, 2 * qk + GROUP_W + 128:]
    lg = -_softplus(-(_dot(dg, wg2_ref[...]) + bg_ref[...])) * (1.0 / GLA_TAU)
    incl, _ = _chunk_masks(rows, cs)
    b = _dot_exact_lhs(jnp.where(incl, 1.0, 0.0), lg)
    bl, bl_b = _bcast_last(b, units, cs)
    qd = q * jnp.exp(b)
    kd = k * jnp.exp(-b)
    kw = k * jnp.exp(bl_b - b)
    ebl = jnp.exp(bl)
    gmask, _ = _chunk_masks(GLA_CS, cs)

    for h in range(GLA_HEADS):
        ks = slice(h * dk, (h + 1) * dk)
        vs = slice(h * dv, (h + 1) * dv)
        states = [sout_ref[i, h] for i in range(nb)]
        for g in range(rows // GLA_CS):
            gs = slice(g * GLA_CS, (g + 1) * GLA_CS)
            att = jnp.where(gmask, _dot(qd[gs, ks], kd[gs, ks], _NT), 0.0)
            yh = _dot(att, v[gs, vs])
            inter = []
            for j in range(per_group):
                un = g * per_group + j
                us = slice(un * cs, (un + 1) * cs)
                i = un if nb > 1 else 0
                inter.append(_dot(qd[us, ks], states[i], _NT))
                states[i] = states[i] * ebl[un][:, ks] + _dot(v[us, vs], kw[us, ks], _TN)
            inter = inter[0] if per_group == 1 else jnp.concatenate(inter, axis=0)
            yscr_ref[gs, vs] = yh + inter
        for i in range(nb):
            sout_ref[i, h] = states[i]

    y = yscr_ref[...]
    y = y * lax.rsqrt(_segsum(y * y, dv) * (1.0 / dv) + EPS) * ng_ref[...]
    y_ref[...] = (y * _silu(og)).reshape(nb, lt, GROUP_W)


def _gla(p, s0, prm, nb, lt, cs):
    bsz, seq, _ = p.shape
    assert (nb == 1 and cs == GLA_CS and lt % cs == 0) or (lt == cs and nb * cs == GLA_CS)
    row = lambda i, c: (0, 0)
    kern = functools.partial(_gla_kernel, nb=nb, lt=lt, cs=cs)
    return pl.pallas_call(
        kern,
        out_shape=(jax.ShapeDtypeStruct((bsz, seq, GROUP_W), F32),
                   jax.ShapeDtypeStruct(s0.shape, F32)),
        grid=(bsz // nb, seq // lt),
        in_specs=[
            pl.BlockSpec((nb, lt, GLA_PW), lambda i, c: (i, c, 0)),
            pl.BlockSpec((nb,) + s0.shape[1:], lambda i, c: (i, 0, 0, 0)),
            pl.BlockSpec((128, GLA_QK), row),
            pl.BlockSpec((1, GLA_QK), row),
            pl.BlockSpec((1, GROUP_W), row),
        ],
        out_specs=(pl.BlockSpec((nb, lt, GROUP_W), lambda i, c: (i, c, 0)),
                   pl.BlockSpec((nb,) + s0.shape[1:], lambda i, c: (i, 0, 0, 0))),
        scratch_shapes=[pltpu.VMEM((nb * lt, GROUP_W), F32)],
        compiler_params=pltpu.CompilerParams(
            dimension_semantics=("parallel", "arbitrary"), vmem_limit_bytes=VMEM_LIMIT),
        name="gla",
    )(p, s0, *prm)


def _s5_kernel(u_ref, hr0_ref, hi0_ref, wb_ref, wc_ref, ar_ref, ai_ref, d_ref, wglu_ref, bglu_ref,
               y_ref, hr_ref, hi_ref, *, nb, lt):
    rows = nb * lt
    n = S5_CH

    @pl.when(pl.program_id(1) == 0)
    def _():
        hr_ref[...] = hr0_ref[...]
        hi_ref[...] = hi0_ref[...]

    u = u_ref[...].reshape(rows, GROUP_W)
    bu = _dot(u, wb_ref[...])
    ar = ar_ref[...]
    ai = ai_ref[...]
    h0r = hr_ref[...]
    h0i = hi_ref[...]
    cr = jnp.broadcast_to(ar * h0r - ai * h0i, (nb, lt, n)).reshape(rows, n)
    ci = jnp.broadcast_to(ar * h0i + ai * h0r, (nb, lt, n)).reshape(rows, n)
    pos = lax.broadcasted_iota(jnp.int32, (rows, n), 0) % lt
    hr = bu[:, 0:n] + jnp.where(pos == 0, cr, 0.0)
    hi = bu[:, n:] + jnp.where(pos == 0, ci, 0.0)
    pr, pi = ar, ai
    s = 1
    while s < lt:
        sr = jnp.where(pos >= s, pltpu.roll(hr, s, 0), 0.0)
        si = jnp.where(pos >= s, pltpu.roll(hi, s, 0), 0.0)
        hr, hi = hr + (pr * sr - pi * si), hi + (pr * si + pi * sr)
        pr, pi = pr * pr - pi * pi, 2.0 * (pr * pi)
        s *= 2
    hr_ref[...] = hr.reshape(nb, lt, n)[:, lt - 1:lt, :]
    hi_ref[...] = hi.reshape(nb, lt, n)[:, lt - 1:lt, :]
    y = _dot(hr, wc_ref[0]) - _dot(hi, wc_ref[1])
    y = y + d_ref[...] * u
    y = 0.5 * y * (1.0 + jnp.tanh(math.sqrt(2.0 / math.pi) * (y + 0.044715 * (y * y * y))))
    y = y * _sigmoid(_dot(y, wglu_ref[...]) + bglu_ref[...])
    y_ref[...] = y.reshape(nb, lt, GROUP_W)


def _s5(u, hr0, hi0, prm, nb, lt):
    bsz, seq, _ = u.shape
    row = lambda i, c: (0, 0)
    st = pl.BlockSpec((nb, 1, S5_CH), lambda i, c: (i, 0, 0))
    kern = functools.partial(_s5_kernel, nb=nb, lt=lt)
    return pl.pallas_call(
        kern,
        out_shape=(jax.ShapeDtypeStruct((bsz, seq, GROUP_W), F32),
                   jax.ShapeDtypeStruct((bsz, 1, S5_CH), F32),
                   jax.ShapeDtypeStruct((bsz, 1, S5_CH), F32)),
        grid=(bsz // nb, seq // lt),
        in_specs=[
            pl.BlockSpec((nb, lt, GROUP_W), lambda i, c: (i, c, 0)), st, st,
            pl.BlockSpec((GROUP_W, 2 * S5_CH), row),
            pl.BlockSpec((2, S5_CH, GROUP_W), lambda i, c: (0, 0, 0)),
            pl.BlockSpec((1, S5_CH), row),
            pl.BlockSpec((1, S5_CH), row),
            pl.BlockSpec((1, GROUP_W), row),
            pl.BlockSpec((GROUP_W, GROUP_W), row),
            pl.BlockSpec((1, GROUP_W), row),
        ],
        out_specs=(pl.BlockSpec((nb, lt, GROUP_W), lambda i, c: (i, c, 0)), st, st),
        compiler_params=pltpu.CompilerParams(
            dimension_semantics=("parallel", "arbitrary"), vmem_limit_bytes=VMEM_LIMIT),
        name="s5",
    )(u, hr0, hi0, *prm)


def _block_diag_in(b):
    g, n, i = b.shape
    eye = jnp.eye(g, dtype=b.dtype)
    return jnp.einsum('gni,gh->gihn', b, eye).reshape(g * i, g * n)


def _block_diag_out(c):
    g, o, n = c.shape
    eye = jnp.eye(g, dtype=c.dtype)
    return jnp.einsum('gon,gh->gnho', c, eye).reshape(g * n, g * o)


def _layer_params(l, W):
    f = lambda name: W[name][l]
    row = lambda a: a.reshape(1, -1).astype(F32)
    rep = lambda a, n: jnp.repeat(a.reshape(-1), n).reshape(1, -1).astype(F32)
    w_in = f('w_in')
    o_r = SSD_PROJ
    o_g = o_r + RWKV_PROJ
    o_s = o_g + GLA_PROJ
    w_ssd = jnp.concatenate([w_in[:, :GROUP_W + SSD_CONV_CH],
                             jnp.repeat(w_in[:, GROUP_W + SSD_CONV_CH:o_r], SSD_HEADDIM, axis=1)], axis=1)
    w_rwkv = w_in[:, o_r:o_g]
    gq = o_g + 2 * GLA_QK + GROUP_W
    w_gla = jnp.concatenate([w_in[:, o_g:gq], w_in[:, gq:gq + GLA_GATE_LORA],
                             jnp.zeros((D_MODEL, 128 - GLA_GATE_LORA), F32),
                             w_in[:, gq + GLA_GATE_LORA:o_s]], axis=1)
    w_s5 = w_in[:, o_s:]
    in_w = tuple(w.astype(BF16) for w in (w_ssd, w_rwkv, w_gla, w_s5))

    ssd = (f('ssd_conv_w'), row(f('ssd_conv_b')), rep(f('ssd_a_log'), SSD_HEADDIM),
           rep(f('ssd_dt_bias'), SSD_HEADDIM), rep(f('ssd_d'), SSD_HEADDIM), row(f('ssd_norm_g')))

    z = lambda r, c: jnp.zeros((r, c), F32)
    w_lora = jnp.concatenate([
        jnp.concatenate([f('rwkv_w2'), z(32, 2 * GROUP_W)], axis=1),
        jnp.concatenate([z(32, GROUP_W), f('rwkv_a2'), z(32, GROUP_W)], axis=1),
        jnp.concatenate([z(64, 2 * GROUP_W), f('rwkv_g2')], axis=1)], axis=0).astype(BF16)
    rwkv = (row(f('rwkv_mu')), w_lora, row(f('rwkv_w0')), row(f('rwkv_a0')), row(f('rwkv_k_k')),
            row(f('rwkv_k_a')), row(f('rwkv_r_k')), row(f('rwkv_lnx_w')), row(f('rwkv_lnx_b')))

    wg2 = jnp.concatenate([f('gla_wg2'), z(128 - GLA_GATE_LORA, GLA_QK)], axis=0).astype(BF16)
    gla = (wg2, row(f('gla_bg')), row(f('gla_norm_g')))

    lr = f('s5_lam_re').astype(F32)
    li = f('s5_lam_im').astype(F32)
    dts = jnp.exp(f('s5_log_step').astype(F32))[:, None]
    mag = jnp.exp(lr * dts)
    ar = mag * jnp.cos(li * dts)
    ai = mag * jnp.sin(li * dts)
    den = lr * lr + li * li
    fr = ((ar - 1.0) * lr + ai * li) / den
    fi = (ai * lr - (ar - 1.0) * li) / den
    b_re, b_im = f('s5_b_re'), f('s5_b_im')
    bb_re = fr[..., None] * b_re - fi[..., None] * b_im
    bb_im = fr[..., None] * b_im + fi[..., None] * b_re
    wb = jnp.concatenate([_block_diag_in(bb_re), _block_diag_in(bb_im)], axis=1).astype(BF16)
    wc = jnp.stack([_block_diag_out(f('s5_c_re')), _block_diag_out(f('s5_c_im'))]).astype(BF16)
    s5 = (wb, wc, row(ar), row(ai), row(f('s5_d')), f('s5_w_glu').astype(BF16), row(f('s5_b_glu')))

    g3 = lambda name: f(name).reshape(1, 1, D_MODEL).astype(F32)
    return dict(
        in_w=in_w, ssd=ssd, rwkv=rwkv, gla=gla, s5=s5,
        g_mix_pre=g3('g_mix_pre'), g_mix_post=g3('g_mix_post'),
        g_ffn_pre=g3('g_ffn_pre'), g_ffn_post=g3('g_ffn_post'),
        w_out=f('w_out').reshape(4, GROUP_W, D_MODEL).astype(BF16),
        w1=f('mlp_w1').astype(BF16), w2=f('mlp_w2').astype(BF16))


def _run_group(x, mod, st, layers, cfg):
    bsz = x.shape[0]
    out = {n: [] for n in st}
    for l in range(DEPTH):
        P = layers[l]
        p_ssd, p_rwkv, p_gla, p_s5 = _in_proj(x, mod[l], P['g_mix_pre'], P['in_w'], *cfg['tok'])

        hist = jnp.pad(st['ssd_conv'][l], ((0, 0), (HIST - (SSD_CONV - 1), 0), (0, 0)))
        y_ssd, hist_new, ssd_new = _ssd(p_ssd, hist, st['ssd'][l], P['ssd'], *cfg['ssd'])
        y_rwkv, shift_new, rwkv_new = _rwkv(p_rwkv, st['rwkv_shift'][l][:, None, :], st['rwkv'][l],
                                            P['rwkv'], *cfg['rwkv'])
        y_gla, gla_new = _gla(p_gla, jnp.swapaxes(st['gla'][l], -1, -2), P['gla'], *cfg['gla'])
        y_s5, s5r_new, s5i_new = _s5(p_s5, st['s5_re'][l].reshape(bsz, 1, S5_CH),
                                     st['s5_im'][l].reshape(bsz, 1, S5_CH), P['s5'], *cfg['s5'])

        x = _out_mlp(x, (y_ssd, y_rwkv, y_gla, y_s5), mod[l], P['g_mix_post'], P['g_ffn_pre'],
                     P['g_ffn_post'], P['w_out'], P['w1'], P['w2'], *cfg['tok'])

        out['ssd'].append(ssd_new)
        out['ssd_conv'].append(hist_new[:, HIST - (SSD_CONV - 1):, :])
        out['rwkv'].append(rwkv_new)
        out['rwkv_shift'].append(shift_new[:, 0, :])
        out['gla'].append(jnp.swapaxes(gla_new, -1, -2))
        out['s5_re'].append(s5r_new.reshape(bsz, S5_GROUPS, S5_STATE))
        out['s5_im'].append(s5i_new.reshape(bsz, S5_GROUPS, S5_STATE))
    return x, {n: jnp.stack(v) for n, v in out.items()}


_PROMPT_CFG = dict(tok=(1, 512), ssd=(1, CHUNK), rwkv=(1, CHUNK), gla=(1, 256, GLA_CS), s5=(1, 128))
_SAMPLE_CFG = dict(tok=(64, 8), ssd=(8, 8), rwkv=(8, 8), gla=(4, 8, 8), s5=(64, 8))


def kernel(x_prompt, x_sample, c_prompt, c_sample, state_ssd, state_ssd_conv, state_rwkv, state_rwkv_shift, state_gla, state_s5_re, state_s5_im, w_ada, b_ada, g_mix_pre, g_mix_post, g_ffn_pre, g_ffn_post, w_in, w_out, ssd_conv_w, ssd_conv_b, ssd_a_log, ssd_dt_bias, ssd_d, ssd_norm_g, rwkv_mu, rwkv_w0, rwkv_w2, rwkv_a0, rwkv_a2, rwkv_g2, rwkv_k_k, rwkv_k_a, rwkv_r_k, rwkv_lnx_w, rwkv_lnx_b, gla_wg2, gla_bg, gla_norm_g, s5_lam_re, s5_lam_im, s5_log_step, s5_b_re, s5_b_im, s5_c_re, s5_c_im, s5_d, s5_w_glu, s5_b_glu, mlp_w1, mlp_w2):
    W = dict(w_in=w_in, w_out=w_out, g_mix_pre=g_mix_pre, g_mix_post=g_mix_post,
             g_ffn_pre=g_ffn_pre, g_ffn_post=g_ffn_post,
             ssd_conv_w=ssd_conv_w, ssd_conv_b=ssd_conv_b, ssd_a_log=ssd_a_log,
             ssd_dt_bias=ssd_dt_bias, ssd_d=ssd_d, ssd_norm_g=ssd_norm_g,
             rwkv_mu=rwkv_mu, rwkv_w0=rwkv_w0, rwkv_w2=rwkv_w2, rwkv_a0=rwkv_a0, rwkv_a2=rwkv_a2,
             rwkv_g2=rwkv_g2, rwkv_k_k=rwkv_k_k, rwkv_k_a=rwkv_k_a, rwkv_r_k=rwkv_r_k,
             rwkv_lnx_w=rwkv_lnx_w, rwkv_lnx_b=rwkv_lnx_b,
             gla_wg2=gla_wg2, gla_bg=gla_bg, gla_norm_g=gla_norm_g,
             s5_lam_re=s5_lam_re, s5_lam_im=s5_lam_im, s5_log_step=s5_log_step,
             s5_b_re=s5_b_re, s5_b_im=s5_b_im, s5_c_re=s5_c_re, s5_c_im=s5_c_im,
             s5_d=s5_d, s5_w_glu=s5_w_glu, s5_b_glu=s5_b_glu, mlp_w1=mlp_w1, mlp_w2=mlp_w2)
    layers = [_layer_params(l, W) for l in range(DEPTH)]

    nbp = x_prompt.shape[0]
    mod = _ada(jnp.concatenate([c_prompt, c_sample], axis=0), w_ada, b_ada)
    mod = mod.reshape(DEPTH, -1, N_MOD, D_MODEL)

    st_sample = dict(ssd=state_ssd, ssd_conv=state_ssd_conv, rwkv=state_rwkv,
                     rwkv_shift=state_rwkv_shift, gla=state_gla, s5_re=state_s5_re, s5_im=state_s5_im)
    st_prompt = {n: jnp.zeros((DEPTH, nbp) + v.shape[2:], v.dtype) for n, v in st_sample.items()}

    y_prompt, sp = _run_group(x_prompt, mod[:, :nbp], st_prompt, layers, _PROMPT_CFG)
    y_sample, ss = _run_group(x_sample, mod[:, nbp:], st_sample, layers, _SAMPLE_CFG)
    return (y_prompt, y_sample,
            sp['ssd'], ss['ssd'], sp['ssd_conv'], ss['ssd_conv'],
            sp['rwkv'], ss['rwkv'], sp['rwkv_shift'], ss['rwkv_shift'],
            sp['gla'], ss['gla'], sp['s5_re'], ss['s5_re'], sp['s5_im'], ss['s5_im'])
```

```python
import functools
import math

import jax
import jax.numpy as jnp
from jax import lax
from jax.experimental import pallas as pl
from jax.experimental.pallas import tpu as pltpu

F32 = jnp.float32
BF16 = jnp.bfloat16

D_MODEL = 1024
DEPTH = 2
GROUP_W = 256
D_FF = 4096
N_MOD = 6
EPS = 1e-6

SSD_HEADS = 4
SSD_HEADDIM = 64
SSD_STATE = 64
SSD_NGROUPS = 2
SSD_BC = SSD_NGROUPS * SSD_STATE
SSD_CONV = 4
SSD_CONV_CH = GROUP_W + 2 * SSD_BC
SSD_PROJ = GROUP_W + SSD_CONV_CH + SSD_HEADS
SSD_PW = GROUP_W + SSD_CONV_CH + GROUP_W

RWKV_HEADS = 4
RWKV_HEADDIM = 64
RWKV_LORA = 128
RWKV_PROJ = 3 * GROUP_W + RWKV_LORA
RWKV_LN_EPS = 64e-5

GLA_HEADS = 4
GLA_DK = 32
GLA_DV = 64
GLA_GATE_LORA = 16
GLA_TAU = 16.0
GLA_QK = GLA_HEADS * GLA_DK
GLA_PROJ = 2 * GLA_QK + GROUP_W + GLA_GATE_LORA + GROUP_W
GLA_PW = 2 * GLA_QK + GROUP_W + 128 + GROUP_W

S5_GROUP = 16
S5_GROUPS = 16
S5_STATE = 64
S5_CH = S5_GROUPS * S5_STATE
S5_PROJ = GROUP_W

HIST = 8
CHUNK = 64
GLA_CS = 32
VMEM_LIMIT = 56 * 1024 * 1024


def _dot(a, b, dims=(((1,), (0,)), ((), ()))):
    return lax.dot_general(a.astype(BF16), b.astype(BF16), dims, preferred_element_type=F32)


_NT = (((1,), (1,)), ((), ()))
_TN = (((0,), (0,)), ((), ()))
_NN = (((1,), (0,)), ((), ()))


def _split2(a):
    hi = a.astype(BF16)
    lo = (a - hi.astype(F32)).astype(BF16)
    return hi, lo


def _dot3(a, b, dims=_NN):
    ah, al = _split2(a)
    bh, bl = _split2(b)
    d = lambda x, y: lax.dot_general(x, y, dims, preferred_element_type=F32)
    return d(ah, bh) + (d(ah, bl) + d(al, bh))


def _dot_exact_lhs(m, x, dims=_NN):
    mb = m.astype(BF16)
    x1 = x.astype(BF16)
    r1 = x - x1.astype(F32)
    x2 = r1.astype(BF16)
    x3 = (r1 - x2.astype(F32)).astype(BF16)
    d = lambda y: lax.dot_general(mb, y, dims, preferred_element_type=F32)
    return d(x1) + (d(x2) + d(x3))


def _segsum(x, width):
    n = x.shape[-1]
    r = lax.broadcasted_iota(jnp.int32, (n, n), 0) // width
    c = lax.broadcasted_iota(jnp.int32, (n, n), 1) // width
    j = jnp.where(r == c, 1.0, 0.0).astype(BF16)
    x1 = x.astype(BF16)
    r1 = x - x1.astype(F32)
    x2 = r1.astype(BF16)
    x3 = (r1 - x2.astype(F32)).astype(BF16)
    d = lambda y: lax.dot_general(y, j, _NN, preferred_element_type=F32)
    return d(x1) + (d(x2) + d(x3))


def _sigmoid(x):
    return 1.0 / (1.0 + jnp.exp(-x))


def _silu(x):
    return x * _sigmoid(x)


def _softplus(x):
    return jnp.maximum(x, 0.0) + jnp.log1p(jnp.exp(-jnp.abs(x)))


def _rms(x, g):
    return x * lax.rsqrt(jnp.mean(x * x, axis=-1, keepdims=True) + EPS) * g


def _chunk_masks(rows, lt):
    r = lax.broadcasted_iota(jnp.int32, (rows, rows), 0)
    c = lax.broadcasted_iota(jnp.int32, (rows, rows), 1)
    same = (r // lt) == (c // lt)
    return same & (c <= r), same & (c < r)


def _bcast_last(x, nb, lt):
    c = x.shape[-1]
    last = x.reshape(nb, lt, c)[:, lt - 1:lt, :]
    return last, jnp.broadcast_to(last, (nb, lt, c)).reshape(nb * lt, c)


def _ada_kernel(c_ref, w_ref, b_ref, o_ref):
    c = c_ref[...]
    o_ref[0] = _dot(_silu(c), w_ref[0]) + b_ref[0]


def _ada(c_all, w_ada, b_ada):
    rows = c_all.shape[0]
    n = N_MOD * D_MODEL
    tn = 1536
    return pl.pallas_call(
        _ada_kernel,
        out_shape=jax.ShapeDtypeStruct((DEPTH, rows, n), F32),
        grid=(DEPTH, n // tn),
        in_specs=[
            pl.BlockSpec((rows, D_MODEL), lambda l, j: (0, 0)),
            pl.BlockSpec((1, D_MODEL, tn), lambda l, j: (l, 0, j)),
            pl.BlockSpec((1, 1, tn), lambda l, j: (l, 0, j)),
        ],
        out_specs=pl.BlockSpec((1, rows, tn), lambda l, j: (l, 0, j)),
        compiler_params=pltpu.CompilerParams(
            dimension_semantics=("parallel", "parallel"), vmem_limit_bytes=VMEM_LIMIT),
        name="ada",
    )(c_all, w_ada, b_ada.reshape(DEPTH, 1, n))


def _in_kernel(x_ref, mod_ref, g_ref, wssd_ref, wrwkv_ref, wgla_ref, ws5_ref,
               ossd_ref, orwkv_ref, ogla_ref, os5_ref):
    nb, lt, d = x_ref.shape
    m = mod_ref[...]
    h = _rms(x_ref[...], g_ref[...]) * (1.0 + m[:, 1:2, :]) + m[:, 0:1, :]
    hb = h.reshape(nb * lt, d).astype(BF16)
    for w_ref, o_ref in ((wssd_ref, ossd_ref), (wrwkv_ref, orwkv_ref),
                         (wgla_ref, ogla_ref), (ws5_ref, os5_ref)):
        o_ref[...] = jnp.dot(hb, w_ref[...], preferred_element_type=F32).reshape(o_ref.shape)


def _in_proj(x, mod, g, wts, nb, lt):
    bsz, seq, d = x.shape
    widths = (SSD_PW, RWKV_PROJ, GLA_PW, S5_PROJ)
    const2 = lambda i, j: (0, 0)
    return pl.pallas_call(
        _in_kernel,
        out_shape=tuple(jax.ShapeDtypeStruct((bsz, seq, w), F32) for w in widths),
        grid=(bsz // nb, seq // lt),
        in_specs=[
            pl.BlockSpec((nb, lt, d), lambda i, j: (i, j, 0)),
            pl.BlockSpec((nb, N_MOD, d), lambda i, j: (i, 0, 0)),
            pl.BlockSpec((1, 1, d), lambda i, j: (0, 0, 0)),
        ] + [pl.BlockSpec((d, w), const2) for w in widths],
        out_specs=tuple(pl.BlockSpec((nb, lt, w), lambda i, j: (i, j, 0)) for w in widths),
        compiler_params=pltpu.CompilerParams(
            dimension_semantics=("parallel", "parallel"), vmem_limit_bytes=VMEM_LIMIT),
        name="in_proj",
    )(x, mod, g, *wts)


def _out_mlp_kernel(x_ref, y0_ref, y1_ref, y2_ref, y3_ref, mod_ref, gpost_ref, gpre_ref, gfpost_ref,
                    wout_ref, w1_ref, w2_ref, o_ref):
    nb, lt, d = x_ref.shape
    rows = nb * lt
    m = mod_ref[...]
    mix = None
    for i, y_ref in enumerate((y0_ref, y1_ref, y2_ref, y3_ref)):
        t = jnp.dot(y_ref[...].reshape(rows, GROUP_W).astype(BF16), wout_ref[i],
                    preferred_element_type=F32)
        mix = t if mix is None else mix + t
    x1 = x_ref[...] + m[:, 2:3, :] * _rms(mix, gpost_ref[0]).reshape(nb, lt, d)
    h = _rms(x1, gpre_ref[...]) * (1.0 + m[:, 4:5, :]) + m[:, 3:4, :]
    hb = h.reshape(rows, d).astype(BF16)
    acc = None
    step = 1024
    for j in range(D_FF // step):
        a = jnp.dot(hb, w1_ref[:, j * step:(j + 1) * step], preferred_element_type=F32)
        a = jnp.square(jnp.maximum(a, 0.0)).astype(BF16)
        t = jnp.dot(a, w2_ref[j * step:(j + 1) * step, :], preferred_element_type=F32)
        acc = t if acc is None else acc + t
    o_ref[...] = x1 + m[:, 5:6, :] * _rms(acc, gfpost_ref[0]).reshape(nb, lt, d)


def _out_mlp(x, ys, mod, gpost, gpre, gfpost, wout, w1, w2, nb, lt):
    bsz, seq, d = x.shape
    tok = lambda i, j: (i, j, 0)
    c3 = lambda i, j: (0, 0, 0)
    c2 = lambda i, j: (0, 0)
    one = pl.Buffered(1)
    return pl.pallas_call(
        _out_mlp_kernel,
        out_shape=jax.ShapeDtypeStruct((bsz, seq, d), F32),
        grid=(bsz // nb, seq // lt),
        in_specs=[pl.BlockSpec((nb, lt, d), tok)]
        + [pl.BlockSpec((nb, lt, GROUP_W), tok)] * 4
        + [pl.BlockSpec((nb, N_MOD, d), lambda i, j: (i, 0, 0))]
        + [pl.BlockSpec((1, 1, d), c3)] * 3
        + [pl.BlockSpec((4, GROUP_W, d), c3, pipeline_mode=one),
           pl.BlockSpec((d, D_FF), c2, pipeline_mode=one),
           pl.BlockSpec((D_FF, d), c2, pipeline_mode=one)],
        out_specs=pl.BlockSpec((nb, lt, d), tok),
        compiler_params=pltpu.CompilerParams(
            dimension_semantics=("parallel", "parallel"), vmem_limit_bytes=VMEM_LIMIT),
        name="out_mlp",
    )(x, *ys, mod, gpost, gpre, gfpost, wout, w1, w2)


def _ssd_kernel(p_ref, hist_ref, s0_ref, cw_ref, cb_ref, alog_ref, dtb_ref, dsk_ref, ng_ref,
                y_ref, hout_ref, sout_ref, ext_ref, yscr_ref, *, nb, lt):
    rows = nb * lt
    hd = SSD_HEADDIM

    @pl.when(pl.program_id(1) == 0)
    def _():
        ext_ref[:, 0:HIST, :] = hist_ref[...]
        sout_ref[...] = s0_ref[...]

    p = p_ref[...]
    ext_ref[:, HIST:HIST + lt, :] = p[:, :, GROUP_W:GROUP_W + SSD_CONV_CH]
    conv = cb_ref[...]
    for j in range(SSD_CONV):
        o = HIST - (SSD_CONV - 1) + j
        conv = conv + ext_ref[:, o:o + lt, :] * cw_ref[j:j + 1, :]
    tail = ext_ref[:, lt:lt + HIST, :]
    ext_ref[:, 0:HIST, :] = tail
    hout_ref[...] = tail

    xbc = _silu(conv).reshape(rows, SSD_CONV_CH)
    xs = xbc[:, 0:GROUP_W]
    bm = xbc[:, GROUP_W:GROUP_W + SSD_BC]
    cm = xbc[:, GROUP_W + SSD_BC:]
    z = p[:, :, 0:GROUP_W].reshape(rows, GROUP_W)
    dt = _softplus(p[:, :, GROUP_W + SSD_CONV_CH:].reshape(rows, GROUP_W) + dtb_ref[...])
    a = dt * (-jnp.exp(alog_ref[...]))
    incl, _ = _chunk_masks(rows, lt)
    cum = _dot_exact_lhs(jnp.where(incl, 1.0, 0.0), a)
    last, last_b = _bcast_last(cum, nb, lt)
    xdt = xs * dt
    xw = xdt * jnp.exp(last_b - cum)
    ecum = jnp.exp(cum)
    elast = jnp.exp(last)
    ones_row = jnp.full((rows, hd), 1.0 / hd, F32)

    for g in range(SSD_NGROUPS):
        cg = cm[:, g * SSD_STATE:(g + 1) * SSD_STATE]
        bg = bm[:, g * SSD_STATE:(g + 1) * SSD_STATE]
        cb = _dot(cg, bg, _NT)
        for h in range(g * 2, g * 2 + 2):
            sl = slice(h * hd, (h + 1) * hd)
            ccol = cum[:, sl]
            crow = _dot_exact_lhs(ones_row, ccol, _NT)
            decay = jnp.exp(jnp.where(incl, ccol - crow, -jnp.inf))
            yh = _dot(cb * decay, xdt[:, sl])
            inter = []
            for i in range(nb):
                rs = slice(i * lt, (i + 1) * lt)
                s_old = sout_ref[i, h]
                inter.append(_dot(cg[rs], s_old, _NT))
                sout_ref[i, h] = s_old * elast[i][:, sl] + _dot(xw[rs, sl], bg[rs], _TN)
            inter = inter[0] if nb == 1 else jnp.concatenate(inter, axis=0)
            yscr_ref[:, sl] = yh + inter * ecum[:, sl]

    y = yscr_ref[...] + dsk_ref[...] * xs
    y = y * _silu(z)
    y = y * lax.rsqrt(_segsum(y * y, hd) * (1.0 / hd) + EPS) * ng_ref[...]
    y_ref[...] = y.reshape(nb, lt, GROUP_W)


def _ssd(p, hist, s0, prm, nb, lt):
    bsz, seq, _ = p.shape
    assert nb * lt == CHUNK and CHUNK == SSD_HEADDIM
    row = lambda i, c: (0, 0)
    kern = functools.partial(_ssd_kernel, nb=nb, lt=lt)
    return pl.pallas_call(
        kern,
        out_shape=(jax.ShapeDtypeStruct((bsz, seq, GROUP_W), F32),
                   jax.ShapeDtypeStruct((bsz, HIST, SSD_CONV_CH), F32),
                   jax.ShapeDtypeStruct(s0.shape, F32)),
        grid=(bsz // nb, seq // lt),
        in_specs=[
            pl.BlockSpec((nb, lt, SSD_PW), lambda i, c: (i, c, 0)),
            pl.BlockSpec((nb, HIST, SSD_CONV_CH), lambda i, c: (i, 0, 0)),
            pl.BlockSpec((nb,) + s0.shape[1:], lambda i, c: (i, 0, 0, 0)),
            pl.BlockSpec((SSD_CONV, SSD_CONV_CH), row),
            pl.BlockSpec((1, SSD_CONV_CH), row),
        ] + [pl.BlockSpec((1, GROUP_W), row)] * 4,
        out_specs=(pl.BlockSpec((nb, lt, GROUP_W), lambda i, c: (i, c, 0)),
                   pl.BlockSpec((nb, HIST, SSD_CONV_CH), lambda i, c: (i, 0, 0)),
                   pl.BlockSpec((nb,) + s0.shape[1:], lambda i, c: (i, 0, 0, 0))),
        scratch_shapes=[pltpu.VMEM((nb, HIST + lt, SSD_CONV_CH), F32),
                        pltpu.VMEM((nb * lt, GROUP_W), F32)],
        compiler_params=pltpu.CompilerParams(
            dimension_semantics=("parallel", "arbitrary"), vmem_limit_bytes=VMEM_LIMIT),
        name="ssd",
    )(p, hist, s0, *prm)


def _unit_lower_inverse(a2, n, levels):
    ri = lax.broadcasted_iota(jnp.int32, (n, n), 0)
    ci = lax.broadcasted_iota(jnp.int32, (n, n), 1)
    x = [jnp.where(ri == ci, 1.0, 0.0) - jnp.where((ri % 2 == 1) & (ci == ri - 1), a, 0.0) for a in a2]
    s = 2
    while s < levels:
        off = ((ri // s) % 2 == 1) & ((ci // s) == (ri // s) - 1)
        t = [_dot3(xi, jnp.where(off, a, 0.0)) for xi, a in zip(x, a2)]
        x = [xi - _dot3(ti, xi) for xi, ti in zip(x, t)]
        s *= 2
    return x


def _rwkv_chunk_multi(ops, sout_ref, yscr_ref, nb, lt):
    hd = RWKV_HEADDIM
    rows = nb * lt
    heads = range(RWKV_HEADS)
    incl, strict = _chunk_masks(rows, lt)
    hs = lambda name: [ops[name][:, h * hd:(h + 1) * hd] for h in heads]
    rt, kt, k2t, bt, k2g, bg, v = (hs(n) for n in ('rt', 'kt', 'k2t', 'bt', 'k2g', 'bg', 'v'))
    a1 = [jnp.where(strict, _dot3(kt[h], k2t[h], _NT), 0.0) for h in heads]
    a2 = [jnp.where(strict, _dot3(kt[h], bt[h], _NT), 0.0) for h in heads]
    l1 = [jnp.where(incl, _dot3(rt[h], k2t[h], _NT), 0.0) for h in heads]
    l2 = [jnp.where(incl, _dot3(rt[h], bt[h], _NT), 0.0) for h in heads]
    x = _unit_lower_inverse(a2, rows, lt)
    a1v = [_dot3(a1[h], v[h]) for h in heads]
    l1v = [_dot3(l1[h], v[h]) for h in heads]
    cat = lambda parts: parts[0] if nb == 1 else jnp.concatenate(parts, axis=0)
    seq = lambda i: slice(i * lt, (i + 1) * lt)
    s_old = [[sout_ref[i, h] for i in range(nb)] for h in heads]
    p0 = [cat([_dot3(kt[h][seq(i)], s_old[h][i], _NT) for i in range(nb)]) for h in heads]
    q0 = [cat([_dot3(rt[h][seq(i)], s_old[h][i], _NT) for i in range(nb)]) for h in heads]
    u = [_dot3(x[h], p0[h] + a1v[h]) for h in heads]
    l2u = [_dot3(l2[h], u[h]) for h in heads]
    for h in heads:
        yscr_ref[:, h * hd:(h + 1) * hd] = q0[h] + l1v[h] - l2u[h]
        for i in range(nb):
            sout_ref[i, h] = (s_old[h][i] * ops['elast'][i][:, h * hd:(h + 1) * hd]
                              + _dot3(v[h][seq(i)], k2g[h][seq(i)], _TN)
                              - _dot3(u[h][seq(i)], bg[h][seq(i)], _TN))


def _rwkv_chunks_seq(ops, sout_ref, yscr_ref, nc):
    hd = RWKV_HEADDIM
    idx = [(c, h) for c in range(nc) for h in range(RWKV_HEADS)]
    incl, strict = _chunk_masks(CHUNK, CHUNK)
    ri = lax.broadcasted_iota(jnp.int32, (hd, hd), 0)
    ci = lax.broadcasted_iota(jnp.int32, (hd, hd), 1)
    eye = jnp.where(ri == ci, 1.0, 0.0)
    bl = lambda name: [ops[name][c * CHUNK:(c + 1) * CHUNK, h * hd:(h + 1) * hd] for c, h in idx]
    rt, kt, k2t, bt, k2g, bg, v = (bl(n) for n in ('rt', 'kt', 'k2t', 'bt', 'k2g', 'bg', 'v'))
    n = range(len(idx))
    a1 = [jnp.where(strict, _dot3(kt[j], k2t[j], _NT), 0.0) for j in n]
    a2 = [jnp.where(strict, _dot3(kt[j], bt[j], _NT), 0.0) for j in n]
    l1 = [jnp.where(incl, _dot3(rt[j], k2t[j], _NT), 0.0) for j in n]
    l2 = [jnp.where(incl, _dot3(rt[j], bt[j], _NT), 0.0) for j in n]
    x = _unit_lower_inverse(a2, CHUNK, CHUNK)
    xa1 = [_dot3(x[j], a1[j]) for j in n]
    l2x = [_dot3(l2[j], x[j]) for j in n]
    kp = [_dot3(x[j], kt[j]) for j in n]
    rp = [rt[j] - _dot3(l2x[j], kt[j]) for j in n]
    yl = [_dot3(l1[j] - _dot3(l2[j], xa1[j]), v[j]) for j in n]
    ul = [_dot3(xa1[j], v[j]) for j in n]
    mc = [eye * ops['elast'][c][:, h * hd:(h + 1) * hd] - _dot3(kp[j], bg[j], _TN)
          for j, (c, h) in enumerate(idx)]
    dc = [_dot3(v[j], k2g[j], _TN) - _dot3(ul[j], bg[j], _TN) for j in n]
    state = [sout_ref[0, h] for h in range(RWKV_HEADS)]
    for c in range(nc):
        js = [c * RWKV_HEADS + h for h in range(RWKV_HEADS)]
        ys = [_dot3(rp[j], state[h], _NT) + yl[j] for h, j in enumerate(js)]
        state = [_dot3(state[h], mc[j]) + dc[j] for h, j in enumerate(js)]
        for h in range(RWKV_HEADS):
            yscr_ref[c * CHUNK:(c + 1) * CHUNK, h * hd:(h + 1) * hd] = ys[h]
    for h in range(RWKV_HEADS):
        sout_ref[0, h] = state[h]


def _rwkv_kernel(p_ref, sh0_ref, s0_ref, mu_ref, wl_ref, w0_ref, a0_ref, kk_ref, ka_ref, rk_ref,
                 lnw_ref, lnb_ref, y_ref, shout_ref, sout_ref, ext_ref, yscr_ref, *, nb, lt):
    rows = nb * lt
    hd = RWKV_HEADDIM
    gw = GROUP_W

    @pl.when(pl.program_id(1) == 0)
    def _():
        ext_ref[:, HIST - 1:HIST, :] = sh0_ref[...]
        sout_ref[...] = s0_ref[...]

    p = p_ref[...]
    ext_ref[:, HIST:HIST + lt, :] = p
    prev = ext_ref[:, HIST - 1:HIST - 1 + lt, :]
    pm = (p + (prev - p) * mu_ref[...]).reshape(rows, RWKV_PROJ)
    last_p = p[:, lt - 1:lt, :]
    ext_ref[:, HIST - 1:HIST, :] = last_p
    shout_ref[...] = last_p

    r = pm[:, 0:gw]
    k = pm[:, gw:2 * gw]
    v = pm[:, 2 * gw:3 * gw]
    lo = pm[:, 3 * gw:]
    lane = lax.broadcasted_iota(jnp.int32, (rows, RWKV_LORA), 1)
    t = jnp.where(lane < 32, jnp.tanh(lo), jnp.where(lane < 64, lo, _sigmoid(lo)))
    lora = _dot(t, wl_ref[...])
    wlog = -_softplus(-(w0_ref[...] + lora[:, 0:gw])) - 0.5
    lw = -jnp.exp(wlog)
    a = _sigmoid(a0_ref[...] + lora[:, gw:2 * gw])
    gate = lora[:, 2 * gw:]
    kk = k * kk_ref[...]
    kk = kk / jnp.maximum(jnp.sqrt(_segsum(kk * kk, hd)), 1e-12)
    k2 = k * (1.0 + (a - 1.0) * ka_ref[...])
    beta = kk * a

    unit = lt if nb > 1 else CHUNK
    incl, _ = _chunk_masks(rows, unit)
    gcum = _dot_exact_lhs(jnp.where(incl, 1.0, 0.0), lw)
    glast, glast_b = _bcast_last(gcum, rows // unit, unit)
    eg = jnp.exp(gcum)
    einv = jnp.exp(-gcum)
    etail = jnp.exp(glast_b - gcum)
    ops = dict(rt=r * eg, kt=kk * jnp.exp(gcum - lw), k2t=k2 * einv, bt=beta * einv,
               k2g=k2 * etail, bg=beta * etail, v=v, elast=jnp.exp(glast))
    if nb > 1:
        _rwkv_chunk_multi(ops, sout_ref, yscr_ref, nb, lt)
    else:
        _rwkv_chunks_seq(ops, sout_ref, yscr_ref, lt // CHUNK)

    y = yscr_ref[...]
    mean = _segsum(y, hd) * (1.0 / hd)
    dlt = y - mean
    var = _segsum(dlt * dlt, hd) * (1.0 / hd)
    yn = dlt * lax.rsqrt(var + RWKV_LN_EPS) * lnw_ref[...] + lnb_ref[...]
    bonus = _segsum(r * k2 * rk_ref[...], hd) * v
    y_ref[...] = ((yn + bonus) * gate).reshape(nb, lt, gw)


def _rwkv(p, sh0, s0, prm, nb, lt):
    bsz, seq, _ = p.shape
    assert (nb > 1 and nb * lt == CHUNK) or (nb == 1 and lt % CHUNK == 0)
    row = lambda i, c: (0, 0)
    kern = functools.partial(_rwkv_kernel, nb=nb, lt=lt)
    return pl.pallas_call(
        kern,
        out_shape=(jax.ShapeDtypeStruct((bsz, seq, GROUP_W), F32),
                   jax.ShapeDtypeStruct((bsz, 1, RWKV_PROJ), F32),
                   jax.ShapeDtypeStruct(s0.shape, F32)),
        grid=(bsz // nb, seq // lt),
        in_specs=[
            pl.BlockSpec((nb, lt, RWKV_PROJ), lambda i, c: (i, c, 0)),
            pl.BlockSpec((nb, 1, RWKV_PROJ), lambda i, c: (i, 0, 0)),
            pl.BlockSpec((nb,) + s0.shape[1:], lambda i, c: (i, 0, 0, 0)),
            pl.BlockSpec((1, RWKV_PROJ), row),
            pl.BlockSpec((RWKV_LORA, 3 * GROUP_W), row),
        ] + [pl.BlockSpec((1, GROUP_W), row)] * 7,
        out_specs=(pl.BlockSpec((nb, lt, GROUP_W), lambda i, c: (i, c, 0)),
                   pl.BlockSpec((nb, 1, RWKV_PROJ), lambda i, c: (i, 0, 0)),
                   pl.BlockSpec((nb,) + s0.shape[1:], lambda i, c: (i, 0, 0, 0))),
        scratch_shapes=[pltpu.VMEM((nb, HIST + lt, RWKV_PROJ), F32),
                        pltpu.VMEM((nb * lt, GROUP_W), F32)],
        compiler_params=pltpu.CompilerParams(
            dimension_semantics=("parallel", "arbitrary"), vmem_limit_bytes=VMEM_LIMIT),
        name="rwkv",
    )(p, sh0, s0, *prm)


def _gla_kernel(p_ref, s0_ref, wg2_ref, bg_ref, ng_ref, y_ref, sout_ref, yscr_ref, *, nb, lt, cs):
    rows = nb * lt
    dk, dv = GLA_DK, GLA_DV
    qk = GLA_QK
    units = rows // cs
    per_group = GLA_CS // cs

    @pl.when(pl.program_id(1) == 0)
    def _():
        sout_ref[...] = s0_ref[...]

    p = p_ref[...].reshape(rows, GLA_PW)
    q = p[:, 0:qk] * (dk ** -0.5)
    k = p[:, qk:2 * qk]
    v = p[:, 2 * qk:2 * qk + GROUP_W]
    dg = p[:, 2 * qk + GROUP_W:2 * qk + GROUP_W + 128]
    og = p[:, 2 * qk + GROUP_W + 128:]
    lg = -_softplus(-(_dot(dg, wg2_ref[...]) + bg_ref[...])) * (1.0 / GLA_TAU)
    incl, _ = _chunk_masks(rows, cs)
    b = _dot_exact_lhs(jnp.where(incl, 1.0, 0.0), lg)
    bl, bl_b = _bcast_last(b, units, cs)
    qd = q * jnp.exp(b)
    kd = k * jnp.exp(-b)
    kw = k * jnp.exp(bl_b - b)
    ebl = jnp.exp(bl)
    gmask, _ = _chunk_masks(GLA_CS, cs)

    for h in range(GLA_HEADS):
        ks = slice(h * dk, (h + 1) * dk)
        vs = slice(h * dv, (h + 1) * dv)
        states = [sout_ref[i, h] for i in range(nb)]
        for g in range(rows // GLA_CS):
            gs = slice(g * GLA_CS, (g + 1) * GLA_CS)
            att = jnp.where(gmask, _dot(qd[gs, ks], kd[gs, ks], _NT), 0.0)
            yh = _dot(att, v[gs, vs])
            inter = []
            for j in range(per_group):
                un = g * per_group + j
                us = slice(un * cs, (un + 1) * cs)
                i = un if nb > 1 else 0
                inter.append(_dot(qd[us, ks], states[i], _NT))
                states[i] = states[i] * ebl[un][:, ks] + _dot(v[us, vs], kw[us, ks], _TN)
            inter = inter[0] if per_group == 1 else jnp.concatenate(inter, axis=0)
            yscr_ref[gs, vs] = yh + inter
        for i in range(nb):
            sout_ref[i, h] = states[i]

    y = yscr_ref[...]
    y = y * lax.rsqrt(_segsum(y * y, dv) * (1.0 / dv) + EPS) * ng_ref[...]
    y_ref[...] = (y * _silu(og)).reshape(nb, lt, GROUP_W)


def _gla(p, s0, prm, nb, lt, cs):
    bsz, seq, _ = p.shape
    assert (nb == 1 and cs == GLA_CS and lt % cs == 0) or (lt == cs and nb * cs == GLA_CS)
    row = lambda i, c: (0, 0)
    kern = functools.partial(_gla_kernel, nb=nb, lt=lt, cs=cs)
    return pl.pallas_call(
        kern,
        out_shape=(jax.ShapeDtypeStruct((bsz, seq, GROUP_W), F32),
                   jax.ShapeDtypeStruct(s0.shape, F32)),
        grid=(bsz // nb, seq // lt),
        in_specs=[
            pl.BlockSpec((nb, lt, GLA_PW), lambda i, c: (i, c, 0)),
            pl.BlockSpec((nb,) + s0.shape[1:], lambda i, c: (i, 0, 0, 0)),
            pl.BlockSpec((128, GLA_QK), row),
            pl.BlockSpec((1, GLA_QK), row),
            pl.BlockSpec((1, GROUP_W), row),
        ],
        out_specs=(pl.BlockSpec((nb, lt, GROUP_W), lambda i, c: (i, c, 0)),
                   pl.BlockSpec((nb,) + s0.shape[1:], lambda i, c: (i, 0, 0, 0))),
        scratch_shapes=[pltpu.VMEM((nb * lt, GROUP_W), F32)],
        compiler_params=pltpu.CompilerParams(
            dimension_semantics=("parallel", "arbitrary"), vmem_limit_bytes=VMEM_LIMIT),
        name="gla",
    )(p, s0, *prm)


def _s5_kernel(u_ref, hr0_ref, hi0_ref, wb_ref, wc_ref, ar_ref, ai_ref, d_ref, wglu_ref, bglu_ref,
               y_ref, hr_ref, hi_ref, *, nb, lt):
    rows = nb * lt
    n = S5_CH

    @pl.when(pl.program_id(1) == 0)
    def _():
        hr_ref[...] = hr0_ref[...]
        hi_ref[...] = hi0_ref[...]

    u = u_ref[...].reshape(rows, GROUP_W)
    bu = _dot(u, wb_ref[...])
    ar = ar_ref[...]
    ai = ai_ref[...]
    h0r = hr_ref[...]
    h0i = hi_ref[...]
    cr = jnp.broadcast_to(ar * h0r - ai * h0i, (nb, lt, n)).reshape(rows, n)
    ci = jnp.broadcast_to(ar * h0i + ai * h0r, (nb, lt, n)).reshape(rows, n)
    pos = lax.broadcasted_iota(jnp.int32, (rows, n), 0) % lt
    hr = bu[:, 0:n] + jnp.where(pos == 0, cr, 0.0)
    hi = bu[:, n:] + jnp.where(pos == 0, ci, 0.0)
    pr, pi = ar, ai
    s = 1
    while s < lt:
        sr = jnp.where(pos >= s, pltpu.roll(hr, s, 0), 0.0)
        si = jnp.where(pos >= s, pltpu.roll(hi, s, 0), 0.0)
        hr, hi = hr + (pr * sr - pi * si), hi + (pr * si + pi * sr)
        pr, pi = pr * pr - pi * pi, 2.0 * (pr * pi)
        s *= 2
    hr_ref[...] = hr.reshape(nb, lt, n)[:, lt - 1:lt, :]
    hi_ref[...] = hi.reshape(nb, lt, n)[:, lt - 1:lt, :]
    y = _dot(hr, wc_ref[0]) - _dot(hi, wc_ref[1])
    y = y + d_ref[...] * u
    y = 0.5 * y * (1.0 + jnp.tanh(math.sqrt(2.0 / math.pi) * (y + 0.044715 * (y * y * y))))
    y = y * _sigmoid(_dot(y, wglu_ref[...]) + bglu_ref[...])
    y_ref[...] = y.reshape(nb, lt, GROUP_W)


def _s5(u, hr0, hi0, prm, nb, lt):
    bsz, seq, _ = u.shape
    row = lambda i, c: (0, 0)
    st = pl.BlockSpec((nb, 1, S5_CH), lambda i, c: (i, 0, 0))
    kern = functools.partial(_s5_kernel, nb=nb, lt=lt)
    return pl.pallas_call(
        kern,
        out_shape=(jax.ShapeDtypeStruct((bsz, seq, GROUP_W), F32),
                   jax.ShapeDtypeStruct((bsz, 1, S5_CH), F32),
                   jax.ShapeDtypeStruct((bsz, 1, S5_CH), F32)),
        grid=(bsz // nb, seq // lt),
        in_specs=[
            pl.BlockSpec((nb, lt, GROUP_W), lambda i, c: (i, c, 0)), st, st,
            pl.BlockSpec((GROUP_W, 2 * S5_CH), row),
            pl.BlockSpec((2, S5_CH, GROUP_W), lambda i, c: (0, 0, 0)),
            pl.BlockSpec((1, S5_CH), row),
            pl.BlockSpec((1, S5_CH), row),
            pl.BlockSpec((1, GROUP_W), row),
            pl.BlockSpec((GROUP_W, GROUP_W), row),
            pl.BlockSpec((1, GROUP_W), row),
        ],
        out_specs=(pl.BlockSpec((nb, lt, GROUP_W), lambda i, c: (i, c, 0)), st, st),
        compiler_params=pltpu.CompilerParams(
            dimension_semantics=("parallel", "arbitrary"), vmem_limit_bytes=VMEM_LIMIT),
        name="s5",
    )(u, hr0, hi0, *prm)


def _block_diag_in(b):
    g, n, i = b.shape
    eye = jnp.eye(g, dtype=b.dtype)
    return jnp.einsum('gni,gh->gihn', b, eye).reshape(g * i, g * n)


def _block_diag_out(c):
    g, o, n = c.shape
    eye = jnp.eye(g, dtype=c.dtype)
    return jnp.einsum('gon,gh->gnho', c, eye).reshape(g * n, g * o)


def _layer_params(l, W):
    f = lambda name: W[name][l]
    row = lambda a: a.reshape(1, -1).astype(F32)
    rep = lambda a, n: jnp.repeat(a.reshape(-1), n).reshape(1, -1).astype(F32)
    w_in = f('w_in')
    o_r = SSD_PROJ
    o_g = o_r + RWKV_PROJ
    o_s = o_g + GLA_PROJ
    w_ssd = jnp.concatenate([w_in[:, :GROUP_W + SSD_CONV_CH],
                             jnp.repeat(w_in[:, GROUP_W + SSD_CONV_CH:o_r], SSD_HEADDIM, axis=1)], axis=1)
    w_rwkv = w_in[:, o_r:o_g]
    gq = o_g + 2 * GLA_QK + GROUP_W
    w_gla = jnp.concatenate([w_in[:, o_g:gq], w_in[:, gq:gq + GLA_GATE_LORA],
                             jnp.zeros((D_MODEL, 128 - GLA_GATE_LORA), F32),
                             w_in[:, gq + GLA_GATE_LORA:o_s]], axis=1)
    w_s5 = w_in[:, o_s:]
    in_w = tuple(w.astype(BF16) for w in (w_ssd, w_rwkv, w_gla, w_s5))

    ssd = (f('ssd_conv_w'), row(f('ssd_conv_b')), rep(f('ssd_a_log'), SSD_HEADDIM),
           rep(f('ssd_dt_bias'), SSD_HEADDIM), rep(f('ssd_d'), SSD_HEADDIM), row(f('ssd_norm_g')))

    z = lambda r, c: jnp.zeros((r, c), F32)
    w_lora = jnp.concatenate([
        jnp.concatenate([f('rwkv_w2'), z(32, 2 * GROUP_W)], axis=1),
        jnp.concatenate([z(32, GROUP_W), f('rwkv_a2'), z(32, GROUP_W)], axis=1),
        jnp.concatenate([z(64, 2 * GROUP_W), f('rwkv_g2')], axis=1)], axis=0).astype(BF16)
    rwkv = (row(f('rwkv_mu')), w_lora, row(f('rwkv_w0')), row(f('rwkv_a0')), row(f('rwkv_k_k')),
            row(f('rwkv_k_a')), row(f('rwkv_r_k')), row(f('rwkv_lnx_w')), row(f('rwkv_lnx_b')))

    wg2 = jnp.concatenate([f('gla_wg2'), z(128 - GLA_GATE_LORA, GLA_QK)], axis=0).astype(BF16)
    gla = (wg2, row(f('gla_bg')), row(f('gla_norm_g')))

    lr = f('s5_lam_re').astype(F32)
    li = f('s5_lam_im').astype(F32)
    dts = jnp.exp(f('s5_log_step').astype(F32))[:, None]
    mag = jnp.exp(lr * dts)
    ar = mag * jnp.cos(li * dts)
    ai = mag * jnp.sin(li * dts)
    den = lr * lr + li * li
    fr = ((ar - 1.0) * lr + ai * li) / den
    fi = (ai * lr - (ar - 1.0) * li) / den
    b_re, b_im = f('s5_b_re'), f('s5_b_im')
    bb_re = fr[..., None] * b_re - fi[..., None] * b_im
    bb_im = fr[..., None] * b_im + fi[..., None] * b_re
    wb = jnp.concatenate([_block_diag_in(bb_re), _block_diag_in(bb_im)], axis=1).astype(BF16)
    wc = jnp.stack([_block_diag_out(f('s5_c_re')), _block_diag_out(f('s5_c_im'))]).astype(BF16)
    s5 = (wb, wc, row(ar), row(ai), row(f('s5_d')), f('s5_w_glu').astype(BF16), row(f('s5_b_glu')))

    g3 = lambda name: f(name).reshape(1, 1, D_MODEL).astype(F32)
    return dict(
        in_w=in_w, ssd=ssd, rwkv=rwkv, gla=gla, s5=s5,
        g_mix_pre=g3('g_mix_pre'), g_mix_post=g3('g_mix_post'),
        g_ffn_pre=g3('g_ffn_pre'), g_ffn_post=g3('g_ffn_post'),
        w_out=f('w_out').reshape(4, GROUP_W, D_MODEL).astype(BF16),
        w1=f('mlp_w1').astype(BF16), w2=f('mlp_w2').astype(BF16))


def _run_group(x, mod, st, layers, cfg):
    bsz = x.shape[0]
    out = {n: [] for n in st}
    for l in range(DEPTH):
        P = layers[l]
        p_ssd, p_rwkv, p_gla, p_s5 = _in_proj(x, mod[l], P['g_mix_pre'], P['in_w'], *cfg['tok'])

        hist = jnp.pad(st['ssd_conv'][l], ((0, 0), (HIST - (SSD_CONV - 1), 0), (0, 0)))
        y_ssd, hist_new, ssd_new = _ssd(p_ssd, hist, st['ssd'][l], P['ssd'], *cfg['ssd'])
        y_rwkv, shift_new, rwkv_new = _rwkv(p_rwkv, st['rwkv_shift'][l][:, None, :], st['rwkv'][l],
                                            P['rwkv'], *cfg['rwkv'])
        y_gla, gla_new = _gla(p_gla, jnp.swapaxes(st['gla'][l], -1, -2), P['gla'], *cfg['gla'])
        y_s5, s5r_new, s5i_new = _s5(p_s5, st['s5_re'][l].reshape(bsz, 1, S5_CH),
                                     st['s5_im'][l].reshape(bsz, 1, S5_CH), P['s5'], *cfg['s5'])

        x = _out_mlp(x, (y_ssd, y_rwkv, y_gla, y_s5), mod[l], P['g_mix_post'], P['g_ffn_pre'],
                     P['g_ffn_post'], P['w_out'], P['w1'], P['w2'], *cfg['tok'])

        out['ssd'].append(ssd_new)
        out['ssd_conv'].append(hist_new[:, HIST - (SSD_CONV - 1):, :])
        out['rwkv'].append(rwkv_new)
        out['rwkv_shift'].append(shift_new[:, 0, :])
        out['gla'].append(jnp.swapaxes(gla_new, -1, -2))
        out['s5_re'].append(s5r_new.reshape(bsz, S5_GROUPS, S5_STATE))
        out['s5_im'].append(s5i_new.reshape(bsz, S5_GROUPS, S5_STATE))
    return x, {n: jnp.stack(v) for n, v in out.items()}


_PROMPT_CFG = dict(tok=(1, 512), ssd=(1, CHUNK), rwkv=(1, 256), gla=(1, 256, GLA_CS), s5=(1, 128))
_SAMPLE_CFG = dict(tok=(64, 8), ssd=(8, 8), rwkv=(8, 8), gla=(4, 8, 8), s5=(64, 8))


def kernel(x_prompt, x_sample, c_prompt, c_sample, state_ssd, state_ssd_conv, state_rwkv, state_rwkv_shift, state_gla, state_s5_re, state_s5_im, w_ada, b_ada, g_mix_pre, g_mix_post, g_ffn_pre, g_ffn_post, w_in, w_out, ssd_conv_w, ssd_conv_b, ssd_a_log, ssd_dt_bias, ssd_d, ssd_norm_g, rwkv_mu, rwkv_w0, rwkv_w2, rwkv_a0, rwkv_a2, rwkv_g2, rwkv_k_k, rwkv_k_a, rwkv_r_k, rwkv_lnx_w, rwkv_lnx_b, gla_wg2, gla_bg, gla_norm_g, s5_lam_re, s5_lam_im, s5_log_step, s5_b_re, s5_b_im, s5_c_re, s5_c_im, s5_d, s5_w_glu, s5_b_glu, mlp_w1, mlp_w2):
    W = dict(w_in=w_in, w_out=w_out, g_mix_pre=g_mix_pre, g_mix_post=g_mix_post,
             g_ffn_pre=g_ffn_pre, g_ffn_post=g_ffn_post,
             ssd_conv_w=ssd_conv_w, ssd_conv_b=ssd_conv_b, ssd_a_log=ssd_a_log,
             ssd_dt_bias=ssd_dt_bias, ssd_d=ssd_d, ssd_norm_g=ssd_norm_g,
             rwkv_mu=rwkv_mu, rwkv_w0=rwkv_w0, rwkv_w2=rwkv_w2, rwkv_a0=rwkv_a0, rwkv_a2=rwkv_a2,
             rwkv_g2=rwkv_g2, rwkv_k_k=rwkv_k_k, rwkv_k_a=rwkv_k_a, rwkv_r_k=rwkv_r_k,
             rwkv_lnx_w=rwkv_lnx_w, rwkv_lnx_b=rwkv_lnx_b,
             gla_wg2=gla_wg2, gla_bg=gla_bg, gla_norm_g=gla_norm_g,
             s5_lam_re=s5_lam_re, s5_lam_im=s5_lam_im, s5_log_step=s5_log_step,
             s5_b_re=s5_b_re, s5_b_im=s5_b_im, s5_c_re=s5_c_re, s5_c_im=s5_c_im,
             s5_d=s5_d, s5_w_glu=s5_w_glu, s5_b_glu=s5_b_glu, mlp_w1=mlp_w1, mlp_w2=mlp_w2)
    layers = [_layer_params(l, W) for l in range(DEPTH)]

    nbp = x_prompt.shape[0]
    mod = _ada(jnp.concatenate([c_prompt, c_sample], axis=0), w_ada, b_ada)
    mod = mod.reshape(DEPTH, -1, N_MOD, D_MODEL)

    st_sample = dict(ssd=state_ssd, ssd_conv=state_ssd_conv, rwkv=state_rwkv,
                     rwkv_shift=state_rwkv_shift, gla=state_gla, s5_re=state_s5_re, s5_im=state_s5_im)
    st_prompt = {n: jnp.zeros((DEPTH, nbp) + v.shape[2:], v.dtype) for n, v in st_sample.items()}

    y_prompt, sp = _run_group(x_prompt, mod[:, :nbp], st_prompt, layers, _PROMPT_CFG)
    y_sample, ss = _run_group(x_sample, mod[:, nbp:], st_sample, layers, _SAMPLE_CFG)
    return (y_prompt, y_sample,
            sp['ssd'], ss['ssd'], sp['ssd_conv'], ss['ssd_conv'],
            sp['rwkv'], ss['rwkv'], sp['rwkv_shift'], ss['rwkv_shift'],
            sp['gla'], ss['gla'], sp['s5_re'], ss['s5_re'], sp['s5_im'], ss['s5_im'])
```

```python
import functools
import math

import jax
import jax.numpy as jnp
from jax import lax
from jax.experimental import pallas as pl
from jax.experimental.pallas import tpu as pltpu

F32 = jnp.float32
BF16 = jnp.bfloat16

D_MODEL = 1024
DEPTH = 2
GROUP_W = 256
D_FF = 4096
N_MOD = 6
EPS = 1e-6

SSD_HEADS = 4
SSD_HEADDIM = 64
SSD_STATE = 64
SSD_NGROUPS = 2
SSD_BC = SSD_NGROUPS * SSD_STATE
SSD_CONV = 4
SSD_CONV_CH = GROUP_W + 2 * SSD_BC
SSD_PROJ = GROUP_W + SSD_CONV_CH + SSD_HEADS
SSD_PW = GROUP_W + SSD_CONV_CH + GROUP_W

RWKV_HEADS = 4
RWKV_HEADDIM = 64
RWKV_LORA = 128
RWKV_PROJ = 3 * GROUP_W + RWKV_LORA
RWKV_LN_EPS = 64e-5

GLA_HEADS = 4
GLA_DK = 32
GLA_DV = 64
GLA_GATE_LORA = 16
GLA_TAU = 16.0
GLA_QK = GLA_HEADS * GLA_DK
GLA_PROJ = 2 * GLA_QK + GROUP_W + GLA_GATE_LORA + GROUP_W
GLA_PW = 2 * GLA_QK + GROUP_W + 128 + GROUP_W

S5_GROUP = 16
S5_GROUPS = 16
S5_STATE = 64
S5_CH = S5_GROUPS * S5_STATE
S5_PROJ = GROUP_W

LANES = 128
HIST = 8
CHUNK = 64
GLA_CS = 32
VMEM_LIMIT = 56 * 1024 * 1024


def _dot(a, b, dims=(((1,), (0,)), ((), ()))):
    return lax.dot_general(a.astype(BF16), b.astype(BF16), dims, preferred_element_type=F32)


_NT = (((1,), (1,)), ((), ()))
_TN = (((0,), (0,)), ((), ()))
_NN = (((1,), (0,)), ((), ()))


def _split2(a):
    hi = a.astype(BF16)
    lo = (a - hi.astype(F32)).astype(BF16)
    return hi, lo


def _dot3(a, b, dims=_NN):
    ah, al = _split2(a)
    bh, bl = _split2(b)
    d = lambda x, y: lax.dot_general(x, y, dims, preferred_element_type=F32)
    return d(ah, bh) + (d(ah, bl) + d(al, bh))


def _dot_exact_lhs(m, x, dims=_NN):
    mb = m.astype(BF16)
    x1 = x.astype(BF16)
    r1 = x - x1.astype(F32)
    x2 = r1.astype(BF16)
    x3 = (r1 - x2.astype(F32)).astype(BF16)
    d = lambda y: lax.dot_general(mb, y, dims, preferred_element_type=F32)
    return d(x1) + (d(x2) + d(x3))


def _segsum(x, width):
    n = x.shape[-1]
    r = lax.broadcasted_iota(jnp.int32, (n, n), 0) // width
    c = lax.broadcasted_iota(jnp.int32, (n, n), 1) // width
    j = jnp.where(r == c, 1.0, 0.0).astype(BF16)
    x1 = x.astype(BF16)
    r1 = x - x1.astype(F32)
    x2 = r1.astype(BF16)
    x3 = (r1 - x2.astype(F32)).astype(BF16)
    d = lambda y: lax.dot_general(y, j, _NN, preferred_element_type=F32)
    return d(x1) + (d(x2) + d(x3))


def _sigmoid(x):
    return 1.0 / (1.0 + jnp.exp(-x))


def _silu(x):
    return x * _sigmoid(x)


def _softplus(x):
    return jnp.maximum(x, 0.0) + jnp.log1p(jnp.exp(-jnp.abs(x)))


def _rms(x, g):
    return x * lax.rsqrt(jnp.mean(x * x, axis=-1, keepdims=True) + EPS) * g


def _chunk_masks(rows, lt):
    r = lax.broadcasted_iota(jnp.int32, (rows, rows), 0)
    c = lax.broadcasted_iota(jnp.int32, (rows, rows), 1)
    same = (r // lt) == (c // lt)
    return same & (c <= r), same & (c < r)


def _bcast_last(x, nb, lt):
    c = x.shape[-1]
    last = x.reshape(nb, lt, c)[:, lt - 1:lt, :]
    return last, jnp.broadcast_to(last, (nb, lt, c)).reshape(nb * lt, c)


def _ada_kernel(c_ref, w_ref, b_ref, o_ref):
    c = c_ref[...]
    o_ref[0] = _dot(_silu(c), w_ref[0]) + b_ref[0]


def _ada(c_all, w_ada, b_ada):
    rows = c_all.shape[0]
    n = N_MOD * D_MODEL
    tn = 1536
    return pl.pallas_call(
        _ada_kernel,
        out_shape=jax.ShapeDtypeStruct((DEPTH, rows, n), F32),
        grid=(DEPTH, n // tn),
        in_specs=[
            pl.BlockSpec((rows, D_MODEL), lambda l, j: (0, 0)),
            pl.BlockSpec((1, D_MODEL, tn), lambda l, j: (l, 0, j)),
            pl.BlockSpec((1, 1, tn), lambda l, j: (l, 0, j)),
        ],
        out_specs=pl.BlockSpec((1, rows, tn), lambda l, j: (l, 0, j)),
        compiler_params=pltpu.CompilerParams(
            dimension_semantics=("parallel", "parallel"), vmem_limit_bytes=VMEM_LIMIT),
        name="ada",
    )(c_all, w_ada, b_ada.reshape(DEPTH, 1, n))


def _in_kernel(x_ref, mod_ref, g_ref, wssd_ref, wrwkv_ref, wgla_ref, ws5_ref,
               ossd_ref, orwkv_ref, ogla_ref, os5_ref):
    nb, lt, d = x_ref.shape
    m = mod_ref[...]
    h = _rms(x_ref[...], g_ref[...]) * (1.0 + m[:, 1:2, :]) + m[:, 0:1, :]
    hb = h.reshape(nb * lt, d).astype(BF16)
    for w_ref, o_ref in ((wssd_ref, ossd_ref), (wrwkv_ref, orwkv_ref),
                         (wgla_ref, ogla_ref), (ws5_ref, os5_ref)):
        o_ref[...] = jnp.dot(hb, w_ref[...], preferred_element_type=F32).reshape(o_ref.shape)


def _in_proj(x, mod, g, wts, nb, lt):
    bsz, seq, d = x.shape
    widths = (SSD_PW, RWKV_PROJ, GLA_PW, S5_PROJ)
    const2 = lambda i, j: (0, 0)
    return pl.pallas_call(
        _in_kernel,
        out_shape=tuple(jax.ShapeDtypeStruct((bsz, seq, w), F32) for w in widths),
        grid=(bsz // nb, seq // lt),
        in_specs=[
            pl.BlockSpec((nb, lt, d), lambda i, j: (i, j, 0)),
            pl.BlockSpec((nb, N_MOD, d), lambda i, j: (i, 0, 0)),
            pl.BlockSpec((1, 1, d), lambda i, j: (0, 0, 0)),
        ] + [pl.BlockSpec((d, w), const2) for w in widths],
        out_specs=tuple(pl.BlockSpec((nb, lt, w), lambda i, j: (i, j, 0)) for w in widths),
        compiler_params=pltpu.CompilerParams(
            dimension_semantics=("parallel", "parallel"), vmem_limit_bytes=VMEM_LIMIT),
        name="in_proj",
    )(x, mod, g, *wts)


def _out_mlp_kernel(x_ref, y0_ref, y1_ref, y2_ref, y3_ref, mod_ref, gpost_ref, gpre_ref, gfpost_ref,
                    wout_ref, w1_ref, w2_ref, o_ref):
    nb, lt, d = x_ref.shape
    rows = nb * lt
    m = mod_ref[...]
    mix = None
    for i, y_ref in enumerate((y0_ref, y1_ref, y2_ref, y3_ref)):
        t = jnp.dot(y_ref[...].reshape(rows, GROUP_W).astype(BF16), wout_ref[i],
                    preferred_element_type=F32)
        mix = t if mix is None else mix + t
    x1 = x_ref[...] + m[:, 2:3, :] * _rms(mix, gpost_ref[0]).reshape(nb, lt, d)
    h = _rms(x1, gpre_ref[...]) * (1.0 + m[:, 4:5, :]) + m[:, 3:4, :]
    hb = h.reshape(rows, d).astype(BF16)
    acc = None
    step = 1024
    for j in range(D_FF // step):
        a = jnp.dot(hb, w1_ref[:, j * step:(j + 1) * step], preferred_element_type=F32)
        a = jnp.square(jnp.maximum(a, 0.0)).astype(BF16)
        t = jnp.dot(a, w2_ref[j * step:(j + 1) * step, :], preferred_element_type=F32)
        acc = t if acc is None else acc + t
    o_ref[...] = x1 + m[:, 5:6, :] * _rms(acc, gfpost_ref[0]).reshape(nb, lt, d)


def _out_mlp(x, ys, mod, gpost, gpre, gfpost, wout, w1, w2, nb, lt):
    bsz, seq, d = x.shape
    tok = lambda i, j: (i, j, 0)
    c3 = lambda i, j: (0, 0, 0)
    c2 = lambda i, j: (0, 0)
    one = pl.Buffered(1)
    return pl.pallas_call(
        _out_mlp_kernel,
        out_shape=jax.ShapeDtypeStruct((bsz, seq, d), F32),
        grid=(bsz // nb, seq // lt),
        in_specs=[pl.BlockSpec((nb, lt, d), tok)]
        + [pl.BlockSpec((nb, lt, GROUP_W), tok)] * 4
        + [pl.BlockSpec((nb, N_MOD, d), lambda i, j: (i, 0, 0))]
        + [pl.BlockSpec((1, 1, d), c3)] * 3
        + [pl.BlockSpec((4, GROUP_W, d), c3, pipeline_mode=one),
           pl.BlockSpec((d, D_FF), c2, pipeline_mode=one),
           pl.BlockSpec((D_FF, d), c2, pipeline_mode=one)],
        out_specs=pl.BlockSpec((nb, lt, d), tok),
        compiler_params=pltpu.CompilerParams(
            dimension_semantics=("parallel", "parallel"), vmem_limit_bytes=VMEM_LIMIT),
        name="out_mlp",
    )(x, *ys, mod, gpost, gpre, gfpost, wout, w1, w2)


def _ssd_kernel(p_ref, hist_ref, s0_ref, cw_ref, cb_ref, alog_ref, dtb_ref, dsk_ref, ng_ref,
                y_ref, hout_ref, sout_ref, ext_ref, yscr_ref, *, nb, lt):
    rows = nb * lt
    hd = SSD_HEADDIM

    @pl.when(pl.program_id(1) == 0)
    def _():
        ext_ref[:, 0:HIST, :] = hist_ref[...]
        sout_ref[...] = s0_ref[...]

    p = p_ref[...]
    ext_ref[:, HIST:HIST + lt, :] = p[:, :, GROUP_W:GROUP_W + SSD_CONV_CH]
    conv = cb_ref[...]
    for j in range(SSD_CONV):
        o = HIST - (SSD_CONV - 1) + j
        conv = conv + ext_ref[:, o:o + lt, :] * cw_ref[j:j + 1, :]
    tail = ext_ref[:, lt:lt + HIST, :]
    ext_ref[:, 0:HIST, :] = tail
    hout_ref[...] = tail

    xbc = _silu(conv).reshape(rows, SSD_CONV_CH)
    xs = xbc[:, 0:GROUP_W]
    bm = xbc[:, GROUP_W:GROUP_W + SSD_BC]
    cm = xbc[:, GROUP_W + SSD_BC:]
    z = p[:, :, 0:GROUP_W].reshape(rows, GROUP_W)
    dt = _softplus(p[:, :, GROUP_W + SSD_CONV_CH:].reshape(rows, GROUP_W) + dtb_ref[...])
    a = dt * (-jnp.exp(alog_ref[...]))
    unit = lt if nb > 1 else CHUNK
    units = rows // unit
    per_chunk = CHUNK // unit
    incl, _ = _chunk_masks(rows, unit)
    cum = _dot_exact_lhs(jnp.where(incl, 1.0, 0.0), a)
    last, last_b = _bcast_last(cum, units, unit)
    xdt = xs * dt
    xw = xdt * jnp.exp(last_b - cum)
    ecum = jnp.exp(cum)
    elast = jnp.exp(last)
    ones_row = jnp.full((CHUNK, hd), 1.0 / hd, F32)
    cmask, _ = _chunk_masks(CHUNK, unit)
    heads = range(SSD_HEADS)
    hs = lambda h: slice(h * hd, (h + 1) * hd)
    gs = lambda h: slice((h // 2) * SSD_STATE, (h // 2 + 1) * SSD_STATE)
    cs = lambda j: slice(j * CHUNK, (j + 1) * CHUNK)
    us = lambda u: slice(u * unit, (u + 1) * unit)

    yh = {}
    for j in range(rows // CHUNK):
        cb = [_dot(cm[cs(j), gs(2 * g)], bm[cs(j), gs(2 * g)], _NT) for g in range(SSD_NGROUPS)]
        for h in heads:
            ccol = cum[cs(j), hs(h)]
            crow = _dot_exact_lhs(ones_row, ccol, _NT)
            decay = jnp.exp(jnp.where(cmask, ccol - crow, -jnp.inf))
            yh[j, h] = _dot(cb[h // 2] * decay, xdt[cs(j), hs(h)])
    ds = {(u, h): _dot(xw[us(u), hs(h)], bm[us(u), gs(h)], _TN) for u in range(units) for h in heads}

    before = {}
    for h in heads:
        if nb > 1:
            for u in range(units):
                before[u, h] = sout_ref[u, h]
                sout_ref[u, h] = before[u, h] * elast[u][:, hs(h)] + ds[u, h]
        else:
            s_cur = sout_ref[0, h]
            for u in range(units):
                before[u, h] = s_cur
                s_cur = s_cur * elast[u][:, hs(h)] + ds[u, h]
            sout_ref[0, h] = s_cur

    for j in range(rows // CHUNK):
        for h in heads:
            inter = [_dot(cm[us(u), gs(h)], before[u, h], _NT)
                     for u in range(j * per_chunk, (j + 1) * per_chunk)]
            inter = inter[0] if per_chunk == 1 else jnp.concatenate(inter, axis=0)
            yscr_ref[cs(j), hs(h)] = yh[j, h] + inter * ecum[cs(j), hs(h)]

    y = yscr_ref[...] + dsk_ref[...] * xs
    y = y * _silu(z)
    y = y * lax.rsqrt(_segsum(y * y, hd) * (1.0 / hd) + EPS) * ng_ref[...]
    y_ref[...] = y.reshape(nb, lt, GROUP_W)


def _ssd(p, hist, s0, prm, nb, lt):
    bsz, seq, _ = p.shape
    assert CHUNK == SSD_HEADDIM
    assert (nb > 1 and (nb * lt) % CHUNK == 0 and CHUNK % lt == 0) or (nb == 1 and lt % CHUNK == 0)
    row = lambda i, c: (0, 0)
    kern = functools.partial(_ssd_kernel, nb=nb, lt=lt)
    return pl.pallas_call(
        kern,
        out_shape=(jax.ShapeDtypeStruct((bsz, seq, GROUP_W), F32),
                   jax.ShapeDtypeStruct((bsz, HIST, SSD_CONV_CH), F32),
                   jax.ShapeDtypeStruct(s0.shape, F32)),
        grid=(bsz // nb, seq // lt),
        in_specs=[
            pl.BlockSpec((nb, lt, SSD_PW), lambda i, c: (i, c, 0)),
            pl.BlockSpec((nb, HIST, SSD_CONV_CH), lambda i, c: (i, 0, 0)),
            pl.BlockSpec((nb,) + s0.shape[1:], lambda i, c: (i, 0, 0, 0)),
            pl.BlockSpec((SSD_CONV, SSD_CONV_CH), row),
            pl.BlockSpec((1, SSD_CONV_CH), row),
        ] + [pl.BlockSpec((1, GROUP_W), row)] * 4,
        out_specs=(pl.BlockSpec((nb, lt, GROUP_W), lambda i, c: (i, c, 0)),
                   pl.BlockSpec((nb, HIST, SSD_CONV_CH), lambda i, c: (i, 0, 0)),
                   pl.BlockSpec((nb,) + s0.shape[1:], lambda i, c: (i, 0, 0, 0))),
        scratch_shapes=[pltpu.VMEM((nb, HIST + lt, SSD_CONV_CH), F32),
                        pltpu.VMEM((nb * lt, GROUP_W), F32)],
        compiler_params=pltpu.CompilerParams(
            dimension_semantics=("parallel", "arbitrary"), vmem_limit_bytes=VMEM_LIMIT),
        name="ssd",
    )(p, hist, s0, *prm)


def _unit_lower_inverse(a2, n, levels):
    ri = lax.broadcasted_iota(jnp.int32, (n, n), 0)
    ci = lax.broadcasted_iota(jnp.int32, (n, n), 1)
    x = [jnp.where(ri == ci, 1.0, 0.0) - jnp.where((ri % 2 == 1) & (ci == ri - 1), a, 0.0) for a in a2]
    s = 2
    while s < levels:
        off = ((ri // s) % 2 == 1) & ((ci // s) == (ri // s) - 1)
        t = [_dot3(xi, jnp.where(off, a, 0.0)) for xi, a in zip(x, a2)]
        x = [xi - _dot3(ti, xi) for xi, ti in zip(x, t)]
        s *= 2
    return x


def _rwkv_chunk_multi(ops, sout_ref, yscr_ref, nb, lt):
    hd = RWKV_HEADDIM
    rows = nb * lt
    heads = range(RWKV_HEADS)
    incl, strict = _chunk_masks(rows, lt)
    hs = lambda name: [ops[name][:, h * hd:(h + 1) * hd] for h in heads]
    rt, kt, k2t, bt, k2g, bg, v = (hs(n) for n in ('rt', 'kt', 'k2t', 'bt', 'k2g', 'bg', 'v'))
    a1 = [jnp.where(strict, _dot3(kt[h], k2t[h], _NT), 0.0) for h in heads]
    a2 = [jnp.where(strict, _dot3(kt[h], bt[h], _NT), 0.0) for h in heads]
    l1 = [jnp.where(incl, _dot3(rt[h], k2t[h], _NT), 0.0) for h in heads]
    l2 = [jnp.where(incl, _dot3(rt[h], bt[h], _NT), 0.0) for h in heads]
    x = _unit_lower_inverse(a2, rows, lt)
    a1v = [_dot3(a1[h], v[h]) for h in heads]
    l1v = [_dot3(l1[h], v[h]) for h in heads]
    cat = lambda parts: parts[0] if nb == 1 else jnp.concatenate(parts, axis=0)
    seq = lambda i: slice(i * lt, (i + 1) * lt)
    s_old = [[sout_ref[i, h] for i in range(nb)] for h in heads]
    p0 = [cat([_dot3(kt[h][seq(i)], s_old[h][i], _NT) for i in range(nb)]) for h in heads]
    q0 = [cat([_dot3(rt[h][seq(i)], s_old[h][i], _NT) for i in range(nb)]) for h in heads]
    u = [_dot3(x[h], p0[h] + a1v[h]) for h in heads]
    l2u = [_dot3(l2[h], u[h]) for h in heads]
    for h in heads:
        yscr_ref[:, h * hd:(h + 1) * hd] = q0[h] + l1v[h] - l2u[h]
        for i in range(nb):
            sout_ref[i, h] = (s_old[h][i] * ops['elast'][i][:, h * hd:(h + 1) * hd]
                              + _dot3(v[h][seq(i)], k2g[h][seq(i)], _TN)
                              - _dot3(u[h][seq(i)], bg[h][seq(i)], _TN))


def _rwkv_chunks_seq(ops, sout_ref, yscr_ref, nc):
    hd = RWKV_HEADDIM
    idx = [(c, h) for c in range(nc) for h in range(RWKV_HEADS)]
    incl, strict = _chunk_masks(CHUNK, CHUNK)
    ri = lax.broadcasted_iota(jnp.int32, (hd, hd), 0)
    ci = lax.broadcasted_iota(jnp.int32, (hd, hd), 1)
    eye = jnp.where(ri == ci, 1.0, 0.0)
    bl = lambda name: [ops[name][c * CHUNK:(c + 1) * CHUNK, h * hd:(h + 1) * hd] for c, h in idx]
    rt, kt, k2t, bt, k2g, bg, v = (bl(n) for n in ('rt', 'kt', 'k2t', 'bt', 'k2g', 'bg', 'v'))
    n = range(len(idx))
    a1 = [jnp.where(strict, _dot3(kt[j], k2t[j], _NT), 0.0) for j in n]
    a2 = [jnp.where(strict, _dot3(kt[j], bt[j], _NT), 0.0) for j in n]
    l1 = [jnp.where(incl, _dot3(rt[j], k2t[j], _NT), 0.0) for j in n]
    l2 = [jnp.where(incl, _dot3(rt[j], bt[j], _NT), 0.0) for j in n]
    x = _unit_lower_inverse(a2, CHUNK, CHUNK)
    xa1 = [_dot3(x[j], a1[j]) for j in n]
    l2x = [_dot3(l2[j], x[j]) for j in n]
    kp = [_dot3(x[j], kt[j]) for j in n]
    rp = [rt[j] - _dot3(l2x[j], kt[j]) for j in n]
    yl = [_dot3(l1[j] - _dot3(l2[j], xa1[j]), v[j]) for j in n]
    ul = [_dot3(xa1[j], v[j]) for j in n]
    mc = [eye * ops['elast'][c][:, h * hd:(h + 1) * hd] - _dot3(kp[j], bg[j], _TN)
          for j, (c, h) in enumerate(idx)]
    dc = [_dot3(v[j], k2g[j], _TN) - _dot3(ul[j], bg[j], _TN) for j in n]
    state = [sout_ref[0, h] for h in range(RWKV_HEADS)]
    for c in range(nc):
        js = [c * RWKV_HEADS + h for h in range(RWKV_HEADS)]
        ys = [_dot3(rp[j], state[h], _NT) + yl[j] for h, j in enumerate(js)]
        state = [_dot3(state[h], mc[j]) + dc[j] for h, j in enumerate(js)]
        for h in range(RWKV_HEADS):
            yscr_ref[c * CHUNK:(c + 1) * CHUNK, h * hd:(h + 1) * hd] = ys[h]
    for h in range(RWKV_HEADS):
        sout_ref[0, h] = state[h]


def _rwkv_kernel(p_ref, sh0_ref, s0_ref, mu_ref, wl_ref, w0_ref, a0_ref, kk_ref, ka_ref, rk_ref,
                 lnw_ref, lnb_ref, y_ref, shout_ref, sout_ref, ext_ref, yscr_ref, *, nb, lt):
    rows = nb * lt
    hd = RWKV_HEADDIM
    gw = GROUP_W

    @pl.when(pl.program_id(1) == 0)
    def _():
        ext_ref[:, HIST - 1:HIST, :] = sh0_ref[...]
        sout_ref[...] = s0_ref[...]

    p = p_ref[...]
    ext_ref[:, HIST:HIST + lt, :] = p
    prev = ext_ref[:, HIST - 1:HIST - 1 + lt, :]
    pm = (p + (prev - p) * mu_ref[...]).reshape(rows, RWKV_PROJ)
    last_p = p[:, lt - 1:lt, :]
    ext_ref[:, HIST - 1:HIST, :] = last_p
    shout_ref[...] = last_p

    r = pm[:, 0:gw]
    k = pm[:, gw:2 * gw]
    v = pm[:, 2 * gw:3 * gw]
    lo = pm[:, 3 * gw:]
    lane = lax.broadcasted_iota(jnp.int32, (rows, RWKV_LORA), 1)
    t = jnp.where(lane < 32, jnp.tanh(lo), jnp.where(lane < 64, lo, _sigmoid(lo)))
    lora = _dot(t, wl_ref[...])
    wlog = -_softplus(-(w0_ref[...] + lora[:, 0:gw])) - 0.5
    lw = -jnp.exp(wlog)
    a = _sigmoid(a0_ref[...] + lora[:, gw:2 * gw])
    gate = lora[:, 2 * gw:]
    kk = k * kk_ref[...]
    kk = kk / jnp.maximum(jnp.sqrt(_segsum(kk * kk, hd)), 1e-12)
    k2 = k * (1.0 + (a - 1.0) * ka_ref[...])
    beta = kk * a

    unit = lt if nb > 1 else CHUNK
    incl, _ = _chunk_masks(rows, unit)
    gcum = _dot_exact_lhs(jnp.where(incl, 1.0, 0.0), lw)
    glast, glast_b = _bcast_last(gcum, rows // unit, unit)
    eg = jnp.exp(gcum)
    einv = jnp.exp(-gcum)
    etail = jnp.exp(glast_b - gcum)
    ops = dict(rt=r * eg, kt=kk * jnp.exp(gcum - lw), k2t=k2 * einv, bt=beta * einv,
               k2g=k2 * etail, bg=beta * etail, v=v, elast=jnp.exp(glast))
    if nb > 1:
        _rwkv_chunk_multi(ops, sout_ref, yscr_ref, nb, lt)
    else:
        _rwkv_chunks_seq(ops, sout_ref, yscr_ref, lt // CHUNK)

    y = yscr_ref[...]
    mean = _segsum(y, hd) * (1.0 / hd)
    dlt = y - mean
    var = _segsum(dlt * dlt, hd) * (1.0 / hd)
    yn = dlt * lax.rsqrt(var + RWKV_LN_EPS) * lnw_ref[...] + lnb_ref[...]
    bonus = _segsum(r * k2 * rk_ref[...], hd) * v
    y_ref[...] = ((yn + bonus) * gate).reshape(nb, lt, gw)


def _rwkv(p, sh0, s0, prm, nb, lt):
    bsz, seq, _ = p.shape
    assert (nb > 1 and nb * lt == CHUNK) or (nb == 1 and lt % CHUNK == 0)
    row = lambda i, c: (0, 0)
    kern = functools.partial(_rwkv_kernel, nb=nb, lt=lt)
    return pl.pallas_call(
        kern,
        out_shape=(jax.ShapeDtypeStruct((bsz, seq, GROUP_W), F32),
                   jax.ShapeDtypeStruct((bsz, 1, RWKV_PROJ), F32),
                   jax.ShapeDtypeStruct(s0.shape, F32)),
        grid=(bsz // nb, seq // lt),
        in_specs=[
            pl.BlockSpec((nb, lt, RWKV_PROJ), lambda i, c: (i, c, 0)),
            pl.BlockSpec((nb, 1, RWKV_PROJ), lambda i, c: (i, 0, 0)),
            pl.BlockSpec((nb,) + s0.shape[1:], lambda i, c: (i, 0, 0, 0)),
            pl.BlockSpec((1, RWKV_PROJ), row),
            pl.BlockSpec((RWKV_LORA, 3 * GROUP_W), row),
        ] + [pl.BlockSpec((1, GROUP_W), row)] * 7,
        out_specs=(pl.BlockSpec((nb, lt, GROUP_W), lambda i, c: (i, c, 0)),
                   pl.BlockSpec((nb, 1, RWKV_PROJ), lambda i, c: (i, 0, 0)),
                   pl.BlockSpec((nb,) + s0.shape[1:], lambda i, c: (i, 0, 0, 0))),
        scratch_shapes=[pltpu.VMEM((nb, HIST + lt, RWKV_PROJ), F32),
                        pltpu.VMEM((nb * lt, GROUP_W), F32)],
        compiler_params=pltpu.CompilerParams(
            dimension_semantics=("parallel", "arbitrary"), vmem_limit_bytes=VMEM_LIMIT),
        name="rwkv",
    )(p, sh0, s0, *prm)


def _gla_kernel(p_ref, s0_ref, wg2_ref, bg_ref, ng_ref, y_ref, sout_ref, yscr_ref, *, nb, lt, cs):
    rows = nb * lt
    dk, dv = GLA_DK, GLA_DV
    qk = GLA_QK
    units = rows // cs
    per_group = GLA_CS // cs

    @pl.when(pl.program_id(1) == 0)
    def _():
        sout_ref[...] = s0_ref[...]

    p = p_ref[...].reshape(rows, GLA_PW)
    q = p[:, 0:qk] * (dk ** -0.5)
    k = p[:, qk:2 * qk]
    v = p[:, 2 * qk:2 * qk + GROUP_W]
    dg = p[:, 2 * qk + GROUP_W:2 * qk + GROUP_W + 128]
    og = p[:, 2 * qk + GROUP_W + 128:]
    lg = -_softplus(-(_dot(dg, wg2_ref[...]) + bg_ref[...])) * (1.0 / GLA_TAU)
    incl, _ = _chunk_masks(rows, cs)
    b = _dot_exact_lhs(jnp.where(incl, 1.0, 0.0), lg)
    bl, bl_b = _bcast_last(b, units, cs)
    qd = q * jnp.exp(b)
    kd = k * jnp.exp(-b)
    kw = k * jnp.exp(bl_b - b)
    ebl = jnp.exp(bl)
    gmask, _ = _chunk_masks(GLA_CS, cs)

    heads = range(GLA_HEADS)
    ks = lambda h: slice(h * dk, (h + 1) * dk)
    vs = lambda h: slice(h * dv, (h + 1) * dv)
    gs = lambda g: slice(g * GLA_CS, (g + 1) * GLA_CS)
    us = lambda u: slice(u * cs, (u + 1) * cs)

    yh = {}
    for g in range(rows // GLA_CS):
        for h in heads:
            att = jnp.where(gmask, _dot(qd[gs(g), ks(h)], kd[gs(g), ks(h)], _NT), 0.0)
            yh[g, h] = _dot(att, v[gs(g), vs(h)])
    ds = {(u, h): _dot(v[us(u), vs(h)], kw[us(u), ks(h)], _TN) for u in range(units) for h in heads}

    before = {}
    for h in heads:
        if nb > 1:
            for u in range(units):
                before[u, h] = sout_ref[u, h]
                sout_ref[u, h] = before[u, h] * ebl[u][:, ks(h)] + ds[u, h]
        else:
            s_cur = sout_ref[0, h]
            for u in range(units):
                before[u, h] = s_cur
                s_cur = s_cur * ebl[u][:, ks(h)] + ds[u, h]
            sout_ref[0, h] = s_cur

    for g in range(rows // GLA_CS):
        for h in heads:
            inter = [_dot(qd[us(u), ks(h)], before[u, h], _NT)
                     for u in range(g * per_group, (g + 1) * per_group)]
            inter = inter[0] if per_group == 1 else jnp.concatenate(inter, axis=0)
            yscr_ref[gs(g), vs(h)] = yh[g, h] + inter

    y = yscr_ref[...]
    y = y * lax.rsqrt(_segsum(y * y, dv) * (1.0 / dv) + EPS) * ng_ref[...]
    y_ref[...] = (y * _silu(og)).reshape(nb, lt, GROUP_W)


def _gla(p, s0, prm, nb, lt, cs):
    bsz, seq, _ = p.shape
    assert (nb == 1 and cs == GLA_CS and lt % cs == 0) or (
        lt == cs and (nb * cs) % GLA_CS == 0 and GLA_CS % cs == 0)
    row = lambda i, c: (0, 0)
    kern = functools.partial(_gla_kernel, nb=nb, lt=lt, cs=cs)
    return pl.pallas_call(
        kern,
        out_shape=(jax.ShapeDtypeStruct((bsz, seq, GROUP_W), F32),
                   jax.ShapeDtypeStruct(s0.shape, F32)),
        grid=(bsz // nb, seq // lt),
        in_specs=[
            pl.BlockSpec((nb, lt, GLA_PW), lambda i, c: (i, c, 0)),
            pl.BlockSpec((nb,) + s0.shape[1:], lambda i, c: (i, 0, 0, 0)),
            pl.BlockSpec((128, GLA_QK), row),
            pl.BlockSpec((1, GLA_QK), row),
            pl.BlockSpec((1, GROUP_W), row),
        ],
        out_specs=(pl.BlockSpec((nb, lt, GROUP_W), lambda i, c: (i, c, 0)),
                   pl.BlockSpec((nb,) + s0.shape[1:], lambda i, c: (i, 0, 0, 0))),
        scratch_shapes=[pltpu.VMEM((nb * lt, GROUP_W), F32)],
        compiler_params=pltpu.CompilerParams(
            dimension_semantics=("parallel", "arbitrary"), vmem_limit_bytes=VMEM_LIMIT),
        name="gla",
    )(p, s0, *prm)


def _s5_pitch(lt):
    return lt if (lt // 8) % 2 == 1 else lt + 8


def _s5_kernel(u_ref, hr0_ref, hi0_ref, wb_ref, wc_ref, ar_ref, ai_ref, d_ref, wglu_ref, bglu_ref,
               y_ref, hr_ref, hi_ref, h_ref, *, nb, lt):
    rows = nb * lt
    n = S5_CH

    @pl.when(pl.program_id(1) == 0)
    def _():
        hr_ref[...] = hr0_ref[...]
        hi_ref[...] = hi0_ref[...]

    u = u_ref[...].reshape(rows, GROUP_W)
    bu = _dot(u, wb_ref[...])
    nt = n // LANES
    lanes = lambda x, j: x[:, j * LANES:(j + 1) * LANES]
    pitch = _s5_pitch(lt)
    seq_rows = lambda i: slice(i * pitch, i * pitch + lt)
    for j in range(2 * nt):
        for i in range(nb):
            h_ref[j, seq_rows(i), :] = lanes(bu, j)[i * lt:(i + 1) * lt]
    ar = [jnp.broadcast_to(lanes(ar_ref[...], j), (nb, LANES)) for j in range(nt)]
    ai = [jnp.broadcast_to(lanes(ai_ref[...], j), (nb, LANES)) for j in range(nt)]
    hr = [lanes(hr_ref[...], j) for j in range(nt)]
    hi = [lanes(hi_ref[...], j) for j in range(nt)]
    for t in range(lt):
        at_t = pl.ds(t, nb, stride=pitch)
        for j in range(nt):
            hr[j], hi[j] = (ar[j] * hr[j] - ai[j] * hi[j] + h_ref[j, at_t, :],
                            ar[j] * hi[j] + ai[j] * hr[j] + h_ref[nt + j, at_t, :])
            h_ref[j, at_t, :] = hr[j]
            h_ref[nt + j, at_t, :] = hi[j]
    hr_ref[...] = jnp.concatenate(hr, axis=1)
    hi_ref[...] = jnp.concatenate(hi, axis=1)
    tile = lambda j: jnp.concatenate([h_ref[j, seq_rows(i), :] for i in range(nb)], axis=0)
    h_re = jnp.concatenate([tile(j) for j in range(nt)], axis=1)
    h_im = jnp.concatenate([tile(nt + j) for j in range(nt)], axis=1)
    y = _dot(h_re, wc_ref[0]) - _dot(h_im, wc_ref[1])
    y = y + d_ref[...] * u
    y = 0.5 * y * (1.0 + jnp.tanh(math.sqrt(2.0 / math.pi) * (y + 0.044715 * (y * y * y))))
    y = y * _sigmoid(_dot(y, wglu_ref[...]) + bglu_ref[...])
    y_ref[...] = y.reshape(nb, lt, GROUP_W)


def _s5(u, hr0, hi0, prm, nb, lt):
    bsz, seq, _ = u.shape
    row = lambda i, c: (0, 0)
    st = pl.BlockSpec((nb, S5_CH), lambda i, c: (i, 0))
    kern = functools.partial(_s5_kernel, nb=nb, lt=lt)
    return pl.pallas_call(
        kern,
        out_shape=(jax.ShapeDtypeStruct((bsz, seq, GROUP_W), F32),
                   jax.ShapeDtypeStruct((bsz, S5_CH), F32),
                   jax.ShapeDtypeStruct((bsz, S5_CH), F32)),
        grid=(bsz // nb, seq // lt),
        in_specs=[
            pl.BlockSpec((nb, lt, GROUP_W), lambda i, c: (i, c, 0)), st, st,
            pl.BlockSpec((GROUP_W, 2 * S5_CH), row),
            pl.BlockSpec((2, S5_CH, GROUP_W), lambda i, c: (0, 0, 0)),
            pl.BlockSpec((1, S5_CH), row),
            pl.BlockSpec((1, S5_CH), row),
            pl.BlockSpec((1, GROUP_W), row),
            pl.BlockSpec((GROUP_W, GROUP_W), row),
            pl.BlockSpec((1, GROUP_W), row),
        ],
        out_specs=(pl.BlockSpec((nb, lt, GROUP_W), lambda i, c: (i, c, 0)), st, st),
        scratch_shapes=[pltpu.VMEM((2 * S5_CH // LANES, nb * _s5_pitch(lt), LANES), F32)],
        compiler_params=pltpu.CompilerParams(
            dimension_semantics=("parallel", "arbitrary"), vmem_limit_bytes=VMEM_LIMIT),
        name="s5",
    )(u, hr0, hi0, *prm)


def _block_diag_in(b):
    g, n, i = b.shape
    eye = jnp.eye(g, dtype=b.dtype)
    return jnp.einsum('gni,gh->gihn', b, eye).reshape(g * i, g * n)


def _block_diag_out(c):
    g, o, n = c.shape
    eye = jnp.eye(g, dtype=c.dtype)
    return jnp.einsum('gon,gh->gnho', c, eye).reshape(g * n, g * o)


def _layer_params(l, W):
    f = lambda name: W[name][l]
    row = lambda a: a.reshape(1, -1).astype(F32)
    rep = lambda a, n: jnp.repeat(a.reshape(-1), n).reshape(1, -1).astype(F32)
    w_in = f('w_in')
    o_r = SSD_PROJ
    o_g = o_r + RWKV_PROJ
    o_s = o_g + GLA_PROJ
    w_ssd = jnp.concatenate([w_in[:, :GROUP_W + SSD_CONV_CH],
                             jnp.repeat(w_in[:, GROUP_W + SSD_CONV_CH:o_r], SSD_HEADDIM, axis=1)], axis=1)
    w_rwkv = w_in[:, o_r:o_g]
    gq = o_g + 2 * GLA_QK + GROUP_W
    w_gla = jnp.concatenate([w_in[:, o_g:gq], w_in[:, gq:gq + GLA_GATE_LORA],
                             jnp.zeros((D_MODEL, 128 - GLA_GATE_LORA), F32),
                             w_in[:, gq + GLA_GATE_LORA:o_s]], axis=1)
    w_s5 = w_in[:, o_s:]
    in_w = tuple(w.astype(BF16) for w in (w_ssd, w_rwkv, w_gla, w_s5))

    ssd = (f('ssd_conv_w'), row(f('ssd_conv_b')), rep(f('ssd_a_log'), SSD_HEADDIM),
           rep(f('ssd_dt_bias'), SSD_HEADDIM), rep(f('ssd_d'), SSD_HEADDIM), row(f('ssd_norm_g')))

    z = lambda r, c: jnp.zeros((r, c), F32)
    w_lora = jnp.concatenate([
        jnp.concatenate([f('rwkv_w2'), z(32, 2 * GROUP_W)], axis=1),
        jnp.concatenate([z(32, GROUP_W), f('rwkv_a2'), z(32, GROUP_W)], axis=1),
        jnp.concatenate([z(64, 2 * GROUP_W), f('rwkv_g2')], axis=1)], axis=0).astype(BF16)
    rwkv = (row(f('rwkv_mu')), w_lora, row(f('rwkv_w0')), row(f('rwkv_a0')), row(f('rwkv_k_k')),
            row(f('rwkv_k_a')), row(f('rwkv_r_k')), row(f('rwkv_lnx_w')), row(f('rwkv_lnx_b')))

    wg2 = jnp.concatenate([f('gla_wg2'), z(128 - GLA_GATE_LORA, GLA_QK)], axis=0).astype(BF16)
    gla = (wg2, row(f('gla_bg')), row(f('gla_norm_g')))

    lr = f('s5_lam_re').astype(F32)
    li = f('s5_lam_im').astype(F32)
    dts = jnp.exp(f('s5_log_step').astype(F32))[:, None]
    mag = jnp.exp(lr * dts)
    ar = mag * jnp.cos(li * dts)
    ai = mag * jnp.sin(li * dts)
    den = lr * lr + li * li
    fr = ((ar - 1.0) * lr + ai * li) / den
    fi = (ai * lr - (ar - 1.0) * li) / den
    b_re, b_im = f('s5_b_re'), f('s5_b_im')
    bb_re = fr[..., None] * b_re - fi[..., None] * b_im
    bb_im = fr[..., None] * b_im + fi[..., None] * b_re
    wb = jnp.concatenate([_block_diag_in(bb_re), _block_diag_in(bb_im)], axis=1).astype(BF16)
    wc = jnp.stack([_block_diag_out(f('s5_c_re')), _block_diag_out(f('s5_c_im'))]).astype(BF16)
    s5 = (wb, wc, row(ar), row(ai), row(f('s5_d')), f('s5_w_glu').astype(BF16), row(f('s5_b_glu')))

    g3 = lambda name: f(name).reshape(1, 1, D_MODEL).astype(F32)
    return dict(
        in_w=in_w, ssd=ssd, rwkv=rwkv, gla=gla, s5=s5,
        g_mix_pre=g3('g_mix_pre'), g_mix_post=g3('g_mix_post'),
        g_ffn_pre=g3('g_ffn_pre'), g_ffn_post=g3('g_ffn_post'),
        w_out=f('w_out').reshape(4, GROUP_W, D_MODEL).astype(BF16),
        w1=f('mlp_w1').astype(BF16), w2=f('mlp_w2').astype(BF16))


def _run_group(x, mod, st, layers, cfg):
    bsz = x.shape[0]
    out = {n: [] for n in st}
    for l in range(DEPTH):
        P = layers[l]
        p_ssd, p_rwkv, p_gla, p_s5 = _in_proj(x, mod[l], P['g_mix_pre'], P['in_w'], *cfg['tok'])

        hist = jnp.pad(st['ssd_conv'][l], ((0, 0), (HIST - (SSD_CONV - 1), 0), (0, 0)))
        y_ssd, hist_new, ssd_new = _ssd(p_ssd, hist, st['ssd'][l], P['ssd'], *cfg['ssd'])
        y_rwkv, shift_new, rwkv_new = _rwkv(p_rwkv, st['rwkv_shift'][l][:, None, :], st['rwkv'][l],
                                            P['rwkv'], *cfg['rwkv'])
        y_gla, gla_new = _gla(p_gla, jnp.swapaxes(st['gla'][l], -1, -2), P['gla'], *cfg['gla'])
        y_s5, s5r_new, s5i_new = _s5(p_s5, st['s5_re'][l].reshape(bsz, S5_CH),
                                     st['s5_im'][l].reshape(bsz, S5_CH), P['s5'], *cfg['s5'])

        x = _out_mlp(x, (y_ssd, y_rwkv, y_gla, y_s5), mod[l], P['g_mix_post'], P['g_ffn_pre'],
                     P['g_ffn_post'], P['w_out'], P['w1'], P['w2'], *cfg['tok'])

        out['ssd'].append(ssd_new)
        out['ssd_conv'].append(hist_new[:, HIST - (SSD_CONV - 1):, :])
        out['rwkv'].append(rwkv_new)
        out['rwkv_shift'].append(shift_new[:, 0, :])
        out['gla'].append(jnp.swapaxes(gla_new, -1, -2))
        out['s5_re'].append(s5r_new.reshape(bsz, S5_GROUPS, S5_STATE))
        out['s5_im'].append(s5i_new.reshape(bsz, S5_GROUPS, S5_STATE))
    return x, {n: jnp.stack(v) for n, v in out.items()}


_PROMPT_CFG = dict(tok=(1, 512), ssd=(1, 256), rwkv=(1, 256), gla=(1, 256, GLA_CS), s5=(8, 128))
_SAMPLE_CFG = dict(tok=(64, 8), ssd=(32, 8), rwkv=(8, 8), gla=(16, 8, 8), s5=(64, 8))


def kernel(x_prompt, x_sample, c_prompt, c_sample, state_ssd, state_ssd_conv, state_rwkv, state_rwkv_shift, state_gla, state_s5_re, state_s5_im, w_ada, b_ada, g_mix_pre, g_mix_post, g_ffn_pre, g_ffn_post, w_in, w_out, ssd_conv_w, ssd_conv_b, ssd_a_log, ssd_dt_bias, ssd_d, ssd_norm_g, rwkv_mu, rwkv_w0, rwkv_w2, rwkv_a0, rwkv_a2, rwkv_g2, rwkv_k_k, rwkv_k_a, rwkv_r_k, rwkv_lnx_w, rwkv_lnx_b, gla_wg2, gla_bg, gla_norm_g, s5_lam_re, s5_lam_im, s5_log_step, s5_b_re, s5_b_im, s5_c_re, s5_c_im, s5_d, s5_w_glu, s5_b_glu, mlp_w1, mlp_w2):
    W = dict(w_in=w_in, w_out=w_out, g_mix_pre=g_mix_pre, g_mix_post=g_mix_post,
             g_ffn_pre=g_ffn_pre, g_ffn_post=g_ffn_post,
             ssd_conv_w=ssd_conv_w, ssd_conv_b=ssd_conv_b, ssd_a_log=ssd_a_log,
             ssd_dt_bias=ssd_dt_bias, ssd_d=ssd_d, ssd_norm_g=ssd_norm_g,
             rwkv_mu=rwkv_mu, rwkv_w0=rwkv_w0, rwkv_w2=rwkv_w2, rwkv_a0=rwkv_a0, rwkv_a2=rwkv_a2,
             rwkv_g2=rwkv_g2, rwkv_k_k=rwkv_k_k, rwkv_k_a=rwkv_k_a, rwkv_r_k=rwkv_r_k,
             rwkv_lnx_w=rwkv_lnx_w, rwkv_lnx_b=rwkv_lnx_b,
             gla_wg2=gla_wg2, gla_bg=gla_bg, gla_norm_g=gla_norm_g,
             s5_lam_re=s5_lam_re, s5_lam_im=s5_lam_im, s5_log_step=s5_log_step,
             s5_b_re=s5_b_re, s5_b_im=s5_b_im, s5_c_re=s5_c_re, s5_c_im=s5_c_im,
             s5_d=s5_d, s5_w_glu=s5_w_glu, s5_b_glu=s5_b_glu, mlp_w1=mlp_w1, mlp_w2=mlp_w2)
    layers = [_layer_params(l, W) for l in range(DEPTH)]

    nbp = x_prompt.shape[0]
    mod = _ada(jnp.concatenate([c_prompt, c_sample], axis=0), w_ada, b_ada)
    mod = mod.reshape(DEPTH, -1, N_MOD, D_MODEL)

    st_sample = dict(ssd=state_ssd, ssd_conv=state_ssd_conv, rwkv=state_rwkv,
                     rwkv_shift=state_rwkv_shift, gla=state_gla, s5_re=state_s5_re, s5_im=state_s5_im)
    st_prompt = {n: jnp.zeros((DEPTH, nbp) + v.shape[2:], v.dtype) for n, v in st_sample.items()}

    y_prompt, sp = _run_group(x_prompt, mod[:, :nbp], st_prompt, layers, _PROMPT_CFG)
    y_sample, ss = _run_group(x_sample, mod[:, nbp:], st_sample, layers, _SAMPLE_CFG)
    return (y_prompt, y_sample,
            sp['ssd'], ss['ssd'], sp['ssd_conv'], ss['ssd_conv'],
            sp['rwkv'], ss['rwkv'], sp['rwkv_shift'], ss['rwkv_shift'],
            sp['gla'], ss['gla'], sp['s5_re'], ss['s5_re'], sp['s5_im'], ss['s5_im'])
```

```python
import functools
import math

import jax
import jax.numpy as jnp
from jax import lax
from jax.experimental import pallas as pl
from jax.experimental.pallas import tpu as pltpu

F32 = jnp.float32
BF16 = jnp.bfloat16

D_MODEL = 1024
DEPTH = 2
GROUP_W = 256
D_FF = 4096
N_MOD = 6
EPS = 1e-6

SSD_HEADS = 4
SSD_HEADDIM = 64
SSD_STATE = 64
SSD_NGROUPS = 2
SSD_BC = SSD_NGROUPS * SSD_STATE
SSD_CONV = 4
SSD_CONV_CH = GROUP_W + 2 * SSD_BC
SSD_PROJ = GROUP_W + SSD_CONV_CH + SSD_HEADS
SSD_PW = GROUP_W + SSD_CONV_CH + GROUP_W

RWKV_HEADS = 4
RWKV_HEADDIM = 64
RWKV_LORA = 128
RWKV_PROJ = 3 * GROUP_W + RWKV_LORA
RWKV_LN_EPS = 64e-5

GLA_HEADS = 4
GLA_DK = 32
GLA_DV = 64
GLA_GATE_LORA = 16
GLA_TAU = 16.0
GLA_QK = GLA_HEADS * GLA_DK
GLA_PROJ = 2 * GLA_QK + GROUP_W + GLA_GATE_LORA + GROUP_W
GLA_PW = 2 * GLA_QK + GROUP_W + 128 + GROUP_W

S5_GROUP = 16
S5_GROUPS = 16
S5_STATE = 64
S5_CH = S5_GROUPS * S5_STATE
S5_PROJ = GROUP_W

LANES = 128
HIST = 8
CHUNK = 64
GLA_CS = 32
VMEM_LIMIT = 56 * 1024 * 1024


def _dot(a, b, dims=(((1,), (0,)), ((), ()))):
    return lax.dot_general(a.astype(BF16), b.astype(BF16), dims, preferred_element_type=F32)


_NT = (((1,), (1,)), ((), ()))
_TN = (((0,), (0,)), ((), ()))
_NN = (((1,), (0,)), ((), ()))


def _split2(a):
    hi = a.astype(BF16)
    lo = (a - hi.astype(F32)).astype(BF16)
    return hi, lo


def _dot3(a, b, dims=_NN):
    ah, al = _split2(a)
    bh, bl = _split2(b)
    d = lambda x, y: lax.dot_general(x, y, dims, preferred_element_type=F32)
    return d(ah, bh) + (d(ah, bl) + d(al, bh))


_dot_a = _dot
_dot_inv = _dot
_dot_app = _dot
_dot_st = _dot3


def _dot_exact_lhs(m, x, dims=_NN):
    mb = m.astype(BF16)
    x1 = x.astype(BF16)
    r1 = x - x1.astype(F32)
    x2 = r1.astype(BF16)
    x3 = (r1 - x2.astype(F32)).astype(BF16)
    d = lambda y: lax.dot_general(mb, y, dims, preferred_element_type=F32)
    return d(x1) + (d(x2) + d(x3))


def _dot_exact_rhs(x, m):
    mb = m.astype(BF16)
    x1 = x.astype(BF16)
    r1 = x - x1.astype(F32)
    x2 = r1.astype(BF16)
    x3 = (r1 - x2.astype(F32)).astype(BF16)
    d = lambda y: lax.dot_general(y, mb, _NN, preferred_element_type=F32)
    return d(x1) + (d(x2) + d(x3))


def _segsum(x, width):
    n = x.shape[-1]
    r = lax.broadcasted_iota(jnp.int32, (n, n), 0) // width
    c = lax.broadcasted_iota(jnp.int32, (n, n), 1) // width
    return _dot_exact_rhs(x, jnp.where(r == c, 1.0, 0.0))


def _sigmoid(x):
    return 1.0 / (1.0 + jnp.exp(-x))


def _silu(x):
    return x * _sigmoid(x)


def _softplus(x):
    return jnp.maximum(x, 0.0) + jnp.log1p(jnp.exp(-jnp.abs(x)))


def _rms(x, g):
    return x * lax.rsqrt(jnp.mean(x * x, axis=-1, keepdims=True) + EPS) * g


def _chunk_masks(rows, lt):
    r = lax.broadcasted_iota(jnp.int32, (rows, rows), 0)
    c = lax.broadcasted_iota(jnp.int32, (rows, rows), 1)
    same = (r // lt) == (c // lt)
    return same & (c <= r), same & (c < r)


def _layer_state(state, nb):
    s_all, l = state
    shape = s_all.shape[2:]
    zeros = (0,) * len(shape)
    return (pl.BlockSpec((None, nb) + shape, lambda i, c: (l, i) + zeros),
            pl.BlockSpec((nb,) + shape, lambda i, c: (i,) + zeros),
            jax.ShapeDtypeStruct(s_all.shape[1:], F32))


def _bcast_last(x, nb, lt):
    c = x.shape[-1]
    last = x.reshape(nb, lt, c)[:, lt - 1:lt, :]
    return last, jnp.broadcast_to(last, (nb, lt, c)).reshape(nb * lt, c)


def _ada_kernel(c_ref, w_ref, b_ref, o_ref):
    c = c_ref[...]
    o_ref[0] = _dot(_silu(c), w_ref[0]) + b_ref[0]


def _ada(c_all, w_ada, b_ada):
    rows = c_all.shape[0]
    n = N_MOD * D_MODEL
    tn = 1536
    return pl.pallas_call(
        _ada_kernel,
        out_shape=jax.ShapeDtypeStruct((DEPTH, rows, n), F32),
        grid=(DEPTH, n // tn),
        in_specs=[
            pl.BlockSpec((rows, D_MODEL), lambda l, j: (0, 0)),
            pl.BlockSpec((1, D_MODEL, tn), lambda l, j: (l, 0, j)),
            pl.BlockSpec((1, 1, tn), lambda l, j: (l, 0, j)),
        ],
        out_specs=pl.BlockSpec((1, rows, tn), lambda l, j: (l, 0, j)),
        compiler_params=pltpu.CompilerParams(
            dimension_semantics=("parallel", "parallel"), vmem_limit_bytes=VMEM_LIMIT),
        name="ada",
    )(c_all, w_ada, b_ada.reshape(DEPTH, 1, n))


def _in_kernel(x_ref, mod_ref, g_ref, wssd_ref, wrwkv_ref, wgla_ref, ws5_ref,
               ossd_ref, orwkv_ref, ogla_ref, os5_ref):
    nb, lt, d = x_ref.shape
    m = mod_ref[...]
    h = _rms(x_ref[...], g_ref[...]) * (1.0 + m[:, 1:2, :]) + m[:, 0:1, :]
    hb = h.reshape(nb * lt, d).astype(BF16)
    for w_ref, o_ref in ((wssd_ref, ossd_ref), (wrwkv_ref, orwkv_ref),
                         (wgla_ref, ogla_ref), (ws5_ref, os5_ref)):
        o_ref[...] = jnp.dot(hb, w_ref[...], preferred_element_type=F32).reshape(o_ref.shape)


def _in_proj(x, mod, g, wts, nb, lt):
    bsz, seq, d = x.shape
    widths = (SSD_PW, RWKV_PROJ, GLA_PW, S5_PROJ)
    const2 = lambda i, j: (0, 0)
    return pl.pallas_call(
        _in_kernel,
        out_shape=tuple(jax.ShapeDtypeStruct((bsz, seq, w), F32) for w in widths),
        grid=(bsz // nb, seq // lt),
        in_specs=[
            pl.BlockSpec((nb, lt, d), lambda i, j: (i, j, 0)),
            pl.BlockSpec((nb, N_MOD, d), lambda i, j: (i, 0, 0)),
            pl.BlockSpec((1, 1, d), lambda i, j: (0, 0, 0)),
        ] + [pl.BlockSpec((d, w), const2) for w in widths],
        out_specs=tuple(pl.BlockSpec((nb, lt, w), lambda i, j: (i, j, 0)) for w in widths),
        compiler_params=pltpu.CompilerParams(
            dimension_semantics=("parallel", "parallel"), vmem_limit_bytes=VMEM_LIMIT),
        name="in_proj",
    )(x, mod, g, *wts)


def _out_mlp_kernel(x_ref, y0_ref, y1_ref, y2_ref, y3_ref, mod_ref, gpost_ref, gpre_ref, gfpost_ref,
                    wout_ref, w1_ref, w2_ref, o_ref):
    nb, lt, d = x_ref.shape
    rows = nb * lt
    m = mod_ref[...]
    mix = None
    for i, y_ref in enumerate((y0_ref, y1_ref, y2_ref, y3_ref)):
        t = jnp.dot(y_ref[...].reshape(rows, GROUP_W).astype(BF16), wout_ref[i],
                    preferred_element_type=F32)
        mix = t if mix is None else mix + t
    x1 = x_ref[...] + m[:, 2:3, :] * _rms(mix, gpost_ref[0]).reshape(nb, lt, d)
    h = _rms(x1, gpre_ref[...]) * (1.0 + m[:, 4:5, :]) + m[:, 3:4, :]
    hb = h.reshape(rows, d).astype(BF16)
    acc = None
    step = 1024
    for j in range(D_FF // step):
        a = jnp.dot(hb, w1_ref[:, j * step:(j + 1) * step], preferred_element_type=F32)
        a = jnp.square(jnp.maximum(a, 0.0)).astype(BF16)
        t = jnp.dot(a, w2_ref[j * step:(j + 1) * step, :], preferred_element_type=F32)
        acc = t if acc is None else acc + t
    o_ref[...] = x1 + m[:, 5:6, :] * _rms(acc, gfpost_ref[0]).reshape(nb, lt, d)


def _out_mlp(x, ys, mod, gpost, gpre, gfpost, wout, w1, w2, nb, lt):
    bsz, seq, d = x.shape
    tok = lambda i, j: (i, j, 0)
    c3 = lambda i, j: (0, 0, 0)
    c2 = lambda i, j: (0, 0)
    one = pl.Buffered(1)
    return pl.pallas_call(
        _out_mlp_kernel,
        out_shape=jax.ShapeDtypeStruct((bsz, seq, d), F32),
        grid=(bsz // nb, seq // lt),
        in_specs=[pl.BlockSpec((nb, lt, d), tok)]
        + [pl.BlockSpec((nb, lt, GROUP_W), tok)] * 4
        + [pl.BlockSpec((nb, N_MOD, d), lambda i, j: (i, 0, 0))]
        + [pl.BlockSpec((1, 1, d), c3)] * 3
        + [pl.BlockSpec((4, GROUP_W, d), c3, pipeline_mode=one),
           pl.BlockSpec((d, D_FF), c2, pipeline_mode=one),
           pl.BlockSpec((D_FF, d), c2, pipeline_mode=one)],
        out_specs=pl.BlockSpec((nb, lt, d), tok),
        compiler_params=pltpu.CompilerParams(
            dimension_semantics=("parallel", "parallel"), vmem_limit_bytes=VMEM_LIMIT),
        name="out_mlp",
    )(x, *ys, mod, gpost, gpre, gfpost, wout, w1, w2)


def _ssd_kernel(p_ref, hist_ref, s0_ref, cw_ref, cb_ref, alog_ref, dtb_ref, dsk_ref, ng_ref,
                y_ref, hout_ref, sout_ref, ext_ref, yscr_ref, *, nb, lt):
    rows = nb * lt
    hd = SSD_HEADDIM

    @pl.when(pl.program_id(1) == 0)
    def _():
        ext_ref[:, 0:HIST, :] = hist_ref[...]
        sout_ref[...] = s0_ref[...]

    p = p_ref[...]
    ext_ref[:, HIST:HIST + lt, :] = p[:, :, GROUP_W:GROUP_W + SSD_CONV_CH]
    conv = cb_ref[...]
    for j in range(SSD_CONV):
        o = HIST - (SSD_CONV - 1) + j
        conv = conv + ext_ref[:, o:o + lt, :] * cw_ref[j:j + 1, :]
    tail = ext_ref[:, lt:lt + HIST, :]
    ext_ref[:, 0:HIST, :] = tail
    hout_ref[...] = tail

    xbc = _silu(conv).reshape(rows, SSD_CONV_CH)
    xs = xbc[:, 0:GROUP_W]
    bm = xbc[:, GROUP_W:GROUP_W + SSD_BC]
    cm = xbc[:, GROUP_W + SSD_BC:]
    z = p[:, :, 0:GROUP_W].reshape(rows, GROUP_W)
    dt = _softplus(p[:, :, GROUP_W + SSD_CONV_CH:].reshape(rows, GROUP_W) + dtb_ref[...])
    a = dt * (-jnp.exp(alog_ref[...]))
    unit = lt if nb > 1 else CHUNK
    units = rows // unit
    per_chunk = CHUNK // unit
    incl, _ = _chunk_masks(rows, unit)
    cum = _dot_exact_lhs(jnp.where(incl, 1.0, 0.0), a)
    last, last_b = _bcast_last(cum, units, unit)
    xdt = xs * dt
    xw = xdt * jnp.exp(last_b - cum)
    ecum = jnp.exp(cum)
    elast = jnp.exp(last)
    ones_row = jnp.full((CHUNK, hd), 1.0 / hd, F32)
    cmask, _ = _chunk_masks(CHUNK, unit)
    heads = range(SSD_HEADS)
    hs = lambda h: slice(h * hd, (h + 1) * hd)
    gs = lambda h: slice((h // 2) * SSD_STATE, (h // 2 + 1) * SSD_STATE)
    cs = lambda j: slice(j * CHUNK, (j + 1) * CHUNK)
    us = lambda u: slice(u * unit, (u + 1) * unit)

    yh = {}
    for j in range(rows // CHUNK):
        cb = [_dot(cm[cs(j), gs(2 * g)], bm[cs(j), gs(2 * g)], _NT) for g in range(SSD_NGROUPS)]
        for h in heads:
            ccol = cum[cs(j), hs(h)]
            crow = _dot_exact_lhs(ones_row, ccol, _NT)
            decay = jnp.exp(jnp.where(cmask, ccol - crow, -jnp.inf))
            yh[j, h] = _dot(cb[h // 2] * decay, xdt[cs(j), hs(h)])
    ds = {(u, h): _dot(xw[us(u), hs(h)], bm[us(u), gs(h)], _TN) for u in range(units) for h in heads}

    before = {}
    for h in heads:
        if nb > 1:
            for u in range(units):
                before[u, h] = sout_ref[u, h]
                sout_ref[u, h] = before[u, h] * elast[u][:, hs(h)] + ds[u, h]
        else:
            s_cur = sout_ref[0, h]
            for u in range(units):
                before[u, h] = s_cur
                s_cur = s_cur * elast[u][:, hs(h)] + ds[u, h]
            sout_ref[0, h] = s_cur

    for j in range(rows // CHUNK):
        for h in heads:
            inter = [_dot(cm[us(u), gs(h)], before[u, h], _NT)
                     for u in range(j * per_chunk, (j + 1) * per_chunk)]
            inter = inter[0] if per_chunk == 1 else jnp.concatenate(inter, axis=0)
            yscr_ref[cs(j), hs(h)] = yh[j, h] + inter * ecum[cs(j), hs(h)]

    y = yscr_ref[...] + dsk_ref[...] * xs
    y = y * _silu(z)
    y = y * lax.rsqrt(_segsum(y * y, hd) * (1.0 / hd) + EPS) * ng_ref[...]
    y_ref[...] = y.reshape(nb, lt, GROUP_W)


def _ssd(p, hist, s0, prm, nb, lt):
    bsz, seq, _ = p.shape
    assert CHUNK == SSD_HEADDIM
    assert (nb > 1 and (nb * lt) % CHUNK == 0 and CHUNK % lt == 0) or (nb == 1 and lt % CHUNK == 0)
    row = lambda i, c: (0, 0)
    kern = functools.partial(_ssd_kernel, nb=nb, lt=lt)
    return pl.pallas_call(
        kern,
        out_shape=(jax.ShapeDtypeStruct((bsz, seq, GROUP_W), F32),
                   jax.ShapeDtypeStruct((bsz, HIST, SSD_CONV_CH), F32),
                   _layer_state(s0, nb)[2]),
        grid=(bsz // nb, seq // lt),
        in_specs=[
            pl.BlockSpec((nb, lt, SSD_PW), lambda i, c: (i, c, 0)),
            pl.BlockSpec((nb, HIST, SSD_CONV_CH), lambda i, c: (i, 0, 0)),
            _layer_state(s0, nb)[0],
            pl.BlockSpec((SSD_CONV, SSD_CONV_CH), row),
            pl.BlockSpec((1, SSD_CONV_CH), row),
        ] + [pl.BlockSpec((1, GROUP_W), row)] * 4,
        out_specs=(pl.BlockSpec((nb, lt, GROUP_W), lambda i, c: (i, c, 0)),
                   pl.BlockSpec((nb, HIST, SSD_CONV_CH), lambda i, c: (i, 0, 0)),
                   _layer_state(s0, nb)[1]),
        scratch_shapes=[pltpu.VMEM((nb, HIST + lt, SSD_CONV_CH), F32),
                        pltpu.VMEM((nb * lt, GROUP_W), F32)],
        compiler_params=pltpu.CompilerParams(
            dimension_semantics=("parallel", "arbitrary"), vmem_limit_bytes=VMEM_LIMIT),
        name="ssd",
    )(p, hist, s0[0], *prm)


def _unit_lower_inverse(a2, n, levels, dot):
    ri = lax.broadcasted_iota(jnp.int32, (n, n), 0)
    ci = lax.broadcasted_iota(jnp.int32, (n, n), 1)
    x = [jnp.where(ri == ci, 1.0, 0.0) - jnp.where((ri % 2 == 1) & (ci == ri - 1), a, 0.0) for a in a2]
    s = 2
    while s < levels:
        off = ((ri // s) % 2 == 1) & ((ci // s) == (ri // s) - 1)
        t = [dot(xi, jnp.where(off, a, 0.0)) for xi, a in zip(x, a2)]
        x = [xi - dot(ti, xi) for xi, ti in zip(x, t)]
        s *= 2
    return x


def _rwkv_chunk_multi(ops, sout_ref, yscr_ref, nb, lt):
    hd = RWKV_HEADDIM
    per_chunk = CHUNK // lt
    idx = [(c, h) for c in range(nb // per_chunk) for h in range(RWKV_HEADS)]
    n = range(len(idx))
    incl, strict = _chunk_masks(CHUNK, lt)
    mask2 = jnp.concatenate([strict, incl], axis=0)
    top = lambda m: m[0:CHUNK]
    bot = lambda m: m[CHUNK:2 * CHUNK]
    bl = lambda name: [ops[name][c * CHUNK:(c + 1) * CHUNK, h * hd:(h + 1) * hd] for c, h in idx]
    rt, kt, k2t, bt, k2g, bg, v = (bl(name) for name in ('rt', 'kt', 'k2t', 'bt', 'k2g', 'bg', 'v'))
    kr = [jnp.concatenate([kt[j], rt[j]], axis=0) for j in n]
    m1 = [jnp.where(mask2, _dot_a(kr[j], k2t[j], _NT), 0.0) for j in n]
    m2 = [jnp.where(mask2, _dot_a(kr[j], bt[j], _NT), 0.0) for j in n]
    x = _unit_lower_inverse([top(m) for m in m2], CHUNK, lt, _dot_inv)
    m1v = [_dot_app(m1[j], v[j]) for j in n]
    seq = lambda i: slice(i * lt, (i + 1) * lt)
    seqs = range(per_chunk)
    s_old = [[sout_ref[c * per_chunk + i, h] for i in seqs] for c, h in idx]
    pq = [[_dot_st(jnp.concatenate([kt[j][seq(i)], rt[j][seq(i)]], axis=0), s_old[j][i], _NT)
           for i in seqs] for j in n]
    p0 = [jnp.concatenate([pq[j][i][0:lt] for i in seqs], axis=0) for j in n]
    q0 = [jnp.concatenate([pq[j][i][lt:2 * lt] for i in seqs], axis=0) for j in n]
    u = [_dot_app(x[j], p0[j] + top(m1v[j])) for j in n]
    l2u = [_dot_app(bot(m2[j]), u[j]) for j in n]
    for j, (c, h) in enumerate(idx):
        yscr_ref[c * CHUNK:(c + 1) * CHUNK, h * hd:(h + 1) * hd] = q0[j] + bot(m1v[j]) - l2u[j]
        for i in seqs:
            sq = c * per_chunk + i
            sout_ref[sq, h] = (s_old[j][i] * ops['elast'][sq][:, h * hd:(h + 1) * hd]
                               + _dot_st(v[j][seq(i)], k2g[j][seq(i)], _TN)
                               - _dot_st(u[j][seq(i)], bg[j][seq(i)], _TN))


def _rwkv_chunks_seq(ops, sout_ref, yscr_ref, nc):
    hd = RWKV_HEADDIM
    idx = [(c, h) for c in range(nc) for h in range(RWKV_HEADS)]
    incl, strict = _chunk_masks(CHUNK, CHUNK)
    ri = lax.broadcasted_iota(jnp.int32, (hd, hd), 0)
    ci = lax.broadcasted_iota(jnp.int32, (hd, hd), 1)
    eye = jnp.where(ri == ci, 1.0, 0.0)
    bl = lambda name: [ops[name][c * CHUNK:(c + 1) * CHUNK, h * hd:(h + 1) * hd] for c, h in idx]
    rt, kt, k2t, bt, k2g, bg, v = (bl(n) for n in ('rt', 'kt', 'k2t', 'bt', 'k2g', 'bg', 'v'))
    n = range(len(idx))
    top = lambda m: m[0:CHUNK]
    bot = lambda m: m[CHUNK:2 * CHUNK]
    mask2 = jnp.concatenate([strict, incl], axis=0)
    kr = [jnp.concatenate([kt[j], rt[j]], axis=0) for j in n]
    m1 = [jnp.where(mask2, _dot_a(kr[j], k2t[j], _NT), 0.0) for j in n]
    m2 = [jnp.where(mask2, _dot_a(kr[j], bt[j], _NT), 0.0) for j in n]
    x = _unit_lower_inverse([top(m) for m in m2], CHUNK, CHUNK, _dot_inv)
    m1v = [_dot_app(m1[j], v[j]) for j in n]
    kp = [_dot_app(x[j], kt[j]) for j in n]
    ul = [_dot_app(x[j], top(m1v[j])) for j in n]
    rp = [rt[j] - _dot_app(bot(m2[j]), kp[j]) for j in n]
    yl = [bot(m1v[j]) - _dot_app(bot(m2[j]), ul[j]) for j in n]
    mc = [eye * ops['elast'][c][:, h * hd:(h + 1) * hd] - _dot_st(kp[j], bg[j], _TN)
          for j, (c, h) in enumerate(idx)]
    dc = [_dot_st(v[j], k2g[j], _TN) - _dot_st(ul[j], bg[j], _TN) for j in n]
    state = [sout_ref[0, h] for h in range(RWKV_HEADS)]
    for c in range(nc):
        js = [c * RWKV_HEADS + h for h in range(RWKV_HEADS)]
        ys = [_dot_st(rp[j], state[h], _NT) + yl[j] for h, j in enumerate(js)]
        state = [_dot_st(state[h], mc[j]) + dc[j] for h, j in enumerate(js)]
        for h in range(RWKV_HEADS):
            yscr_ref[c * CHUNK:(c + 1) * CHUNK, h * hd:(h + 1) * hd] = ys[h]
    for h in range(RWKV_HEADS):
        sout_ref[0, h] = state[h]


def _rwkv_kernel(p_ref, sh0_ref, s0_ref, mu_ref, wl_ref, w0_ref, a0_ref, kk_ref, ka_ref, rk_ref,
                 lnw_ref, lnb_ref, y_ref, shout_ref, sout_ref, ext_ref, yscr_ref, *, nb, lt):
    rows = nb * lt
    hd = RWKV_HEADDIM
    gw = GROUP_W

    @pl.when(pl.program_id(1) == 0)
    def _():
        ext_ref[:, HIST - 1:HIST, :] = sh0_ref[...]
        sout_ref[...] = s0_ref[...]

    p = p_ref[...]
    ext_ref[:, HIST:HIST + lt, :] = p
    prev = ext_ref[:, HIST - 1:HIST - 1 + lt, :]
    pm = (p + (prev - p) * mu_ref[...]).reshape(rows, RWKV_PROJ)
    last_p = p[:, lt - 1:lt, :]
    ext_ref[:, HIST - 1:HIST, :] = last_p
    shout_ref[...] = last_p

    r = pm[:, 0:gw]
    k = pm[:, gw:2 * gw]
    v = pm[:, 2 * gw:3 * gw]
    lo = pm[:, 3 * gw:]
    lane = lax.broadcasted_iota(jnp.int32, (rows, RWKV_LORA), 1)
    t = jnp.where(lane < 32, jnp.tanh(lo), jnp.where(lane < 64, lo, _sigmoid(lo)))
    lora = _dot(t, wl_ref[...])
    wlog = -_softplus(-(w0_ref[...] + lora[:, 0:gw])) - 0.5
    lw = -jnp.exp(wlog)
    a = _sigmoid(a0_ref[...] + lora[:, gw:2 * gw])
    gate = lora[:, 2 * gw:]
    kk = k * kk_ref[...]
    kk = kk / jnp.maximum(jnp.sqrt(_segsum(kk * kk, hd)), 1e-12)
    k2 = k * (1.0 + (a - 1.0) * ka_ref[...])
    beta = kk * a

    unit = lt if nb > 1 else CHUNK
    incl, _ = _chunk_masks(rows, unit)
    gcum = _dot_exact_lhs(jnp.where(incl, 1.0, 0.0), lw)
    glast, glast_b = _bcast_last(gcum, rows // unit, unit)
    eg = jnp.exp(gcum)
    einv = jnp.exp(-gcum)
    etail = jnp.exp(glast_b - gcum)
    ops = dict(rt=r * eg, kt=kk * jnp.exp(gcum - lw), k2t=k2 * einv, bt=beta * einv,
               k2g=k2 * etail, bg=beta * etail, v=v, elast=jnp.exp(glast))
    if nb > 1:
        _rwkv_chunk_multi(ops, sout_ref, yscr_ref, nb, lt)
    else:
        _rwkv_chunks_seq(ops, sout_ref, yscr_ref, lt // CHUNK)

    y = yscr_ref[...]
    mean = _segsum(y, hd) * (1.0 / hd)
    dlt = y - mean
    var = _segsum(dlt * dlt, hd) * (1.0 / hd)
    yn = dlt * lax.rsqrt(var + RWKV_LN_EPS) * lnw_ref[...] + lnb_ref[...]
    bonus = _segsum(r * k2 * rk_ref[...], hd) * v
    y_ref[...] = ((yn + bonus) * gate).reshape(nb, lt, gw)


def _rwkv(p, sh0, s0, prm, nb, lt):
    bsz, seq, _ = p.shape
    assert (nb > 1 and (nb * lt) % CHUNK == 0 and CHUNK % lt == 0) or (nb == 1 and lt % CHUNK == 0)
    row = lambda i, c: (0, 0)
    kern = functools.partial(_rwkv_kernel, nb=nb, lt=lt)
    return pl.pallas_call(
        kern,
        out_shape=(jax.ShapeDtypeStruct((bsz, seq, GROUP_W), F32),
                   jax.ShapeDtypeStruct((bsz, 1, RWKV_PROJ), F32),
                   _layer_state(s0, nb)[2]),
        grid=(bsz // nb, seq // lt),
        in_specs=[
            pl.BlockSpec((nb, lt, RWKV_PROJ), lambda i, c: (i, c, 0)),
            pl.BlockSpec((nb, 1, RWKV_PROJ), lambda i, c: (i, 0, 0)),
            _layer_state(s0, nb)[0],
            pl.BlockSpec((1, RWKV_PROJ), row),
            pl.BlockSpec((RWKV_LORA, 3 * GROUP_W), row),
        ] + [pl.BlockSpec((1, GROUP_W), row)] * 7,
        out_specs=(pl.BlockSpec((nb, lt, GROUP_W), lambda i, c: (i, c, 0)),
                   pl.BlockSpec((nb, 1, RWKV_PROJ), lambda i, c: (i, 0, 0)),
                   _layer_state(s0, nb)[1]),
        scratch_shapes=[pltpu.VMEM((nb, HIST + lt, RWKV_PROJ), F32),
                        pltpu.VMEM((nb * lt, GROUP_W), F32)],
        compiler_params=pltpu.CompilerParams(
            dimension_semantics=("parallel", "arbitrary"), vmem_limit_bytes=VMEM_LIMIT),
        name="rwkv",
    )(p, sh0, s0[0], *prm)


def _gla_kernel(p_ref, s0_ref, wg2_ref, bg_ref, ng_ref, y_ref, sout_ref, yscr_ref, *, nb, lt, cs):
    rows = nb * lt
    dk, dv = GLA_DK, GLA_DV
    qk = GLA_QK
    units = rows // cs
    per_group = GLA_CS // cs

    @pl.when(pl.program_id(1) == 0)
    def _():
        sout_ref[...] = s0_ref[...]

    p = p_ref[...].reshape(rows, GLA_PW)
    q = p[:, 0:qk] * (dk ** -0.5)
    k = p[:, qk:2 * qk]
    v = p[:, 2 * qk:2 * qk + GROUP_W]
    dg = p[:, 2 * qk + GROUP_W:2 * qk + GROUP_W + 128]
    og = p[:, 2 * qk + GROUP_W + 128:]
    lg = -_softplus(-(_dot(dg, wg2_ref[...]) + bg_ref[...])) * (1.0 / GLA_TAU)
    incl, _ = _chunk_masks(rows, cs)
    b = _dot_exact_lhs(jnp.where(incl, 1.0, 0.0), lg)
    bl, bl_b = _bcast_last(b, units, cs)
    qd = q * jnp.exp(b)
    kd = k * jnp.exp(-b)
    kw = k * jnp.exp(bl_b - b)
    ebl = jnp.exp(bl)
    gmask, _ = _chunk_masks(GLA_CS, cs)

    heads = range(GLA_HEADS)
    ks = lambda h: slice(h * dk, (h + 1) * dk)
    vs = lambda h: slice(h * dv, (h + 1) * dv)
    gs = lambda g: slice(g * GLA_CS, (g + 1) * GLA_CS)
    us = lambda u: slice(u * cs, (u + 1) * cs)

    yh = {}
    for g in range(rows // GLA_CS):
        for h in heads:
            att = jnp.where(gmask, _dot(qd[gs(g), ks(h)], kd[gs(g), ks(h)], _NT), 0.0)
            yh[g, h] = _dot(att, v[gs(g), vs(h)])
    ds = {(u, h): _dot(kw[us(u), ks(h)], v[us(u), vs(h)], _TN) for u in range(units) for h in heads}
    ri = lax.broadcasted_iota(jnp.int32, (dk, dk), 0)
    ci = lax.broadcasted_iota(jnp.int32, (dk, dk), 1)
    ones_kv = jnp.ones((dk, dv), F32)
    ecol = {(u, h): _dot_exact_rhs(jnp.where(ri == ci, ebl[u][:, ks(h)], 0.0), ones_kv)
            for u in range(units) for h in heads}

    before = {}
    for h in heads:
        if nb > 1:
            for u in range(units):
                before[u, h] = sout_ref[u, h]
                sout_ref[u, h] = before[u, h] * ecol[u, h] + ds[u, h]
        else:
            s_cur = sout_ref[0, h]
            for u in range(units):
                before[u, h] = s_cur
                s_cur = s_cur * ecol[u, h] + ds[u, h]
            sout_ref[0, h] = s_cur

    for g in range(rows // GLA_CS):
        for h in heads:
            inter = [_dot(qd[us(u), ks(h)], before[u, h])
                     for u in range(g * per_group, (g + 1) * per_group)]
            inter = inter[0] if per_group == 1 else jnp.concatenate(inter, axis=0)
            yscr_ref[gs(g), vs(h)] = yh[g, h] + inter

    y = yscr_ref[...]
    y = y * lax.rsqrt(_segsum(y * y, dv) * (1.0 / dv) + EPS) * ng_ref[...]
    y_ref[...] = (y * _silu(og)).reshape(nb, lt, GROUP_W)


def _gla(p, s0, prm, nb, lt, cs):
    bsz, seq, _ = p.shape
    assert (nb == 1 and cs == GLA_CS and lt % cs == 0) or (
        lt == cs and (nb * cs) % GLA_CS == 0 and GLA_CS % cs == 0)
    row = lambda i, c: (0, 0)
    kern = functools.partial(_gla_kernel, nb=nb, lt=lt, cs=cs)
    return pl.pallas_call(
        kern,
        out_shape=(jax.ShapeDtypeStruct((bsz, seq, GROUP_W), F32),
                   _layer_state(s0, nb)[2]),
        grid=(bsz // nb, seq // lt),
        in_specs=[
            pl.BlockSpec((nb, lt, GLA_PW), lambda i, c: (i, c, 0)),
            _layer_state(s0, nb)[0],
            pl.BlockSpec((128, GLA_QK), row),
            pl.BlockSpec((1, GLA_QK), row),
            pl.BlockSpec((1, GROUP_W), row),
        ],
        out_specs=(pl.BlockSpec((nb, lt, GROUP_W), lambda i, c: (i, c, 0)),
                   _layer_state(s0, nb)[1]),
        scratch_shapes=[pltpu.VMEM((nb * lt, GROUP_W), F32)],
        compiler_params=pltpu.CompilerParams(
            dimension_semantics=("parallel", "arbitrary"), vmem_limit_bytes=VMEM_LIMIT),
        name="gla",
    )(p, s0[0], *prm)


def _s5_pitch(lt):
    return lt if (lt // 8) % 2 == 1 else lt + 8


def _s5_kernel(u_ref, hr0_ref, hi0_ref, wb_ref, wc_ref, ar_ref, ai_ref, d_ref, wglu_ref, bglu_ref,
               y_ref, hr_ref, hi_ref, h_ref, *, nb, lt):
    rows = nb * lt
    n = S5_CH

    @pl.when(pl.program_id(1) == 0)
    def _():
        hr_ref[...] = hr0_ref[...]
        hi_ref[...] = hi0_ref[...]

    u = u_ref[...].reshape(rows, GROUP_W)
    bu = _dot(u, wb_ref[...])
    nt = n // LANES
    lanes = lambda x, j: x[:, j * LANES:(j + 1) * LANES]
    pitch = _s5_pitch(lt)
    seq_rows = lambda i: slice(i * pitch, i * pitch + lt)
    for j in range(2 * nt):
        for i in range(nb):
            h_ref[j, seq_rows(i), :] = lanes(bu, j)[i * lt:(i + 1) * lt]
    ar = [jnp.broadcast_to(lanes(ar_ref[...], j), (nb, LANES)) for j in range(nt)]
    ai = [jnp.broadcast_to(lanes(ai_ref[...], j), (nb, LANES)) for j in range(nt)]
    hr = [lanes(hr_ref[...], j) for j in range(nt)]
    hi = [lanes(hi_ref[...], j) for j in range(nt)]
    for t in range(lt):
        at_t = pl.ds(t, nb, stride=pitch)
        for j in range(nt):
            hr[j], hi[j] = (ar[j] * hr[j] - ai[j] * hi[j] + h_ref[j, at_t, :],
                            ar[j] * hi[j] + ai[j] * hr[j] + h_ref[nt + j, at_t, :])
            h_ref[j, at_t, :] = hr[j]
            h_ref[nt + j, at_t, :] = hi[j]
    hr_ref[...] = jnp.concatenate(hr, axis=1)
    hi_ref[...] = jnp.concatenate(hi, axis=1)
    tile = lambda j: jnp.concatenate([h_ref[j, seq_rows(i), :] for i in range(nb)], axis=0)
    h_re = jnp.concatenate([tile(j) for j in range(nt)], axis=1)
    h_im = jnp.concatenate([tile(nt + j) for j in range(nt)], axis=1)
    y = _dot(h_re, wc_ref[0]) - _dot(h_im, wc_ref[1])
    y = y + d_ref[...] * u
    y = 0.5 * y * (1.0 + jnp.tanh(math.sqrt(2.0 / math.pi) * (y + 0.044715 * (y * y * y))))
    y = y * _sigmoid(_dot(y, wglu_ref[...]) + bglu_ref[...])
    y_ref[...] = y.reshape(nb, lt, GROUP_W)


def _s5(u, hr0, hi0, prm, nb, lt):
    bsz, seq, _ = u.shape
    row = lambda i, c: (0, 0)
    st = pl.BlockSpec((nb, S5_CH), lambda i, c: (i, 0))
    kern = functools.partial(_s5_kernel, nb=nb, lt=lt)
    return pl.pallas_call(
        kern,
        out_shape=(jax.ShapeDtypeStruct((bsz, seq, GROUP_W), F32),
                   jax.ShapeDtypeStruct((bsz, S5_CH), F32),
                   jax.ShapeDtypeStruct((bsz, S5_CH), F32)),
        grid=(bsz // nb, seq // lt),
        in_specs=[
            pl.BlockSpec((nb, lt, GROUP_W), lambda i, c: (i, c, 0)), st, st,
            pl.BlockSpec((GROUP_W, 2 * S5_CH), row),
            pl.BlockSpec((2, S5_CH, GROUP_W), lambda i, c: (0, 0, 0)),
            pl.BlockSpec((1, S5_CH), row),
            pl.BlockSpec((1, S5_CH), row),
            pl.BlockSpec((1, GROUP_W), row),
            pl.BlockSpec((GROUP_W, GROUP_W), row),
            pl.BlockSpec((1, GROUP_W), row),
        ],
        out_specs=(pl.BlockSpec((nb, lt, GROUP_W), lambda i, c: (i, c, 0)), st, st),
        scratch_shapes=[pltpu.VMEM((2 * S5_CH // LANES, nb * _s5_pitch(lt), LANES), F32)],
        compiler_params=pltpu.CompilerParams(
            dimension_semantics=("parallel", "arbitrary"), vmem_limit_bytes=VMEM_LIMIT),
        name="s5",
    )(u, hr0, hi0, *prm)


def _block_diag_in(b):
    g, n, i = b.shape
    eye = jnp.eye(g, dtype=b.dtype)
    return jnp.einsum('gni,gh->gihn', b, eye).reshape(g * i, g * n)


def _block_diag_out(c):
    g, o, n = c.shape
    eye = jnp.eye(g, dtype=c.dtype)
    return jnp.einsum('gon,gh->gnho', c, eye).reshape(g * n, g * o)


def _layer_params(l, W):
    f = lambda name: W[name][l]
    row = lambda a: a.reshape(1, -1).astype(F32)
    rep = lambda a, n: jnp.repeat(a.reshape(-1), n).reshape(1, -1).astype(F32)
    w_in = f('w_in').astype(BF16)
    o_r = SSD_PROJ
    o_g = o_r + RWKV_PROJ
    o_s = o_g + GLA_PROJ
    w_ssd = jnp.concatenate([w_in[:, :GROUP_W + SSD_CONV_CH],
                             jnp.repeat(w_in[:, GROUP_W + SSD_CONV_CH:o_r], SSD_HEADDIM, axis=1)], axis=1)
    w_rwkv = w_in[:, o_r:o_g]
    gq = o_g + 2 * GLA_QK + GROUP_W
    w_gla = jnp.concatenate([w_in[:, o_g:gq], w_in[:, gq:gq + GLA_GATE_LORA],
                             jnp.zeros((D_MODEL, 128 - GLA_GATE_LORA), BF16),
                             w_in[:, gq + GLA_GATE_LORA:o_s]], axis=1)
    w_s5 = w_in[:, o_s:]
    in_w = (w_ssd, w_rwkv, w_gla, w_s5)

    ssd = (f('ssd_conv_w'), row(f('ssd_conv_b')), rep(f('ssd_a_log'), SSD_HEADDIM),
           rep(f('ssd_dt_bias'), SSD_HEADDIM), rep(f('ssd_d'), SSD_HEADDIM), row(f('ssd_norm_g')))

    z = lambda r, c: jnp.zeros((r, c), F32)
    w_lora = jnp.concatenate([
        jnp.concatenate([f('rwkv_w2'), z(32, 2 * GROUP_W)], axis=1),
        jnp.concatenate([z(32, GROUP_W), f('rwkv_a2'), z(32, GROUP_W)], axis=1),
        jnp.concatenate([z(64, 2 * GROUP_W), f('rwkv_g2')], axis=1)], axis=0).astype(BF16)
    rwkv = (row(f('rwkv_mu')), w_lora, row(f('rwkv_w0')), row(f('rwkv_a0')), row(f('rwkv_k_k')),
            row(f('rwkv_k_a')), row(f('rwkv_r_k')), row(f('rwkv_lnx_w')), row(f('rwkv_lnx_b')))

    wg2 = jnp.concatenate([f('gla_wg2'), z(128 - GLA_GATE_LORA, GLA_QK)], axis=0).astype(BF16)
    gla = (wg2, row(f('gla_bg')), row(f('gla_norm_g')))

    lr = f('s5_lam_re').astype(F32)
    li = f('s5_lam_im').astype(F32)
    dts = jnp.exp(f('s5_log_step').astype(F32))[:, None]
    mag = jnp.exp(lr * dts)
    ar = mag * jnp.cos(li * dts)
    ai = mag * jnp.sin(li * dts)
    den = lr * lr + li * li
    fr = ((ar - 1.0) * lr + ai * li) / den
    fi = (ai * lr - (ar - 1.0) * li) / den
    b_re, b_im = f('s5_b_re'), f('s5_b_im')
    bb_re = fr[..., None] * b_re - fi[..., None] * b_im
    bb_im = fr[..., None] * b_im + fi[..., None] * b_re
    wb = jnp.concatenate([_block_diag_in(bb_re), _block_diag_in(bb_im)], axis=1).astype(BF16)
    wc = jnp.stack([_block_diag_out(f('s5_c_re')), _block_diag_out(f('s5_c_im'))]).astype(BF16)
    s5 = (wb, wc, row(ar), row(ai), row(f('s5_d')), f('s5_w_glu').astype(BF16), row(f('s5_b_glu')))

    g3 = lambda name: f(name).reshape(1, 1, D_MODEL).astype(F32)
    return dict(
        in_w=in_w, ssd=ssd, rwkv=rwkv, gla=gla, s5=s5,
        g_mix_pre=g3('g_mix_pre'), g_mix_post=g3('g_mix_post'),
        g_ffn_pre=g3('g_ffn_pre'), g_ffn_post=g3('g_ffn_post'),
        w_out=f('w_out').reshape(4, GROUP_W, D_MODEL).astype(BF16),
        w1=f('mlp_w1').astype(BF16), w2=f('mlp_w2').astype(BF16))


def _run_group(x, mod, st, layers, cfg):
    bsz = x.shape[0]
    out = {n: [] for n in st}
    for l in range(DEPTH):
        P = layers[l]
        p_ssd, p_rwkv, p_gla, p_s5 = _in_proj(x, mod[l], P['g_mix_pre'], P['in_w'], *cfg['tok'])

        hist = jnp.pad(st['ssd_conv'][l], ((0, 0), (HIST - (SSD_CONV - 1), 0), (0, 0)))
        y_ssd, hist_new, ssd_new = _ssd(p_ssd, hist, (st['ssd'], l), P['ssd'], *cfg['ssd'])
        y_rwkv, shift_new, rwkv_new = _rwkv(p_rwkv, st['rwkv_shift'][l][:, None, :], (st['rwkv'], l),
                                            P['rwkv'], *cfg['rwkv'])
        y_gla, gla_new = _gla(p_gla, (st['gla'], l), P['gla'], *cfg['gla'])
        y_s5, s5r_new, s5i_new = _s5(p_s5, st['s5_re'][l].reshape(bsz, S5_CH),
                                     st['s5_im'][l].reshape(bsz, S5_CH), P['s5'], *cfg['s5'])

        x = _out_mlp(x, (y_ssd, y_rwkv, y_gla, y_s5), mod[l], P['g_mix_post'], P['g_ffn_pre'],
                     P['g_ffn_post'], P['w_out'], P['w1'], P['w2'], *cfg['tok'])

        out['ssd'].append(ssd_new)
        out['ssd_conv'].append(hist_new[:, HIST - (SSD_CONV - 1):, :])
        out['rwkv'].append(rwkv_new)
        out['rwkv_shift'].append(shift_new[:, 0, :])
        out['gla'].append(gla_new)
        out['s5_re'].append(s5r_new.reshape(bsz, S5_GROUPS, S5_STATE))
        out['s5_im'].append(s5i_new.reshape(bsz, S5_GROUPS, S5_STATE))
    return x, {n: jnp.stack(v) for n, v in out.items()}


_PROMPT_CFG = dict(tok=(1, 512), ssd=(1, 256), rwkv=(1, 256), gla=(1, 256, GLA_CS), s5=(8, 128))
_SAMPLE_CFG = dict(tok=(64, 8), ssd=(32, 8), rwkv=(32, 8), gla=(16, 8, 8), s5=(64, 8))


def kernel(x_prompt, x_sample, c_prompt, c_sample, state_ssd, state_ssd_conv, state_rwkv, state_rwkv_shift, state_gla, state_s5_re, state_s5_im, w_ada, b_ada, g_mix_pre, g_mix_post, g_ffn_pre, g_ffn_post, w_in, w_out, ssd_conv_w, ssd_conv_b, ssd_a_log, ssd_dt_bias, ssd_d, ssd_norm_g, rwkv_mu, rwkv_w0, rwkv_w2, rwkv_a0, rwkv_a2, rwkv_g2, rwkv_k_k, rwkv_k_a, rwkv_r_k, rwkv_lnx_w, rwkv_lnx_b, gla_wg2, gla_bg, gla_norm_g, s5_lam_re, s5_lam_im, s5_log_step, s5_b_re, s5_b_im, s5_c_re, s5_c_im, s5_d, s5_w_glu, s5_b_glu, mlp_w1, mlp_w2):
    W = dict(w_in=w_in, w_out=w_out, g_mix_pre=g_mix_pre, g_mix_post=g_mix_post,
             g_ffn_pre=g_ffn_pre, g_ffn_post=g_ffn_post,
             ssd_conv_w=ssd_conv_w, ssd_conv_b=ssd_conv_b, ssd_a_log=ssd_a_log,
             ssd_dt_bias=ssd_dt_bias, ssd_d=ssd_d, ssd_norm_g=ssd_norm_g,
             rwkv_mu=rwkv_mu, rwkv_w0=rwkv_w0, rwkv_w2=rwkv_w2, rwkv_a0=rwkv_a0, rwkv_a2=rwkv_a2,
             rwkv_g2=rwkv_g2, rwkv_k_k=rwkv_k_k, rwkv_k_a=rwkv_k_a, rwkv_r_k=rwkv_r_k,
             rwkv_lnx_w=rwkv_lnx_w, rwkv_lnx_b=rwkv_lnx_b,
             gla_wg2=gla_wg2, gla_bg=gla_bg, gla_norm_g=gla_norm_g,
             s5_lam_re=s5_lam_re, s5_lam_im=s5_lam_im, s5_log_step=s5_log_step,
             s5_b_re=s5_b_re, s5_b_im=s5_b_im, s5_c_re=s5_c_re, s5_c_im=s5_c_im,
             s5_d=s5_d, s5_w_glu=s5_w_glu, s5_b_glu=s5_b_glu, mlp_w1=mlp_w1, mlp_w2=mlp_w2)
    layers = [_layer_params(l, W) for l in range(DEPTH)]

    nbp = x_prompt.shape[0]
    mod = _ada(jnp.concatenate([c_prompt, c_sample], axis=0), w_ada, b_ada)
    mod = mod.reshape(DEPTH, -1, N_MOD, D_MODEL)

    st_sample = dict(ssd=state_ssd, ssd_conv=state_ssd_conv, rwkv=state_rwkv,
                     rwkv_shift=state_rwkv_shift, gla=state_gla, s5_re=state_s5_re, s5_im=state_s5_im)
    st_prompt = {n: jnp.zeros((DEPTH, nbp) + v.shape[2:], v.dtype) for n, v in st_sample.items()}

    y_prompt, sp = _run_group(x_prompt, mod[:, :nbp], st_prompt, layers, _PROMPT_CFG)
    y_sample, ss = _run_group(x_sample, mod[:, nbp:], st_sample, layers, _SAMPLE_CFG)
    return (y_prompt, y_sample,
            sp['ssd'], ss['ssd'], sp['ssd_conv'], ss['ssd_conv'],
            sp['rwkv'], ss['rwkv'], sp['rwkv_shift'], ss['rwkv_shift'],
            sp['gla'], ss['gla'], sp['s5_re'], ss['s5_re'], sp['s5_im'], ss['s5_im'])
```

```python
import functools
import math

import jax
import jax.numpy as jnp
from jax import lax
from jax.experimental import pallas as pl
from jax.experimental.pallas import tpu as pltpu

F32 = jnp.float32
BF16 = jnp.bfloat16

D_MODEL = 1024
DEPTH = 2
GROUP_W = 256
D_FF = 4096
N_MOD = 6
EPS = 1e-6

SSD_HEADS = 4
SSD_HEADDIM = 64
SSD_STATE = 64
SSD_NGROUPS = 2
SSD_BC = SSD_NGROUPS * SSD_STATE
SSD_CONV = 4
SSD_CONV_CH = GROUP_W + 2 * SSD_BC
SSD_PROJ = GROUP_W + SSD_CONV_CH + SSD_HEADS
SSD_PW = GROUP_W + SSD_CONV_CH + GROUP_W

RWKV_HEADS = 4
RWKV_HEADDIM = 64
RWKV_LORA = 128
RWKV_PROJ = 3 * GROUP_W + RWKV_LORA
RWKV_LN_EPS = 64e-5

GLA_HEADS = 4
GLA_DK = 32
GLA_DV = 64
GLA_GATE_LORA = 16
GLA_TAU = 16.0
GLA_QK = GLA_HEADS * GLA_DK
GLA_PROJ = 2 * GLA_QK + GROUP_W + GLA_GATE_LORA + GROUP_W
GLA_PW = 2 * GLA_QK + GROUP_W + 128 + GROUP_W

S5_GROUP = 16
S5_GROUPS = 16
S5_STATE = 64
S5_CH = S5_GROUPS * S5_STATE
S5_PROJ = GROUP_W

LANES = 128
HIST = 8
CHUNK = 64
GLA_CS = 32
VMEM_LIMIT = 56 * 1024 * 1024


def _dot(a, b, dims=(((1,), (0,)), ((), ()))):
    return lax.dot_general(a.astype(BF16), b.astype(BF16), dims, preferred_element_type=F32)


_NT = (((1,), (1,)), ((), ()))
_TN = (((0,), (0,)), ((), ()))
_NN = (((1,), (0,)), ((), ()))


def _split2(a):
    hi = a.astype(BF16)
    lo = (a - hi.astype(F32)).astype(BF16)
    return hi, lo


def _dot3(a, b, dims=_NN):
    ah, al = _split2(a)
    bh, bl = _split2(b)
    d = lambda x, y: lax.dot_general(x, y, dims, preferred_element_type=F32)
    return d(ah, bh) + (d(ah, bl) + d(al, bh))


_dot_a = _dot
_dot_inv = _dot
_dot_app = _dot
_dot_st = _dot3


def _dot_exact_lhs(m, x, dims=_NN):
    mb = m.astype(BF16)
    x1 = x.astype(BF16)
    r1 = x - x1.astype(F32)
    x2 = r1.astype(BF16)
    x3 = (r1 - x2.astype(F32)).astype(BF16)
    d = lambda y: lax.dot_general(mb, y, dims, preferred_element_type=F32)
    return d(x1) + (d(x2) + d(x3))


def _dot_exact_rhs(x, m):
    mb = m.astype(BF16)
    x1 = x.astype(BF16)
    r1 = x - x1.astype(F32)
    x2 = r1.astype(BF16)
    x3 = (r1 - x2.astype(F32)).astype(BF16)
    d = lambda y: lax.dot_general(y, mb, _NN, preferred_element_type=F32)
    return d(x1) + (d(x2) + d(x3))


def _segsum(x, width):
    n = x.shape[-1]
    r = lax.broadcasted_iota(jnp.int32, (n, n), 0) // width
    c = lax.broadcasted_iota(jnp.int32, (n, n), 1) // width
    return _dot_exact_rhs(x, jnp.where(r == c, 1.0, 0.0))


def _sigmoid(x):
    return 1.0 / (1.0 + jnp.exp(-x))


def _silu(x):
    return x * _sigmoid(x)


def _softplus(x):
    return jnp.maximum(x, 0.0) + jnp.log1p(jnp.exp(-jnp.abs(x)))


def _rms(x, g):
    return x * lax.rsqrt(jnp.mean(x * x, axis=-1, keepdims=True) + EPS) * g


def _chunk_masks(rows, lt):
    r = lax.broadcasted_iota(jnp.int32, (rows, rows), 0)
    c = lax.broadcasted_iota(jnp.int32, (rows, rows), 1)
    same = (r // lt) == (c // lt)
    return same & (c <= r), same & (c < r)


def _lsel(a, l, **kw):
    shape = a.shape[1:]
    return pl.BlockSpec((None,) + shape, lambda i, c: (l,) + (0,) * len(shape), **kw)


def _lseq(a, l, nb):
    shape = a.shape[2:]
    return pl.BlockSpec((None, nb) + shape, lambda i, c: (l, i) + (0,) * len(shape))


def _mixer_call(kern, name, l, nb, lt, p, states, params, prev, scratch):
    bsz, seq, width = p.shape
    n_in = 1 + len(states) + len(params)
    alias = tuple(prev) if prev is not None else ()

    def body(*refs):
        return kern(*refs[:n_in], *refs[n_in + len(alias):])

    tok = lambda w: pl.BlockSpec((nb, lt, w), lambda i, c: (i, c, 0))
    return pl.pallas_call(
        body,
        out_shape=(jax.ShapeDtypeStruct((bsz, seq, GROUP_W), F32),)
        + tuple(jax.ShapeDtypeStruct(s.shape, F32) for s in states),
        grid=(bsz // nb, seq // lt),
        in_specs=[tok(width)] + [_lseq(s, l, nb) for s in states] + [_lsel(a, l) for a in params]
        + [pl.BlockSpec(memory_space=pl.ANY)] * len(alias),
        out_specs=(tok(GROUP_W),) + tuple(_lseq(s, l, nb) for s in states),
        scratch_shapes=scratch,
        input_output_aliases={n_in + k: 1 + k for k in range(len(alias))},
        compiler_params=pltpu.CompilerParams(
            dimension_semantics=("parallel", "arbitrary"), vmem_limit_bytes=VMEM_LIMIT),
        name=name,
    )(p, *states, *params, *alias)


def _bcast_last(x, nb, lt):
    c = x.shape[-1]
    last = x.reshape(nb, lt, c)[:, lt - 1:lt, :]
    return last, jnp.broadcast_to(last, (nb, lt, c)).reshape(nb * lt, c)


def _ada_kernel(c_ref, w_ref, b_ref, o_ref):
    c = c_ref[...]
    o_ref[0] = _dot(_silu(c), w_ref[0]) + b_ref[0]


def _ada(c_all, w_ada, b_ada):
    rows = c_all.shape[0]
    n = N_MOD * D_MODEL
    tn = 1536
    return pl.pallas_call(
        _ada_kernel,
        out_shape=jax.ShapeDtypeStruct((DEPTH, rows, n), F32),
        grid=(DEPTH, n // tn),
        in_specs=[
            pl.BlockSpec((rows, D_MODEL), lambda l, j: (0, 0)),
            pl.BlockSpec((1, D_MODEL, tn), lambda l, j: (l, 0, j)),
            pl.BlockSpec((1, 1, tn), lambda l, j: (l, 0, j)),
        ],
        out_specs=pl.BlockSpec((1, rows, tn), lambda l, j: (l, 0, j)),
        compiler_params=pltpu.CompilerParams(
            dimension_semantics=("parallel", "parallel"), vmem_limit_bytes=VMEM_LIMIT),
        name="ada",
    )(c_all, w_ada, b_ada.reshape(DEPTH, 1, n))


def _in_kernel(x_ref, mod_ref, g_ref, wssd_ref, wrwkv_ref, wgla_ref, ws5_ref,
               ossd_ref, orwkv_ref, ogla_ref, os5_ref):
    nb, lt, d = x_ref.shape
    m = mod_ref[...]
    h = _rms(x_ref[...], g_ref[...]) * (1.0 + m[:, 1:2, :]) + m[:, 0:1, :]
    hb = h.reshape(nb * lt, d).astype(BF16)
    for w_ref, o_ref in ((wssd_ref, ossd_ref), (wrwkv_ref, orwkv_ref),
                         (wgla_ref, ogla_ref), (ws5_ref, os5_ref)):
        o_ref[...] = jnp.dot(hb, w_ref[...], preferred_element_type=F32).reshape(o_ref.shape)


def _mod_spec(mod, l, first, nb):
    return pl.BlockSpec((None, nb, N_MOD, mod.shape[-1]), lambda i, j: (l, first // nb + i, 0, 0))


def _in_proj(x, mod, first, g, wts, l, nb, lt):
    bsz, seq, d = x.shape
    widths = (SSD_PW, RWKV_PROJ, GLA_PW, S5_PROJ)
    return pl.pallas_call(
        _in_kernel,
        out_shape=tuple(jax.ShapeDtypeStruct((bsz, seq, w), F32) for w in widths),
        grid=(bsz // nb, seq // lt),
        in_specs=[pl.BlockSpec((nb, lt, d), lambda i, j: (i, j, 0)), _mod_spec(mod, l, first, nb),
                  _lsel(g, l)] + [_lsel(w, l) for w in wts],
        out_specs=tuple(pl.BlockSpec((nb, lt, w), lambda i, j: (i, j, 0)) for w in widths),
        compiler_params=pltpu.CompilerParams(
            dimension_semantics=("parallel", "parallel"), vmem_limit_bytes=VMEM_LIMIT),
        name="in_proj",
    )(x, mod, g, *wts)


def _out_mlp_kernel(x_ref, y0_ref, y1_ref, y2_ref, y3_ref, mod_ref, gpost_ref, gpre_ref, gfpost_ref,
                    wout_ref, w1_ref, w2_ref, o_ref):
    nb, lt, d = x_ref.shape
    rows = nb * lt
    m = mod_ref[...]
    mix = None
    for i, y_ref in enumerate((y0_ref, y1_ref, y2_ref, y3_ref)):
        t = jnp.dot(y_ref[...].reshape(rows, GROUP_W).astype(BF16), wout_ref[i],
                    preferred_element_type=F32)
        mix = t if mix is None else mix + t
    x1 = x_ref[...] + m[:, 2:3, :] * _rms(mix, gpost_ref[0]).reshape(nb, lt, d)
    h = _rms(x1, gpre_ref[...]) * (1.0 + m[:, 4:5, :]) + m[:, 3:4, :]
    hb = h.reshape(rows, d).astype(BF16)
    acc = None
    step = 1024
    for j in range(D_FF // step):
        a = jnp.dot(hb, w1_ref[:, j * step:(j + 1) * step], preferred_element_type=F32)
        a = jnp.square(jnp.maximum(a, 0.0)).astype(BF16)
        t = jnp.dot(a, w2_ref[j * step:(j + 1) * step, :], preferred_element_type=F32)
        acc = t if acc is None else acc + t
    o_ref[...] = x1 + m[:, 5:6, :] * _rms(acc, gfpost_ref[0]).reshape(nb, lt, d)


def _out_mlp(x, ys, mod, first, gpost, gpre, gfpost, wout, w1, w2, l, nb, lt):
    bsz, seq, d = x.shape
    tok = lambda i, j: (i, j, 0)
    one = pl.Buffered(1)
    return pl.pallas_call(
        _out_mlp_kernel,
        out_shape=jax.ShapeDtypeStruct((bsz, seq, d), F32),
        grid=(bsz // nb, seq // lt),
        in_specs=[pl.BlockSpec((nb, lt, d), tok)]
        + [pl.BlockSpec((nb, lt, GROUP_W), tok)] * 4
        + [_mod_spec(mod, l, first, nb)]
        + [_lsel(g, l) for g in (gpost, gpre, gfpost)]
        + [_lsel(w, l, pipeline_mode=one) for w in (wout, w1, w2)],
        out_specs=pl.BlockSpec((nb, lt, d), tok),
        compiler_params=pltpu.CompilerParams(
            dimension_semantics=("parallel", "parallel"), vmem_limit_bytes=VMEM_LIMIT),
        name="out_mlp",
    )(x, *ys, mod, gpost, gpre, gfpost, wout, w1, w2)


def _ssd_kernel(p_ref, hist_ref, s0_ref, cw_ref, cb_ref, alog_ref, dtb_ref, dsk_ref, ng_ref,
                y_ref, hout_ref, sout_ref, ext_ref, yscr_ref, *, nb, lt):
    rows = nb * lt
    hd = SSD_HEADDIM

    @pl.when(pl.program_id(1) == 0)
    def _():
        ext_ref[:, 0:HIST, :] = hist_ref[...]
        sout_ref[...] = s0_ref[...]

    p = p_ref[...]
    ext_ref[:, HIST:HIST + lt, :] = p[:, :, GROUP_W:GROUP_W + SSD_CONV_CH]
    conv = cb_ref[...]
    for j in range(SSD_CONV):
        o = HIST - (SSD_CONV - 1) + j
        conv = conv + ext_ref[:, o:o + lt, :] * cw_ref[j:j + 1, :]
    tail = ext_ref[:, lt:lt + HIST, :]
    ext_ref[:, 0:HIST, :] = tail
    hout_ref[...] = tail

    xbc = _silu(conv).reshape(rows, SSD_CONV_CH)
    xs = xbc[:, 0:GROUP_W]
    bm = xbc[:, GROUP_W:GROUP_W + SSD_BC]
    cm = xbc[:, GROUP_W + SSD_BC:]
    z = p[:, :, 0:GROUP_W].reshape(rows, GROUP_W)
    dt = _softplus(p[:, :, GROUP_W + SSD_CONV_CH:].reshape(rows, GROUP_W) + dtb_ref[...])
    a = dt * (-jnp.exp(alog_ref[...]))
    unit = lt if nb > 1 else CHUNK
    units = rows // unit
    per_chunk = CHUNK // unit
    incl, _ = _chunk_masks(rows, unit)
    cum = _dot_exact_lhs(jnp.where(incl, 1.0, 0.0), a)
    last, last_b = _bcast_last(cum, units, unit)
    xdt = xs * dt
    xw = xdt * jnp.exp(last_b - cum)
    ecum = jnp.exp(cum)
    elast = jnp.exp(last)
    ones_row = jnp.full((CHUNK, hd), 1.0 / hd, F32)
    cmask, _ = _chunk_masks(CHUNK, unit)
    heads = range(SSD_HEADS)
    hs = lambda h: slice(h * hd, (h + 1) * hd)
    gs = lambda h: slice((h // 2) * SSD_STATE, (h // 2 + 1) * SSD_STATE)
    cs = lambda j: slice(j * CHUNK, (j + 1) * CHUNK)
    us = lambda u: slice(u * unit, (u + 1) * unit)

    yh = {}
    for j in range(rows // CHUNK):
        cb = [_dot(cm[cs(j), gs(2 * g)], bm[cs(j), gs(2 * g)], _NT) for g in range(SSD_NGROUPS)]
        for h in heads:
            ccol = cum[cs(j), hs(h)]
            crow = _dot_exact_lhs(ones_row, ccol, _NT)
            decay = jnp.exp(jnp.where(cmask, ccol - crow, -jnp.inf))
            yh[j, h] = _dot(cb[h // 2] * decay, xdt[cs(j), hs(h)])
    ds = {(u, h): _dot(xw[us(u), hs(h)], bm[us(u), gs(h)], _TN) for u in range(units) for h in heads}

    before = {}
    for h in heads:
        if nb > 1:
            for u in range(units):
                before[u, h] = sout_ref[u, h]
                sout_ref[u, h] = before[u, h] * elast[u][:, hs(h)] + ds[u, h]
        else:
            s_cur = sout_ref[0, h]
            for u in range(units):
                before[u, h] = s_cur
                s_cur = s_cur * elast[u][:, hs(h)] + ds[u, h]
            sout_ref[0, h] = s_cur

    for j in range(rows // CHUNK):
        for h in heads:
            inter = [_dot(cm[us(u), gs(h)], before[u, h], _NT)
                     for u in range(j * per_chunk, (j + 1) * per_chunk)]
            inter = inter[0] if per_chunk == 1 else jnp.concatenate(inter, axis=0)
            yscr_ref[cs(j), hs(h)] = yh[j, h] + inter * ecum[cs(j), hs(h)]

    y = yscr_ref[...] + dsk_ref[...] * xs
    y = y * _silu(z)
    y = y * lax.rsqrt(_segsum(y * y, hd) * (1.0 / hd) + EPS) * ng_ref[...]
    y_ref[...] = y.reshape(nb, lt, GROUP_W)


def _ssd(p, states, prm, l, prev, nb, lt):
    assert CHUNK == SSD_HEADDIM
    assert (nb > 1 and (nb * lt) % CHUNK == 0 and CHUNK % lt == 0) or (nb == 1 and lt % CHUNK == 0)
    kern = functools.partial(_ssd_kernel, nb=nb, lt=lt)
    return _mixer_call(kern, "ssd", l, nb, lt, p, states, prm, prev,
                       [pltpu.VMEM((nb, HIST + lt, SSD_CONV_CH), F32),
                        pltpu.VMEM((nb * lt, GROUP_W), F32)])


def _unit_lower_inverse(a2, n, levels, dot):
    ri = lax.broadcasted_iota(jnp.int32, (n, n), 0)
    ci = lax.broadcasted_iota(jnp.int32, (n, n), 1)
    x = [jnp.where(ri == ci, 1.0, 0.0) - jnp.where((ri % 2 == 1) & (ci == ri - 1), a, 0.0) for a in a2]
    s = 2
    while s < levels:
        off = ((ri // s) % 2 == 1) & ((ci // s) == (ri // s) - 1)
        t = [dot(xi, jnp.where(off, a, 0.0)) for xi, a in zip(x, a2)]
        x = [xi - dot(ti, xi) for xi, ti in zip(x, t)]
        s *= 2
    return x


def _rwkv_chunk_multi(ops, sout_ref, yscr_ref, nb, lt):
    hd = RWKV_HEADDIM
    per_chunk = CHUNK // lt
    idx = [(c, h) for c in range(nb // per_chunk) for h in range(RWKV_HEADS)]
    n = range(len(idx))
    incl, strict = _chunk_masks(CHUNK, lt)
    mask2 = jnp.concatenate([strict, incl], axis=0)
    top = lambda m: m[0:CHUNK]
    bot = lambda m: m[CHUNK:2 * CHUNK]
    bl = lambda name: [ops[name][c * CHUNK:(c + 1) * CHUNK, h * hd:(h + 1) * hd] for c, h in idx]
    rt, kt, k2t, bt, k2g, bg, v = (bl(name) for name in ('rt', 'kt', 'k2t', 'bt', 'k2g', 'bg', 'v'))
    kr = [jnp.concatenate([kt[j], rt[j]], axis=0) for j in n]
    m1 = [jnp.where(mask2, _dot_a(kr[j], k2t[j], _NT), 0.0) for j in n]
    m2 = [jnp.where(mask2, _dot_a(kr[j], bt[j], _NT), 0.0) for j in n]
    x = _unit_lower_inverse([top(m) for m in m2], CHUNK, lt, _dot_inv)
    m1v = [_dot_app(m1[j], v[j]) for j in n]
    seq = lambda i: slice(i * lt, (i + 1) * lt)
    seqs = range(per_chunk)
    s_old = [[sout_ref[c * per_chunk + i, h] for i in seqs] for c, h in idx]
    pq = [[_dot_st(jnp.concatenate([kt[j][seq(i)], rt[j][seq(i)]], axis=0), s_old[j][i], _NT)
           for i in seqs] for j in n]
    p0 = [jnp.concatenate([pq[j][i][0:lt] for i in seqs], axis=0) for j in n]
    q0 = [jnp.concatenate([pq[j][i][lt:2 * lt] for i in seqs], axis=0) for j in n]
    u = [_dot_app(x[j], p0[j] + top(m1v[j])) for j in n]
    l2u = [_dot_app(bot(m2[j]), u[j]) for j in n]
    for j, (c, h) in enumerate(idx):
        yscr_ref[c * CHUNK:(c + 1) * CHUNK, h * hd:(h + 1) * hd] = q0[j] + bot(m1v[j]) - l2u[j]
        for i in seqs:
            sq = c * per_chunk + i
            sout_ref[sq, h] = (s_old[j][i] * ops['elast'][sq][:, h * hd:(h + 1) * hd]
                               + _dot_st(v[j][seq(i)], k2g[j][seq(i)], _TN)
                               - _dot_st(u[j][seq(i)], bg[j][seq(i)], _TN))


def _rwkv_chunks_seq(ops, sout_ref, yscr_ref, nc):
    hd = RWKV_HEADDIM
    idx = [(c, h) for c in range(nc) for h in range(RWKV_HEADS)]
    incl, strict = _chunk_masks(CHUNK, CHUNK)
    ri = lax.broadcasted_iota(jnp.int32, (hd, hd), 0)
    ci = lax.broadcasted_iota(jnp.int32, (hd, hd), 1)
    eye = jnp.where(ri == ci, 1.0, 0.0)
    bl = lambda name: [ops[name][c * CHUNK:(c + 1) * CHUNK, h * hd:(h + 1) * hd] for c, h in idx]
    rt, kt, k2t, bt, k2g, bg, v = (bl(n) for n in ('rt', 'kt', 'k2t', 'bt', 'k2g', 'bg', 'v'))
    n = range(len(idx))
    top = lambda m: m[0:CHUNK]
    bot = lambda m: m[CHUNK:2 * CHUNK]
    mask2 = jnp.concatenate([strict, incl], axis=0)
    kr = [jnp.concatenate([kt[j], rt[j]], axis=0) for j in n]
    m1 = [jnp.where(mask2, _dot_a(kr[j], k2t[j], _NT), 0.0) for j in n]
    m2 = [jnp.where(mask2, _dot_a(kr[j], bt[j], _NT), 0.0) for j in n]
    x = _unit_lower_inverse([top(m) for m in m2], CHUNK, CHUNK, _dot_inv)
    m1v = [_dot_app(m1[j], v[j]) for j in n]
    kp = [_dot_app(x[j], kt[j]) for j in n]
    ul = [_dot_app(x[j], top(m1v[j])) for j in n]
    rp = [rt[j] - _dot_app(bot(m2[j]), kp[j]) for j in n]
    yl = [bot(m1v[j]) - _dot_app(bot(m2[j]), ul[j]) for j in n]
    mc = [eye * ops['elast'][c][:, h * hd:(h + 1) * hd] - _dot_st(kp[j], bg[j], _TN)
          for j, (c, h) in enumerate(idx)]
    dc = [_dot_st(v[j], k2g[j], _TN) - _dot_st(ul[j], bg[j], _TN) for j in n]
    state = [sout_ref[0, h] for h in range(RWKV_HEADS)]
    for c in range(nc):
        js = [c * RWKV_HEADS + h for h in range(RWKV_HEADS)]
        ys = [_dot_st(rp[j], state[h], _NT) + yl[j] for h, j in enumerate(js)]
        state = [_dot_st(state[h], mc[j]) + dc[j] for h, j in enumerate(js)]
        for h in range(RWKV_HEADS):
            yscr_ref[c * CHUNK:(c + 1) * CHUNK, h * hd:(h + 1) * hd] = ys[h]
    for h in range(RWKV_HEADS):
        sout_ref[0, h] = state[h]


def _rwkv_kernel(p_ref, sh0_ref, s0_ref, mu_ref, wl_ref, w0_ref, a0_ref, kk_ref, ka_ref, rk_ref,
                 lnw_ref, lnb_ref, y_ref, shout_ref, sout_ref, ext_ref, yscr_ref, *, nb, lt):
    rows = nb * lt
    hd = RWKV_HEADDIM
    gw = GROUP_W

    @pl.when(pl.program_id(1) == 0)
    def _():
        ext_ref[:, HIST - 1:HIST, :] = sh0_ref[...]
        sout_ref[...] = s0_ref[...]

    p = p_ref[...]
    ext_ref[:, HIST:HIST + lt, :] = p
    prev = ext_ref[:, HIST - 1:HIST - 1 + lt, :]
    pm = (p + (prev - p) * mu_ref[...]).reshape(rows, RWKV_PROJ)
    last_p = p[:, lt - 1:lt, :]
    ext_ref[:, HIST - 1:HIST, :] = last_p
    shout_ref[...] = last_p

    r = pm[:, 0:gw]
    k = pm[:, gw:2 * gw]
    v = pm[:, 2 * gw:3 * gw]
    lo = pm[:, 3 * gw:]
    lane = lax.broadcasted_iota(jnp.int32, (rows, RWKV_LORA), 1)
    t = jnp.where(lane < 32, jnp.tanh(lo), jnp.where(lane < 64, lo, _sigmoid(lo)))
    lora = _dot(t, wl_ref[...])
    wlog = -_softplus(-(w0_ref[...] + lora[:, 0:gw])) - 0.5
    lw = -jnp.exp(wlog)
    a = _sigmoid(a0_ref[...] + lora[:, gw:2 * gw])
    gate = lora[:, 2 * gw:]
    kk = k * kk_ref[...]
    kk = kk / jnp.maximum(jnp.sqrt(_segsum(kk * kk, hd)), 1e-12)
    k2 = k * (1.0 + (a - 1.0) * ka_ref[...])
    beta = kk * a

    unit = lt if nb > 1 else CHUNK
    incl, _ = _chunk_masks(rows, unit)
    gcum = _dot_exact_lhs(jnp.where(incl, 1.0, 0.0), lw)
    glast, glast_b = _bcast_last(gcum, rows // unit, unit)
    eg = jnp.exp(gcum)
    einv = jnp.exp(-gcum)
    etail = jnp.exp(glast_b - gcum)
    ops = dict(rt=r * eg, kt=kk * jnp.exp(gcum - lw), k2t=k2 * einv, bt=beta * einv,
               k2g=k2 * etail, bg=beta * etail, v=v, elast=jnp.exp(glast))
    if nb > 1:
        _rwkv_chunk_multi(ops, sout_ref, yscr_ref, nb, lt)
    else:
        _rwkv_chunks_seq(ops, sout_ref, yscr_ref, lt // CHUNK)

    y = yscr_ref[...]
    mean = _segsum(y, hd) * (1.0 / hd)
    dlt = y - mean
    var = _segsum(dlt * dlt, hd) * (1.0 / hd)
    yn = dlt * lax.rsqrt(var + RWKV_LN_EPS) * lnw_ref[...] + lnb_ref[...]
    bonus = _segsum(r * k2 * rk_ref[...], hd) * v
    y_ref[...] = ((yn + bonus) * gate).reshape(nb, lt, gw)


def _rwkv(p, states, prm, l, prev, nb, lt):
    assert (nb > 1 and (nb * lt) % CHUNK == 0 and CHUNK % lt == 0) or (nb == 1 and lt % CHUNK == 0)
    kern = functools.partial(_rwkv_kernel, nb=nb, lt=lt)
    return _mixer_call(kern, "rwkv", l, nb, lt, p, states, prm, prev,
                       [pltpu.VMEM((nb, HIST + lt, RWKV_PROJ), F32),
                        pltpu.VMEM((nb * lt, GROUP_W), F32)])


def _gla_kernel(p_ref, s0_ref, wg2_ref, bg_ref, ng_ref, y_ref, sout_ref, yscr_ref, *, nb, lt, cs):
    rows = nb * lt
    dk, dv = GLA_DK, GLA_DV
    qk = GLA_QK
    units = rows // cs
    per_group = GLA_CS // cs

    @pl.when(pl.program_id(1) == 0)
    def _():
        sout_ref[...] = s0_ref[...]

    p = p_ref[...].reshape(rows, GLA_PW)
    q = p[:, 0:qk] * (dk ** -0.5)
    k = p[:, qk:2 * qk]
    v = p[:, 2 * qk:2 * qk + GROUP_W]
    dg = p[:, 2 * qk + GROUP_W:2 * qk + GROUP_W + 128]
    og = p[:, 2 * qk + GROUP_W + 128:]
    lg = -_softplus(-(_dot(dg, wg2_ref[...]) + bg_ref[...])) * (1.0 / GLA_TAU)
    incl, _ = _chunk_masks(rows, cs)
    b = _dot_exact_lhs(jnp.where(incl, 1.0, 0.0), lg)
    bl, bl_b = _bcast_last(b, units, cs)
    qd = q * jnp.exp(b)
    kd = k * jnp.exp(-b)
    kw = k * jnp.exp(bl_b - b)
    ebl = jnp.exp(bl)
    gmask, _ = _chunk_masks(GLA_CS, cs)

    heads = range(GLA_HEADS)
    ks = lambda h: slice(h * dk, (h + 1) * dk)
    vs = lambda h: slice(h * dv, (h + 1) * dv)
    gs = lambda g: slice(g * GLA_CS, (g + 1) * GLA_CS)
    us = lambda u: slice(u * cs, (u + 1) * cs)

    yh = {}
    for g in range(rows // GLA_CS):
        for h in heads:
            att = jnp.where(gmask, _dot(qd[gs(g), ks(h)], kd[gs(g), ks(h)], _NT), 0.0)
            yh[g, h] = _dot(att, v[gs(g), vs(h)])
    ds = {(u, h): _dot(kw[us(u), ks(h)], v[us(u), vs(h)], _TN) for u in range(units) for h in heads}
    ri = lax.broadcasted_iota(jnp.int32, (dk, dk), 0)
    ci = lax.broadcasted_iota(jnp.int32, (dk, dk), 1)
    ones_kv = jnp.ones((dk, dv), F32)
    ecol = {(u, h): _dot_exact_rhs(jnp.where(ri == ci, ebl[u][:, ks(h)], 0.0), ones_kv)
            for u in range(units) for h in heads}

    before = {}
    for h in heads:
        if nb > 1:
            for u in range(units):
                before[u, h] = sout_ref[u, h]
                sout_ref[u, h] = before[u, h] * ecol[u, h] + ds[u, h]
        else:
            s_cur = sout_ref[0, h]
            for u in range(units):
                before[u, h] = s_cur
                s_cur = s_cur * ecol[u, h] + ds[u, h]
            sout_ref[0, h] = s_cur

    for g in range(rows // GLA_CS):
        for h in heads:
            inter = [_dot(qd[us(u), ks(h)], before[u, h])
                     for u in range(g * per_group, (g + 1) * per_group)]
            inter = inter[0] if per_group == 1 else jnp.concatenate(inter, axis=0)
            yscr_ref[gs(g), vs(h)] = yh[g, h] + inter

    y = yscr_ref[...]
    y = y * lax.rsqrt(_segsum(y * y, dv) * (1.0 / dv) + EPS) * ng_ref[...]
    y_ref[...] = (y * _silu(og)).reshape(nb, lt, GROUP_W)


def _gla(p, states, prm, l, prev, nb, lt, cs):
    assert (nb == 1 and cs == GLA_CS and lt % cs == 0) or (
        lt == cs and (nb * cs) % GLA_CS == 0 and GLA_CS % cs == 0)
    kern = functools.partial(_gla_kernel, nb=nb, lt=lt, cs=cs)
    return _mixer_call(kern, "gla", l, nb, lt, p, states, prm, prev,
                       [pltpu.VMEM((nb * lt, GROUP_W), F32)])


def _s5_pitch(lt):
    return lt if (lt // 8) % 2 == 1 else lt + 8


def _s5_kernel(u_ref, hr0_ref, hi0_ref, wb_ref, wc_ref, ar_ref, ai_ref, d_ref, wglu_ref, bglu_ref,
               y_ref, hr_ref, hi_ref, h_ref, *, nb, lt):
    rows = nb * lt
    n = S5_CH

    @pl.when(pl.program_id(1) == 0)
    def _():
        hr_ref[...] = hr0_ref[...]
        hi_ref[...] = hi0_ref[...]

    u = u_ref[...].reshape(rows, GROUP_W)
    bu = _dot(u, wb_ref[...])
    nt = n // LANES
    lanes = lambda x, j: x[:, j * LANES:(j + 1) * LANES]
    pitch = _s5_pitch(lt)
    seq_rows = lambda i: slice(i * pitch, i * pitch + lt)
    for j in range(2 * nt):
        for i in range(nb):
            h_ref[j, seq_rows(i), :] = lanes(bu, j)[i * lt:(i + 1) * lt]
    ar = [jnp.broadcast_to(lanes(ar_ref[...], j), (nb, LANES)) for j in range(nt)]
    ai = [jnp.broadcast_to(lanes(ai_ref[...], j), (nb, LANES)) for j in range(nt)]
    hr = [lanes(hr_ref[...], j) for j in range(nt)]
    hi = [lanes(hi_ref[...], j) for j in range(nt)]
    for t in range(lt):
        at_t = pl.ds(t, nb, stride=pitch)
        for j in range(nt):
            hr[j], hi[j] = (ar[j] * hr[j] - ai[j] * hi[j] + h_ref[j, at_t, :],
                            ar[j] * hi[j] + ai[j] * hr[j] + h_ref[nt + j, at_t, :])
            h_ref[j, at_t, :] = hr[j]
            h_ref[nt + j, at_t, :] = hi[j]
    hr_ref[...] = jnp.concatenate(hr, axis=1)
    hi_ref[...] = jnp.concatenate(hi, axis=1)
    tile = lambda j: jnp.concatenate([h_ref[j, seq_rows(i), :] for i in range(nb)], axis=0)
    h_re = jnp.concatenate([tile(j) for j in range(nt)], axis=1)
    h_im = jnp.concatenate([tile(nt + j) for j in range(nt)], axis=1)
    y = _dot(h_re, wc_ref[0]) - _dot(h_im, wc_ref[1])
    y = y + d_ref[...] * u
    y = 0.5 * y * (1.0 + jnp.tanh(math.sqrt(2.0 / math.pi) * (y + 0.044715 * (y * y * y))))
    y = y * _sigmoid(_dot(y, wglu_ref[...]) + bglu_ref[...])
    y_ref[...] = y.reshape(nb, lt, GROUP_W)


def _s5(u, states, prm, l, prev, nb, lt):
    kern = functools.partial(_s5_kernel, nb=nb, lt=lt)
    return _mixer_call(kern, "s5", l, nb, lt, u, states, prm, prev,
                       [pltpu.VMEM((2 * S5_CH // LANES, nb * _s5_pitch(lt), LANES), F32)])


def _block_diag_in(b):
    dep, g, n, i = b.shape
    eye = jnp.eye(g, dtype=b.dtype)
    return jnp.einsum('lgni,gh->lgihn', b, eye).reshape(dep, g * i, g * n)


def _block_diag_out(c):
    dep, g, o, n = c.shape
    eye = jnp.eye(g, dtype=c.dtype)
    return jnp.einsum('lgon,gh->lgnho', c, eye).reshape(dep, g * n, g * o)


def _prep_params(W):
    dep = W['w_in'].shape[0]
    row = lambda name: W[name].reshape(dep, 1, -1).astype(F32)
    rep = lambda name, n: jnp.repeat(W[name].reshape(dep, -1), n, axis=1).reshape(dep, 1, -1).astype(F32)
    w_in = W['w_in'].astype(BF16)
    o_r = SSD_PROJ
    o_g = o_r + RWKV_PROJ
    o_s = o_g + GLA_PROJ
    w_ssd = jnp.concatenate([w_in[..., :GROUP_W + SSD_CONV_CH],
                             jnp.repeat(w_in[..., GROUP_W + SSD_CONV_CH:o_r], SSD_HEADDIM, axis=2)], axis=2)
    w_rwkv = w_in[..., o_r:o_g]
    gq = o_g + 2 * GLA_QK + GROUP_W
    w_gla = jnp.concatenate([w_in[..., o_g:gq], w_in[..., gq:gq + GLA_GATE_LORA],
                             jnp.zeros((dep, D_MODEL, 128 - GLA_GATE_LORA), BF16),
                             w_in[..., gq + GLA_GATE_LORA:o_s]], axis=2)
    w_s5 = w_in[..., o_s:]
    in_w = (w_ssd, w_rwkv, w_gla, w_s5)

    ssd = (W['ssd_conv_w'].astype(F32), row('ssd_conv_b'), rep('ssd_a_log', SSD_HEADDIM),
           rep('ssd_dt_bias', SSD_HEADDIM), rep('ssd_d', SSD_HEADDIM), row('ssd_norm_g'))

    z = lambda r, c: jnp.zeros((dep, r, c), F32)
    w_lora = jnp.concatenate([
        jnp.concatenate([W['rwkv_w2'], z(32, 2 * GROUP_W)], axis=2),
        jnp.concatenate([z(32, GROUP_W), W['rwkv_a2'], z(32, GROUP_W)], axis=2),
        jnp.concatenate([z(64, 2 * GROUP_W), W['rwkv_g2']], axis=2)], axis=1).astype(BF16)
    rwkv = (row('rwkv_mu'), w_lora, row('rwkv_w0'), row('rwkv_a0'), row('rwkv_k_k'),
            row('rwkv_k_a'), row('rwkv_r_k'), row('rwkv_lnx_w'), row('rwkv_lnx_b'))

    wg2 = jnp.concatenate([W['gla_wg2'], z(128 - GLA_GATE_LORA, GLA_QK)], axis=1).astype(BF16)
    gla = (wg2, row('gla_bg'), row('gla_norm_g'))

    lr = W['s5_lam_re'].astype(F32)
    li = W['s5_lam_im'].astype(F32)
    dts = jnp.exp(W['s5_log_step'].astype(F32))[..., None]
    mag = jnp.exp(lr * dts)
    ar = mag * jnp.cos(li * dts)
    ai = mag * jnp.sin(li * dts)
    den = lr * lr + li * li
    fr = ((ar - 1.0) * lr + ai * li) / den
    fi = (ai * lr - (ar - 1.0) * li) / den
    b_re, b_im = W['s5_b_re'], W['s5_b_im']
    bb_re = fr[..., None] * b_re - fi[..., None] * b_im
    bb_im = fr[..., None] * b_im + fi[..., None] * b_re
    wb = jnp.concatenate([_block_diag_in(bb_re), _block_diag_in(bb_im)], axis=2).astype(BF16)
    wc = jnp.stack([_block_diag_out(W['s5_c_re']), _block_diag_out(W['s5_c_im'])], axis=1).astype(BF16)
    s5 = (wb, wc, ar.reshape(dep, 1, -1), ai.reshape(dep, 1, -1), row('s5_d'),
          W['s5_w_glu'].astype(BF16), row('s5_b_glu'))

    g4 = lambda name: W[name].reshape(dep, 1, 1, D_MODEL).astype(F32)
    return dict(
        in_w=in_w, ssd=ssd, rwkv=rwkv, gla=gla, s5=s5,
        g_mix_pre=g4('g_mix_pre'), g_mix_post=g4('g_mix_post'),
        g_ffn_pre=g4('g_ffn_pre'), g_ffn_post=g4('g_ffn_post'),
        w_out=W['w_out'].reshape(dep, 4, GROUP_W, D_MODEL).astype(BF16),
        w1=W['mlp_w1'].astype(BF16), w2=W['mlp_w2'].astype(BF16))


def _run_group(x, mod, first, st, P, cfg):
    dep, bsz = st['ssd'].shape[:2]
    hist = jnp.pad(st['ssd_conv'], ((0, 0), (0, 0), (HIST - (SSD_CONV - 1), 0), (0, 0)))
    ssd_st = (hist, st['ssd'])
    rwkv_st = (st['rwkv_shift'].reshape(dep, bsz, 1, RWKV_PROJ), st['rwkv'])
    gla_st = (st['gla'],)
    s5_st = (st['s5_re'].reshape(dep, bsz, S5_CH), st['s5_im'].reshape(dep, bsz, S5_CH))
    new = dict(ssd=None, rwkv=None, gla=None, s5=None)
    for l in range(DEPTH):
        p_ssd, p_rwkv, p_gla, p_s5 = _in_proj(x, mod, first, P['g_mix_pre'], P['in_w'], l, *cfg['tok'])
        y_ssd, *new['ssd'] = _ssd(p_ssd, ssd_st, P['ssd'], l, new['ssd'], *cfg['ssd'])
        y_rwkv, *new['rwkv'] = _rwkv(p_rwkv, rwkv_st, P['rwkv'], l, new['rwkv'], *cfg['rwkv'])
        y_gla, *new['gla'] = _gla(p_gla, gla_st, P['gla'], l, new['gla'], *cfg['gla'])
        y_s5, *new['s5'] = _s5(p_s5, s5_st, P['s5'], l, new['s5'], *cfg['s5'])
        x = _out_mlp(x, (y_ssd, y_rwkv, y_gla, y_s5), mod, first, P['g_mix_post'], P['g_ffn_pre'],
                     P['g_ffn_post'], P['w_out'], P['w1'], P['w2'], l, *cfg['tok'])
    hist_new, ssd_new = new['ssd']
    shift_new, rwkv_new = new['rwkv']
    s5r_new, s5i_new = new['s5']
    return x, dict(
        ssd=ssd_new, ssd_conv=hist_new[:, :, HIST - (SSD_CONV - 1):, :],
        rwkv=rwkv_new, rwkv_shift=shift_new.reshape(dep, bsz, RWKV_PROJ), gla=new['gla'][0],
        s5_re=s5r_new.reshape(dep, bsz, S5_GROUPS, S5_STATE),
        s5_im=s5i_new.reshape(dep, bsz, S5_GROUPS, S5_STATE))


_PROMPT_CFG = dict(tok=(1, 512), ssd=(1, 256), rwkv=(1, 256), gla=(1, 256, GLA_CS), s5=(8, 128))
_SAMPLE_CFG = dict(tok=(64, 8), ssd=(32, 8), rwkv=(32, 8), gla=(16, 8, 8), s5=(64, 8))


def kernel(x_prompt, x_sample, c_prompt, c_sample, state_ssd, state_ssd_conv, state_rwkv, state_rwkv_shift, state_gla, state_s5_re, state_s5_im, w_ada, b_ada, g_mix_pre, g_mix_post, g_ffn_pre, g_ffn_post, w_in, w_out, ssd_conv_w, ssd_conv_b, ssd_a_log, ssd_dt_bias, ssd_d, ssd_norm_g, rwkv_mu, rwkv_w0, rwkv_w2, rwkv_a0, rwkv_a2, rwkv_g2, rwkv_k_k, rwkv_k_a, rwkv_r_k, rwkv_lnx_w, rwkv_lnx_b, gla_wg2, gla_bg, gla_norm_g, s5_lam_re, s5_lam_im, s5_log_step, s5_b_re, s5_b_im, s5_c_re, s5_c_im, s5_d, s5_w_glu, s5_b_glu, mlp_w1, mlp_w2):
    W = dict(w_in=w_in, w_out=w_out, g_mix_pre=g_mix_pre, g_mix_post=g_mix_post,
             g_ffn_pre=g_ffn_pre, g_ffn_post=g_ffn_post,
             ssd_conv_w=ssd_conv_w, ssd_conv_b=ssd_conv_b, ssd_a_log=ssd_a_log,
             ssd_dt_bias=ssd_dt_bias, ssd_d=ssd_d, ssd_norm_g=ssd_norm_g,
             rwkv_mu=rwkv_mu, rwkv_w0=rwkv_w0, rwkv_w2=rwkv_w2, rwkv_a0=rwkv_a0, rwkv_a2=rwkv_a2,
             rwkv_g2=rwkv_g2, rwkv_k_k=rwkv_k_k, rwkv_k_a=rwkv_k_a, rwkv_r_k=rwkv_r_k,
             rwkv_lnx_w=rwkv_lnx_w, rwkv_lnx_b=rwkv_lnx_b,
             gla_wg2=gla_wg2, gla_bg=gla_bg, gla_norm_g=gla_norm_g,
             s5_lam_re=s5_lam_re, s5_lam_im=s5_lam_im, s5_log_step=s5_log_step,
             s5_b_re=s5_b_re, s5_b_im=s5_b_im, s5_c_re=s5_c_re, s5_c_im=s5_c_im,
             s5_d=s5_d, s5_w_glu=s5_w_glu, s5_b_glu=s5_b_glu, mlp_w1=mlp_w1, mlp_w2=mlp_w2)
    P = _prep_params(W)

    nbp, nbs = x_prompt.shape[0], x_sample.shape[0]
    mod = _ada(jnp.concatenate([c_sample, c_prompt], axis=0), w_ada, b_ada)
    mod = mod.reshape(DEPTH, nbs + nbp, N_MOD, D_MODEL)

    st_sample = dict(ssd=state_ssd, ssd_conv=state_ssd_conv, rwkv=state_rwkv,
                     rwkv_shift=state_rwkv_shift, gla=state_gla, s5_re=state_s5_re, s5_im=state_s5_im)
    st_prompt = {n: jnp.zeros((DEPTH, nbp) + v.shape[2:], v.dtype) for n, v in st_sample.items()}

    y_prompt, sp = _run_group(x_prompt, mod, nbs, st_prompt, P, _PROMPT_CFG)
    y_sample, ss = _run_group(x_sample, mod, 0, st_sample, P, _SAMPLE_CFG)
    return (y_prompt, y_sample,
            sp['ssd'], ss['ssd'], sp['ssd_conv'], ss['ssd_conv'],
            sp['rwkv'], ss['rwkv'], sp['rwkv_shift'], ss['rwkv_shift'],
            sp['gla'], ss['gla'], sp['s5_re'], ss['s5_re'], sp['s5_im'], ss['s5_im'])
```

```python
import functools
import math

import jax
import jax.numpy as jnp
from jax import lax
from jax.experimental import pallas as pl
from jax.experimental.pallas import tpu as pltpu

F32 = jnp.float32
BF16 = jnp.bfloat16

D_MODEL = 1024
DEPTH = 2
GROUP_W = 256
D_FF = 4096
N_MOD = 6
EPS = 1e-6

SSD_HEADS = 4
SSD_HEADDIM = 64
SSD_STATE = 64
SSD_NGROUPS = 2
SSD_BC = SSD_NGROUPS * SSD_STATE
SSD_CONV = 4
SSD_CONV_CH = GROUP_W + 2 * SSD_BC
SSD_PROJ = GROUP_W + SSD_CONV_CH + SSD_HEADS
SSD_PW = GROUP_W + SSD_CONV_CH + GROUP_W

RWKV_HEADS = 4
RWKV_HEADDIM = 64
RWKV_LORA = 128
RWKV_PROJ = 3 * GROUP_W + RWKV_LORA
RWKV_LN_EPS = 64e-5

GLA_HEADS = 4
GLA_DK = 32
GLA_DV = 64
GLA_GATE_LORA = 16
GLA_TAU = 16.0
GLA_QK = GLA_HEADS * GLA_DK
GLA_PROJ = 2 * GLA_QK + GROUP_W + GLA_GATE_LORA + GROUP_W
GLA_PW = 2 * GLA_QK + GROUP_W + 128 + GROUP_W

S5_GROUP = 16
S5_GROUPS = 16
S5_STATE = 64
S5_CH = S5_GROUPS * S5_STATE
S5_PROJ = GROUP_W

LANES = 128
HIST = 8
CHUNK = 64
GLA_CS = 32
VMEM_LIMIT = 56 * 1024 * 1024


def _dot(a, b, dims=(((1,), (0,)), ((), ()))):
    return lax.dot_general(a.astype(BF16), b.astype(BF16), dims, preferred_element_type=F32)


_NT = (((1,), (1,)), ((), ()))
_TN = (((0,), (0,)), ((), ()))
_NN = (((1,), (0,)), ((), ()))


def _split2(a):
    hi = a.astype(BF16)
    lo = (a - hi.astype(F32)).astype(BF16)
    return hi, lo


def _dot3(a, b, dims=_NN):
    ah, al = _split2(a)
    bh, bl = _split2(b)
    d = lambda x, y: lax.dot_general(x, y, dims, preferred_element_type=F32)
    return d(ah, bh) + (d(ah, bl) + d(al, bh))


_dot_a = _dot
_dot_inv = _dot
_dot_app = _dot
_dot_st = _dot


def _dot_exact_lhs(m, x, dims=_NN):
    mb = m.astype(BF16)
    x1 = x.astype(BF16)
    r1 = x - x1.astype(F32)
    x2 = r1.astype(BF16)
    x3 = (r1 - x2.astype(F32)).astype(BF16)
    d = lambda y: lax.dot_general(mb, y, dims, preferred_element_type=F32)
    return d(x1) + (d(x2) + d(x3))


def _dot_exact_rhs(x, m):
    mb = m.astype(BF16)
    x1 = x.astype(BF16)
    r1 = x - x1.astype(F32)
    x2 = r1.astype(BF16)
    x3 = (r1 - x2.astype(F32)).astype(BF16)
    d = lambda y: lax.dot_general(y, mb, _NN, preferred_element_type=F32)
    return d(x1) + (d(x2) + d(x3))


def _segsum(x, width):
    n = x.shape[-1]
    r = lax.broadcasted_iota(jnp.int32, (n, n), 0) // width
    c = lax.broadcasted_iota(jnp.int32, (n, n), 1) // width
    j = jnp.where(r == c, 1.0, 0.0).astype(BF16)
    hi, lo = _split2(x)
    d = lambda y: lax.dot_general(y, j, _NN, preferred_element_type=F32)
    return d(hi) + d(lo)


def _sigmoid(x):
    return 1.0 / (1.0 + jnp.exp(-x))


def _silu(x):
    return x * _sigmoid(x)


def _softplus(x):
    return jnp.maximum(x, 0.0) + jnp.log1p(jnp.exp(-jnp.abs(x)))


def _rms(x, g):
    return x * lax.rsqrt(jnp.mean(x * x, axis=-1, keepdims=True) + EPS) * g


def _chunk_masks(rows, lt):
    r = lax.broadcasted_iota(jnp.int32, (rows, rows), 0)
    c = lax.broadcasted_iota(jnp.int32, (rows, rows), 1)
    same = (r // lt) == (c // lt)
    return same & (c <= r), same & (c < r)


def _lsel(a, l, **kw):
    shape = a.shape[1:]
    return pl.BlockSpec((None,) + shape, lambda i, c: (l,) + (0,) * len(shape), **kw)


def _lseq(a, l, nb):
    shape = a.shape[2:]
    return pl.BlockSpec((None, nb) + shape, lambda i, c: (l, i) + (0,) * len(shape))


def _mixer_call(kern, name, l, nb, lt, p, states, params, prev, scratch):
    bsz, seq, width = p.shape
    n_in = 1 + len(states) + len(params)
    alias = tuple(prev) if prev is not None else ()

    def body(*refs):
        return kern(*refs[:n_in], *refs[n_in + len(alias):])

    tok = lambda w: pl.BlockSpec((nb, lt, w), lambda i, c: (i, c, 0))
    return pl.pallas_call(
        body,
        out_shape=(jax.ShapeDtypeStruct((bsz, seq, GROUP_W), F32),)
        + tuple(jax.ShapeDtypeStruct(s.shape, F32) for s in states),
        grid=(bsz // nb, seq // lt),
        in_specs=[tok(width)] + [_lseq(s, l, nb) for s in states] + [_lsel(a, l) for a in params]
        + [pl.BlockSpec(memory_space=pl.ANY)] * len(alias),
        out_specs=(tok(GROUP_W),) + tuple(_lseq(s, l, nb) for s in states),
        scratch_shapes=scratch,
        input_output_aliases={n_in + k: 1 + k for k in range(len(alias))},
        compiler_params=pltpu.CompilerParams(
            dimension_semantics=("parallel", "arbitrary"), vmem_limit_bytes=VMEM_LIMIT),
        name=name,
    )(p, *states, *params, *alias)


def _bcast_last(x, nb, lt):
    c = x.shape[-1]
    last = x.reshape(nb, lt, c)[:, lt - 1:lt, :]
    return last, jnp.broadcast_to(last, (nb, lt, c)).reshape(nb * lt, c)


def _ada_kernel(c_ref, w_ref, b_ref, o_ref):
    c = c_ref[...]
    o_ref[0] = _dot(_silu(c), w_ref[0]) + b_ref[0]


def _ada(c_all, w_ada, b_ada):
    rows = c_all.shape[0]
    n = N_MOD * D_MODEL
    tn = 1536
    return pl.pallas_call(
        _ada_kernel,
        out_shape=jax.ShapeDtypeStruct((DEPTH, rows, n), F32),
        grid=(DEPTH, n // tn),
        in_specs=[
            pl.BlockSpec((rows, D_MODEL), lambda l, j: (0, 0)),
            pl.BlockSpec((1, D_MODEL, tn), lambda l, j: (l, 0, j)),
            pl.BlockSpec((1, 1, tn), lambda l, j: (l, 0, j)),
        ],
        out_specs=pl.BlockSpec((1, rows, tn), lambda l, j: (l, 0, j)),
        compiler_params=pltpu.CompilerParams(
            dimension_semantics=("parallel", "parallel"), vmem_limit_bytes=VMEM_LIMIT),
        name="ada",
    )(c_all, w_ada, b_ada.reshape(DEPTH, 1, n))


def _in_kernel(x_ref, mod_ref, g_ref, wssd_ref, wrwkv_ref, wgla_ref, ws5_ref,
               ossd_ref, orwkv_ref, ogla_ref, os5_ref):
    nb, lt, d = x_ref.shape
    m = mod_ref[...]
    h = _rms(x_ref[...], g_ref[...]) * (1.0 + m[:, 1:2, :]) + m[:, 0:1, :]
    hb = h.reshape(nb * lt, d).astype(BF16)
    for w_ref, o_ref in ((wssd_ref, ossd_ref), (wrwkv_ref, orwkv_ref),
                         (wgla_ref, ogla_ref), (ws5_ref, os5_ref)):
        o_ref[...] = jnp.dot(hb, w_ref[...], preferred_element_type=F32).reshape(o_ref.shape)


def _mod_spec(mod, l, first, nb):
    return pl.BlockSpec((None, nb, N_MOD, mod.shape[-1]), lambda i, j: (l, first // nb + i, 0, 0))


def _in_proj(x, mod, first, g, wts, l, nb, lt):
    bsz, seq, d = x.shape
    widths = (SSD_PW, RWKV_PROJ, GLA_PW, S5_PROJ)
    return pl.pallas_call(
        _in_kernel,
        out_shape=tuple(jax.ShapeDtypeStruct((bsz, seq, w), F32) for w in widths),
        grid=(bsz // nb, seq // lt),
        in_specs=[pl.BlockSpec((nb, lt, d), lambda i, j: (i, j, 0)), _mod_spec(mod, l, first, nb),
                  _lsel(g, l)] + [_lsel(w, l) for w in wts],
        out_specs=tuple(pl.BlockSpec((nb, lt, w), lambda i, j: (i, j, 0)) for w in widths),
        compiler_params=pltpu.CompilerParams(
            dimension_semantics=("parallel", "parallel"), vmem_limit_bytes=VMEM_LIMIT),
        name="in_proj",
    )(x, mod, g, *wts)


def _out_mlp_kernel(x_ref, y0_ref, y1_ref, y2_ref, y3_ref, mod_ref, gpost_ref, gpre_ref, gfpost_ref,
                    wout_ref, w1_ref, w2_ref, o_ref):
    nb, lt, d = x_ref.shape
    rows = nb * lt
    m = mod_ref[...]
    mix = None
    for i, y_ref in enumerate((y0_ref, y1_ref, y2_ref, y3_ref)):
        t = jnp.dot(y_ref[...].reshape(rows, GROUP_W).astype(BF16), wout_ref[i],
                    preferred_element_type=F32)
        mix = t if mix is None else mix + t
    x1 = x_ref[...] + m[:, 2:3, :] * _rms(mix, gpost_ref[0]).reshape(nb, lt, d)
    h = _rms(x1, gpre_ref[...]) * (1.0 + m[:, 4:5, :]) + m[:, 3:4, :]
    hb = h.reshape(rows, d).astype(BF16)
    acc = None
    step = 1024
    for j in range(D_FF // step):
        a = jnp.dot(hb, w1_ref[:, j * step:(j + 1) * step], preferred_element_type=F32)
        a = jnp.square(jnp.maximum(a, 0.0)).astype(BF16)
        t = jnp.dot(a, w2_ref[j * step:(j + 1) * step, :], preferred_element_type=F32)
        acc = t if acc is None else acc + t
    o_ref[...] = x1 + m[:, 5:6, :] * _rms(acc, gfpost_ref[0]).reshape(nb, lt, d)


def _out_mlp(x, ys, mod, first, gpost, gpre, gfpost, wout, w1, w2, l, nb, lt):
    bsz, seq, d = x.shape
    tok = lambda i, j: (i, j, 0)
    one = pl.Buffered(1)
    return pl.pallas_call(
        _out_mlp_kernel,
        out_shape=jax.ShapeDtypeStruct((bsz, seq, d), F32),
        grid=(bsz // nb, seq // lt),
        in_specs=[pl.BlockSpec((nb, lt, d), tok)]
        + [pl.BlockSpec((nb, lt, GROUP_W), tok)] * 4
        + [_mod_spec(mod, l, first, nb)]
        + [_lsel(g, l) for g in (gpost, gpre, gfpost)]
        + [_lsel(w, l, pipeline_mode=one) for w in (wout, w1, w2)],
        out_specs=pl.BlockSpec((nb, lt, d), tok),
        compiler_params=pltpu.CompilerParams(
            dimension_semantics=("parallel", "parallel"), vmem_limit_bytes=VMEM_LIMIT),
        name="out_mlp",
    )(x, *ys, mod, gpost, gpre, gfpost, wout, w1, w2)


def _ssd_kernel(p_ref, hist_ref, s0_ref, cw_ref, cb_ref, alog_ref, dtb_ref, dsk_ref, ng_ref,
                y_ref, hout_ref, sout_ref, ext_ref, yscr_ref, *, nb, lt):
    rows = nb * lt
    hd = SSD_HEADDIM

    @pl.when(pl.program_id(1) == 0)
    def _():
        ext_ref[:, 0:HIST, :] = hist_ref[...]
        sout_ref[...] = s0_ref[...]

    p = p_ref[...]
    ext_ref[:, HIST:HIST + lt, :] = p[:, :, GROUP_W:GROUP_W + SSD_CONV_CH]
    conv = cb_ref[...]
    for j in range(SSD_CONV):
        o = HIST - (SSD_CONV - 1) + j
        conv = conv + ext_ref[:, o:o + lt, :] * cw_ref[j:j + 1, :]
    tail = ext_ref[:, lt:lt + HIST, :]
    ext_ref[:, 0:HIST, :] = tail
    hout_ref[...] = tail

    xbc = _silu(conv).reshape(rows, SSD_CONV_CH)
    xs = xbc[:, 0:GROUP_W]
    bm = xbc[:, GROUP_W:GROUP_W + SSD_BC]
    cm = xbc[:, GROUP_W + SSD_BC:]
    z = p[:, :, 0:GROUP_W].reshape(rows, GROUP_W)
    dt = _softplus(p[:, :, GROUP_W + SSD_CONV_CH:].reshape(rows, GROUP_W) + dtb_ref[...])
    a = dt * (-jnp.exp(alog_ref[...]))
    unit = lt if nb > 1 else CHUNK
    units = rows // unit
    per_chunk = CHUNK // unit
    incl, _ = _chunk_masks(rows, unit)
    cum = _dot_exact_lhs(jnp.where(incl, 1.0, 0.0), a)
    last, last_b = _bcast_last(cum, units, unit)
    xdt = xs * dt
    xw = xdt * jnp.exp(last_b - cum)
    ecum = jnp.exp(cum)
    elast = jnp.exp(last)
    ones_row = jnp.full((CHUNK, hd), 1.0 / hd, F32)
    cmask, _ = _chunk_masks(CHUNK, unit)
    heads = range(SSD_HEADS)
    hs = lambda h: slice(h * hd, (h + 1) * hd)
    gs = lambda h: slice((h // 2) * SSD_STATE, (h // 2 + 1) * SSD_STATE)
    cs = lambda j: slice(j * CHUNK, (j + 1) * CHUNK)
    us = lambda u: slice(u * unit, (u + 1) * unit)

    yh = {}
    for j in range(rows // CHUNK):
        cb = [_dot(cm[cs(j), gs(2 * g)], bm[cs(j), gs(2 * g)], _NT) for g in range(SSD_NGROUPS)]
        for h in heads:
            ccol = cum[cs(j), hs(h)]
            crow = _dot_exact_lhs(ones_row, ccol, _NT)
            decay = jnp.exp(jnp.where(cmask, ccol - crow, -jnp.inf))
            yh[j, h] = _dot(cb[h // 2] * decay, xdt[cs(j), hs(h)])
    ds = {(u, h): _dot(xw[us(u), hs(h)], bm[us(u), gs(h)], _TN) for u in range(units) for h in heads}

    before = {}
    for h in heads:
        if nb > 1:
            for u in range(units):
                before[u, h] = sout_ref[u, h]
                sout_ref[u, h] = before[u, h] * elast[u][:, hs(h)] + ds[u, h]
        else:
            s_cur = sout_ref[0, h]
            for u in range(units):
                before[u, h] = s_cur
                s_cur = s_cur * elast[u][:, hs(h)] + ds[u, h]
            sout_ref[0, h] = s_cur

    for j in range(rows // CHUNK):
        for h in heads:
            inter = [_dot(cm[us(u), gs(h)], before[u, h], _NT)
                     for u in range(j * per_chunk, (j + 1) * per_chunk)]
            inter = inter[0] if per_chunk == 1 else jnp.concatenate(inter, axis=0)
            yscr_ref[cs(j), hs(h)] = yh[j, h] + inter * ecum[cs(j), hs(h)]

    y = yscr_ref[...] + dsk_ref[...] * xs
    y = y * _silu(z)
    y = y * lax.rsqrt(_segsum(y * y, hd) * (1.0 / hd) + EPS) * ng_ref[...]
    y_ref[...] = y.reshape(nb, lt, GROUP_W)


def _ssd(p, states, prm, l, prev, nb, lt):
    assert CHUNK == SSD_HEADDIM
    assert (nb > 1 and (nb * lt) % CHUNK == 0 and CHUNK % lt == 0) or (nb == 1 and lt % CHUNK == 0)
    kern = functools.partial(_ssd_kernel, nb=nb, lt=lt)
    return _mixer_call(kern, "ssd", l, nb, lt, p, states, prm, prev,
                       [pltpu.VMEM((nb, HIST + lt, SSD_CONV_CH), F32),
                        pltpu.VMEM((nb * lt, GROUP_W), F32)])


def _unit_lower_inverse(a2, n, levels, dot):
    ri = lax.broadcasted_iota(jnp.int32, (n, n), 0)
    ci = lax.broadcasted_iota(jnp.int32, (n, n), 1)
    x = [jnp.where(ri == ci, 1.0, 0.0) - jnp.where((ri % 2 == 1) & (ci == ri - 1), a, 0.0) for a in a2]
    s = 2
    while s < levels:
        off = ((ri // s) % 2 == 1) & ((ci // s) == (ri // s) - 1)
        t = [dot(xi, jnp.where(off, a, 0.0)) for xi, a in zip(x, a2)]
        x = [xi - dot(ti, xi) for xi, ti in zip(x, t)]
        s *= 2
    return x


def _rwkv_chunk_multi(ops, sout_ref, yscr_ref, nb, lt):
    hd = RWKV_HEADDIM
    per_chunk = CHUNK // lt
    idx = [(c, h) for c in range(nb // per_chunk) for h in range(RWKV_HEADS)]
    n = range(len(idx))
    incl, strict = _chunk_masks(CHUNK, lt)
    mask2 = jnp.concatenate([strict, incl], axis=0)
    top = lambda m: m[0:CHUNK]
    bot = lambda m: m[CHUNK:2 * CHUNK]
    bl = lambda name: [ops[name][c * CHUNK:(c + 1) * CHUNK, h * hd:(h + 1) * hd] for c, h in idx]
    rt, kt, k2t, bt, k2g, bg, v = (bl(name) for name in ('rt', 'kt', 'k2t', 'bt', 'k2g', 'bg', 'v'))
    kr = [jnp.concatenate([kt[j], rt[j]], axis=0) for j in n]
    m1 = [jnp.where(mask2, _dot_a(kr[j], k2t[j], _NT), 0.0) for j in n]
    m2 = [jnp.where(mask2, _dot_a(kr[j], bt[j], _NT), 0.0) for j in n]
    x = _unit_lower_inverse([top(m) for m in m2], CHUNK, lt, _dot_inv)
    m1v = [_dot_app(m1[j], v[j]) for j in n]
    seq = lambda i: slice(i * lt, (i + 1) * lt)
    seqs = range(per_chunk)
    s_old = [[sout_ref[c * per_chunk + i, h] for i in seqs] for c, h in idx]
    pq = [[_dot_st(jnp.concatenate([kt[j][seq(i)], rt[j][seq(i)]], axis=0), s_old[j][i], _NT)
           for i in seqs] for j in n]
    p0 = [jnp.concatenate([pq[j][i][0:lt] for i in seqs], axis=0) for j in n]
    q0 = [jnp.concatenate([pq[j][i][lt:2 * lt] for i in seqs], axis=0) for j in n]
    u = [_dot_app(x[j], p0[j] + top(m1v[j])) for j in n]
    l2u = [_dot_app(bot(m2[j]), u[j]) for j in n]
    for j, (c, h) in enumerate(idx):
        yscr_ref[c * CHUNK:(c + 1) * CHUNK, h * hd:(h + 1) * hd] = q0[j] + bot(m1v[j]) - l2u[j]
        for i in seqs:
            sq = c * per_chunk + i
            sout_ref[sq, h] = (s_old[j][i] * ops['elast'][sq][:, h * hd:(h + 1) * hd]
                               + _dot_st(v[j][seq(i)], k2g[j][seq(i)], _TN)
                               - _dot_st(u[j][seq(i)], bg[j][seq(i)], _TN))


def _rwkv_chunks_seq(ops, sout_ref, yscr_ref, nc):
    hd = RWKV_HEADDIM
    idx = [(c, h) for c in range(nc) for h in range(RWKV_HEADS)]
    incl, strict = _chunk_masks(CHUNK, CHUNK)
    ri = lax.broadcasted_iota(jnp.int32, (hd, hd), 0)
    ci = lax.broadcasted_iota(jnp.int32, (hd, hd), 1)
    eye = jnp.where(ri == ci, 1.0, 0.0)
    bl = lambda name: [ops[name][c * CHUNK:(c + 1) * CHUNK, h * hd:(h + 1) * hd] for c, h in idx]
    rt, kt, k2t, bt, k2g, bg, v = (bl(n) for n in ('rt', 'kt', 'k2t', 'bt', 'k2g', 'bg', 'v'))
    n = range(len(idx))
    top = lambda m: m[0:CHUNK]
    bot = lambda m: m[CHUNK:2 * CHUNK]
    mask2 = jnp.concatenate([strict, incl], axis=0)
    kr = [jnp.concatenate([kt[j], rt[j]], axis=0) for j in n]
    m1 = [jnp.where(mask2, _dot_a(kr[j], k2t[j], _NT), 0.0) for j in n]
    m2 = [jnp.where(mask2, _dot_a(kr[j], bt[j], _NT), 0.0) for j in n]
    x = _unit_lower_inverse([top(m) for m in m2], CHUNK, CHUNK, _dot_inv)
    m1v = [_dot_app(m1[j], v[j]) for j in n]
    kp = [_dot_app(x[j], kt[j]) for j in n]
    ul = [_dot_app(x[j], top(m1v[j])) for j in n]
    rp = [rt[j] - _dot_app(bot(m2[j]), kp[j]) for j in n]
    yl = [bot(m1v[j]) - _dot_app(bot(m2[j]), ul[j]) for j in n]
    mc = [eye * ops['elast'][c][:, h * hd:(h + 1) * hd] - _dot_st(kp[j], bg[j], _TN)
          for j, (c, h) in enumerate(idx)]
    dc = [_dot_st(v[j], k2g[j], _TN) - _dot_st(ul[j], bg[j], _TN) for j in n]
    state = [sout_ref[0, h] for h in range(RWKV_HEADS)]
    for c in range(nc):
        js = [c * RWKV_HEADS + h for h in range(RWKV_HEADS)]
        ys = [_dot_st(rp[j], state[h], _NT) + yl[j] for h, j in enumerate(js)]
        state = [_dot_st(state[h], mc[j]) + dc[j] for h, j in enumerate(js)]
        for h in range(RWKV_HEADS):
            yscr_ref[c * CHUNK:(c + 1) * CHUNK, h * hd:(h + 1) * hd] = ys[h]
    for h in range(RWKV_HEADS):
        sout_ref[0, h] = state[h]


def _rwkv_kernel(p_ref, sh0_ref, s0_ref, mu_ref, wl_ref, w0_ref, a0_ref, kk_ref, ka_ref, rk_ref,
                 lnw_ref, lnb_ref, y_ref, shout_ref, sout_ref, ext_ref, yscr_ref, *, nb, lt):
    rows = nb * lt
    hd = RWKV_HEADDIM
    gw = GROUP_W

    @pl.when(pl.program_id(1) == 0)
    def _():
        ext_ref[:, HIST - 1:HIST, :] = sh0_ref[...]
        sout_ref[...] = s0_ref[...]

    p = p_ref[...]
    ext_ref[:, HIST:HIST + lt, :] = p
    prev = ext_ref[:, HIST - 1:HIST - 1 + lt, :]
    pm = (p + (prev - p) * mu_ref[...]).reshape(rows, RWKV_PROJ)
    last_p = p[:, lt - 1:lt, :]
    ext_ref[:, HIST - 1:HIST, :] = last_p
    shout_ref[...] = last_p

    r = pm[:, 0:gw]
    k = pm[:, gw:2 * gw]
    v = pm[:, 2 * gw:3 * gw]
    lo = pm[:, 3 * gw:]
    lane = lax.broadcasted_iota(jnp.int32, (rows, RWKV_LORA), 1)
    t = jnp.where(lane < 32, jnp.tanh(lo), jnp.where(lane < 64, lo, _sigmoid(lo)))
    lora = _dot(t, wl_ref[...])
    wlog = -_softplus(-(w0_ref[...] + lora[:, 0:gw])) - 0.5
    lw = -jnp.exp(wlog)
    a = _sigmoid(a0_ref[...] + lora[:, gw:2 * gw])
    gate = lora[:, 2 * gw:]
    kk = k * kk_ref[...]
    kk = kk / jnp.maximum(jnp.sqrt(_segsum(kk * kk, hd)), 1e-12)
    k2 = k * (1.0 + (a - 1.0) * ka_ref[...])
    beta = kk * a

    unit = lt if nb > 1 else CHUNK
    incl, _ = _chunk_masks(rows, unit)
    gcum = _dot_exact_lhs(jnp.where(incl, 1.0, 0.0), lw)
    glast, glast_b = _bcast_last(gcum, rows // unit, unit)
    eg = jnp.exp(gcum)
    einv = jnp.exp(-gcum)
    etail = jnp.exp(glast_b - gcum)
    ops = dict(rt=r * eg, kt=kk * jnp.exp(gcum - lw), k2t=k2 * einv, bt=beta * einv,
               k2g=k2 * etail, bg=beta * etail, v=v, elast=jnp.exp(glast))
    if nb > 1:
        _rwkv_chunk_multi(ops, sout_ref, yscr_ref, nb, lt)
    else:
        _rwkv_chunks_seq(ops, sout_ref, yscr_ref, lt // CHUNK)

    y = yscr_ref[...]
    mean = _segsum(y, hd) * (1.0 / hd)
    dlt = y - mean
    var = _segsum(dlt * dlt, hd) * (1.0 / hd)
    yn = dlt * lax.rsqrt(var + RWKV_LN_EPS) * lnw_ref[...] + lnb_ref[...]
    bonus = _segsum(r * k2 * rk_ref[...], hd) * v
    y_ref[...] = ((yn + bonus) * gate).reshape(nb, lt, gw)


def _rwkv(p, states, prm, l, prev, nb, lt):
    assert (nb > 1 and (nb * lt) % CHUNK == 0 and CHUNK % lt == 0) or (nb == 1 and lt % CHUNK == 0)
    kern = functools.partial(_rwkv_kernel, nb=nb, lt=lt)
    return _mixer_call(kern, "rwkv", l, nb, lt, p, states, prm, prev,
                       [pltpu.VMEM((nb, HIST + lt, RWKV_PROJ), F32),
                        pltpu.VMEM((nb * lt, GROUP_W), F32)])


def _gla_kernel(p_ref, s0_ref, wg2_ref, bg_ref, ng_ref, y_ref, sout_ref, yscr_ref, *, nb, lt, cs):
    rows = nb * lt
    dk, dv = GLA_DK, GLA_DV
    qk = GLA_QK
    units = rows // cs
    per_group = GLA_CS // cs

    @pl.when(pl.program_id(1) == 0)
    def _():
        sout_ref[...] = s0_ref[...]

    p = p_ref[...].reshape(rows, GLA_PW)
    q = p[:, 0:qk] * (dk ** -0.5)
    k = p[:, qk:2 * qk]
    v = p[:, 2 * qk:2 * qk + GROUP_W]
    dg = p[:, 2 * qk + GROUP_W:2 * qk + GROUP_W + 128]
    og = p[:, 2 * qk + GROUP_W + 128:]
    lg = -_softplus(-(_dot(dg, wg2_ref[...]) + bg_ref[...])) * (1.0 / GLA_TAU)
    incl, _ = _chunk_masks(rows, cs)
    b = _dot_exact_lhs(jnp.where(incl, 1.0, 0.0), lg)
    bl, bl_b = _bcast_last(b, units, cs)
    qd = q * jnp.exp(b)
    kd = k * jnp.exp(-b)
    kw = k * jnp.exp(bl_b - b)
    ebl = jnp.exp(bl)
    gmask, _ = _chunk_masks(GLA_CS, cs)

    heads = range(GLA_HEADS)
    ks = lambda h: slice(h * dk, (h + 1) * dk)
    vs = lambda h: slice(h * dv, (h + 1) * dv)
    gs = lambda g: slice(g * GLA_CS, (g + 1) * GLA_CS)
    us = lambda u: slice(u * cs, (u + 1) * cs)

    yh = {}
    for g in range(rows // GLA_CS):
        for h in heads:
            att = jnp.where(gmask, _dot(qd[gs(g), ks(h)], kd[gs(g), ks(h)], _NT), 0.0)
            yh[g, h] = _dot(att, v[gs(g), vs(h)])
    ds = {(u, h): _dot(kw[us(u), ks(h)], v[us(u), vs(h)], _TN) for u in range(units) for h in heads}
    ri = lax.broadcasted_iota(jnp.int32, (dk, dk), 0)
    ci = lax.broadcasted_iota(jnp.int32, (dk, dk), 1)
    ones_kv = jnp.ones((dk, dv), F32)
    ecol = {(u, h): _dot_exact_rhs(jnp.where(ri == ci, ebl[u][:, ks(h)], 0.0), ones_kv)
            for u in range(units) for h in heads}

    before = {}
    for h in heads:
        if nb > 1:
            for u in range(units):
                before[u, h] = sout_ref[u, h]
                sout_ref[u, h] = before[u, h] * ecol[u, h] + ds[u, h]
        else:
            s_cur = sout_ref[0, h]
            for u in range(units):
                before[u, h] = s_cur
                s_cur = s_cur * ecol[u, h] + ds[u, h]
            sout_ref[0, h] = s_cur

    for g in range(rows // GLA_CS):
        for h in heads:
            inter = [_dot(qd[us(u), ks(h)], before[u, h])
                     for u in range(g * per_group, (g + 1) * per_group)]
            inter = inter[0] if per_group == 1 else jnp.concatenate(inter, axis=0)
            yscr_ref[gs(g), vs(h)] = yh[g, h] + inter

    y = yscr_ref[...]
    y = y * lax.rsqrt(_segsum(y * y, dv) * (1.0 / dv) + EPS) * ng_ref[...]
    y_ref[...] = (y * _silu(og)).reshape(nb, lt, GROUP_W)


def _gla(p, states, prm, l, prev, nb, lt, cs):
    assert (nb == 1 and cs == GLA_CS and lt % cs == 0) or (
        lt == cs and (nb * cs) % GLA_CS == 0 and GLA_CS % cs == 0)
    kern = functools.partial(_gla_kernel, nb=nb, lt=lt, cs=cs)
    return _mixer_call(kern, "gla", l, nb, lt, p, states, prm, prev,
                       [pltpu.VMEM((nb * lt, GROUP_W), F32)])


def _s5_pitch(lt):
    return lt if (lt // 8) % 2 == 1 else lt + 8


def _s5_kernel(u_ref, hr0_ref, hi0_ref, wb_ref, wc_ref, ar_ref, ai_ref, d_ref, wglu_ref, bglu_ref,
               y_ref, hr_ref, hi_ref, h_ref, *, nb, lt):
    rows = nb * lt
    n = S5_CH

    @pl.when(pl.program_id(1) == 0)
    def _():
        hr_ref[...] = hr0_ref[...]
        hi_ref[...] = hi0_ref[...]

    u = u_ref[...].reshape(rows, GROUP_W)
    bu = _dot(u, wb_ref[...])
    nt = n // LANES
    lanes = lambda x, j: x[:, j * LANES:(j + 1) * LANES]
    pitch = _s5_pitch(lt)
    seq_rows = lambda i: slice(i * pitch, i * pitch + lt)
    for j in range(2 * nt):
        for i in range(nb):
            h_ref[j, seq_rows(i), :] = lanes(bu, j)[i * lt:(i + 1) * lt]
    ar = [jnp.broadcast_to(lanes(ar_ref[...], j), (nb, LANES)) for j in range(nt)]
    ai = [jnp.broadcast_to(lanes(ai_ref[...], j), (nb, LANES)) for j in range(nt)]
    hr = [lanes(hr_ref[...], j) for j in range(nt)]
    hi = [lanes(hi_ref[...], j) for j in range(nt)]
    for t in range(lt):
        at_t = pl.ds(t, nb, stride=pitch)
        for j in range(nt):
            hr[j], hi[j] = (ar[j] * hr[j] - ai[j] * hi[j] + h_ref[j, at_t, :],
                            ar[j] * hi[j] + ai[j] * hr[j] + h_ref[nt + j, at_t, :])
            h_ref[j, at_t, :] = hr[j]
            h_ref[nt + j, at_t, :] = hi[j]
    hr_ref[...] = jnp.concatenate(hr, axis=1)
    hi_ref[...] = jnp.concatenate(hi, axis=1)
    tile = lambda j: jnp.concatenate([h_ref[j, seq_rows(i), :] for i in range(nb)], axis=0)
    h_re = jnp.concatenate([tile(j) for j in range(nt)], axis=1)
    h_im = jnp.concatenate([tile(nt + j) for j in range(nt)], axis=1)
    y = _dot(h_re, wc_ref[0]) - _dot(h_im, wc_ref[1])
    y = y + d_ref[...] * u
    y = 0.5 * y * (1.0 + jnp.tanh(math.sqrt(2.0 / math.pi) * (y + 0.044715 * (y * y * y))))
    y = y * _sigmoid(_dot(y, wglu_ref[...]) + bglu_ref[...])
    y_ref[...] = y.reshape(nb, lt, GROUP_W)


def _s5(u, states, prm, l, prev, nb, lt):
    kern = functools.partial(_s5_kernel, nb=nb, lt=lt)
    return _mixer_call(kern, "s5", l, nb, lt, u, states, prm, prev,
                       [pltpu.VMEM((2 * S5_CH // LANES, nb * _s5_pitch(lt), LANES), F32)])


def _block_diag_in(b):
    dep, g, n, i = b.shape
    eye = jnp.eye(g, dtype=b.dtype)
    return jnp.einsum('lgni,gh->lgihn', b, eye).reshape(dep, g * i, g * n)


def _block_diag_out(c):
    dep, g, o, n = c.shape
    eye = jnp.eye(g, dtype=c.dtype)
    return jnp.einsum('lgon,gh->lgnho', c, eye).reshape(dep, g * n, g * o)


def _prep_params(W):
    dep = W['w_in'].shape[0]
    row = lambda name: W[name].reshape(dep, 1, -1).astype(F32)
    rep = lambda name, n: jnp.repeat(W[name].reshape(dep, -1), n, axis=1).reshape(dep, 1, -1).astype(F32)
    w_in = W['w_in'].astype(BF16)
    o_r = SSD_PROJ
    o_g = o_r + RWKV_PROJ
    o_s = o_g + GLA_PROJ
    w_ssd = jnp.concatenate([w_in[..., :GROUP_W + SSD_CONV_CH],
                             jnp.repeat(w_in[..., GROUP_W + SSD_CONV_CH:o_r], SSD_HEADDIM, axis=2)], axis=2)
    w_rwkv = w_in[..., o_r:o_g]
    gq = o_g + 2 * GLA_QK + GROUP_W
    w_gla = jnp.concatenate([w_in[..., o_g:gq], w_in[..., gq:gq + GLA_GATE_LORA],
                             jnp.zeros((dep, D_MODEL, 128 - GLA_GATE_LORA), BF16),
                             w_in[..., gq + GLA_GATE_LORA:o_s]], axis=2)
    w_s5 = w_in[..., o_s:]
    in_w = (w_ssd, w_rwkv, w_gla, w_s5)

    ssd = (W['ssd_conv_w'].astype(F32), row('ssd_conv_b'), rep('ssd_a_log', SSD_HEADDIM),
           rep('ssd_dt_bias', SSD_HEADDIM), rep('ssd_d', SSD_HEADDIM), row('ssd_norm_g'))

    z = lambda r, c: jnp.zeros((dep, r, c), F32)
    w_lora = jnp.concatenate([
        jnp.concatenate([W['rwkv_w2'], z(32, 2 * GROUP_W)], axis=2),
        jnp.concatenate([z(32, GROUP_W), W['rwkv_a2'], z(32, GROUP_W)], axis=2),
        jnp.concatenate([z(64, 2 * GROUP_W), W['rwkv_g2']], axis=2)], axis=1).astype(BF16)
    rwkv = (row('rwkv_mu'), w_lora, row('rwkv_w0'), row('rwkv_a0'), row('rwkv_k_k'),
            row('rwkv_k_a'), row('rwkv_r_k'), row('rwkv_lnx_w'), row('rwkv_lnx_b'))

    wg2 = jnp.concatenate([W['gla_wg2'], z(128 - GLA_GATE_LORA, GLA_QK)], axis=1).astype(BF16)
    gla = (wg2, row('gla_bg'), row('gla_norm_g'))

    lr = W['s5_lam_re'].astype(F32)
    li = W['s5_lam_im'].astype(F32)
    dts = jnp.exp(W['s5_log_step'].astype(F32))[..., None]
    mag = jnp.exp(lr * dts)
    ar = mag * jnp.cos(li * dts)
    ai = mag * jnp.sin(li * dts)
    den = lr * lr + li * li
    fr = ((ar - 1.0) * lr + ai * li) / den
    fi = (ai * lr - (ar - 1.0) * li) / den
    b_re, b_im = W['s5_b_re'], W['s5_b_im']
    bb_re = fr[..., None] * b_re - fi[..., None] * b_im
    bb_im = fr[..., None] * b_im + fi[..., None] * b_re
    wb = jnp.concatenate([_block_diag_in(bb_re), _block_diag_in(bb_im)], axis=2).astype(BF16)
    wc = jnp.stack([_block_diag_out(W['s5_c_re']), _block_diag_out(W['s5_c_im'])], axis=1).astype(BF16)
    s5 = (wb, wc, ar.reshape(dep, 1, -1), ai.reshape(dep, 1, -1), row('s5_d'),
          W['s5_w_glu'].astype(BF16), row('s5_b_glu'))

    g4 = lambda name: W[name].reshape(dep, 1, 1, D_MODEL).astype(F32)
    return dict(
        in_w=in_w, ssd=ssd, rwkv=rwkv, gla=gla, s5=s5,
        g_mix_pre=g4('g_mix_pre'), g_mix_post=g4('g_mix_post'),
        g_ffn_pre=g4('g_ffn_pre'), g_ffn_post=g4('g_ffn_post'),
        w_out=W['w_out'].reshape(dep, 4, GROUP_W, D_MODEL).astype(BF16),
        w1=W['mlp_w1'].astype(BF16), w2=W['mlp_w2'].astype(BF16))


def _run_group(x, mod, first, st, P, cfg):
    dep, bsz = st['ssd'].shape[:2]
    hist = jnp.pad(st['ssd_conv'], ((0, 0), (0, 0), (HIST - (SSD_CONV - 1), 0), (0, 0)))
    ssd_st = (hist, st['ssd'])
    rwkv_st = (st['rwkv_shift'].reshape(dep, bsz, 1, RWKV_PROJ), st['rwkv'])
    gla_st = (st['gla'],)
    s5_st = (st['s5_re'].reshape(dep, bsz, S5_CH), st['s5_im'].reshape(dep, bsz, S5_CH))
    new = dict(ssd=None, rwkv=None, gla=None, s5=None)
    for l in range(DEPTH):
        p_ssd, p_rwkv, p_gla, p_s5 = _in_proj(x, mod, first, P['g_mix_pre'], P['in_w'], l, *cfg['tok'])
        y_ssd, *new['ssd'] = _ssd(p_ssd, ssd_st, P['ssd'], l, new['ssd'], *cfg['ssd'])
        y_rwkv, *new['rwkv'] = _rwkv(p_rwkv, rwkv_st, P['rwkv'], l, new['rwkv'], *cfg['rwkv'])
        y_gla, *new['gla'] = _gla(p_gla, gla_st, P['gla'], l, new['gla'], *cfg['gla'])
        y_s5, *new['s5'] = _s5(p_s5, s5_st, P['s5'], l, new['s5'], *cfg['s5'])
        x = _out_mlp(x, (y_ssd, y_rwkv, y_gla, y_s5), mod, first, P['g_mix_post'], P['g_ffn_pre'],
                     P['g_ffn_post'], P['w_out'], P['w1'], P['w2'], l, *cfg['tok'])
    hist_new, ssd_new = new['ssd']
    shift_new, rwkv_new = new['rwkv']
    s5r_new, s5i_new = new['s5']
    return x, dict(
        ssd=ssd_new, ssd_conv=hist_new[:, :, HIST - (SSD_CONV - 1):, :],
        rwkv=rwkv_new, rwkv_shift=shift_new.reshape(dep, bsz, RWKV_PROJ), gla=new['gla'][0],
        s5_re=s5r_new.reshape(dep, bsz, S5_GROUPS, S5_STATE),
        s5_im=s5i_new.reshape(dep, bsz, S5_GROUPS, S5_STATE))


_PROMPT_CFG = dict(tok=(1, 512), ssd=(1, 512), rwkv=(1, 512), gla=(1, 512, GLA_CS), s5=(8, 128))
_SAMPLE_CFG = dict(tok=(64, 8), ssd=(32, 8), rwkv=(32, 8), gla=(16, 8, 8), s5=(64, 8))


def kernel(x_prompt, x_sample, c_prompt, c_sample, state_ssd, state_ssd_conv, state_rwkv, state_rwkv_shift, state_gla, state_s5_re, state_s5_im, w_ada, b_ada, g_mix_pre, g_mix_post, g_ffn_pre, g_ffn_post, w_in, w_out, ssd_conv_w, ssd_conv_b, ssd_a_log, ssd_dt_bias, ssd_d, ssd_norm_g, rwkv_mu, rwkv_w0, rwkv_w2, rwkv_a0, rwkv_a2, rwkv_g2, rwkv_k_k, rwkv_k_a, rwkv_r_k, rwkv_lnx_w, rwkv_lnx_b, gla_wg2, gla_bg, gla_norm_g, s5_lam_re, s5_lam_im, s5_log_step, s5_b_re, s5_b_im, s5_c_re, s5_c_im, s5_d, s5_w_glu, s5_b_glu, mlp_w1, mlp_w2):
    W = dict(w_in=w_in, w_out=w_out, g_mix_pre=g_mix_pre, g_mix_post=g_mix_post,
             g_ffn_pre=g_ffn_pre, g_ffn_post=g_ffn_post,
             ssd_conv_w=ssd_conv_w, ssd_conv_b=ssd_conv_b, ssd_a_log=ssd_a_log,
             ssd_dt_bias=ssd_dt_bias, ssd_d=ssd_d, ssd_norm_g=ssd_norm_g,
             rwkv_mu=rwkv_mu, rwkv_w0=rwkv_w0, rwkv_w2=rwkv_w2, rwkv_a0=rwkv_a0, rwkv_a2=rwkv_a2,
             rwkv_g2=rwkv_g2, rwkv_k_k=rwkv_k_k, rwkv_k_a=rwkv_k_a, rwkv_r_k=rwkv_r_k,
             rwkv_lnx_w=rwkv_lnx_w, rwkv_lnx_b=rwkv_lnx_b,
             gla_wg2=gla_wg2, gla_bg=gla_bg, gla_norm_g=gla_norm_g,
             s5_lam_re=s5_lam_re, s5_lam_im=s5_lam_im, s5_log_step=s5_log_step,
             s5_b_re=s5_b_re, s5_b_im=s5_b_im, s5_c_re=s5_c_re, s5_c_im=s5_c_im,
             s5_d=s5_d, s5_w_glu=s5_w_glu, s5_b_glu=s5_b_glu, mlp_w1=mlp_w1, mlp_w2=mlp_w2)
    P = _prep_params(W)

    nbp, nbs = x_prompt.shape[0], x_sample.shape[0]
    mod = _ada(jnp.concatenate([c_sample, c_prompt], axis=0), w_ada, b_ada)
    mod = mod.reshape(DEPTH, nbs + nbp, N_MOD, D_MODEL)

    st_sample = dict(ssd=state_ssd, ssd_conv=state_ssd_conv, rwkv=state_rwkv,
                     rwkv_shift=state_rwkv_shift, gla=state_gla, s5_re=state_s5_re, s5_im=state_s5_im)
    st_prompt = {n: jnp.zeros((DEPTH, nbp) + v.shape[2:], v.dtype) for n, v in st_sample.items()}

    y_prompt, sp = _run_group(x_prompt, mod, nbs, st_prompt, P, _PROMPT_CFG)
    y_sample, ss = _run_group(x_sample, mod, 0, st_sample, P, _SAMPLE_CFG)
    return (y_prompt, y_sample,
            sp['ssd'], ss['ssd'], sp['ssd_conv'], ss['ssd_conv'],
            sp['rwkv'], ss['rwkv'], sp['rwkv_shift'], ss['rwkv_shift'],
            sp['gla'], ss['gla'], sp['s5_re'], ss['s5_re'], sp['s5_im'], ss['s5_im'])
```

```python
import functools
import math

import jax
import jax.numpy as jnp
from jax import lax
from jax.experimental import pallas as pl
from jax.experimental.pallas import tpu as pltpu

F32 = jnp.float32
BF16 = jnp.bfloat16

D_MODEL = 1024
DEPTH = 2
GROUP_W = 256
D_FF = 4096
N_MOD = 6
EPS = 1e-6

SSD_HEADS = 4
SSD_HEADDIM = 64
SSD_STATE = 64
SSD_NGROUPS = 2
SSD_BC = SSD_NGROUPS * SSD_STATE
SSD_CONV = 4
SSD_CONV_CH = GROUP_W + 2 * SSD_BC
SSD_PROJ = GROUP_W + SSD_CONV_CH + SSD_HEADS
SSD_PW = GROUP_W + SSD_CONV_CH + GROUP_W

RWKV_HEADS = 4
RWKV_HEADDIM = 64
RWKV_LORA = 128
RWKV_PROJ = 3 * GROUP_W + RWKV_LORA
RWKV_LN_EPS = 64e-5

GLA_HEADS = 4
GLA_DK = 32
GLA_DV = 64
GLA_GATE_LORA = 16
GLA_TAU = 16.0
GLA_QK = GLA_HEADS * GLA_DK
GLA_PROJ = 2 * GLA_QK + GROUP_W + GLA_GATE_LORA + GROUP_W
GLA_PW = 2 * GLA_QK + GROUP_W + 128 + GROUP_W

S5_GROUP = 16
S5_GROUPS = 16
S5_STATE = 64
S5_CH = S5_GROUPS * S5_STATE
S5_PROJ = GROUP_W

LANES = 128
HIST = 8
CHUNK = 64
GLA_CS = 32
VMEM_LIMIT = 56 * 1024 * 1024


def _dot(a, b, dims=(((1,), (0,)), ((), ()))):
    return lax.dot_general(a.astype(BF16), b.astype(BF16), dims, preferred_element_type=F32)


_NT = (((1,), (1,)), ((), ()))
_TN = (((0,), (0,)), ((), ()))
_NN = (((1,), (0,)), ((), ()))


def _split2(a):
    hi = a.astype(BF16)
    lo = (a - hi.astype(F32)).astype(BF16)
    return hi, lo


def _dot3(a, b, dims=_NN):
    ah, al = _split2(a)
    bh, bl = _split2(b)
    d = lambda x, y: lax.dot_general(x, y, dims, preferred_element_type=F32)
    return d(ah, bh) + (d(ah, bl) + d(al, bh))


_dot_a = _dot
_dot_inv = _dot
_dot_app = _dot
_dot_st = _dot


def _dot_exact_lhs(m, x, dims=_NN):
    mb = m.astype(BF16)
    x1 = x.astype(BF16)
    r1 = x - x1.astype(F32)
    x2 = r1.astype(BF16)
    x3 = (r1 - x2.astype(F32)).astype(BF16)
    d = lambda y: lax.dot_general(mb, y, dims, preferred_element_type=F32)
    return d(x1) + (d(x2) + d(x3))


def _dot_exact_rhs(x, m):
    mb = m.astype(BF16)
    x1 = x.astype(BF16)
    r1 = x - x1.astype(F32)
    x2 = r1.astype(BF16)
    x3 = (r1 - x2.astype(F32)).astype(BF16)
    d = lambda y: lax.dot_general(y, mb, _NN, preferred_element_type=F32)
    return d(x1) + (d(x2) + d(x3))


def _segsum(x, width):
    n = x.shape[-1]
    r = lax.broadcasted_iota(jnp.int32, (n, n), 0) // width
    c = lax.broadcasted_iota(jnp.int32, (n, n), 1) // width
    j = jnp.where(r == c, 1.0, 0.0).astype(BF16)
    hi, lo = _split2(x)
    d = lambda y: lax.dot_general(y, j, _NN, preferred_element_type=F32)
    return d(hi) + d(lo)


def _sigmoid(x):
    return 1.0 / (1.0 + jnp.exp(-x))


def _silu(x):
    return x * _sigmoid(x)


def _softplus(x):
    return jnp.maximum(x, 0.0) + jnp.log1p(jnp.exp(-jnp.abs(x)))


def _rms(x, g):
    return x * lax.rsqrt(jnp.mean(x * x, axis=-1, keepdims=True) + EPS) * g


def _chunk_masks(rows, lt):
    r = lax.broadcasted_iota(jnp.int32, (rows, rows), 0)
    c = lax.broadcasted_iota(jnp.int32, (rows, rows), 1)
    same = (r // lt) == (c // lt)
    return same & (c <= r), same & (c < r)


def _unit_cumsum(x, unit):
    incl, _ = _chunk_masks(CHUNK, unit)
    tri = jnp.where(incl, 1.0, 0.0)
    parts = [_dot_exact_lhs(tri, x[j * CHUNK:(j + 1) * CHUNK]) for j in range(x.shape[0] // CHUNK)]
    return parts[0] if len(parts) == 1 else jnp.concatenate(parts, axis=0)


def _lsel(a, l, **kw):
    shape = a.shape[1:]
    return pl.BlockSpec((None,) + shape, lambda i, c: (l,) + (0,) * len(shape), **kw)


def _lseq(a, l, nb):
    shape = a.shape[2:]
    return pl.BlockSpec((None, nb) + shape, lambda i, c: (l, i) + (0,) * len(shape))


def _mixer_call(kern, name, l, nb, lt, p, states, params, prev, scratch):
    bsz, seq, width = p.shape
    n_in = 1 + len(states) + len(params)
    n_st = len(states)
    alias = tuple(prev) if prev is not None else ()
    first = prev is None

    def body(*refs):
        ins, outs = refs[:n_in], refs[n_in + len(alias):]
        y_ref, st_refs, scratch_refs = outs[0], outs[1:1 + n_st], outs[1 + n_st:]
        if first:
            @pl.when(pl.program_id(1) == 0)
            def _():
                for r in st_refs:
                    r[1:] = jnp.zeros((r.shape[0] - 1,) + r.shape[1:], F32)
            st_refs = [r.at[0] for r in st_refs]
        return kern(*ins, y_ref, *st_refs, *scratch_refs)

    def st_out_spec(s):
        if not first:
            return _lseq(s, l, nb)
        shape = s.shape[2:]
        return pl.BlockSpec((s.shape[0], nb) + shape, lambda i, c: (0, i) + (0,) * len(shape))

    tok = lambda w: pl.BlockSpec((nb, lt, w), lambda i, c: (i, c, 0))
    return pl.pallas_call(
        body,
        out_shape=(jax.ShapeDtypeStruct((bsz, seq, GROUP_W), F32),)
        + tuple(jax.ShapeDtypeStruct(s.shape, F32) for s in states),
        grid=(bsz // nb, seq // lt),
        in_specs=[tok(width)] + [_lseq(s, l, nb) for s in states] + [_lsel(a, l) for a in params]
        + [pl.BlockSpec(memory_space=pl.ANY)] * len(alias),
        out_specs=(tok(GROUP_W),) + tuple(st_out_spec(s) for s in states),
        scratch_shapes=scratch,
        input_output_aliases={n_in + k: 1 + k for k in range(len(alias))},
        compiler_params=pltpu.CompilerParams(
            dimension_semantics=("parallel", "arbitrary"), vmem_limit_bytes=VMEM_LIMIT),
        name=name,
    )(p, *states, *params, *alias)


def _bcast_last(x, nb, lt):
    c = x.shape[-1]
    last = x.reshape(nb, lt, c)[:, lt - 1:lt, :]
    return last, jnp.broadcast_to(last, (nb, lt, c)).reshape(nb * lt, c)


def _ada_kernel(c_ref, w_ref, b_ref, o_ref):
    c = c_ref[...]
    o_ref[0] = _dot(_silu(c), w_ref[0]) + b_ref[0]


def _ada(c_all, w_ada, b_ada):
    rows = c_all.shape[0]
    n = N_MOD * D_MODEL
    tn = 1536
    return pl.pallas_call(
        _ada_kernel,
        out_shape=jax.ShapeDtypeStruct((DEPTH, rows, n), F32),
        grid=(DEPTH, n // tn),
        in_specs=[
            pl.BlockSpec((rows, D_MODEL), lambda l, j: (0, 0)),
            pl.BlockSpec((1, D_MODEL, tn), lambda l, j: (l, 0, j)),
            pl.BlockSpec((1, 1, tn), lambda l, j: (l, 0, j)),
        ],
        out_specs=pl.BlockSpec((1, rows, tn), lambda l, j: (l, 0, j)),
        compiler_params=pltpu.CompilerParams(
            dimension_semantics=("parallel", "parallel"), vmem_limit_bytes=VMEM_LIMIT),
        name="ada",
    )(c_all, w_ada, b_ada.reshape(DEPTH, 1, n))


def _in_kernel(x_ref, mod_ref, g_ref, wssd_ref, wrwkv_ref, wgla_ref, ws5_ref,
               ossd_ref, orwkv_ref, ogla_ref, os5_ref):
    nb, lt, d = x_ref.shape
    m = mod_ref[...]
    h = _rms(x_ref[...], g_ref[...]) * (1.0 + m[:, 1:2, :]) + m[:, 0:1, :]
    hb = h.reshape(nb * lt, d).astype(BF16)
    for w_ref, o_ref in ((wssd_ref, ossd_ref), (wrwkv_ref, orwkv_ref),
                         (wgla_ref, ogla_ref), (ws5_ref, os5_ref)):
        o_ref[...] = jnp.dot(hb, w_ref[...], preferred_element_type=F32).reshape(o_ref.shape)


def _mod_spec(mod, l, first, nb):
    return pl.BlockSpec((None, nb, N_MOD, mod.shape[-1]), lambda i, j: (l, first // nb + i, 0, 0))


def _in_proj(x, mod, first, g, wts, l, nb, lt):
    bsz, seq, d = x.shape
    widths = (SSD_PW, RWKV_PROJ, GLA_PW, S5_PROJ)
    return pl.pallas_call(
        _in_kernel,
        out_shape=tuple(jax.ShapeDtypeStruct((bsz, seq, w), F32) for w in widths),
        grid=(bsz // nb, seq // lt),
        in_specs=[pl.BlockSpec((nb, lt, d), lambda i, j: (i, j, 0)), _mod_spec(mod, l, first, nb),
                  _lsel(g, l)] + [_lsel(w, l) for w in wts],
        out_specs=tuple(pl.BlockSpec((nb, lt, w), lambda i, j: (i, j, 0)) for w in widths),
        compiler_params=pltpu.CompilerParams(
            dimension_semantics=("parallel", "parallel"), vmem_limit_bytes=VMEM_LIMIT),
        name="in_proj",
    )(x, mod, g, *wts)


def _out_mlp_kernel(x_ref, y0_ref, y1_ref, y2_ref, y3_ref, mod_ref, gpost_ref, gpre_ref, gfpost_ref,
                    wout_ref, w1_ref, w2_ref, o_ref):
    nb, lt, d = x_ref.shape
    rows = nb * lt
    m = mod_ref[...]
    mix = None
    for i, y_ref in enumerate((y0_ref, y1_ref, y2_ref, y3_ref)):
        t = jnp.dot(y_ref[...].reshape(rows, GROUP_W).astype(BF16), wout_ref[i],
                    preferred_element_type=F32)
        mix = t if mix is None else mix + t
    x1 = x_ref[...] + m[:, 2:3, :] * _rms(mix, gpost_ref[0]).reshape(nb, lt, d)
    h = _rms(x1, gpre_ref[...]) * (1.0 + m[:, 4:5, :]) + m[:, 3:4, :]
    hb = h.reshape(rows, d).astype(BF16)
    acc = None
    step = 1024
    for j in range(D_FF // step):
        a = jnp.dot(hb, w1_ref[:, j * step:(j + 1) * step], preferred_element_type=F32)
        a = jnp.square(jnp.maximum(a, 0.0)).astype(BF16)
        t = jnp.dot(a, w2_ref[j * step:(j + 1) * step, :], preferred_element_type=F32)
        acc = t if acc is None else acc + t
    o_ref[...] = x1 + m[:, 5:6, :] * _rms(acc, gfpost_ref[0]).reshape(nb, lt, d)


def _out_mlp(x, ys, mod, first, gpost, gpre, gfpost, wout, w1, w2, l, nb, lt):
    bsz, seq, d = x.shape
    tok = lambda i, j: (i, j, 0)
    one = pl.Buffered(1)
    return pl.pallas_call(
        _out_mlp_kernel,
        out_shape=jax.ShapeDtypeStruct((bsz, seq, d), F32),
        grid=(bsz // nb, seq // lt),
        in_specs=[pl.BlockSpec((nb, lt, d), tok)]
        + [pl.BlockSpec((nb, lt, GROUP_W), tok)] * 4
        + [_mod_spec(mod, l, first, nb)]
        + [_lsel(g, l) for g in (gpost, gpre, gfpost)]
        + [_lsel(w, l, pipeline_mode=one) for w in (wout, w1, w2)],
        out_specs=pl.BlockSpec((nb, lt, d), tok),
        compiler_params=pltpu.CompilerParams(
            dimension_semantics=("parallel", "parallel"), vmem_limit_bytes=VMEM_LIMIT),
        name="out_mlp",
    )(x, *ys, mod, gpost, gpre, gfpost, wout, w1, w2)


def _ssd_kernel(p_ref, hist_ref, s0_ref, cw_ref, cb_ref, alog_ref, dtb_ref, dsk_ref, ng_ref,
                y_ref, hout_ref, sout_ref, ext_ref, yscr_ref, *, nb, lt):
    rows = nb * lt
    hd = SSD_HEADDIM

    @pl.when(pl.program_id(1) == 0)
    def _():
        ext_ref[:, 0:HIST, :] = hist_ref[...]
        sout_ref[...] = s0_ref[...]

    p = p_ref[...]
    ext_ref[:, HIST:HIST + lt, :] = p[:, :, GROUP_W:GROUP_W + SSD_CONV_CH]
    conv = cb_ref[...]
    for j in range(SSD_CONV):
        o = HIST - (SSD_CONV - 1) + j
        conv = conv + ext_ref[:, o:o + lt, :] * cw_ref[j:j + 1, :]
    tail = ext_ref[:, lt:lt + HIST, :]
    ext_ref[:, 0:HIST, :] = tail
    hout_ref[...] = tail

    xbc = _silu(conv).reshape(rows, SSD_CONV_CH)
    xs = xbc[:, 0:GROUP_W]
    bm = xbc[:, GROUP_W:GROUP_W + SSD_BC]
    cm = xbc[:, GROUP_W + SSD_BC:]
    z = p[:, :, 0:GROUP_W].reshape(rows, GROUP_W)
    dt = _softplus(p[:, :, GROUP_W + SSD_CONV_CH:].reshape(rows, GROUP_W) + dtb_ref[...])
    a = dt * (-jnp.exp(alog_ref[...]))
    unit = lt if nb > 1 else CHUNK
    units = rows // unit
    per_chunk = CHUNK // unit
    cum = _unit_cumsum(a, unit)
    last, last_b = _bcast_last(cum, units, unit)
    xdt = xs * dt
    xw = xdt * jnp.exp(last_b - cum)
    ecum = jnp.exp(cum)
    elast = jnp.exp(last)
    ones_row = jnp.full((CHUNK, hd), 1.0 / hd, F32)
    cmask, _ = _chunk_masks(CHUNK, unit)
    heads = range(SSD_HEADS)
    hs = lambda h: slice(h * hd, (h + 1) * hd)
    gs = lambda h: slice((h // 2) * SSD_STATE, (h // 2 + 1) * SSD_STATE)
    cs = lambda j: slice(j * CHUNK, (j + 1) * CHUNK)
    us = lambda u: slice(u * unit, (u + 1) * unit)

    yh = {}
    for j in range(rows // CHUNK):
        cb = [_dot(cm[cs(j), gs(2 * g)], bm[cs(j), gs(2 * g)], _NT) for g in range(SSD_NGROUPS)]
        for h in heads:
            ccol = cum[cs(j), hs(h)]
            crow = _dot_exact_lhs(ones_row, ccol, _NT)
            decay = jnp.exp(jnp.where(cmask, ccol - crow, -jnp.inf))
            yh[j, h] = _dot(cb[h // 2] * decay, xdt[cs(j), hs(h)])
    ds = {(u, h): _dot(xw[us(u), hs(h)], bm[us(u), gs(h)], _TN) for u in range(units) for h in heads}

    before = {}
    for h in heads:
        if nb > 1:
            for u in range(units):
                before[u, h] = sout_ref[u, h]
                sout_ref[u, h] = before[u, h] * elast[u][:, hs(h)] + ds[u, h]
        else:
            s_cur = sout_ref[0, h]
            for u in range(units):
                before[u, h] = s_cur
                s_cur = s_cur * elast[u][:, hs(h)] + ds[u, h]
            sout_ref[0, h] = s_cur

    for j in range(rows // CHUNK):
        for h in heads:
            inter = [_dot(cm[us(u), gs(h)], before[u, h], _NT)
                     for u in range(j * per_chunk, (j + 1) * per_chunk)]
            inter = inter[0] if per_chunk == 1 else jnp.concatenate(inter, axis=0)
            yscr_ref[cs(j), hs(h)] = yh[j, h] + inter * ecum[cs(j), hs(h)]

    y = yscr_ref[...] + dsk_ref[...] * xs
    y = y * _silu(z)
    y = y * lax.rsqrt(_segsum(y * y, hd) * (1.0 / hd) + EPS) * ng_ref[...]
    y_ref[...] = y.reshape(nb, lt, GROUP_W)


def _ssd(p, states, prm, l, prev, nb, lt):
    assert CHUNK == SSD_HEADDIM
    assert (nb > 1 and (nb * lt) % CHUNK == 0 and CHUNK % lt == 0) or (nb == 1 and lt % CHUNK == 0)
    kern = functools.partial(_ssd_kernel, nb=nb, lt=lt)
    return _mixer_call(kern, "ssd", l, nb, lt, p, states, prm, prev,
                       [pltpu.VMEM((nb, HIST + lt, SSD_CONV_CH), F32),
                        pltpu.VMEM((nb * lt, GROUP_W), F32)])


def _unit_lower_inverse(a2, n, levels, dot):
    ri = lax.broadcasted_iota(jnp.int32, (n, n), 0)
    ci = lax.broadcasted_iota(jnp.int32, (n, n), 1)
    x = [jnp.where(ri == ci, 1.0, 0.0) - jnp.where((ri % 2 == 1) & (ci == ri - 1), a, 0.0) for a in a2]
    s = 2
    while s < levels:
        off = ((ri // s) % 2 == 1) & ((ci // s) == (ri // s) - 1)
        t = [dot(xi, jnp.where(off, a, 0.0)) for xi, a in zip(x, a2)]
        x = [xi - dot(ti, xi) for xi, ti in zip(x, t)]
        s *= 2
    return x


def _rwkv_chunk_multi(ops, sout_ref, yscr_ref, nb, lt):
    hd = RWKV_HEADDIM
    per_chunk = CHUNK // lt
    idx = [(c, h) for c in range(nb // per_chunk) for h in range(RWKV_HEADS)]
    n = range(len(idx))
    incl, strict = _chunk_masks(CHUNK, lt)
    mask2 = jnp.concatenate([strict, incl], axis=0)
    top = lambda m: m[0:CHUNK]
    bot = lambda m: m[CHUNK:2 * CHUNK]
    bl = lambda name: [ops[name][c * CHUNK:(c + 1) * CHUNK, h * hd:(h + 1) * hd] for c, h in idx]
    rt, kt, k2t, bt, k2g, bg, v = (bl(name) for name in ('rt', 'kt', 'k2t', 'bt', 'k2g', 'bg', 'v'))
    kr = [jnp.concatenate([kt[j], rt[j]], axis=0) for j in n]
    m1 = [jnp.where(mask2, _dot_a(kr[j], k2t[j], _NT), 0.0) for j in n]
    m2 = [jnp.where(mask2, _dot_a(kr[j], bt[j], _NT), 0.0) for j in n]
    x = _unit_lower_inverse([top(m) for m in m2], CHUNK, lt, _dot_inv)
    m1v = [_dot_app(m1[j], v[j]) for j in n]
    seq = lambda i: slice(i * lt, (i + 1) * lt)
    seqs = range(per_chunk)
    s_old = [[sout_ref[c * per_chunk + i, h] for i in seqs] for c, h in idx]
    pq = [[_dot_st(jnp.concatenate([kt[j][seq(i)], rt[j][seq(i)]], axis=0), s_old[j][i], _NT)
           for i in seqs] for j in n]
    p0 = [jnp.concatenate([pq[j][i][0:lt] for i in seqs], axis=0) for j in n]
    q0 = [jnp.concatenate([pq[j][i][lt:2 * lt] for i in seqs], axis=0) for j in n]
    u = [_dot_app(x[j], p0[j] + top(m1v[j])) for j in n]
    l2u = [_dot_app(bot(m2[j]), u[j]) for j in n]
    for j, (c, h) in enumerate(idx):
        yscr_ref[c * CHUNK:(c + 1) * CHUNK, h * hd:(h + 1) * hd] = q0[j] + bot(m1v[j]) - l2u[j]
        for i in seqs:
            sq = c * per_chunk + i
            sout_ref[sq, h] = (s_old[j][i] * ops['elast'][sq][:, h * hd:(h + 1) * hd]
                               + _dot_st(v[j][seq(i)], k2g[j][seq(i)], _TN)
                               - _dot_st(u[j][seq(i)], bg[j][seq(i)], _TN))


def _rwkv_chunks_seq(ops, sout_ref, yscr_ref, nc):
    hd = RWKV_HEADDIM
    idx = [(c, h) for c in range(nc) for h in range(RWKV_HEADS)]
    incl, strict = _chunk_masks(CHUNK, CHUNK)
    ri = lax.broadcasted_iota(jnp.int32, (hd, hd), 0)
    ci = lax.broadcasted_iota(jnp.int32, (hd, hd), 1)
    eye = jnp.where(ri == ci, 1.0, 0.0)
    bl = lambda name: [ops[name][c * CHUNK:(c + 1) * CHUNK, h * hd:(h + 1) * hd] for c, h in idx]
    rt, kt, k2t, bt, k2g, bg, v = (bl(n) for n in ('rt', 'kt', 'k2t', 'bt', 'k2g', 'bg', 'v'))
    n = range(len(idx))
    top = lambda m: m[0:CHUNK]
    bot = lambda m: m[CHUNK:2 * CHUNK]
    mask2 = jnp.concatenate([strict, incl], axis=0)
    kr = [jnp.concatenate([kt[j], rt[j]], axis=0) for j in n]
    m1 = [jnp.where(mask2, _dot_a(kr[j], k2t[j], _NT), 0.0) for j in n]
    m2 = [jnp.where(mask2, _dot_a(kr[j], bt[j], _NT), 0.0) for j in n]
    x = _unit_lower_inverse([top(m) for m in m2], CHUNK, CHUNK, _dot_inv)
    m1v = [_dot_app(m1[j], v[j]) for j in n]
    kp = [_dot_app(x[j], kt[j]) for j in n]
    ul = [_dot_app(x[j], top(m1v[j])) for j in n]
    rp = [rt[j] - _dot_app(bot(m2[j]), kp[j]) for j in n]
    yl = [bot(m1v[j]) - _dot_app(bot(m2[j]), ul[j]) for j in n]
    mc = [eye * ops['elast'][c][:, h * hd:(h + 1) * hd] - _dot_st(kp[j], bg[j], _TN)
          for j, (c, h) in enumerate(idx)]
    dc = [_dot_st(v[j], k2g[j], _TN) - _dot_st(ul[j], bg[j], _TN) for j in n]
    state = [sout_ref[0, h] for h in range(RWKV_HEADS)]
    for c in range(nc):
        js = [c * RWKV_HEADS + h for h in range(RWKV_HEADS)]
        ys = [_dot_st(rp[j], state[h], _NT) + yl[j] for h, j in enumerate(js)]
        state = [_dot_st(state[h], mc[j]) + dc[j] for h, j in enumerate(js)]
        for h in range(RWKV_HEADS):
            yscr_ref[c * CHUNK:(c + 1) * CHUNK, h * hd:(h + 1) * hd] = ys[h]
    for h in range(RWKV_HEADS):
        sout_ref[0, h] = state[h]


def _rwkv_kernel(p_ref, sh0_ref, s0_ref, mu_ref, wl_ref, w0_ref, a0_ref, kk_ref, ka_ref, rk_ref,
                 lnw_ref, lnb_ref, y_ref, shout_ref, sout_ref, ext_ref, yscr_ref, *, nb, lt):
    rows = nb * lt
    hd = RWKV_HEADDIM
    gw = GROUP_W

    @pl.when(pl.program_id(1) == 0)
    def _():
        ext_ref[:, HIST - 1:HIST, :] = sh0_ref[...]
        sout_ref[...] = s0_ref[...]

    p = p_ref[...]
    ext_ref[:, HIST:HIST + lt, :] = p
    prev = ext_ref[:, HIST - 1:HIST - 1 + lt, :]
    pm = (p + (prev - p) * mu_ref[...]).reshape(rows, RWKV_PROJ)
    last_p = p[:, lt - 1:lt, :]
    ext_ref[:, HIST - 1:HIST, :] = last_p
    shout_ref[...] = last_p

    r = pm[:, 0:gw]
    k = pm[:, gw:2 * gw]
    v = pm[:, 2 * gw:3 * gw]
    lo = pm[:, 3 * gw:]
    lane = lax.broadcasted_iota(jnp.int32, (rows, RWKV_LORA), 1)
    t = jnp.where(lane < 32, jnp.tanh(lo), jnp.where(lane < 64, lo, _sigmoid(lo)))
    lora = _dot(t, wl_ref[...])
    wlog = -_softplus(-(w0_ref[...] + lora[:, 0:gw])) - 0.5
    lw = -jnp.exp(wlog)
    a = _sigmoid(a0_ref[...] + lora[:, gw:2 * gw])
    gate = lora[:, 2 * gw:]
    kk = k * kk_ref[...]
    kk = kk / jnp.maximum(jnp.sqrt(_segsum(kk * kk, hd)), 1e-12)
    k2 = k * (1.0 + (a - 1.0) * ka_ref[...])
    beta = kk * a

    unit = lt if nb > 1 else CHUNK
    gcum = _unit_cumsum(lw, unit)
    glast, glast_b = _bcast_last(gcum, rows // unit, unit)
    eg = jnp.exp(gcum)
    einv = jnp.exp(-gcum)
    etail = jnp.exp(glast_b - gcum)
    ops = dict(rt=r * eg, kt=kk * jnp.exp(gcum - lw), k2t=k2 * einv, bt=beta * einv,
               k2g=k2 * etail, bg=beta * etail, v=v, elast=jnp.exp(glast))
    if nb > 1:
        _rwkv_chunk_multi(ops, sout_ref, yscr_ref, nb, lt)
    else:
        _rwkv_chunks_seq(ops, sout_ref, yscr_ref, lt // CHUNK)

    y = yscr_ref[...]
    mean = _segsum(y, hd) * (1.0 / hd)
    dlt = y - mean
    var = _segsum(dlt * dlt, hd) * (1.0 / hd)
    yn = dlt * lax.rsqrt(var + RWKV_LN_EPS) * lnw_ref[...] + lnb_ref[...]
    bonus = _segsum(r * k2 * rk_ref[...], hd) * v
    y_ref[...] = ((yn + bonus) * gate).reshape(nb, lt, gw)


def _rwkv(p, states, prm, l, prev, nb, lt):
    assert (nb > 1 and (nb * lt) % CHUNK == 0 and CHUNK % lt == 0) or (nb == 1 and lt % CHUNK == 0)
    kern = functools.partial(_rwkv_kernel, nb=nb, lt=lt)
    return _mixer_call(kern, "rwkv", l, nb, lt, p, states, prm, prev,
                       [pltpu.VMEM((nb, HIST + lt, RWKV_PROJ), F32),
                        pltpu.VMEM((nb * lt, GROUP_W), F32)])


def _gla_kernel(p_ref, s0_ref, wg2_ref, bg_ref, ng_ref, y_ref, sout_ref, yscr_ref, *, nb, lt, cs):
    rows = nb * lt
    dk, dv = GLA_DK, GLA_DV
    qk = GLA_QK
    units = rows // cs
    per_group = GLA_CS // cs

    @pl.when(pl.program_id(1) == 0)
    def _():
        sout_ref[...] = s0_ref[...]

    p = p_ref[...].reshape(rows, GLA_PW)
    q = p[:, 0:qk] * (dk ** -0.5)
    k = p[:, qk:2 * qk]
    v = p[:, 2 * qk:2 * qk + GROUP_W]
    dg = p[:, 2 * qk + GROUP_W:2 * qk + GROUP_W + 128]
    og = p[:, 2 * qk + GROUP_W + 128:]
    lg = -_softplus(-(_dot(dg, wg2_ref[...]) + bg_ref[...])) * (1.0 / GLA_TAU)
    b = _unit_cumsum(lg, cs)
    bl, bl_b = _bcast_last(b, units, cs)
    qd = q * jnp.exp(b)
    kd = k * jnp.exp(-b)
    kw = k * jnp.exp(bl_b - b)
    ebl = jnp.exp(bl)
    gmask, _ = _chunk_masks(GLA_CS, cs)

    heads = range(GLA_HEADS)
    ks = lambda h: slice(h * dk, (h + 1) * dk)
    vs = lambda h: slice(h * dv, (h + 1) * dv)
    gs = lambda g: slice(g * GLA_CS, (g + 1) * GLA_CS)
    us = lambda u: slice(u * cs, (u + 1) * cs)

    yh = {}
    for g in range(rows // GLA_CS):
        for h in heads:
            att = jnp.where(gmask, _dot(qd[gs(g), ks(h)], kd[gs(g), ks(h)], _NT), 0.0)
            yh[g, h] = _dot(att, v[gs(g), vs(h)])
    ds = {(u, h): _dot(kw[us(u), ks(h)], v[us(u), vs(h)], _TN) for u in range(units) for h in heads}
    ri = lax.broadcasted_iota(jnp.int32, (dk, dk), 0)
    ci = lax.broadcasted_iota(jnp.int32, (dk, dk), 1)
    ones_kv = jnp.ones((dk, dv), F32)
    ecol = {(u, h): _dot_exact_rhs(jnp.where(ri == ci, ebl[u][:, ks(h)], 0.0), ones_kv)
            for u in range(units) for h in heads}

    before = {}
    for h in heads:
        if nb > 1:
            for u in range(units):
                before[u, h] = sout_ref[u, h]
                sout_ref[u, h] = before[u, h] * ecol[u, h] + ds[u, h]
        else:
            s_cur = sout_ref[0, h]
            for u in range(units):
                before[u, h] = s_cur
                s_cur = s_cur * ecol[u, h] + ds[u, h]
            sout_ref[0, h] = s_cur

    for g in range(rows // GLA_CS):
        for h in heads:
            inter = [_dot(qd[us(u), ks(h)], before[u, h])
                     for u in range(g * per_group, (g + 1) * per_group)]
            inter = inter[0] if per_group == 1 else jnp.concatenate(inter, axis=0)
            yscr_ref[gs(g), vs(h)] = yh[g, h] + inter

    y = yscr_ref[...]
    y = y * lax.rsqrt(_segsum(y * y, dv) * (1.0 / dv) + EPS) * ng_ref[...]
    y_ref[...] = (y * _silu(og)).reshape(nb, lt, GROUP_W)


def _gla(p, states, prm, l, prev, nb, lt, cs):
    assert (nb == 1 and cs == GLA_CS and lt % cs == 0) or (
        lt == cs and (nb * cs) % GLA_CS == 0 and GLA_CS % cs == 0)
    kern = functools.partial(_gla_kernel, nb=nb, lt=lt, cs=cs)
    return _mixer_call(kern, "gla", l, nb, lt, p, states, prm, prev,
                       [pltpu.VMEM((nb * lt, GROUP_W), F32)])


def _s5_pitch(lt):
    return lt if (lt // 8) % 2 == 1 else lt + 8


def _s5_kernel(u_ref, hr0_ref, hi0_ref, wb_ref, wc_ref, ar_ref, ai_ref, d_ref, wglu_ref, bglu_ref,
               y_ref, hr_ref, hi_ref, h_ref, *, nb, lt):
    rows = nb * lt
    n = S5_CH

    @pl.when(pl.program_id(1) == 0)
    def _():
        hr_ref[...] = hr0_ref[...]
        hi_ref[...] = hi0_ref[...]

    u = u_ref[...].reshape(rows, GROUP_W)
    bu = _dot(u, wb_ref[...])
    nt = n // LANES
    lanes = lambda x, j: x[:, j * LANES:(j + 1) * LANES]
    pitch = _s5_pitch(lt)
    seq_rows = lambda i: slice(i * pitch, i * pitch + lt)
    for j in range(2 * nt):
        for i in range(nb):
            h_ref[j, seq_rows(i), :] = lanes(bu, j)[i * lt:(i + 1) * lt]
    ar = [jnp.broadcast_to(lanes(ar_ref[...], j), (nb, LANES)) for j in range(nt)]
    ai = [jnp.broadcast_to(lanes(ai_ref[...], j), (nb, LANES)) for j in range(nt)]
    hr = [lanes(hr_ref[...], j) for j in range(nt)]
    hi = [lanes(hi_ref[...], j) for j in range(nt)]
    for t in range(lt):
        at_t = pl.ds(t, nb, stride=pitch)
        for j in range(nt):
            hr[j], hi[j] = (ar[j] * hr[j] - ai[j] * hi[j] + h_ref[j, at_t, :],
                            ar[j] * hi[j] + ai[j] * hr[j] + h_ref[nt + j, at_t, :])
            h_ref[j, at_t, :] = hr[j]
            h_ref[nt + j, at_t, :] = hi[j]
    hr_ref[...] = jnp.concatenate(hr, axis=1)
    hi_ref[...] = jnp.concatenate(hi, axis=1)
    tile = lambda j: jnp.concatenate([h_ref[j, seq_rows(i), :] for i in range(nb)], axis=0)
    h_re = jnp.concatenate([tile(j) for j in range(nt)], axis=1)
    h_im = jnp.concatenate([tile(nt + j) for j in range(nt)], axis=1)
    y = _dot(h_re, wc_ref[0]) - _dot(h_im, wc_ref[1])
    y = y + d_ref[...] * u
    y = 0.5 * y * (1.0 + jnp.tanh(math.sqrt(2.0 / math.pi) * (y + 0.044715 * (y * y * y))))
    y = y * _sigmoid(_dot(y, wglu_ref[...]) + bglu_ref[...])
    y_ref[...] = y.reshape(nb, lt, GROUP_W)


def _s5(u, states, prm, l, prev, nb, lt):
    kern = functools.partial(_s5_kernel, nb=nb, lt=lt)
    return _mixer_call(kern, "s5", l, nb, lt, u, states, prm, prev,
                       [pltpu.VMEM((2 * S5_CH // LANES, nb * _s5_pitch(lt), LANES), F32)])


def _block_diag_in(b):
    dep, g, n, i = b.shape
    eye = jnp.eye(g, dtype=b.dtype)
    return jnp.einsum('lgni,gh->lgihn', b, eye).reshape(dep, g * i, g * n)


def _block_diag_out(c):
    dep, g, o, n = c.shape
    eye = jnp.eye(g, dtype=c.dtype)
    return jnp.einsum('lgon,gh->lgnho', c, eye).reshape(dep, g * n, g * o)


def _prep_params(W):
    dep = W['w_in'].shape[0]
    row = lambda name: W[name].reshape(dep, 1, -1).astype(F32)
    rep = lambda name, n: jnp.repeat(W[name].reshape(dep, -1), n, axis=1).reshape(dep, 1, -1).astype(F32)
    w_in = W['w_in'].astype(BF16)
    o_r = SSD_PROJ
    o_g = o_r + RWKV_PROJ
    o_s = o_g + GLA_PROJ
    w_ssd = jnp.concatenate([w_in[..., :GROUP_W + SSD_CONV_CH],
                             jnp.repeat(w_in[..., GROUP_W + SSD_CONV_CH:o_r], SSD_HEADDIM, axis=2)], axis=2)
    w_rwkv = w_in[..., o_r:o_g]
    gq = o_g + 2 * GLA_QK + GROUP_W
    w_gla = jnp.concatenate([w_in[..., o_g:gq], w_in[..., gq:gq + GLA_GATE_LORA],
                             jnp.zeros((dep, D_MODEL, 128 - GLA_GATE_LORA), BF16),
                             w_in[..., gq + GLA_GATE_LORA:o_s]], axis=2)
    w_s5 = w_in[..., o_s:]
    in_w = (w_ssd, w_rwkv, w_gla, w_s5)

    ssd = (W['ssd_conv_w'].astype(F32), row('ssd_conv_b'), rep('ssd_a_log', SSD_HEADDIM),
           rep('ssd_dt_bias', SSD_HEADDIM), rep('ssd_d', SSD_HEADDIM), row('ssd_norm_g'))

    z = lambda r, c: jnp.zeros((dep, r, c), F32)
    w_lora = jnp.concatenate([
        jnp.concatenate([W['rwkv_w2'], z(32, 2 * GROUP_W)], axis=2),
        jnp.concatenate([z(32, GROUP_W), W['rwkv_a2'], z(32, GROUP_W)], axis=2),
        jnp.concatenate([z(64, 2 * GROUP_W), W['rwkv_g2']], axis=2)], axis=1).astype(BF16)
    rwkv = (row('rwkv_mu'), w_lora, row('rwkv_w0'), row('rwkv_a0'), row('rwkv_k_k'),
            row('rwkv_k_a'), row('rwkv_r_k'), row('rwkv_lnx_w'), row('rwkv_lnx_b'))

    wg2 = jnp.concatenate([W['gla_wg2'], z(128 - GLA_GATE_LORA, GLA_QK)], axis=1).astype(BF16)
    gla = (wg2, row('gla_bg'), row('gla_norm_g'))

    lr = W['s5_lam_re'].astype(F32)
    li = W['s5_lam_im'].astype(F32)
    dts = jnp.exp(W['s5_log_step'].astype(F32))[..., None]
    mag = jnp.exp(lr * dts)
    ar = mag * jnp.cos(li * dts)
    ai = mag * jnp.sin(li * dts)
    den = lr * lr + li * li
    fr = ((ar - 1.0) * lr + ai * li) / den
    fi = (ai * lr - (ar - 1.0) * li) / den
    b_re, b_im = W['s5_b_re'], W['s5_b_im']
    bb_re = fr[..., None] * b_re - fi[..., None] * b_im
    bb_im = fr[..., None] * b_im + fi[..., None] * b_re
    wb = jnp.concatenate([_block_diag_in(bb_re), _block_diag_in(bb_im)], axis=2).astype(BF16)
    wc = jnp.stack([_block_diag_out(W['s5_c_re']), _block_diag_out(W['s5_c_im'])], axis=1).astype(BF16)
    s5 = (wb, wc, ar.reshape(dep, 1, -1), ai.reshape(dep, 1, -1), row('s5_d'),
          W['s5_w_glu'].astype(BF16), row('s5_b_glu'))

    g4 = lambda name: W[name].reshape(dep, 1, 1, D_MODEL).astype(F32)
    return dict(
        in_w=in_w, ssd=ssd, rwkv=rwkv, gla=gla, s5=s5,
        g_mix_pre=g4('g_mix_pre'), g_mix_post=g4('g_mix_post'),
        g_ffn_pre=g4('g_ffn_pre'), g_ffn_post=g4('g_ffn_post'),
        w_out=W['w_out'].reshape(dep, 4, GROUP_W, D_MODEL).astype(BF16),
        w1=W['mlp_w1'].astype(BF16), w2=W['mlp_w2'].astype(BF16))


def _run_group(x, mod, first, st, P, cfg):
    dep, bsz = st['ssd'].shape[:2]
    hist = jnp.pad(st['ssd_conv'], ((0, 0), (0, 0), (HIST - (SSD_CONV - 1), 0), (0, 0)))
    ssd_st = (hist, st['ssd'])
    rwkv_st = (st['rwkv_shift'].reshape(dep, bsz, 1, RWKV_PROJ), st['rwkv'])
    gla_st = (st['gla'],)
    s5_st = (st['s5_re'].reshape(dep, bsz, S5_CH), st['s5_im'].reshape(dep, bsz, S5_CH))
    new = dict(ssd=None, rwkv=None, gla=None, s5=None)
    for l in range(DEPTH):
        p_ssd, p_rwkv, p_gla, p_s5 = _in_proj(x, mod, first, P['g_mix_pre'], P['in_w'], l, *cfg['tok'])
        y_ssd, *new['ssd'] = _ssd(p_ssd, ssd_st, P['ssd'], l, new['ssd'], *cfg['ssd'])
        y_rwkv, *new['rwkv'] = _rwkv(p_rwkv, rwkv_st, P['rwkv'], l, new['rwkv'], *cfg['rwkv'])
        y_gla, *new['gla'] = _gla(p_gla, gla_st, P['gla'], l, new['gla'], *cfg['gla'])
        y_s5, *new['s5'] = _s5(p_s5, s5_st, P['s5'], l, new['s5'], *cfg['s5'])
        x = _out_mlp(x, (y_ssd, y_rwkv, y_gla, y_s5), mod, first, P['g_mix_post'], P['g_ffn_pre'],
                     P['g_ffn_post'], P['w_out'], P['w1'], P['w2'], l, *cfg['tok'])
    hist_new, ssd_new = new['ssd']
    shift_new, rwkv_new = new['rwkv']
    s5r_new, s5i_new = new['s5']
    return x, dict(
        ssd=ssd_new, ssd_conv=hist_new[:, :, HIST - (SSD_CONV - 1):, :],
        rwkv=rwkv_new, rwkv_shift=shift_new.reshape(dep, bsz, RWKV_PROJ), gla=new['gla'][0],
        s5_re=s5r_new.reshape(dep, bsz, S5_GROUPS, S5_STATE),
        s5_im=s5i_new.reshape(dep, bsz, S5_GROUPS, S5_STATE))


_PROMPT_CFG = dict(tok=(1, 512), ssd=(1, 512), rwkv=(1, 512), gla=(1, 512, GLA_CS), s5=(8, 128))
_SAMPLE_CFG = dict(tok=(64, 8), ssd=(32, 8), rwkv=(32, 8), gla=(16, 8, 8), s5=(64, 8))


def kernel(x_prompt, x_sample, c_prompt, c_sample, state_ssd, state_ssd_conv, state_rwkv, state_rwkv_shift, state_gla, state_s5_re, state_s5_im, w_ada, b_ada, g_mix_pre, g_mix_post, g_ffn_pre, g_ffn_post, w_in, w_out, ssd_conv_w, ssd_conv_b, ssd_a_log, ssd_dt_bias, ssd_d, ssd_norm_g, rwkv_mu, rwkv_w0, rwkv_w2, rwkv_a0, rwkv_a2, rwkv_g2, rwkv_k_k, rwkv_k_a, rwkv_r_k, rwkv_lnx_w, rwkv_lnx_b, gla_wg2, gla_bg, gla_norm_g, s5_lam_re, s5_lam_im, s5_log_step, s5_b_re, s5_b_im, s5_c_re, s5_c_im, s5_d, s5_w_glu, s5_b_glu, mlp_w1, mlp_w2):
    W = dict(w_in=w_in, w_out=w_out, g_mix_pre=g_mix_pre, g_mix_post=g_mix_post,
             g_ffn_pre=g_ffn_pre, g_ffn_post=g_ffn_post,
             ssd_conv_w=ssd_conv_w, ssd_conv_b=ssd_conv_b, ssd_a_log=ssd_a_log,
             ssd_dt_bias=ssd_dt_bias, ssd_d=ssd_d, ssd_norm_g=ssd_norm_g,
             rwkv_mu=rwkv_mu, rwkv_w0=rwkv_w0, rwkv_w2=rwkv_w2, rwkv_a0=rwkv_a0, rwkv_a2=rwkv_a2,
             rwkv_g2=rwkv_g2, rwkv_k_k=rwkv_k_k, rwkv_k_a=rwkv_k_a, rwkv_r_k=rwkv_r_k,
             rwkv_lnx_w=rwkv_lnx_w, rwkv_lnx_b=rwkv_lnx_b,
             gla_wg2=gla_wg2, gla_bg=gla_bg, gla_norm_g=gla_norm_g,
             s5_lam_re=s5_lam_re, s5_lam_im=s5_lam_im, s5_log_step=s5_log_step,
             s5_b_re=s5_b_re, s5_b_im=s5_b_im, s5_c_re=s5_c_re, s5_c_im=s5_c_im,
             s5_d=s5_d, s5_w_glu=s5_w_glu, s5_b_glu=s5_b_glu, mlp_w1=mlp_w1, mlp_w2=mlp_w2)
    P = _prep_params(W)

    nbp, nbs = x_prompt.shape[0], x_sample.shape[0]
    mod = _ada(jnp.concatenate([c_sample, c_prompt], axis=0), w_ada, b_ada)
    mod = mod.reshape(DEPTH, nbs + nbp, N_MOD, D_MODEL)

    st_sample = dict(ssd=state_ssd, ssd_conv=state_ssd_conv, rwkv=state_rwkv,
                     rwkv_shift=state_rwkv_shift, gla=state_gla, s5_re=state_s5_re, s5_im=state_s5_im)
    st_prompt = {n: jnp.zeros((DEPTH, nbp) + v.shape[2:], v.dtype) for n, v in st_sample.items()}

    y_prompt, sp = _run_group(x_prompt, mod, nbs, st_prompt, P, _PROMPT_CFG)
    y_sample, ss = _run_group(x_sample, mod, 0, st_sample, P, _SAMPLE_CFG)
    return (y_prompt, y_sample,
            sp['ssd'], ss['ssd'], sp['ssd_conv'], ss['ssd_conv'],
            sp['rwkv'], ss['rwkv'], sp['rwkv_shift'], ss['rwkv_shift'],
            sp['gla'], ss['gla'], sp['s5_re'], ss['s5_re'], sp['s5_im'], ss['s5_im'])
```

```python
import functools
import math

import jax
import jax.numpy as jnp
from jax import lax
from jax.experimental import pallas as pl
from jax.experimental.pallas import tpu as pltpu

F32 = jnp.float32
BF16 = jnp.bfloat16

D_MODEL = 1024
DEPTH = 2
GROUP_W = 256
D_FF = 4096
N_MOD = 6
EPS = 1e-6

SSD_HEADS = 4
SSD_HEADDIM = 64
SSD_STATE = 64
SSD_NGROUPS = 2
SSD_BC = SSD_NGROUPS * SSD_STATE
SSD_CONV = 4
SSD_CONV_CH = GROUP_W + 2 * SSD_BC
SSD_PROJ = GROUP_W + SSD_CONV_CH + SSD_HEADS
SSD_PW = GROUP_W + SSD_CONV_CH + GROUP_W

RWKV_HEADS = 4
RWKV_HEADDIM = 64
RWKV_LORA = 128
RWKV_PROJ = 3 * GROUP_W + RWKV_LORA
RWKV_LN_EPS = 64e-5

GLA_HEADS = 4
GLA_DK = 32
GLA_DV = 64
GLA_GATE_LORA = 16
GLA_TAU = 16.0
GLA_QK = GLA_HEADS * GLA_DK
GLA_PROJ = 2 * GLA_QK + GROUP_W + GLA_GATE_LORA + GROUP_W
GLA_PW = 2 * GLA_QK + GROUP_W + 128 + GROUP_W

S5_GROUP = 16
S5_GROUPS = 16
S5_STATE = 64
S5_CH = S5_GROUPS * S5_STATE
S5_PROJ = GROUP_W

LANES = 128
HIST = 8
CHUNK = 64
GLA_CS = 32
VMEM_LIMIT = 56 * 1024 * 1024


def _dot(a, b, dims=(((1,), (0,)), ((), ()))):
    return lax.dot_general(a.astype(BF16), b.astype(BF16), dims, preferred_element_type=F32)


_NT = (((1,), (1,)), ((), ()))
_TN = (((0,), (0,)), ((), ()))
_NN = (((1,), (0,)), ((), ()))


def _split2(a):
    hi = a.astype(BF16)
    lo = (a - hi.astype(F32)).astype(BF16)
    return hi, lo


def _dot3(a, b, dims=_NN):
    ah, al = _split2(a)
    bh, bl = _split2(b)
    d = lambda x, y: lax.dot_general(x, y, dims, preferred_element_type=F32)
    return d(ah, bh) + (d(ah, bl) + d(al, bh))


_dot_a = _dot
_dot_inv = _dot
_dot_app = _dot
_dot_st = _dot


def _dot_exact_lhs(m, x, dims=_NN):
    mb = m.astype(BF16)
    x1 = x.astype(BF16)
    r1 = x - x1.astype(F32)
    x2 = r1.astype(BF16)
    x3 = (r1 - x2.astype(F32)).astype(BF16)
    d = lambda y: lax.dot_general(mb, y, dims, preferred_element_type=F32)
    return d(x1) + (d(x2) + d(x3))


def _dot_exact_rhs(x, m):
    mb = m.astype(BF16)
    x1 = x.astype(BF16)
    r1 = x - x1.astype(F32)
    x2 = r1.astype(BF16)
    x3 = (r1 - x2.astype(F32)).astype(BF16)
    d = lambda y: lax.dot_general(y, mb, _NN, preferred_element_type=F32)
    return d(x1) + (d(x2) + d(x3))


def _segsum(x, width):
    n = x.shape[-1]
    r = lax.broadcasted_iota(jnp.int32, (n, n), 0) // width
    c = lax.broadcasted_iota(jnp.int32, (n, n), 1) // width
    j = jnp.where(r == c, 1.0, 0.0).astype(BF16)
    hi, lo = _split2(x)
    d = lambda y: lax.dot_general(y, j, _NN, preferred_element_type=F32)
    return d(hi) + d(lo)


def _sigmoid(x):
    return 1.0 / (1.0 + jnp.exp(-x))


def _silu(x):
    return x * _sigmoid(x)


def _softplus(x):
    return jnp.maximum(x, 0.0) + jnp.log1p(jnp.exp(-jnp.abs(x)))


def _rms(x, g):
    return x * lax.rsqrt(jnp.mean(x * x, axis=-1, keepdims=True) + EPS) * g


def _chunk_masks(rows, lt):
    r = lax.broadcasted_iota(jnp.int32, (rows, rows), 0)
    c = lax.broadcasted_iota(jnp.int32, (rows, rows), 1)
    same = (r // lt) == (c // lt)
    return same & (c <= r), same & (c < r)


def _unit_cumsum(x, unit):
    incl, _ = _chunk_masks(CHUNK, unit)
    tri = jnp.where(incl, 1.0, 0.0)
    parts = [_dot_exact_lhs(tri, x[j * CHUNK:(j + 1) * CHUNK]) for j in range(x.shape[0] // CHUNK)]
    return parts[0] if len(parts) == 1 else jnp.concatenate(parts, axis=0)


def _lsel(a, l, **kw):
    shape = a.shape[1:]
    return pl.BlockSpec((None,) + shape, lambda i, c: (l,) + (0,) * len(shape), **kw)


def _lseq(a, l, nb):
    shape = a.shape[2:]
    return pl.BlockSpec((None, nb) + shape, lambda i, c: (l, i) + (0,) * len(shape))


def _mixer_call(kern, name, l, nb, lt, p, states, params, prev, scratch):
    bsz, seq, width = p.shape
    n_in = 1 + len(states) + len(params)
    n_st = len(states)
    alias = tuple(prev) if prev is not None else ()
    first = prev is None

    def body(*refs):
        ins, outs = refs[:n_in], refs[n_in + len(alias):]
        y_ref, st_refs, scratch_refs = outs[0], outs[1:1 + n_st], outs[1 + n_st:]
        if first:
            @pl.when(pl.program_id(1) == 0)
            def _():
                for r in st_refs:
                    r[1:] = jnp.zeros((r.shape[0] - 1,) + r.shape[1:], F32)
            st_refs = [r.at[0] for r in st_refs]
        return kern(*ins, y_ref, *st_refs, *scratch_refs)

    def st_out_spec(s):
        if not first:
            return _lseq(s, l, nb)
        shape = s.shape[2:]
        return pl.BlockSpec((s.shape[0], nb) + shape, lambda i, c: (0, i) + (0,) * len(shape))

    tok = lambda w: pl.BlockSpec((nb, lt, w), lambda i, c: (i, c, 0))
    return pl.pallas_call(
        body,
        out_shape=(jax.ShapeDtypeStruct((bsz, seq, GROUP_W), F32),)
        + tuple(jax.ShapeDtypeStruct(s.shape, F32) for s in states),
        grid=(bsz // nb, seq // lt),
        in_specs=[tok(width)] + [_lseq(s, l, nb) for s in states] + [_lsel(a, l) for a in params]
        + [pl.BlockSpec(memory_space=pl.ANY)] * len(alias),
        out_specs=(tok(GROUP_W),) + tuple(st_out_spec(s) for s in states),
        scratch_shapes=scratch,
        input_output_aliases={n_in + k: 1 + k for k in range(len(alias))},
        compiler_params=pltpu.CompilerParams(
            dimension_semantics=("parallel", "arbitrary"), vmem_limit_bytes=VMEM_LIMIT),
        name=name,
    )(p, *states, *params, *alias)


def _bcast_last(x, nb, lt):
    c = x.shape[-1]
    last = x.reshape(nb, lt, c)[:, lt - 1:lt, :]
    return last, jnp.broadcast_to(last, (nb, lt, c)).reshape(nb * lt, c)


def _ada_kernel(c_ref, w_ref, b_ref, o_ref):
    c = c_ref[...]
    o_ref[0] = _dot(_silu(c), w_ref[0]) + b_ref[0]


def _ada(c_all, w_ada, b_ada):
    rows = c_all.shape[0]
    n = N_MOD * D_MODEL
    tn = 1536
    return pl.pallas_call(
        _ada_kernel,
        out_shape=jax.ShapeDtypeStruct((DEPTH, rows, n), F32),
        grid=(DEPTH, n // tn),
        in_specs=[
            pl.BlockSpec((rows, D_MODEL), lambda l, j: (0, 0)),
            pl.BlockSpec((1, D_MODEL, tn), lambda l, j: (l, 0, j)),
            pl.BlockSpec((1, 1, tn), lambda l, j: (l, 0, j)),
        ],
        out_specs=pl.BlockSpec((1, rows, tn), lambda l, j: (l, 0, j)),
        compiler_params=pltpu.CompilerParams(
            dimension_semantics=("parallel", "parallel"), vmem_limit_bytes=VMEM_LIMIT),
        name="ada",
    )(c_all, w_ada, b_ada.reshape(DEPTH, 1, n))


def _in_kernel(x_ref, mod_ref, g_ref, wssd_ref, wrwkv_ref, wgla_ref, ws5_ref,
               ossd_ref, orwkv_ref, ogla_ref, os5_ref):
    nb, lt, d = x_ref.shape
    m = mod_ref[...]
    h = _rms(x_ref[...], g_ref[...]) * (1.0 + m[:, 1:2, :]) + m[:, 0:1, :]
    hb = h.reshape(nb * lt, d).astype(BF16)
    for w_ref, o_ref in ((wssd_ref, ossd_ref), (wrwkv_ref, orwkv_ref),
                         (wgla_ref, ogla_ref), (ws5_ref, os5_ref)):
        o_ref[...] = jnp.dot(hb, w_ref[...], preferred_element_type=F32).reshape(o_ref.shape)


def _mod_spec(mod, l, first, nb):
    return pl.BlockSpec((None, nb, N_MOD, mod.shape[-1]), lambda i, j: (l, first // nb + i, 0, 0))


def _in_proj(x, mod, first, g, wts, l, nb, lt):
    bsz, seq, d = x.shape
    widths = (SSD_PW, RWKV_PROJ, GLA_PW, S5_PROJ)
    return pl.pallas_call(
        _in_kernel,
        out_shape=tuple(jax.ShapeDtypeStruct((bsz, seq, w), F32) for w in widths),
        grid=(bsz // nb, seq // lt),
        in_specs=[pl.BlockSpec((nb, lt, d), lambda i, j: (i, j, 0)), _mod_spec(mod, l, first, nb),
                  _lsel(g, l)] + [_lsel(w, l) for w in wts],
        out_specs=tuple(pl.BlockSpec((nb, lt, w), lambda i, j: (i, j, 0)) for w in widths),
        compiler_params=pltpu.CompilerParams(
            dimension_semantics=("parallel", "parallel"), vmem_limit_bytes=VMEM_LIMIT),
        name="in_proj",
    )(x, mod, g, *wts)


def _out_mlp_kernel(x_ref, y0_ref, y1_ref, y2_ref, y3_ref, mod_ref, gpost_ref, gpre_ref, gfpost_ref,
                    wout_ref, w1_ref, w2_ref, o_ref):
    nb, lt, d = x_ref.shape
    rows = nb * lt
    m = mod_ref[...]
    mix = None
    for i, y_ref in enumerate((y0_ref, y1_ref, y2_ref, y3_ref)):
        t = jnp.dot(y_ref[...].reshape(rows, GROUP_W).astype(BF16), wout_ref[i],
                    preferred_element_type=F32)
        mix = t if mix is None else mix + t
    x1 = x_ref[...] + m[:, 2:3, :] * _rms(mix, gpost_ref[0]).reshape(nb, lt, d)
    h = _rms(x1, gpre_ref[...]) * (1.0 + m[:, 4:5, :]) + m[:, 3:4, :]
    hb = h.reshape(rows, d).astype(BF16)
    acc = None
    step = 1024
    for j in range(D_FF // step):
        a = jnp.dot(hb, w1_ref[:, j * step:(j + 1) * step], preferred_element_type=F32)
        a = jnp.square(jnp.maximum(a, 0.0)).astype(BF16)
        t = jnp.dot(a, w2_ref[j * step:(j + 1) * step, :], preferred_element_type=F32)
        acc = t if acc is None else acc + t
    o_ref[...] = x1 + m[:, 5:6, :] * _rms(acc, gfpost_ref[0]).reshape(nb, lt, d)


def _out_mlp(x, ys, mod, first, gpost, gpre, gfpost, wout, w1, w2, l, nb, lt):
    bsz, seq, d = x.shape
    tok = lambda i, j: (i, j, 0)
    one = pl.Buffered(1)
    return pl.pallas_call(
        _out_mlp_kernel,
        out_shape=jax.ShapeDtypeStruct((bsz, seq, d), F32),
        grid=(bsz // nb, seq // lt),
        in_specs=[pl.BlockSpec((nb, lt, d), tok)]
        + [pl.BlockSpec((nb, lt, GROUP_W), tok)] * 4
        + [_mod_spec(mod, l, first, nb)]
        + [_lsel(g, l) for g in (gpost, gpre, gfpost)]
        + [_lsel(w, l, pipeline_mode=one) for w in (wout, w1, w2)],
        out_specs=pl.BlockSpec((nb, lt, d), tok),
        compiler_params=pltpu.CompilerParams(
            dimension_semantics=("parallel", "parallel"), vmem_limit_bytes=VMEM_LIMIT),
        name="out_mlp",
    )(x, *ys, mod, gpost, gpre, gfpost, wout, w1, w2)


def _ssd_kernel(p_ref, hist_ref, s0_ref, cw_ref, cb_ref, alog_ref, dtb_ref, dsk_ref, ng_ref,
                y_ref, hout_ref, sout_ref, ext_ref, yscr_ref, *, nb, lt):
    rows = nb * lt
    hd = SSD_HEADDIM

    @pl.when(pl.program_id(1) == 0)
    def _():
        ext_ref[:, 0:HIST, :] = hist_ref[...]
        sout_ref[...] = s0_ref[...]

    p = p_ref[...]
    ext_ref[:, HIST:HIST + lt, :] = p[:, :, GROUP_W:GROUP_W + SSD_CONV_CH]
    conv = cb_ref[...]
    for j in range(SSD_CONV):
        o = HIST - (SSD_CONV - 1) + j
        conv = conv + ext_ref[:, o:o + lt, :] * cw_ref[j:j + 1, :]
    tail = ext_ref[:, lt:lt + HIST, :]
    ext_ref[:, 0:HIST, :] = tail
    hout_ref[...] = tail

    xbc = _silu(conv).reshape(rows, SSD_CONV_CH)
    xs = xbc[:, 0:GROUP_W]
    bm = xbc[:, GROUP_W:GROUP_W + SSD_BC]
    cm = xbc[:, GROUP_W + SSD_BC:]
    z = p[:, :, 0:GROUP_W].reshape(rows, GROUP_W)
    dt = _softplus(p[:, :, GROUP_W + SSD_CONV_CH:].reshape(rows, GROUP_W) + dtb_ref[...])
    a = dt * (-jnp.exp(alog_ref[...]))
    unit = lt if nb > 1 else CHUNK
    units = rows // unit
    per_chunk = CHUNK // unit
    cum = _unit_cumsum(a, unit)
    last, last_b = _bcast_last(cum, units, unit)
    xdt = xs * dt
    xw = xdt * jnp.exp(last_b - cum)
    ecum = jnp.exp(cum)
    elast = jnp.exp(last)
    ones_row = jnp.full((CHUNK, hd), 1.0 / hd, F32)
    cmask, _ = _chunk_masks(CHUNK, unit)
    heads = range(SSD_HEADS)
    hs = lambda h: slice(h * hd, (h + 1) * hd)
    gs = lambda h: slice((h // 2) * SSD_STATE, (h // 2 + 1) * SSD_STATE)
    cs = lambda j: slice(j * CHUNK, (j + 1) * CHUNK)
    us = lambda u: slice(u * unit, (u + 1) * unit)

    yh = {}
    for j in range(rows // CHUNK):
        cb = [_dot(cm[cs(j), gs(2 * g)], bm[cs(j), gs(2 * g)], _NT) for g in range(SSD_NGROUPS)]
        for h in heads:
            ccol = cum[cs(j), hs(h)]
            crow = _dot_exact_lhs(ones_row, ccol, _NT)
            decay = jnp.exp(jnp.where(cmask, ccol - crow, -jnp.inf))
            yh[j, h] = _dot(cb[h // 2] * decay, xdt[cs(j), hs(h)])
    ds = {(u, h): _dot(xw[us(u), hs(h)], bm[us(u), gs(h)], _TN) for u in range(units) for h in heads}

    before = {}
    for h in heads:
        if nb > 1:
            for u in range(units):
                before[u, h] = sout_ref[u, h]
                sout_ref[u, h] = before[u, h] * elast[u][:, hs(h)] + ds[u, h]
        else:
            s_cur = sout_ref[0, h]
            for u in range(units):
                before[u, h] = s_cur
                s_cur = s_cur * elast[u][:, hs(h)] + ds[u, h]
            sout_ref[0, h] = s_cur

    for j in range(rows // CHUNK):
        for h in heads:
            inter = [_dot(cm[us(u), gs(h)], before[u, h], _NT)
                     for u in range(j * per_chunk, (j + 1) * per_chunk)]
            inter = inter[0] if per_chunk == 1 else jnp.concatenate(inter, axis=0)
            yscr_ref[cs(j), hs(h)] = yh[j, h] + inter * ecum[cs(j), hs(h)]

    y = yscr_ref[...] + dsk_ref[...] * xs
    y = y * _silu(z)
    y = y * lax.rsqrt(_segsum(y * y, hd) * (1.0 / hd) + EPS) * ng_ref[...]
    y_ref[...] = y.reshape(nb, lt, GROUP_W)


def _ssd(p, states, prm, l, prev, nb, lt):
    assert CHUNK == SSD_HEADDIM
    assert (nb > 1 and (nb * lt) % CHUNK == 0 and CHUNK % lt == 0) or (nb == 1 and lt % CHUNK == 0)
    kern = functools.partial(_ssd_kernel, nb=nb, lt=lt)
    return _mixer_call(kern, "ssd", l, nb, lt, p, states, prm, prev,
                       [pltpu.VMEM((nb, HIST + lt, SSD_CONV_CH), F32),
                        pltpu.VMEM((nb * lt, GROUP_W), F32)])


def _unit_lower_inverse(a2, n, levels, dot):
    ri = lax.broadcasted_iota(jnp.int32, (n, n), 0)
    ci = lax.broadcasted_iota(jnp.int32, (n, n), 1)
    x = [jnp.where(ri == ci, 1.0, 0.0) - jnp.where((ri % 2 == 1) & (ci == ri - 1), a, 0.0) for a in a2]
    s = 2
    while s < levels:
        off = ((ri // s) % 2 == 1) & ((ci // s) == (ri // s) - 1)
        t = [dot(xi, jnp.where(off, a, 0.0)) for xi, a in zip(x, a2)]
        x = [xi - dot(ti, xi) for xi, ti in zip(x, t)]
        s *= 2
    return x


def _rwkv_chunk_multi(ops, sout_ref, yscr_ref, nb, lt):
    hd = RWKV_HEADDIM
    per_chunk = CHUNK // lt
    idx = [(c, h) for c in range(nb // per_chunk) for h in range(RWKV_HEADS)]
    n = range(len(idx))
    incl, strict = _chunk_masks(CHUNK, lt)
    mask2 = jnp.concatenate([strict, incl], axis=0)
    top = lambda m: m[0:CHUNK]
    bot = lambda m: m[CHUNK:2 * CHUNK]
    bl = lambda name: [ops[name][c * CHUNK:(c + 1) * CHUNK, h * hd:(h + 1) * hd] for c, h in idx]
    rt, kt, k2t, bt, k2g, bg, v = (bl(name) for name in ('rt', 'kt', 'k2t', 'bt', 'k2g', 'bg', 'v'))
    kr = [jnp.concatenate([kt[j], rt[j]], axis=0) for j in n]
    m1 = [jnp.where(mask2, _dot_a(kr[j], k2t[j], _NT), 0.0) for j in n]
    m2 = [jnp.where(mask2, _dot_a(kr[j], bt[j], _NT), 0.0) for j in n]
    x = _unit_lower_inverse([top(m) for m in m2], CHUNK, lt, _dot_inv)
    m1v = [_dot_app(m1[j], v[j]) for j in n]
    seq = lambda i: slice(i * lt, (i + 1) * lt)
    seqs = range(per_chunk)
    s_old = [[sout_ref[c * per_chunk + i, h] for i in seqs] for c, h in idx]
    pq = [[_dot_st(jnp.concatenate([kt[j][seq(i)], rt[j][seq(i)]], axis=0), s_old[j][i], _NT)
           for i in seqs] for j in n]
    p0 = [jnp.concatenate([pq[j][i][0:lt] for i in seqs], axis=0) for j in n]
    q0 = [jnp.concatenate([pq[j][i][lt:2 * lt] for i in seqs], axis=0) for j in n]
    u = [_dot_app(x[j], p0[j] + top(m1v[j])) for j in n]
    l2u = [_dot_app(bot(m2[j]), u[j]) for j in n]
    for j, (c, h) in enumerate(idx):
        yscr_ref[c * CHUNK:(c + 1) * CHUNK, h * hd:(h + 1) * hd] = q0[j] + bot(m1v[j]) - l2u[j]
        for i in seqs:
            sq = c * per_chunk + i
            sout_ref[sq, h] = (s_old[j][i] * ops['elast'][sq][:, h * hd:(h + 1) * hd]
                               + _dot_st(v[j][seq(i)], k2g[j][seq(i)], _TN)
                               - _dot_st(u[j][seq(i)], bg[j][seq(i)], _TN))


def _rwkv_chunks_seq(ops, state, yscr_ref, nc):
    hd = RWKV_HEADDIM
    pw = 2 * hd
    pairs = RWKV_HEADS // 2
    idx = [(c, q) for c in range(nc) for q in range(pairs)]
    n = range(len(idx))
    top = lambda m: m[0:CHUNK]
    bot = lambda m: m[CHUNK:2 * CHUNK]
    ri = lax.broadcasted_iota(jnp.int32, (CHUNK, pw), 0)
    cl = lax.broadcasted_iota(jnp.int32, (CHUNK, pw), 1) % hd
    left = lax.broadcasted_iota(jnp.int32, (CHUNK, pw), 1) < hd
    strict, incl = cl < ri, cl <= ri
    mask2 = jnp.concatenate([strict, incl], axis=0)
    eye = jnp.where(cl == ri, 1.0, 0.0)
    bd = lambda p: jnp.concatenate([jnp.where(left, p, 0.0), jnp.where(left, 0.0, p)], axis=0)
    on_diag = jnp.concatenate([left, ~left], axis=0)
    r2 = lax.broadcasted_iota(jnp.int32, (pw, pw), 0)
    c2 = lax.broadcasted_iota(jnp.int32, (pw, pw), 1)
    eye2 = jnp.where(r2 == c2, 1.0, 0.0)
    diag_pack = lambda m: jnp.where(left, top(m), bot(m))

    bl = lambda name: [ops[name][c * CHUNK:(c + 1) * CHUNK, q * pw:(q + 1) * pw] for c, q in idx]
    rt, kt, k2t, bt, k2g, bg, v = (bl(name) for name in ('rt', 'kt', 'k2t', 'bt', 'k2g', 'bg', 'v'))
    kr = [jnp.concatenate([kt[j], rt[j]], axis=0) for j in n]
    m1 = [jnp.where(mask2, _dot_a(kr[j], bd(k2t[j]), _NT), 0.0) for j in n]
    m2 = [jnp.where(mask2, _dot_a(kr[j], bd(bt[j]), _NT), 0.0) for j in n]
    x = [eye - jnp.where((ri % 2 == 1) & (cl == ri - 1), top(m), 0.0) for m in m2]
    s = 2
    while s < CHUNK:
        off = ((ri // s) % 2 == 1) & ((cl // s) == (ri // s) - 1)
        t = [_dot_inv(x[j], bd(jnp.where(off, top(m2[j]), 0.0))) for j in n]
        x = [x[j] - _dot_inv(t[j], bd(x[j])) for j in n]
        s *= 2
    m1v = [_dot_app(m1[j], bd(v[j])) for j in n]
    kp = [_dot_app(x[j], bd(kt[j])) for j in n]
    ul = [_dot_app(x[j], bd(top(m1v[j]))) for j in n]
    rp = [rt[j] - _dot_app(bot(m2[j]), bd(kp[j])) for j in n]
    yl = [bot(m1v[j]) - _dot_app(bot(m2[j]), bd(ul[j])) for j in n]
    mc = [eye2 * ops['elast'][c][:, q * pw:(q + 1) * pw]
          - jnp.where(on_diag, _dot_st(kp[j], bg[j], _TN), 0.0) for j, (c, q) in enumerate(idx)]
    dc = [diag_pack(_dot_st(v[j], k2g[j], _TN) - _dot_st(ul[j], bg[j], _TN)) for j in n]
    for c in range(nc):
        js = [c * pairs + q for q in range(pairs)]
        ys = [_dot_st(rp[j], bd(state[q]), _NT) + yl[j] for q, j in enumerate(js)]
        state = [_dot_st(state[q], mc[j]) + dc[j] for q, j in enumerate(js)]
        for q in range(pairs):
            yscr_ref[c * CHUNK:(c + 1) * CHUNK, q * pw:(q + 1) * pw] = ys[q]
    return state


def _rwkv_kernel(p_ref, sh0_ref, s0_ref, mu_ref, wl_ref, w0_ref, a0_ref, kk_ref, ka_ref, rk_ref,
                 lnw_ref, lnb_ref, y_ref, shout_ref, sout_ref, ext_ref, yscr_ref, *, nb, lt):
    rows = nb * lt
    hd = RWKV_HEADDIM
    gw = GROUP_W

    @pl.when(pl.program_id(1) == 0)
    def _():
        ext_ref[:, HIST - 1:HIST, :] = sh0_ref[...]
        sout_ref[...] = s0_ref[...]

    p = p_ref[...]
    ext_ref[:, HIST:HIST + lt, :] = p
    prev = ext_ref[:, HIST - 1:HIST - 1 + lt, :].reshape(rows, RWKV_PROJ)
    last_p = p[:, lt - 1:lt, :]
    ext_ref[:, HIST - 1:HIST, :] = last_p
    shout_ref[...] = last_p
    p = p.reshape(rows, RWKV_PROJ)

    pm = p + (prev - p) * mu_ref[...]
    r = pm[:, 0:gw]
    k = pm[:, gw:2 * gw]
    v = pm[:, 2 * gw:3 * gw]
    lo = pm[:, 3 * gw:]
    lane = lax.broadcasted_iota(jnp.int32, (rows, RWKV_LORA), 1)
    t = jnp.where(lane < 32, jnp.tanh(lo), jnp.where(lane < 64, lo, _sigmoid(lo)))
    lora = _dot(t, wl_ref[...])
    wlog = -_softplus(-(w0_ref[...] + lora[:, 0:gw])) - 0.5
    lw = -jnp.exp(wlog)
    a = _sigmoid(a0_ref[...] + lora[:, gw:2 * gw])
    gate = lora[:, 2 * gw:]
    kk = k * kk_ref[...]
    kk = kk / jnp.maximum(jnp.sqrt(_segsum(kk * kk, hd)), 1e-12)
    k2 = k * (1.0 + (a - 1.0) * ka_ref[...])
    beta = kk * a

    unit = lt if nb > 1 else CHUNK
    gcum = _unit_cumsum(lw, unit)
    glast, glast_b = _bcast_last(gcum, rows // unit, unit)
    eg = jnp.exp(gcum)
    einv = jnp.exp(-gcum)
    etail = jnp.exp(glast_b - gcum)
    ops = dict(rt=r * eg, kt=kk * jnp.exp(gcum - lw), k2t=k2 * einv, bt=beta * einv,
               k2g=k2 * etail, bg=beta * etail, v=v, elast=jnp.exp(glast))
    if nb > 1:
        _rwkv_chunk_multi(ops, sout_ref, yscr_ref, nb, lt)
    else:
        pairs = range(RWKV_HEADS // 2)
        state = [jnp.concatenate([sout_ref[0, 2 * q], sout_ref[0, 2 * q + 1]], axis=1) for q in pairs]
        state = _rwkv_chunks_seq(ops, state, yscr_ref, rows // CHUNK)
        for q in pairs:
            sout_ref[0, 2 * q] = state[q][:, 0:hd]
            sout_ref[0, 2 * q + 1] = state[q][:, hd:2 * hd]

    y = yscr_ref[...]
    mean = _segsum(y, hd) * (1.0 / hd)
    dlt = y - mean
    var = _segsum(dlt * dlt, hd) * (1.0 / hd)
    yn = dlt * lax.rsqrt(var + RWKV_LN_EPS) * lnw_ref[...] + lnb_ref[...]
    bonus = _segsum(r * k2 * rk_ref[...], hd) * v
    y_ref[...] = ((yn + bonus) * gate).reshape(nb, lt, gw)


def _rwkv(p, states, prm, l, prev, nb, lt):
    assert (nb > 1 and (nb * lt) % CHUNK == 0 and CHUNK % lt == 0) or (nb == 1 and lt % CHUNK == 0)
    kern = functools.partial(_rwkv_kernel, nb=nb, lt=lt)
    return _mixer_call(kern, "rwkv", l, nb, lt, p, states, prm, prev,
                       [pltpu.VMEM((nb, HIST + lt, RWKV_PROJ), F32),
                        pltpu.VMEM((nb * lt, GROUP_W), F32)])


def _gla_kernel(p_ref, s0_ref, wg2_ref, bg_ref, ng_ref, y_ref, sout_ref, yscr_ref, *, nb, lt, cs):
    rows = nb * lt
    dk, dv = GLA_DK, GLA_DV
    qk = GLA_QK
    units = rows // cs
    per_group = GLA_CS // cs

    @pl.when(pl.program_id(1) == 0)
    def _():
        sout_ref[...] = s0_ref[...]

    p = p_ref[...].reshape(rows, GLA_PW)
    q = p[:, 0:qk] * (dk ** -0.5)
    k = p[:, qk:2 * qk]
    v = p[:, 2 * qk:2 * qk + GROUP_W]
    dg = p[:, 2 * qk + GROUP_W:2 * qk + GROUP_W + 128]
    og = p[:, 2 * qk + GROUP_W + 128:]
    lg = -_softplus(-(_dot(dg, wg2_ref[...]) + bg_ref[...])) * (1.0 / GLA_TAU)
    b = _unit_cumsum(lg, cs)
    bl, bl_b = _bcast_last(b, units, cs)
    qd = q * jnp.exp(b)
    kd = k * jnp.exp(-b)
    kw = k * jnp.exp(bl_b - b)
    ebl = jnp.exp(bl)
    gmask, _ = _chunk_masks(GLA_CS, cs)

    heads = range(GLA_HEADS)
    ks = lambda h: slice(h * dk, (h + 1) * dk)
    vs = lambda h: slice(h * dv, (h + 1) * dv)
    gs = lambda g: slice(g * GLA_CS, (g + 1) * GLA_CS)
    us = lambda u: slice(u * cs, (u + 1) * cs)

    yh = {}
    for g in range(rows // GLA_CS):
        for h in heads:
            att = jnp.where(gmask, _dot(qd[gs(g), ks(h)], kd[gs(g), ks(h)], _NT), 0.0)
            yh[g, h] = _dot(att, v[gs(g), vs(h)])
    ds = {(u, h): _dot(kw[us(u), ks(h)], v[us(u), vs(h)], _TN) for u in range(units) for h in heads}
    ri = lax.broadcasted_iota(jnp.int32, (dk, dk), 0)
    ci = lax.broadcasted_iota(jnp.int32, (dk, dk), 1)
    ones_kv = jnp.ones((dk, dv), F32)
    ecol = {(u, h): _dot_exact_rhs(jnp.where(ri == ci, ebl[u][:, ks(h)], 0.0), ones_kv)
            for u in range(units) for h in heads}

    before = {}
    for h in heads:
        if nb > 1:
            for u in range(units):
                before[u, h] = sout_ref[u, h]
                sout_ref[u, h] = before[u, h] * ecol[u, h] + ds[u, h]
        else:
            s_cur = sout_ref[0, h]
            for u in range(units):
                before[u, h] = s_cur
                s_cur = s_cur * ecol[u, h] + ds[u, h]
            sout_ref[0, h] = s_cur

    for g in range(rows // GLA_CS):
        for h in heads:
            inter = [_dot(qd[us(u), ks(h)], before[u, h])
                     for u in range(g * per_group, (g + 1) * per_group)]
            inter = inter[0] if per_group == 1 else jnp.concatenate(inter, axis=0)
            yscr_ref[gs(g), vs(h)] = yh[g, h] + inter

    y = yscr_ref[...]
    y = y * lax.rsqrt(_segsum(y * y, dv) * (1.0 / dv) + EPS) * ng_ref[...]
    y_ref[...] = (y * _silu(og)).reshape(nb, lt, GROUP_W)


def _gla(p, states, prm, l, prev, nb, lt, cs):
    assert (nb == 1 and cs == GLA_CS and lt % cs == 0) or (
        lt == cs and (nb * cs) % GLA_CS == 0 and GLA_CS % cs == 0)
    kern = functools.partial(_gla_kernel, nb=nb, lt=lt, cs=cs)
    return _mixer_call(kern, "gla", l, nb, lt, p, states, prm, prev,
                       [pltpu.VMEM((nb * lt, GROUP_W), F32)])


def _s5_pitch(lt):
    return lt if (lt // 8) % 2 == 1 else lt + 8


def _s5_kernel(u_ref, hr0_ref, hi0_ref, wb_ref, wc_ref, ar_ref, ai_ref, d_ref, wglu_ref, bglu_ref,
               y_ref, hr_ref, hi_ref, h_ref, *, nb, lt):
    rows = nb * lt
    n = S5_CH

    @pl.when(pl.program_id(1) == 0)
    def _():
        hr_ref[...] = hr0_ref[...]
        hi_ref[...] = hi0_ref[...]

    u = u_ref[...].reshape(rows, GROUP_W)
    bu = _dot(u, wb_ref[...])
    nt = n // LANES
    lanes = lambda x, j: x[:, j * LANES:(j + 1) * LANES]
    pitch = _s5_pitch(lt)
    seq_rows = lambda i: slice(i * pitch, i * pitch + lt)
    for j in range(2 * nt):
        for i in range(nb):
            h_ref[j, seq_rows(i), :] = lanes(bu, j)[i * lt:(i + 1) * lt]
    ar = [jnp.broadcast_to(lanes(ar_ref[...], j), (nb, LANES)) for j in range(nt)]
    ai = [jnp.broadcast_to(lanes(ai_ref[...], j), (nb, LANES)) for j in range(nt)]
    hr = [lanes(hr_ref[...], j) for j in range(nt)]
    hi = [lanes(hi_ref[...], j) for j in range(nt)]
    for t in range(lt):
        at_t = pl.ds(t, nb, stride=pitch)
        for j in range(nt):
            hr[j], hi[j] = (ar[j] * hr[j] - ai[j] * hi[j] + h_ref[j, at_t, :],
                            ar[j] * hi[j] + ai[j] * hr[j] + h_ref[nt + j, at_t, :])
            h_ref[j, at_t, :] = hr[j]
            h_ref[nt + j, at_t, :] = hi[j]
    hr_ref[...] = jnp.concatenate(hr, axis=1)
    hi_ref[...] = jnp.concatenate(hi, axis=1)
    tile = lambda j: jnp.concatenate([h_ref[j, seq_rows(i), :] for i in range(nb)], axis=0)
    h_re = jnp.concatenate([tile(j) for j in range(nt)], axis=1)
    h_im = jnp.concatenate([tile(nt + j) for j in range(nt)], axis=1)
    y = _dot(h_re, wc_ref[0]) - _dot(h_im, wc_ref[1])
    y = y + d_ref[...] * u
    y = 0.5 * y * (1.0 + jnp.tanh(math.sqrt(2.0 / math.pi) * (y + 0.044715 * (y * y * y))))
    y = y * _sigmoid(_dot(y, wglu_ref[...]) + bglu_ref[...])
    y_ref[...] = y.reshape(nb, lt, GROUP_W)


def _s5(u, states, prm, l, prev, nb, lt):
    kern = functools.partial(_s5_kernel, nb=nb, lt=lt)
    return _mixer_call(kern, "s5", l, nb, lt, u, states, prm, prev,
                       [pltpu.VMEM((2 * S5_CH // LANES, nb * _s5_pitch(lt), LANES), F32)])


def _block_diag_in(b):
    dep, g, n, i = b.shape
    eye = jnp.eye(g, dtype=b.dtype)
    return jnp.einsum('lgni,gh->lgihn', b, eye).reshape(dep, g * i, g * n)


def _block_diag_out(c):
    dep, g, o, n = c.shape
    eye = jnp.eye(g, dtype=c.dtype)
    return jnp.einsum('lgon,gh->lgnho', c, eye).reshape(dep, g * n, g * o)


def _prep_params(W):
    dep = W['w_in'].shape[0]
    row = lambda name: W[name].reshape(dep, 1, -1).astype(F32)
    rep = lambda name, n: jnp.repeat(W[name].reshape(dep, -1), n, axis=1).reshape(dep, 1, -1).astype(F32)
    w_in = W['w_in'].astype(BF16)
    o_r = SSD_PROJ
    o_g = o_r + RWKV_PROJ
    o_s = o_g + GLA_PROJ
    w_ssd = jnp.concatenate([w_in[..., :GROUP_W + SSD_CONV_CH],
                             jnp.repeat(w_in[..., GROUP_W + SSD_CONV_CH:o_r], SSD_HEADDIM, axis=2)], axis=2)
    w_rwkv = w_in[..., o_r:o_g]
    gq = o_g + 2 * GLA_QK + GROUP_W
    w_gla = jnp.concatenate([w_in[..., o_g:gq], w_in[..., gq:gq + GLA_GATE_LORA],
                             jnp.zeros((dep, D_MODEL, 128 - GLA_GATE_LORA), BF16),
                             w_in[..., gq + GLA_GATE_LORA:o_s]], axis=2)
    w_s5 = w_in[..., o_s:]
    in_w = (w_ssd, w_rwkv, w_gla, w_s5)

    ssd = (W['ssd_conv_w'].astype(F32), row('ssd_conv_b'), rep('ssd_a_log', SSD_HEADDIM),
           rep('ssd_dt_bias', SSD_HEADDIM), rep('ssd_d', SSD_HEADDIM), row('ssd_norm_g'))

    z = lambda r, c: jnp.zeros((dep, r, c), F32)
    w_lora = jnp.concatenate([
        jnp.concatenate([W['rwkv_w2'], z(32, 2 * GROUP_W)], axis=2),
        jnp.concatenate([z(32, GROUP_W), W['rwkv_a2'], z(32, GROUP_W)], axis=2),
        jnp.concatenate([z(64, 2 * GROUP_W), W['rwkv_g2']], axis=2)], axis=1).astype(BF16)
    rwkv = (row('rwkv_mu'), w_lora, row('rwkv_w0'), row('rwkv_a0'), row('rwkv_k_k'),
            row('rwkv_k_a'), row('rwkv_r_k'), row('rwkv_lnx_w'), row('rwkv_lnx_b'))

    wg2 = jnp.concatenate([W['gla_wg2'], z(128 - GLA_GATE_LORA, GLA_QK)], axis=1).astype(BF16)
    gla = (wg2, row('gla_bg'), row('gla_norm_g'))

    lr = W['s5_lam_re'].astype(F32)
    li = W['s5_lam_im'].astype(F32)
    dts = jnp.exp(W['s5_log_step'].astype(F32))[..., None]
    mag = jnp.exp(lr * dts)
    ar = mag * jnp.cos(li * dts)
    ai = mag * jnp.sin(li * dts)
    den = lr * lr + li * li
    fr = ((ar - 1.0) * lr + ai * li) / den
    fi = (ai * lr - (ar - 1.0) * li) / den
    b_re, b_im = W['s5_b_re'], W['s5_b_im']
    bb_re = fr[..., None] * b_re - fi[..., None] * b_im
    bb_im = fr[..., None] * b_im + fi[..., None] * b_re
    wb = jnp.concatenate([_block_diag_in(bb_re), _block_diag_in(bb_im)], axis=2).astype(BF16)
    wc = jnp.stack([_block_diag_out(W['s5_c_re']), _block_diag_out(W['s5_c_im'])], axis=1).astype(BF16)
    s5 = (wb, wc, ar.reshape(dep, 1, -1), ai.reshape(dep, 1, -1), row('s5_d'),
          W['s5_w_glu'].astype(BF16), row('s5_b_glu'))

    g4 = lambda name: W[name].reshape(dep, 1, 1, D_MODEL).astype(F32)
    return dict(
        in_w=in_w, ssd=ssd, rwkv=rwkv, gla=gla, s5=s5,
        g_mix_pre=g4('g_mix_pre'), g_mix_post=g4('g_mix_post'),
        g_ffn_pre=g4('g_ffn_pre'), g_ffn_post=g4('g_ffn_post'),
        w_out=W['w_out'].reshape(dep, 4, GROUP_W, D_MODEL).astype(BF16),
        w1=W['mlp_w1'].astype(BF16), w2=W['mlp_w2'].astype(BF16))


def _run_group(x, mod, first, st, P, cfg):
    dep, bsz = st['ssd'].shape[:2]
    hist = jnp.pad(st['ssd_conv'], ((0, 0), (0, 0), (HIST - (SSD_CONV - 1), 0), (0, 0)))
    ssd_st = (hist, st['ssd'])
    rwkv_st = (st['rwkv_shift'].reshape(dep, bsz, 1, RWKV_PROJ), st['rwkv'])
    gla_st = (st['gla'],)
    s5_st = (st['s5_re'].reshape(dep, bsz, S5_CH), st['s5_im'].reshape(dep, bsz, S5_CH))
    new = dict(ssd=None, rwkv=None, gla=None, s5=None)
    for l in range(DEPTH):
        p_ssd, p_rwkv, p_gla, p_s5 = _in_proj(x, mod, first, P['g_mix_pre'], P['in_w'], l, *cfg['tok'])
        y_ssd, *new['ssd'] = _ssd(p_ssd, ssd_st, P['ssd'], l, new['ssd'], *cfg['ssd'])
        y_rwkv, *new['rwkv'] = _rwkv(p_rwkv, rwkv_st, P['rwkv'], l, new['rwkv'], *cfg['rwkv'])
        y_gla, *new['gla'] = _gla(p_gla, gla_st, P['gla'], l, new['gla'], *cfg['gla'])
        y_s5, *new['s5'] = _s5(p_s5, s5_st, P['s5'], l, new['s5'], *cfg['s5'])
        x = _out_mlp(x, (y_ssd, y_rwkv, y_gla, y_s5), mod, first, P['g_mix_post'], P['g_ffn_pre'],
                     P['g_ffn_post'], P['w_out'], P['w1'], P['w2'], l, *cfg['tok'])
    hist_new, ssd_new = new['ssd']
    shift_new, rwkv_new = new['rwkv']
    s5r_new, s5i_new = new['s5']
    return x, dict(
        ssd=ssd_new, ssd_conv=hist_new[:, :, HIST - (SSD_CONV - 1):, :],
        rwkv=rwkv_new, rwkv_shift=shift_new.reshape(dep, bsz, RWKV_PROJ), gla=new['gla'][0],
        s5_re=s5r_new.reshape(dep, bsz, S5_GROUPS, S5_STATE),
        s5_im=s5i_new.reshape(dep, bsz, S5_GROUPS, S5_STATE))


_PROMPT_CFG = dict(tok=(1, 512), ssd=(1, 512), rwkv=(1, 512), gla=(1, 512, GLA_CS), s5=(8, 128))
_SAMPLE_CFG = dict(tok=(64, 8), ssd=(32, 8), rwkv=(32, 8), gla=(16, 8, 8), s5=(64, 8))


def kernel(x_prompt, x_sample, c_prompt, c_sample, state_ssd, state_ssd_conv, state_rwkv, state_rwkv_shift, state_gla, state_s5_re, state_s5_im, w_ada, b_ada, g_mix_pre, g_mix_post, g_ffn_pre, g_ffn_post, w_in, w_out, ssd_conv_w, ssd_conv_b, ssd_a_log, ssd_dt_bias, ssd_d, ssd_norm_g, rwkv_mu, rwkv_w0, rwkv_w2, rwkv_a0, rwkv_a2, rwkv_g2, rwkv_k_k, rwkv_k_a, rwkv_r_k, rwkv_lnx_w, rwkv_lnx_b, gla_wg2, gla_bg, gla_norm_g, s5_lam_re, s5_lam_im, s5_log_step, s5_b_re, s5_b_im, s5_c_re, s5_c_im, s5_d, s5_w_glu, s5_b_glu, mlp_w1, mlp_w2):
    W = dict(w_in=w_in, w_out=w_out, g_mix_pre=g_mix_pre, g_mix_post=g_mix_post,
             g_ffn_pre=g_ffn_pre, g_ffn_post=g_ffn_post,
             ssd_conv_w=ssd_conv_w, ssd_conv_b=ssd_conv_b, ssd_a_log=ssd_a_log,
             ssd_dt_bias=ssd_dt_bias, ssd_d=ssd_d, ssd_norm_g=ssd_norm_g,
             rwkv_mu=rwkv_mu, rwkv_w0=rwkv_w0, rwkv_w2=rwkv_w2, rwkv_a0=rwkv_a0, rwkv_a2=rwkv_a2,
             rwkv_g2=rwkv_g2, rwkv_k_k=rwkv_k_k, rwkv_k_a=rwkv_k_a, rwkv_r_k=rwkv_r_k,
             rwkv_lnx_w=rwkv_lnx_w, rwkv_lnx_b=rwkv_lnx_b,
             gla_wg2=gla_wg2, gla_bg=gla_bg, gla_norm_g=gla_norm_g,
             s5_lam_re=s5_lam_re, s5_lam_im=s5_lam_im, s5_log_step=s5_log_step,
             s5_b_re=s5_b_re, s5_b_im=s5_b_im, s5_c_re=s5_c_re, s5_c_im=s5_c_im,
             s5_d=s5_d, s5_w_glu=s5_w_glu, s5_b_glu=s5_b_glu, mlp_w1=mlp_w1, mlp_w2=mlp_w2)
    P = _prep_params(W)

    nbp, nbs = x_prompt.shape[0], x_sample.shape[0]
    mod = _ada(jnp.concatenate([c_sample, c_prompt], axis=0), w_ada, b_ada)
    mod = mod.reshape(DEPTH, nbs + nbp, N_MOD, D_MODEL)

    st_sample = dict(ssd=state_ssd, ssd_conv=state_ssd_conv, rwkv=state_rwkv,
                     rwkv_shift=state_rwkv_shift, gla=state_gla, s5_re=state_s5_re, s5_im=state_s5_im)
    st_prompt = {n: jnp.zeros((DEPTH, nbp) + v.shape[2:], v.dtype) for n, v in st_sample.items()}

    y_prompt, sp = _run_group(x_prompt, mod, nbs, st_prompt, P, _PROMPT_CFG)
    y_sample, ss = _run_group(x_sample, mod, 0, st_sample, P, _SAMPLE_CFG)
    return (y_prompt, y_sample,
            sp['ssd'], ss['ssd'], sp['ssd_conv'], ss['ssd_conv'],
            sp['rwkv'], ss['rwkv'], sp['rwkv_shift'], ss['rwkv_shift'],
            sp['gla'], ss['gla'], sp['s5_re'], ss['s5_re'], sp['s5_im'], ss['s5_im'])
```

```python
import functools
import math

import jax
import jax.numpy as jnp
from jax import lax
from jax.experimental import pallas as pl
from jax.experimental.pallas import tpu as pltpu

F32 = jnp.float32
BF16 = jnp.bfloat16

D_MODEL = 1024
DEPTH = 2
GROUP_W = 256
D_FF = 4096
N_MOD = 6
EPS = 1e-6

SSD_HEADS = 4
SSD_HEADDIM = 64
SSD_STATE = 64
SSD_NGROUPS = 2
SSD_BC = SSD_NGROUPS * SSD_STATE
SSD_CONV = 4
SSD_CONV_CH = GROUP_W + 2 * SSD_BC
SSD_PROJ = GROUP_W + SSD_CONV_CH + SSD_HEADS
SSD_PW = GROUP_W + SSD_CONV_CH + GROUP_W

RWKV_HEADS = 4
RWKV_HEADDIM = 64
RWKV_LORA = 128
RWKV_PROJ = 3 * GROUP_W + RWKV_LORA
RWKV_LN_EPS = 64e-5

GLA_HEADS = 4
GLA_DK = 32
GLA_DV = 64
GLA_GATE_LORA = 16
GLA_TAU = 16.0
GLA_QK = GLA_HEADS * GLA_DK
GLA_PROJ = 2 * GLA_QK + GROUP_W + GLA_GATE_LORA + GROUP_W
GLA_PW = 2 * GLA_QK + GROUP_W + 128 + GROUP_W

S5_GROUP = 16
S5_GROUPS = 16
S5_STATE = 64
S5_CH = S5_GROUPS * S5_STATE
S5_PROJ = GROUP_W

LANES = 128
HIST = 8
CHUNK = 64
GLA_CS = 32
VMEM_LIMIT = 56 * 1024 * 1024


def _dot(a, b, dims=(((1,), (0,)), ((), ()))):
    return lax.dot_general(a.astype(BF16), b.astype(BF16), dims, preferred_element_type=F32)


_NT = (((1,), (1,)), ((), ()))
_TN = (((0,), (0,)), ((), ()))
_NN = (((1,), (0,)), ((), ()))


def _split2(a):
    hi = a.astype(BF16)
    lo = (a - hi.astype(F32)).astype(BF16)
    return hi, lo


def _dot3(a, b, dims=_NN):
    ah, al = _split2(a)
    bh, bl = _split2(b)
    d = lambda x, y: lax.dot_general(x, y, dims, preferred_element_type=F32)
    return d(ah, bh) + (d(ah, bl) + d(al, bh))


_dot_a = _dot
_dot_inv = _dot
_dot_app = _dot
_dot_st = _dot


def _dot_exact_lhs(m, x, dims=_NN):
    mb = m.astype(BF16)
    x1 = x.astype(BF16)
    r1 = x - x1.astype(F32)
    x2 = r1.astype(BF16)
    x3 = (r1 - x2.astype(F32)).astype(BF16)
    d = lambda y: lax.dot_general(mb, y, dims, preferred_element_type=F32)
    return d(x1) + (d(x2) + d(x3))


def _dot_exact_rhs(x, m):
    mb = m.astype(BF16)
    x1 = x.astype(BF16)
    r1 = x - x1.astype(F32)
    x2 = r1.astype(BF16)
    x3 = (r1 - x2.astype(F32)).astype(BF16)
    d = lambda y: lax.dot_general(y, mb, _NN, preferred_element_type=F32)
    return d(x1) + (d(x2) + d(x3))


def _segsum(x, width):
    n = x.shape[-1]
    r = lax.broadcasted_iota(jnp.int32, (n, n), 0) // width
    c = lax.broadcasted_iota(jnp.int32, (n, n), 1) // width
    j = jnp.where(r == c, 1.0, 0.0).astype(BF16)
    hi, lo = _split2(x)
    d = lambda y: lax.dot_general(y, j, _NN, preferred_element_type=F32)
    return d(hi) + d(lo)


def _sigmoid(x):
    return 1.0 / (1.0 + jnp.exp(-x))


def _silu(x):
    return x * _sigmoid(x)


def _softplus(x):
    return jnp.maximum(x, 0.0) + jnp.log1p(jnp.exp(-jnp.abs(x)))


def _rms(x, g):
    return x * lax.rsqrt(jnp.mean(x * x, axis=-1, keepdims=True) + EPS) * g


def _chunk_masks(rows, lt):
    r = lax.broadcasted_iota(jnp.int32, (rows, rows), 0)
    c = lax.broadcasted_iota(jnp.int32, (rows, rows), 1)
    same = (r // lt) == (c // lt)
    return same & (c <= r), same & (c < r)


def _unit_cumsum(x, unit):
    incl, _ = _chunk_masks(CHUNK, unit)
    tri = jnp.where(incl, 1.0, 0.0)
    parts = [_dot_exact_lhs(tri, x[j * CHUNK:(j + 1) * CHUNK]) for j in range(x.shape[0] // CHUNK)]
    return parts[0] if len(parts) == 1 else jnp.concatenate(parts, axis=0)


def _lsel(a, l, **kw):
    shape = a.shape[1:]
    return pl.BlockSpec((None,) + shape, lambda i, c: (l,) + (0,) * len(shape), **kw)


def _lseq(a, l, nb):
    shape = a.shape[2:]
    return pl.BlockSpec((None, nb) + shape, lambda i, c: (l, i) + (0,) * len(shape))


def _mixer_call(kern, name, l, nb, lt, p, states, params, prev, scratch):
    bsz, seq, width = p.shape
    n_in = 1 + len(states) + len(params)
    n_st = len(states)
    alias = tuple(prev) if prev is not None else ()
    first = prev is None

    def body(*refs):
        ins, outs = refs[:n_in], refs[n_in + len(alias):]
        y_ref, st_refs, scratch_refs = outs[0], outs[1:1 + n_st], outs[1 + n_st:]
        if first:
            @pl.when(pl.program_id(1) == 0)
            def _():
                for r in st_refs:
                    r[1:] = jnp.zeros((r.shape[0] - 1,) + r.shape[1:], F32)
            st_refs = [r.at[0] for r in st_refs]
        return kern(*ins, y_ref, *st_refs, *scratch_refs)

    def st_out_spec(s):
        if not first:
            return _lseq(s, l, nb)
        shape = s.shape[2:]
        return pl.BlockSpec((s.shape[0], nb) + shape, lambda i, c: (0, i) + (0,) * len(shape))

    tok = lambda w: pl.BlockSpec((nb, lt, w), lambda i, c: (i, c, 0))
    return pl.pallas_call(
        body,
        out_shape=(jax.ShapeDtypeStruct((bsz, seq, GROUP_W), F32),)
        + tuple(jax.ShapeDtypeStruct(s.shape, F32) for s in states),
        grid=(bsz // nb, seq // lt),
        in_specs=[tok(width)] + [_lseq(s, l, nb) for s in states] + [_lsel(a, l) for a in params]
        + [pl.BlockSpec(memory_space=pl.ANY)] * len(alias),
        out_specs=(tok(GROUP_W),) + tuple(st_out_spec(s) for s in states),
        scratch_shapes=scratch,
        input_output_aliases={n_in + k: 1 + k for k in range(len(alias))},
        compiler_params=pltpu.CompilerParams(
            dimension_semantics=("parallel", "arbitrary"), vmem_limit_bytes=VMEM_LIMIT),
        name=name,
    )(p, *states, *params, *alias)


def _bcast_last(x, nb, lt):
    c = x.shape[-1]
    last = x.reshape(nb, lt, c)[:, lt - 1:lt, :]
    return last, jnp.broadcast_to(last, (nb, lt, c)).reshape(nb * lt, c)


def _ada_kernel(c_ref, w_ref, b_ref, o_ref):
    c = c_ref[...]
    o_ref[0] = _dot(_silu(c), w_ref[0]) + b_ref[0]


def _ada(c_all, w_ada, b_ada):
    rows = c_all.shape[0]
    n = N_MOD * D_MODEL
    tn = 1536
    return pl.pallas_call(
        _ada_kernel,
        out_shape=jax.ShapeDtypeStruct((DEPTH, rows, n), F32),
        grid=(DEPTH, n // tn),
        in_specs=[
            pl.BlockSpec((rows, D_MODEL), lambda l, j: (0, 0)),
            pl.BlockSpec((1, D_MODEL, tn), lambda l, j: (l, 0, j)),
            pl.BlockSpec((1, 1, tn), lambda l, j: (l, 0, j)),
        ],
        out_specs=pl.BlockSpec((1, rows, tn), lambda l, j: (l, 0, j)),
        compiler_params=pltpu.CompilerParams(
            dimension_semantics=("parallel", "parallel"), vmem_limit_bytes=VMEM_LIMIT),
        name="ada",
    )(c_all, w_ada, b_ada.reshape(DEPTH, 1, n))


def _in_kernel(x_ref, mod_ref, g_ref, wssd_ref, wrwkv_ref, wgla_ref, ws5_ref,
               ossd_ref, orwkv_ref, ogla_ref, os5_ref):
    nb, lt, d = x_ref.shape
    m = mod_ref[...]
    h = _rms(x_ref[...], g_ref[...]) * (1.0 + m[:, 1:2, :]) + m[:, 0:1, :]
    hb = h.reshape(nb * lt, d).astype(BF16)
    for w_ref, o_ref in ((wssd_ref, ossd_ref), (wrwkv_ref, orwkv_ref),
                         (wgla_ref, ogla_ref), (ws5_ref, os5_ref)):
        o_ref[...] = jnp.dot(hb, w_ref[...], preferred_element_type=F32).reshape(o_ref.shape)


def _mod_spec(mod, l, first, nb):
    return pl.BlockSpec((None, nb, N_MOD, mod.shape[-1]), lambda i, j: (l, first // nb + i, 0, 0))


def _in_proj(x, mod, first, g, wts, l, nb, lt):
    bsz, seq, d = x.shape
    widths = (SSD_PW, RWKV_PROJ, GLA_PW, S5_PROJ)
    return pl.pallas_call(
        _in_kernel,
        out_shape=tuple(jax.ShapeDtypeStruct((bsz, seq, w), F32) for w in widths),
        grid=(bsz // nb, seq // lt),
        in_specs=[pl.BlockSpec((nb, lt, d), lambda i, j: (i, j, 0)), _mod_spec(mod, l, first, nb),
                  _lsel(g, l)] + [_lsel(w, l) for w in wts],
        out_specs=tuple(pl.BlockSpec((nb, lt, w), lambda i, j: (i, j, 0)) for w in widths),
        compiler_params=pltpu.CompilerParams(
            dimension_semantics=("parallel", "parallel"), vmem_limit_bytes=VMEM_LIMIT),
        name="in_proj",
    )(x, mod, g, *wts)


def _out_mlp_kernel(x_ref, y0_ref, y1_ref, y2_ref, y3_ref, mod_ref, gpost_ref, gpre_ref, gfpost_ref,
                    wout_ref, w1_ref, w2_ref, o_ref):
    nb, lt, d = x_ref.shape
    rows = nb * lt
    m = mod_ref[...]
    mix = None
    for i, y_ref in enumerate((y0_ref, y1_ref, y2_ref, y3_ref)):
        t = jnp.dot(y_ref[...].reshape(rows, GROUP_W).astype(BF16), wout_ref[i],
                    preferred_element_type=F32)
        mix = t if mix is None else mix + t
    x1 = x_ref[...] + m[:, 2:3, :] * _rms(mix, gpost_ref[0]).reshape(nb, lt, d)
    h = _rms(x1, gpre_ref[...]) * (1.0 + m[:, 4:5, :]) + m[:, 3:4, :]
    hb = h.reshape(rows, d).astype(BF16)
    acc = None
    step = 1024
    for j in range(D_FF // step):
        a = jnp.dot(hb, w1_ref[:, j * step:(j + 1) * step], preferred_element_type=F32)
        a = jnp.square(jnp.maximum(a, 0.0)).astype(BF16)
        t = jnp.dot(a, w2_ref[j * step:(j + 1) * step, :], preferred_element_type=F32)
        acc = t if acc is None else acc + t
    o_ref[...] = x1 + m[:, 5:6, :] * _rms(acc, gfpost_ref[0]).reshape(nb, lt, d)


def _out_mlp(x, ys, mod, first, gpost, gpre, gfpost, wout, w1, w2, l, nb, lt):
    bsz, seq, d = x.shape
    tok = lambda i, j: (i, j, 0)
    one = pl.Buffered(1)
    return pl.pallas_call(
        _out_mlp_kernel,
        out_shape=jax.ShapeDtypeStruct((bsz, seq, d), F32),
        grid=(bsz // nb, seq // lt),
        in_specs=[pl.BlockSpec((nb, lt, d), tok)]
        + [pl.BlockSpec((nb, lt, GROUP_W), tok)] * 4
        + [_mod_spec(mod, l, first, nb)]
        + [_lsel(g, l) for g in (gpost, gpre, gfpost)]
        + [_lsel(w, l, pipeline_mode=one) for w in (wout, w1, w2)],
        out_specs=pl.BlockSpec((nb, lt, d), tok),
        compiler_params=pltpu.CompilerParams(
            dimension_semantics=("parallel", "parallel"), vmem_limit_bytes=VMEM_LIMIT),
        name="out_mlp",
    )(x, *ys, mod, gpost, gpre, gfpost, wout, w1, w2)


def _ssd_kernel(p_ref, hist_ref, s0_ref, cw_ref, cb_ref, alog_ref, dtb_ref, dsk_ref, ng_ref,
                y_ref, hout_ref, sout_ref, ext_ref, yscr_ref, *, nb, lt):
    rows = nb * lt
    hd = SSD_HEADDIM

    @pl.when(pl.program_id(1) == 0)
    def _():
        ext_ref[:, 0:HIST, :] = hist_ref[...]
        sout_ref[...] = s0_ref[...]

    p = p_ref[...]
    ext_ref[:, HIST:HIST + lt, :] = p[:, :, GROUP_W:GROUP_W + SSD_CONV_CH]
    conv = cb_ref[...]
    for j in range(SSD_CONV):
        o = HIST - (SSD_CONV - 1) + j
        conv = conv + ext_ref[:, o:o + lt, :] * cw_ref[j:j + 1, :]
    tail = ext_ref[:, lt:lt + HIST, :]
    ext_ref[:, 0:HIST, :] = tail
    hout_ref[...] = tail

    xbc = _silu(conv).reshape(rows, SSD_CONV_CH)
    xs = xbc[:, 0:GROUP_W]
    bm = xbc[:, GROUP_W:GROUP_W + SSD_BC]
    cm = xbc[:, GROUP_W + SSD_BC:]
    z = p[:, :, 0:GROUP_W].reshape(rows, GROUP_W)
    dt = _softplus(p[:, :, GROUP_W + SSD_CONV_CH:].reshape(rows, GROUP_W) + dtb_ref[...])
    a = dt * (-jnp.exp(alog_ref[...]))
    unit = lt if nb > 1 else CHUNK
    units = rows // unit
    per_chunk = CHUNK // unit
    cum = _unit_cumsum(a, unit)
    last, last_b = _bcast_last(cum, units, unit)
    xdt = xs * dt
    xw = xdt * jnp.exp(last_b - cum)
    ecum = jnp.exp(cum)
    elast = jnp.exp(last)
    ones_row = jnp.full((CHUNK, hd), 1.0 / hd, F32)
    cmask, _ = _chunk_masks(CHUNK, unit)
    heads = range(SSD_HEADS)
    hs = lambda h: slice(h * hd, (h + 1) * hd)
    gs = lambda h: slice((h // 2) * SSD_STATE, (h // 2 + 1) * SSD_STATE)
    cs = lambda j: slice(j * CHUNK, (j + 1) * CHUNK)
    us = lambda u: slice(u * unit, (u + 1) * unit)

    yh = {}
    for j in range(rows // CHUNK):
        cb = [_dot(cm[cs(j), gs(2 * g)], bm[cs(j), gs(2 * g)], _NT) for g in range(SSD_NGROUPS)]
        for h in heads:
            ccol = cum[cs(j), hs(h)]
            crow = _dot_exact_lhs(ones_row, ccol, _NT)
            decay = jnp.exp(jnp.where(cmask, ccol - crow, -jnp.inf))
            yh[j, h] = _dot(cb[h // 2] * decay, xdt[cs(j), hs(h)])
    ds = {(u, h): _dot(xw[us(u), hs(h)], bm[us(u), gs(h)], _TN) for u in range(units) for h in heads}

    before = {}
    for h in heads:
        if nb > 1:
            for u in range(units):
                before[u, h] = sout_ref[u, h]
                sout_ref[u, h] = before[u, h] * elast[u][:, hs(h)] + ds[u, h]
        else:
            s_cur = sout_ref[0, h]
            for u in range(units):
                before[u, h] = s_cur
                s_cur = s_cur * elast[u][:, hs(h)] + ds[u, h]
            sout_ref[0, h] = s_cur

    for j in range(rows // CHUNK):
        for h in heads:
            inter = [_dot(cm[us(u), gs(h)], before[u, h], _NT)
                     for u in range(j * per_chunk, (j + 1) * per_chunk)]
            inter = inter[0] if per_chunk == 1 else jnp.concatenate(inter, axis=0)
            yscr_ref[cs(j), hs(h)] = yh[j, h] + inter * ecum[cs(j), hs(h)]

    y = yscr_ref[...] + dsk_ref[...] * xs
    y = y * _silu(z)
    y = y * lax.rsqrt(_segsum(y * y, hd) * (1.0 / hd) + EPS) * ng_ref[...]
    y_ref[...] = y.reshape(nb, lt, GROUP_W)


def _ssd(p, states, prm, l, prev, nb, lt):
    assert CHUNK == SSD_HEADDIM
    assert (nb > 1 and (nb * lt) % CHUNK == 0 and CHUNK % lt == 0) or (nb == 1 and lt % CHUNK == 0)
    kern = functools.partial(_ssd_kernel, nb=nb, lt=lt)
    return _mixer_call(kern, "ssd", l, nb, lt, p, states, prm, prev,
                       [pltpu.VMEM((nb, HIST + lt, SSD_CONV_CH), F32),
                        pltpu.VMEM((nb * lt, GROUP_W), F32)])


def _unit_lower_inverse(a2, n, levels, dot):
    ri = lax.broadcasted_iota(jnp.int32, (n, n), 0)
    ci = lax.broadcasted_iota(jnp.int32, (n, n), 1)
    x = [jnp.where(ri == ci, 1.0, 0.0) - jnp.where((ri % 2 == 1) & (ci == ri - 1), a, 0.0) for a in a2]
    s = 2
    while s < levels:
        off = ((ri // s) % 2 == 1) & ((ci // s) == (ri // s) - 1)
        t = [dot(xi, jnp.where(off, a, 0.0)) for xi, a in zip(x, a2)]
        x = [xi - dot(ti, xi) for xi, ti in zip(x, t)]
        s *= 2
    return x


def _rwkv_chunk_multi(ops, sout_ref, yscr_ref, nb, lt):
    hd = RWKV_HEADDIM
    per_chunk = CHUNK // lt
    idx = [(c, h) for c in range(nb // per_chunk) for h in range(RWKV_HEADS)]
    n = range(len(idx))
    incl, strict = _chunk_masks(CHUNK, lt)
    mask2 = jnp.concatenate([strict, incl], axis=0)
    top = lambda m: m[0:CHUNK]
    bot = lambda m: m[CHUNK:2 * CHUNK]
    bl = lambda name: [ops[name][c * CHUNK:(c + 1) * CHUNK, h * hd:(h + 1) * hd] for c, h in idx]
    rt, kt, k2t, bt, k2g, bg, v = (bl(name) for name in ('rt', 'kt', 'k2t', 'bt', 'k2g', 'bg', 'v'))
    kr = [jnp.concatenate([kt[j], rt[j]], axis=0) for j in n]
    m1 = [jnp.where(mask2, _dot_a(kr[j], k2t[j], _NT), 0.0) for j in n]
    m2 = [jnp.where(mask2, _dot_a(kr[j], bt[j], _NT), 0.0) for j in n]
    x = _unit_lower_inverse([top(m) for m in m2], CHUNK, lt, _dot_inv)
    m1v = [_dot_app(m1[j], v[j]) for j in n]
    seq = lambda i: slice(i * lt, (i + 1) * lt)
    seqs = range(per_chunk)
    s_old = [[sout_ref[c * per_chunk + i, h] for i in seqs] for c, h in idx]
    pq = [[_dot_st(jnp.concatenate([kt[j][seq(i)], rt[j][seq(i)]], axis=0), s_old[j][i], _NT)
           for i in seqs] for j in n]
    p0 = [jnp.concatenate([pq[j][i][0:lt] for i in seqs], axis=0) for j in n]
    q0 = [jnp.concatenate([pq[j][i][lt:2 * lt] for i in seqs], axis=0) for j in n]
    u = [_dot_app(x[j], p0[j] + top(m1v[j])) for j in n]
    l2u = [_dot_app(bot(m2[j]), u[j]) for j in n]
    for j, (c, h) in enumerate(idx):
        yscr_ref[c * CHUNK:(c + 1) * CHUNK, h * hd:(h + 1) * hd] = q0[j] + bot(m1v[j]) - l2u[j]
        for i in seqs:
            sq = c * per_chunk + i
            sout_ref[sq, h] = (s_old[j][i] * ops['elast'][sq][:, h * hd:(h + 1) * hd]
                               + _dot_st(v[j][seq(i)], k2g[j][seq(i)], _TN)
                               - _dot_st(u[j][seq(i)], bg[j][seq(i)], _TN))


def _rwkv_chunks_seq(ops, state, yscr_ref, nc):
    hd = RWKV_HEADDIM
    pw = 2 * hd
    pairs = RWKV_HEADS // 2
    idx = [(c, q) for c in range(nc) for q in range(pairs)]
    n = range(len(idx))
    top = lambda m: m[0:CHUNK]
    bot = lambda m: m[CHUNK:2 * CHUNK]
    ri = lax.broadcasted_iota(jnp.int32, (CHUNK, pw), 0)
    cl = lax.broadcasted_iota(jnp.int32, (CHUNK, pw), 1) % hd
    left = lax.broadcasted_iota(jnp.int32, (CHUNK, pw), 1) < hd
    strict, incl = cl < ri, cl <= ri
    mask2 = jnp.concatenate([strict, incl], axis=0)
    eye = jnp.where(cl == ri, 1.0, 0.0)
    bd = lambda p: jnp.concatenate([jnp.where(left, p, 0.0), jnp.where(left, 0.0, p)], axis=0)
    on_diag = jnp.concatenate([left, ~left], axis=0)
    r2 = lax.broadcasted_iota(jnp.int32, (pw, pw), 0)
    c2 = lax.broadcasted_iota(jnp.int32, (pw, pw), 1)
    eye2 = jnp.where(r2 == c2, 1.0, 0.0)
    diag_pack = lambda m: jnp.where(left, top(m), bot(m))

    bl = lambda name: [ops[name][c * CHUNK:(c + 1) * CHUNK, q * pw:(q + 1) * pw] for c, q in idx]
    rt, kt, k2t, bt, k2g, bg, v = (bl(name) for name in ('rt', 'kt', 'k2t', 'bt', 'k2g', 'bg', 'v'))
    kr = [jnp.concatenate([kt[j], rt[j]], axis=0) for j in n]
    m1 = [jnp.where(mask2, _dot_a(kr[j], bd(k2t[j]), _NT), 0.0) for j in n]
    m2 = [jnp.where(mask2, _dot_a(kr[j], bd(bt[j]), _NT), 0.0) for j in n]
    x = [eye - jnp.where((ri % 2 == 1) & (cl == ri - 1), top(m), 0.0) for m in m2]
    s = 2
    while s < CHUNK:
        off = ((ri // s) % 2 == 1) & ((cl // s) == (ri // s) - 1)
        t = [_dot_inv(x[j], bd(jnp.where(off, top(m2[j]), 0.0))) for j in n]
        x = [x[j] - _dot_inv(t[j], bd(x[j])) for j in n]
        s *= 2
    m1v = [_dot_app(m1[j], bd(v[j])) for j in n]
    kp = [_dot_app(x[j], bd(kt[j])) for j in n]
    ul = [_dot_app(x[j], bd(top(m1v[j]))) for j in n]
    rp = [rt[j] - _dot_app(bot(m2[j]), bd(kp[j])) for j in n]
    yl = [bot(m1v[j]) - _dot_app(bot(m2[j]), bd(ul[j])) for j in n]
    mc = [eye2 * ops['elast'][c][:, q * pw:(q + 1) * pw]
          - jnp.where(on_diag, _dot_st(kp[j], bg[j], _TN), 0.0) for j, (c, q) in enumerate(idx)]
    dc = [diag_pack(_dot_st(v[j], k2g[j], _TN) - _dot_st(ul[j], bg[j], _TN)) for j in n]
    for c in range(nc):
        js = [c * pairs + q for q in range(pairs)]
        ys = [_dot_st(rp[j], bd(state[q]), _NT) + yl[j] for q, j in enumerate(js)]
        state = [_dot_st(state[q], mc[j]) + dc[j] for q, j in enumerate(js)]
        for q in range(pairs):
            yscr_ref[c * CHUNK:(c + 1) * CHUNK, q * pw:(q + 1) * pw] = ys[q]
    return state


def _rwkv_kernel(p_ref, sh0_ref, s0_ref, mu_ref, wl_ref, w0_ref, a0_ref, kk_ref, ka_ref, rk_ref,
                 lnw_ref, lnb_ref, y_ref, shout_ref, sout_ref, ext_ref, yscr_ref, *, nb, lt):
    rows = nb * lt
    hd = RWKV_HEADDIM
    gw = GROUP_W

    @pl.when(pl.program_id(1) == 0)
    def _():
        ext_ref[:, HIST - 1:HIST, :] = sh0_ref[...]
        sout_ref[...] = s0_ref[...]

    p = p_ref[...]
    ext_ref[:, HIST:HIST + lt, :] = p
    prev = ext_ref[:, HIST - 1:HIST - 1 + lt, :].reshape(rows, RWKV_PROJ)
    last_p = p[:, lt - 1:lt, :]
    ext_ref[:, HIST - 1:HIST, :] = last_p
    shout_ref[...] = last_p
    p = p.reshape(rows, RWKV_PROJ)

    pm = p + (prev - p) * mu_ref[...]
    r = pm[:, 0:gw]
    k = pm[:, gw:2 * gw]
    v = pm[:, 2 * gw:3 * gw]
    lo = pm[:, 3 * gw:]
    lane = lax.broadcasted_iota(jnp.int32, (rows, RWKV_LORA), 1)
    t = jnp.where(lane < 32, jnp.tanh(lo), jnp.where(lane < 64, lo, _sigmoid(lo)))
    lora = _dot(t, wl_ref[...])
    wlog = -_softplus(-(w0_ref[...] + lora[:, 0:gw])) - 0.5
    lw = -jnp.exp(wlog)
    a = _sigmoid(a0_ref[...] + lora[:, gw:2 * gw])
    gate = lora[:, 2 * gw:]
    kk = k * kk_ref[...]
    kk = kk * jnp.minimum(lax.rsqrt(_segsum(kk * kk, hd)), 1e12)
    k2 = k * (1.0 + (a - 1.0) * ka_ref[...])
    beta = kk * a

    unit = lt if nb > 1 else CHUNK
    gcum = _unit_cumsum(lw, unit)
    glast, glast_b = _bcast_last(gcum, rows // unit, unit)
    eg = jnp.exp(gcum)
    einv = jnp.exp(-gcum)
    etail = jnp.exp(glast_b - gcum)
    ops = dict(rt=r * eg, kt=kk * jnp.exp(gcum - lw), k2t=k2 * einv, bt=beta * einv,
               k2g=k2 * etail, bg=beta * etail, v=v, elast=jnp.exp(glast))
    if nb > 1:
        _rwkv_chunk_multi(ops, sout_ref, yscr_ref, nb, lt)
    else:
        pairs = range(RWKV_HEADS // 2)
        state = [jnp.concatenate([sout_ref[0, 2 * q], sout_ref[0, 2 * q + 1]], axis=1) for q in pairs]
        state = _rwkv_chunks_seq(ops, state, yscr_ref, rows // CHUNK)
        for q in pairs:
            sout_ref[0, 2 * q] = state[q][:, 0:hd]
            sout_ref[0, 2 * q + 1] = state[q][:, hd:2 * hd]

    y = yscr_ref[...]
    mean = _segsum(y, hd) * (1.0 / hd)
    dlt = y - mean
    var = _segsum(dlt * dlt, hd) * (1.0 / hd)
    yn = dlt * lax.rsqrt(var + RWKV_LN_EPS) * lnw_ref[...] + lnb_ref[...]
    bonus = _segsum(r * k2 * rk_ref[...], hd) * v
    y_ref[...] = ((yn + bonus) * gate).reshape(nb, lt, gw)


def _rwkv(p, states, prm, l, prev, nb, lt):
    assert (nb > 1 and (nb * lt) % CHUNK == 0 and CHUNK % lt == 0) or (nb == 1 and lt % CHUNK == 0)
    kern = functools.partial(_rwkv_kernel, nb=nb, lt=lt)
    return _mixer_call(kern, "rwkv", l, nb, lt, p, states, prm, prev,
                       [pltpu.VMEM((nb, HIST + lt, RWKV_PROJ), F32),
                        pltpu.VMEM((nb * lt, GROUP_W), F32)])


def _gla_kernel(p_ref, s0_ref, wg2_ref, bg_ref, ng_ref, y_ref, sout_ref, yscr_ref, *, nb, lt, cs):
    rows = nb * lt
    dk, dv = GLA_DK, GLA_DV
    qk = GLA_QK
    units = rows // cs
    per_group = GLA_CS // cs

    @pl.when(pl.program_id(1) == 0)
    def _():
        sout_ref[...] = s0_ref[...]

    p = p_ref[...].reshape(rows, GLA_PW)
    q = p[:, 0:qk] * (dk ** -0.5)
    k = p[:, qk:2 * qk]
    v = p[:, 2 * qk:2 * qk + GROUP_W]
    dg = p[:, 2 * qk + GROUP_W:2 * qk + GROUP_W + 128]
    og = p[:, 2 * qk + GROUP_W + 128:]
    lg = -_softplus(-(_dot(dg, wg2_ref[...]) + bg_ref[...])) * (1.0 / GLA_TAU)
    b = _unit_cumsum(lg, cs)
    bl, bl_b = _bcast_last(b, units, cs)
    qd = q * jnp.exp(b)
    kd = k * jnp.exp(-b)
    kw = k * jnp.exp(bl_b - b)
    ebl = jnp.exp(bl)
    gmask, _ = _chunk_masks(GLA_CS, cs)

    heads = range(GLA_HEADS)
    ks = lambda h: slice(h * dk, (h + 1) * dk)
    vs = lambda h: slice(h * dv, (h + 1) * dv)
    gs = lambda g: slice(g * GLA_CS, (g + 1) * GLA_CS)
    us = lambda u: slice(u * cs, (u + 1) * cs)

    yh = {}
    for g in range(rows // GLA_CS):
        for h in heads:
            att = jnp.where(gmask, _dot(qd[gs(g), ks(h)], kd[gs(g), ks(h)], _NT), 0.0)
            yh[g, h] = _dot(att, v[gs(g), vs(h)])
    ds = {(u, h): _dot(kw[us(u), ks(h)], v[us(u), vs(h)], _TN) for u in range(units) for h in heads}
    ri = lax.broadcasted_iota(jnp.int32, (qk, qk), 0)
    ci = lax.broadcasted_iota(jnp.int32, (qk, qk), 1)
    ones_kv = jnp.ones((qk, dv), F32)
    ecol_all = [_dot_exact_rhs(jnp.where(ri == ci, ebl[u], 0.0), ones_kv) for u in range(units)]
    ecol = {(u, h): ecol_all[u][ks(h)] for u in range(units) for h in heads}

    before = {}
    for h in heads:
        if nb > 1:
            for u in range(units):
                before[u, h] = sout_ref[u, h]
                sout_ref[u, h] = before[u, h] * ecol[u, h] + ds[u, h]
        else:
            s_cur = sout_ref[0, h]
            for u in range(units):
                before[u, h] = s_cur
                s_cur = s_cur * ecol[u, h] + ds[u, h]
            sout_ref[0, h] = s_cur

    for g in range(rows // GLA_CS):
        for h in heads:
            inter = [_dot(qd[us(u), ks(h)], before[u, h])
                     for u in range(g * per_group, (g + 1) * per_group)]
            inter = inter[0] if per_group == 1 else jnp.concatenate(inter, axis=0)
            yscr_ref[gs(g), vs(h)] = yh[g, h] + inter

    y = yscr_ref[...]
    y = y * lax.rsqrt(_segsum(y * y, dv) * (1.0 / dv) + EPS) * ng_ref[...]
    y_ref[...] = (y * _silu(og)).reshape(nb, lt, GROUP_W)


def _gla(p, states, prm, l, prev, nb, lt, cs):
    assert (nb == 1 and cs == GLA_CS and lt % cs == 0) or (
        lt == cs and (nb * cs) % GLA_CS == 0 and GLA_CS % cs == 0)
    kern = functools.partial(_gla_kernel, nb=nb, lt=lt, cs=cs)
    return _mixer_call(kern, "gla", l, nb, lt, p, states, prm, prev,
                       [pltpu.VMEM((nb * lt, GROUP_W), F32)])


def _s5_pitch(lt):
    return lt if (lt // 8) % 2 == 1 else lt + 8


def _s5_kernel(u_ref, hr0_ref, hi0_ref, wb_ref, wc_ref, ar_ref, ai_ref, d_ref, wglu_ref, bglu_ref,
               y_ref, hr_ref, hi_ref, h_ref, *, nb, lt):
    rows = nb * lt
    n = S5_CH

    @pl.when(pl.program_id(1) == 0)
    def _():
        hr_ref[...] = hr0_ref[...]
        hi_ref[...] = hi0_ref[...]

    u = u_ref[...].reshape(rows, GROUP_W)
    bu = _dot(u, wb_ref[...])
    nt = n // LANES
    lanes = lambda x, j: x[:, j * LANES:(j + 1) * LANES]
    pitch = _s5_pitch(lt)
    seq_rows = lambda i: slice(i * pitch, i * pitch + lt)
    for j in range(2 * nt):
        for i in range(nb):
            h_ref[j, seq_rows(i), :] = lanes(bu, j)[i * lt:(i + 1) * lt]
    ar = [jnp.broadcast_to(lanes(ar_ref[...], j), (nb, LANES)) for j in range(nt)]
    ai = [jnp.broadcast_to(lanes(ai_ref[...], j), (nb, LANES)) for j in range(nt)]
    hr = [lanes(hr_ref[...], j) for j in range(nt)]
    hi = [lanes(hi_ref[...], j) for j in range(nt)]
    for t in range(lt):
        at_t = pl.ds(t, nb, stride=pitch)
        for j in range(nt):
            hr[j], hi[j] = (ar[j] * hr[j] - ai[j] * hi[j] + h_ref[j, at_t, :],
                            ar[j] * hi[j] + ai[j] * hr[j] + h_ref[nt + j, at_t, :])
            h_ref[j, at_t, :] = hr[j]
            h_ref[nt + j, at_t, :] = hi[j]
    hr_ref[...] = jnp.concatenate(hr, axis=1)
    hi_ref[...] = jnp.concatenate(hi, axis=1)
    tile = lambda j: jnp.concatenate([h_ref[j, seq_rows(i), :] for i in range(nb)], axis=0)
    h_re = jnp.concatenate([tile(j) for j in range(nt)], axis=1)
    h_im = jnp.concatenate([tile(nt + j) for j in range(nt)], axis=1)
    y = _dot(h_re, wc_ref[0]) - _dot(h_im, wc_ref[1])
    y = y + d_ref[...] * u
    y = 0.5 * y * (1.0 + jnp.tanh(math.sqrt(2.0 / math.pi) * (y + 0.044715 * (y * y * y))))
    y = y * _sigmoid(_dot(y, wglu_ref[...]) + bglu_ref[...])
    y_ref[...] = y.reshape(nb, lt, GROUP_W)


def _s5(u, states, prm, l, prev, nb, lt):
    kern = functools.partial(_s5_kernel, nb=nb, lt=lt)
    return _mixer_call(kern, "s5", l, nb, lt, u, states, prm, prev,
                       [pltpu.VMEM((2 * S5_CH // LANES, nb * _s5_pitch(lt), LANES), F32)])


def _s5_zoh_kernel(lr_ref, li_ref, ls_ref, bre_ref, bim_ref, ar_ref, ai_ref, bbre_ref, bbim_ref):
    lr = lr_ref[...]
    li = li_ref[...]
    dts = jnp.exp(ls_ref[...])
    mag = jnp.exp(lr * dts)
    ar = mag * jnp.cos(li * dts)
    ai = mag * jnp.sin(li * dts)
    den = lr * lr + li * li
    fr = ((ar - 1.0) * lr + ai * li) / den
    fi = (ai * lr - (ar - 1.0) * li) / den
    ar_ref[...] = ar
    ai_ref[...] = ai
    r, n = lr.shape
    per_input = lambda f: jnp.broadcast_to(f[:, None, :], (r, S5_GROUP, n)).reshape(r * S5_GROUP, n)
    fre, fie = per_input(fr), per_input(fi)
    bre = bre_ref[...]
    bim = bim_ref[...]
    bbre_ref[...] = fre * bre - fie * bim
    bbim_ref[...] = fre * bim + fie * bre


def _s5_zoh(lam_re, lam_im, log_step, b_re, b_im):
    dep, g, n = lam_re.shape
    rows = dep * g
    flat = lambda a: a.astype(F32).reshape(rows, n)
    flat_b = lambda b: jnp.swapaxes(b.astype(F32), -1, -2).reshape(rows * S5_GROUP, n)
    vm = pl.BlockSpec(memory_space=pltpu.VMEM)
    ar, ai, bbr, bbi = pl.pallas_call(
        _s5_zoh_kernel,
        out_shape=(jax.ShapeDtypeStruct((rows, n), F32),) * 2
        + (jax.ShapeDtypeStruct((rows * S5_GROUP, n), F32),) * 2,
        in_specs=[vm] * 5,
        out_specs=(vm,) * 4,
        name="s5_zoh",
    )(flat(lam_re), flat(lam_im), log_step.astype(F32).reshape(rows, 1), flat_b(b_re), flat_b(b_im))
    return (ar.reshape(dep, g, n), ai.reshape(dep, g, n),
            bbr.reshape(dep, g, S5_GROUP, n), bbi.reshape(dep, g, S5_GROUP, n))


def _block_diag_in(b):
    dep, g, i, n = b.shape
    eye = jnp.eye(g, dtype=b.dtype)
    return jnp.einsum('lgin,gh->lgihn', b, eye).reshape(dep, g * i, g * n)


def _block_diag_out(c):
    dep, g, o, n = c.shape
    eye = jnp.eye(g, dtype=c.dtype)
    return jnp.einsum('lgon,gh->lgnho', c, eye).reshape(dep, g * n, g * o)


def _prep_params(W):
    dep = W['w_in'].shape[0]
    row = lambda name: W[name].reshape(dep, 1, -1).astype(F32)
    rep = lambda name, n: jnp.repeat(W[name].reshape(dep, -1), n, axis=1).reshape(dep, 1, -1).astype(F32)
    w_in = W['w_in'].astype(BF16)
    o_r = SSD_PROJ
    o_g = o_r + RWKV_PROJ
    o_s = o_g + GLA_PROJ
    w_ssd = jnp.concatenate([w_in[..., :GROUP_W + SSD_CONV_CH],
                             jnp.repeat(w_in[..., GROUP_W + SSD_CONV_CH:o_r], SSD_HEADDIM, axis=2)], axis=2)
    w_rwkv = w_in[..., o_r:o_g]
    gq = o_g + 2 * GLA_QK + GROUP_W
    w_gla = jnp.concatenate([w_in[..., o_g:gq], w_in[..., gq:gq + GLA_GATE_LORA],
                             jnp.zeros((dep, D_MODEL, 128 - GLA_GATE_LORA), BF16),
                             w_in[..., gq + GLA_GATE_LORA:o_s]], axis=2)
    w_s5 = w_in[..., o_s:]
    in_w = (w_ssd, w_rwkv, w_gla, w_s5)

    ssd = (W['ssd_conv_w'].astype(F32), row('ssd_conv_b'), rep('ssd_a_log', SSD_HEADDIM),
           rep('ssd_dt_bias', SSD_HEADDIM), rep('ssd_d', SSD_HEADDIM), row('ssd_norm_g'))

    z = lambda r, c: jnp.zeros((dep, r, c), F32)
    w_lora = jnp.concatenate([
        jnp.concatenate([W['rwkv_w2'], z(32, 2 * GROUP_W)], axis=2),
        jnp.concatenate([z(32, GROUP_W), W['rwkv_a2'], z(32, GROUP_W)], axis=2),
        jnp.concatenate([z(64, 2 * GROUP_W), W['rwkv_g2']], axis=2)], axis=1).astype(BF16)
    rwkv = (row('rwkv_mu'), w_lora, row('rwkv_w0'), row('rwkv_a0'), row('rwkv_k_k'),
            row('rwkv_k_a'), row('rwkv_r_k'), row('rwkv_lnx_w'), row('rwkv_lnx_b'))

    wg2 = jnp.concatenate([W['gla_wg2'], z(128 - GLA_GATE_LORA, GLA_QK)], axis=1).astype(BF16)
    gla = (wg2, row('gla_bg'), row('gla_norm_g'))

    ar, ai, bb_re, bb_im = _s5_zoh(W['s5_lam_re'], W['s5_lam_im'], W['s5_log_step'],
                                   W['s5_b_re'], W['s5_b_im'])
    wb = jnp.concatenate([_block_diag_in(bb_re), _block_diag_in(bb_im)], axis=2).astype(BF16)
    wc = jnp.stack([_block_diag_out(W['s5_c_re']), _block_diag_out(W['s5_c_im'])], axis=1).astype(BF16)
    s5 = (wb, wc, ar.reshape(dep, 1, -1), ai.reshape(dep, 1, -1), row('s5_d'),
          W['s5_w_glu'].astype(BF16), row('s5_b_glu'))

    g4 = lambda name: W[name].reshape(dep, 1, 1, D_MODEL).astype(F32)
    return dict(
        in_w=in_w, ssd=ssd, rwkv=rwkv, gla=gla, s5=s5,
        g_mix_pre=g4('g_mix_pre'), g_mix_post=g4('g_mix_post'),
        g_ffn_pre=g4('g_ffn_pre'), g_ffn_post=g4('g_ffn_post'),
        w_out=W['w_out'].reshape(dep, 4, GROUP_W, D_MODEL).astype(BF16),
        w1=W['mlp_w1'].astype(BF16), w2=W['mlp_w2'].astype(BF16))


def _run_group(x, mod, first, st, P, cfg):
    dep, bsz = st['ssd'].shape[:2]
    hist = jnp.pad(st['ssd_conv'], ((0, 0), (0, 0), (HIST - (SSD_CONV - 1), 0), (0, 0)))
    ssd_st = (hist, st['ssd'])
    rwkv_st = (st['rwkv_shift'].reshape(dep, bsz, 1, RWKV_PROJ), st['rwkv'])
    gla_st = (st['gla'],)
    s5_st = (st['s5_re'].reshape(dep, bsz, S5_CH), st['s5_im'].reshape(dep, bsz, S5_CH))
    new = dict(ssd=None, rwkv=None, gla=None, s5=None)
    for l in range(DEPTH):
        p_ssd, p_rwkv, p_gla, p_s5 = _in_proj(x, mod, first, P['g_mix_pre'], P['in_w'], l, *cfg['tok'])
        y_ssd, *new['ssd'] = _ssd(p_ssd, ssd_st, P['ssd'], l, new['ssd'], *cfg['ssd'])
        y_rwkv, *new['rwkv'] = _rwkv(p_rwkv, rwkv_st, P['rwkv'], l, new['rwkv'], *cfg['rwkv'])
        y_gla, *new['gla'] = _gla(p_gla, gla_st, P['gla'], l, new['gla'], *cfg['gla'])
        y_s5, *new['s5'] = _s5(p_s5, s5_st, P['s5'], l, new['s5'], *cfg['s5'])
        x = _out_mlp(x, (y_ssd, y_rwkv, y_gla, y_s5), mod, first, P['g_mix_post'], P['g_ffn_pre'],
                     P['g_ffn_post'], P['w_out'], P['w1'], P['w2'], l, *cfg['tok'])
    hist_new, ssd_new = new['ssd']
    shift_new, rwkv_new = new['rwkv']
    s5r_new, s5i_new = new['s5']
    return x, dict(
        ssd=ssd_new, ssd_conv=hist_new[:, :, HIST - (SSD_CONV - 1):, :],
        rwkv=rwkv_new, rwkv_shift=shift_new.reshape(dep, bsz, RWKV_PROJ), gla=new['gla'][0],
        s5_re=s5r_new.reshape(dep, bsz, S5_GROUPS, S5_STATE),
        s5_im=s5i_new.reshape(dep, bsz, S5_GROUPS, S5_STATE))


_PROMPT_CFG = dict(tok=(1, 512), ssd=(1, 512), rwkv=(1, 512), gla=(1, 512, GLA_CS), s5=(8, 128))
_SAMPLE_CFG = dict(tok=(64, 8), ssd=(32, 8), rwkv=(32, 8), gla=(16, 8, 8), s5=(64, 8))


def kernel(x_prompt, x_sample, c_prompt, c_sample, state_ssd, state_ssd_conv, state_rwkv, state_rwkv_shift, state_gla, state_s5_re, state_s5_im, w_ada, b_ada, g_mix_pre, g_mix_post, g_ffn_pre, g_ffn_post, w_in, w_out, ssd_conv_w, ssd_conv_b, ssd_a_log, ssd_dt_bias, ssd_d, ssd_norm_g, rwkv_mu, rwkv_w0, rwkv_w2, rwkv_a0, rwkv_a2, rwkv_g2, rwkv_k_k, rwkv_k_a, rwkv_r_k, rwkv_lnx_w, rwkv_lnx_b, gla_wg2, gla_bg, gla_norm_g, s5_lam_re, s5_lam_im, s5_log_step, s5_b_re, s5_b_im, s5_c_re, s5_c_im, s5_d, s5_w_glu, s5_b_glu, mlp_w1, mlp_w2):
    W = dict(w_in=w_in, w_out=w_out, g_mix_pre=g_mix_pre, g_mix_post=g_mix_post,
             g_ffn_pre=g_ffn_pre, g_ffn_post=g_ffn_post,
             ssd_conv_w=ssd_conv_w, ssd_conv_b=ssd_conv_b, ssd_a_log=ssd_a_log,
             ssd_dt_bias=ssd_dt_bias, ssd_d=ssd_d, ssd_norm_g=ssd_norm_g,
             rwkv_mu=rwkv_mu, rwkv_w0=rwkv_w0, rwkv_w2=rwkv_w2, rwkv_a0=rwkv_a0, rwkv_a2=rwkv_a2,
             rwkv_g2=rwkv_g2, rwkv_k_k=rwkv_k_k, rwkv_k_a=rwkv_k_a, rwkv_r_k=rwkv_r_k,
             rwkv_lnx_w=rwkv_lnx_w, rwkv_lnx_b=rwkv_lnx_b,
             gla_wg2=gla_wg2, gla_bg=gla_bg, gla_norm_g=gla_norm_g,
             s5_lam_re=s5_lam_re, s5_lam_im=s5_lam_im, s5_log_step=s5_log_step,
             s5_b_re=s5_b_re, s5_b_im=s5_b_im, s5_c_re=s5_c_re, s5_c_im=s5_c_im,
             s5_d=s5_d, s5_w_glu=s5_w_glu, s5_b_glu=s5_b_glu, mlp_w1=mlp_w1, mlp_w2=mlp_w2)
    P = _prep_params(W)

    nbp, nbs = x_prompt.shape[0], x_sample.shape[0]
    mod = _ada(jnp.concatenate([c_sample, c_prompt], axis=0), w_ada, b_ada)
    mod = mod.reshape(DEPTH, nbs + nbp, N_MOD, D_MODEL)

    st_sample = dict(ssd=state_ssd, ssd_conv=state_ssd_conv, rwkv=state_rwkv,
                     rwkv_shift=state_rwkv_shift, gla=state_gla, s5_re=state_s5_re, s5_im=state_s5_im)
    st_prompt = {n: jnp.zeros((DEPTH, nbp) + v.shape[2:], v.dtype) for n, v in st_sample.items()}

    y_prompt, sp = _run_group(x_prompt, mod, nbs, st_prompt, P, _PROMPT_CFG)
    y_sample, ss = _run_group(x_sample, mod, 0, st_sample, P, _SAMPLE_CFG)
    return (y_prompt, y_sample,
            sp['ssd'], ss['ssd'], sp['ssd_conv'], ss['ssd_conv'],
            sp['rwkv'], ss['rwkv'], sp['rwkv_shift'], ss['rwkv_shift'],
            sp['gla'], ss['gla'], sp['s5_re'], ss['s5_re'], sp['s5_im'], ss['s5_im'])
```

```python
import functools
import math

import jax
import jax.numpy as jnp
from jax import lax
from jax.experimental import pallas as pl
from jax.experimental.pallas import tpu as pltpu

F32 = jnp.float32
BF16 = jnp.bfloat16

D_MODEL = 1024
DEPTH = 2
GROUP_W = 256
D_FF = 4096
N_MOD = 6
EPS = 1e-6

SSD_HEADS = 4
SSD_HEADDIM = 64
SSD_STATE = 64
SSD_NGROUPS = 2
SSD_BC = SSD_NGROUPS * SSD_STATE
SSD_CONV = 4
SSD_CONV_CH = GROUP_W + 2 * SSD_BC
SSD_PROJ = GROUP_W + SSD_CONV_CH + SSD_HEADS
SSD_PW = GROUP_W + SSD_CONV_CH + GROUP_W

RWKV_HEADS = 4
RWKV_HEADDIM = 64
RWKV_LORA = 128
RWKV_PROJ = 3 * GROUP_W + RWKV_LORA
RWKV_LN_EPS = 64e-5

GLA_HEADS = 4
GLA_DK = 32
GLA_DV = 64
GLA_GATE_LORA = 16
GLA_TAU = 16.0
GLA_QK = GLA_HEADS * GLA_DK
GLA_PROJ = 2 * GLA_QK + GROUP_W + GLA_GATE_LORA + GROUP_W
GLA_PW = 2 * GLA_QK + GROUP_W + 128 + GROUP_W

S5_GROUP = 16
S5_GROUPS = 16
S5_STATE = 64
S5_CH = S5_GROUPS * S5_STATE
S5_PROJ = GROUP_W

LANES = 128
HIST = 8
CHUNK = 64
GLA_CS = 32
VMEM_LIMIT = 56 * 1024 * 1024


def _dot(a, b, dims=(((1,), (0,)), ((), ()))):
    return lax.dot_general(a.astype(BF16), b.astype(BF16), dims, preferred_element_type=F32)


_NT = (((1,), (1,)), ((), ()))
_TN = (((0,), (0,)), ((), ()))
_NN = (((1,), (0,)), ((), ()))


def _split2(a):
    hi = a.astype(BF16)
    lo = (a - hi.astype(F32)).astype(BF16)
    return hi, lo


def _dot3(a, b, dims=_NN):
    ah, al = _split2(a)
    bh, bl = _split2(b)
    d = lambda x, y: lax.dot_general(x, y, dims, preferred_element_type=F32)
    return d(ah, bh) + (d(ah, bl) + d(al, bh))


_dot_a = _dot
_dot_inv = _dot
_dot_app = _dot
_dot_st = _dot


def _dot_exact_lhs(m, x, dims=_NN):
    mb = m.astype(BF16)
    x1 = x.astype(BF16)
    r1 = x - x1.astype(F32)
    x2 = r1.astype(BF16)
    x3 = (r1 - x2.astype(F32)).astype(BF16)
    d = lambda y: lax.dot_general(mb, y, dims, preferred_element_type=F32)
    return d(x1) + (d(x2) + d(x3))


def _dot_exact_rhs(x, m):
    mb = m.astype(BF16)
    x1 = x.astype(BF16)
    r1 = x - x1.astype(F32)
    x2 = r1.astype(BF16)
    x3 = (r1 - x2.astype(F32)).astype(BF16)
    d = lambda y: lax.dot_general(y, mb, _NN, preferred_element_type=F32)
    return d(x1) + (d(x2) + d(x3))


def _segsum(x, width):
    n = x.shape[-1]
    r = lax.broadcasted_iota(jnp.int32, (n, n), 0) // width
    c = lax.broadcasted_iota(jnp.int32, (n, n), 1) // width
    j = jnp.where(r == c, 1.0, 0.0).astype(BF16)
    hi, lo = _split2(x)
    d = lambda y: lax.dot_general(y, j, _NN, preferred_element_type=F32)
    return d(hi) + d(lo)


def _sigmoid(x):
    return 1.0 / (1.0 + jnp.exp(-x))


def _silu(x):
    return x * _sigmoid(x)


def _softplus(x):
    return jnp.maximum(x, 0.0) + jnp.log1p(jnp.exp(-jnp.abs(x)))


def _rms(x, g):
    return x * lax.rsqrt(jnp.mean(x * x, axis=-1, keepdims=True) + EPS) * g


def _chunk_masks(rows, lt):
    r = lax.broadcasted_iota(jnp.int32, (rows, rows), 0)
    c = lax.broadcasted_iota(jnp.int32, (rows, rows), 1)
    same = (r // lt) == (c // lt)
    return same & (c <= r), same & (c < r)


def _unit_cumsum(x, unit):
    incl, _ = _chunk_masks(CHUNK, unit)
    tri = jnp.where(incl, 1.0, 0.0)
    parts = [_dot_exact_lhs(tri, x[j * CHUNK:(j + 1) * CHUNK]) for j in range(x.shape[0] // CHUNK)]
    return parts[0] if len(parts) == 1 else jnp.concatenate(parts, axis=0)


def _lsel(a, l, **kw):
    shape = a.shape[1:]
    return pl.BlockSpec((None,) + shape, lambda i, c: (l,) + (0,) * len(shape), **kw)


def _lseq(a, l, nb):
    shape = a.shape[2:]
    return pl.BlockSpec((None, nb) + shape, lambda i, c: (l, i) + (0,) * len(shape))


def _mixer_call(kern, name, l, nb, lt, p, states, params, prev, scratch):
    bsz, seq, width = p.shape
    n_in = 1 + len(states) + len(params)
    n_st = len(states)
    alias = tuple(prev) if prev is not None else ()
    first = prev is None

    def body(*refs):
        ins, outs = refs[:n_in], refs[n_in + len(alias):]
        y_ref, st_refs, scratch_refs = outs[0], outs[1:1 + n_st], outs[1 + n_st:]
        if first:
            @pl.when(pl.program_id(1) == 0)
            def _():
                for r in st_refs:
                    r[1:] = jnp.zeros((r.shape[0] - 1,) + r.shape[1:], F32)
            st_refs = [r.at[0] for r in st_refs]
        return kern(*ins, y_ref, *st_refs, *scratch_refs)

    def st_out_spec(s):
        if not first:
            return _lseq(s, l, nb)
        shape = s.shape[2:]
        return pl.BlockSpec((s.shape[0], nb) + shape, lambda i, c: (0, i) + (0,) * len(shape))

    tok = lambda w: pl.BlockSpec((nb, lt, w), lambda i, c: (i, c, 0))
    return pl.pallas_call(
        body,
        out_shape=(jax.ShapeDtypeStruct((bsz, seq, GROUP_W), F32),)
        + tuple(jax.ShapeDtypeStruct(s.shape, F32) for s in states),
        grid=(bsz // nb, seq // lt),
        in_specs=[tok(width)] + [_lseq(s, l, nb) for s in states] + [_lsel(a, l) for a in params]
        + [pl.BlockSpec(memory_space=pl.ANY)] * len(alias),
        out_specs=(tok(GROUP_W),) + tuple(st_out_spec(s) for s in states),
        scratch_shapes=scratch,
        input_output_aliases={n_in + k: 1 + k for k in range(len(alias))},
        compiler_params=pltpu.CompilerParams(
            dimension_semantics=("parallel", "arbitrary"), vmem_limit_bytes=VMEM_LIMIT),
        name=name,
    )(p, *states, *params, *alias)


def _bcast_last(x, nb, lt):
    c = x.shape[-1]
    last = x.reshape(nb, lt, c)[:, lt - 1:lt, :]
    return last, jnp.broadcast_to(last, (nb, lt, c)).reshape(nb * lt, c)


def _ada_kernel(c_ref, w_ref, b_ref, o_ref):
    c = c_ref[...]
    o_ref[0] = _dot(_silu(c), w_ref[0]) + b_ref[0]


def _ada(c_all, w_ada, b_ada):
    rows = c_all.shape[0]
    n = N_MOD * D_MODEL
    tn = 1536
    return pl.pallas_call(
        _ada_kernel,
        out_shape=jax.ShapeDtypeStruct((DEPTH, rows, n), F32),
        grid=(DEPTH, n // tn),
        in_specs=[
            pl.BlockSpec((rows, D_MODEL), lambda l, j: (0, 0)),
            pl.BlockSpec((1, D_MODEL, tn), lambda l, j: (l, 0, j)),
            pl.BlockSpec((1, 1, tn), lambda l, j: (l, 0, j)),
        ],
        out_specs=pl.BlockSpec((1, rows, tn), lambda l, j: (l, 0, j)),
        compiler_params=pltpu.CompilerParams(
            dimension_semantics=("parallel", "parallel"), vmem_limit_bytes=VMEM_LIMIT),
        name="ada",
    )(c_all, w_ada, b_ada.reshape(DEPTH, 1, n))


def _in_kernel(x_ref, mod_ref, g_ref, wssd_ref, wrwkv_ref, wgla_ref, ws5_ref,
               ossd_ref, orwkv_ref, ogla_ref, os5_ref):
    nb, lt, d = x_ref.shape
    m = mod_ref[...]
    h = _rms(x_ref[...], g_ref[...]) * (1.0 + m[:, 1:2, :]) + m[:, 0:1, :]
    hb = h.reshape(nb * lt, d).astype(BF16)
    for w_ref, o_ref in ((wssd_ref, ossd_ref), (wrwkv_ref, orwkv_ref),
                         (wgla_ref, ogla_ref), (ws5_ref, os5_ref)):
        o_ref[...] = jnp.dot(hb, w_ref[...], preferred_element_type=F32).reshape(o_ref.shape)


def _mod_spec(mod, l, first, nb):
    return pl.BlockSpec((None, nb, N_MOD, mod.shape[-1]), lambda i, j: (l, first // nb + i, 0, 0))


def _in_proj(x, mod, first, g, wts, l, nb, lt):
    bsz, seq, d = x.shape
    widths = (SSD_PW, RWKV_PROJ, GLA_PW, S5_PROJ)
    return pl.pallas_call(
        _in_kernel,
        out_shape=tuple(jax.ShapeDtypeStruct((bsz, seq, w), F32) for w in widths),
        grid=(bsz // nb, seq // lt),
        in_specs=[pl.BlockSpec((nb, lt, d), lambda i, j: (i, j, 0)), _mod_spec(mod, l, first, nb),
                  _lsel(g, l)] + [_lsel(w, l) for w in wts],
        out_specs=tuple(pl.BlockSpec((nb, lt, w), lambda i, j: (i, j, 0)) for w in widths),
        compiler_params=pltpu.CompilerParams(
            dimension_semantics=("parallel", "parallel"), vmem_limit_bytes=VMEM_LIMIT),
        name="in_proj",
    )(x, mod, g, *wts)


def _out_mlp_kernel(x_ref, y0_ref, y1_ref, y2_ref, y3_ref, mod_ref, gpost_ref, gpre_ref, gfpost_ref,
                    wout_ref, w1_ref, w2_ref, o_ref):
    nb, lt, d = x_ref.shape
    rows = nb * lt
    m = mod_ref[...]
    mix = None
    for i, y_ref in enumerate((y0_ref, y1_ref, y2_ref, y3_ref)):
        t = jnp.dot(y_ref[...].reshape(rows, GROUP_W).astype(BF16), wout_ref[i],
                    preferred_element_type=F32)
        mix = t if mix is None else mix + t
    x1 = x_ref[...] + m[:, 2:3, :] * _rms(mix, gpost_ref[0]).reshape(nb, lt, d)
    h = _rms(x1, gpre_ref[...]) * (1.0 + m[:, 4:5, :]) + m[:, 3:4, :]
    hb = h.reshape(rows, d).astype(BF16)
    acc = None
    step = 1024
    for j in range(D_FF // step):
        a = jnp.dot(hb, w1_ref[:, j * step:(j + 1) * step], preferred_element_type=F32)
        a = jnp.square(jnp.maximum(a, 0.0)).astype(BF16)
        t = jnp.dot(a, w2_ref[j * step:(j + 1) * step, :], preferred_element_type=F32)
        acc = t if acc is None else acc + t
    o_ref[...] = x1 + m[:, 5:6, :] * _rms(acc, gfpost_ref[0]).reshape(nb, lt, d)


def _out_mlp(x, ys, mod, first, gpost, gpre, gfpost, wout, w1, w2, l, nb, lt):
    bsz, seq, d = x.shape
    tok = lambda i, j: (i, j, 0)
    one = pl.Buffered(1)
    return pl.pallas_call(
        _out_mlp_kernel,
        out_shape=jax.ShapeDtypeStruct((bsz, seq, d), F32),
        grid=(bsz // nb, seq // lt),
        in_specs=[pl.BlockSpec((nb, lt, d), tok)]
        + [pl.BlockSpec((nb, lt, GROUP_W), tok)] * 4
        + [_mod_spec(mod, l, first, nb)]
        + [_lsel(g, l) for g in (gpost, gpre, gfpost)]
        + [_lsel(w, l, pipeline_mode=one) for w in (wout, w1, w2)],
        out_specs=pl.BlockSpec((nb, lt, d), tok),
        compiler_params=pltpu.CompilerParams(
            dimension_semantics=("parallel", "parallel"), vmem_limit_bytes=VMEM_LIMIT),
        name="out_mlp",
    )(x, *ys, mod, gpost, gpre, gfpost, wout, w1, w2)


def _ssd_kernel(p_ref, hist_ref, s0_ref, cw_ref, cb_ref, alog_ref, dtb_ref, dsk_ref, ng_ref,
                y_ref, hout_ref, sout_ref, ext_ref, yscr_ref, *, nb, lt):
    rows = nb * lt
    hd = SSD_HEADDIM

    @pl.when(pl.program_id(1) == 0)
    def _():
        ext_ref[:, 0:HIST, :] = hist_ref[...]
        sout_ref[...] = s0_ref[...]

    p = p_ref[...]
    ext_ref[:, HIST:HIST + lt, :] = p[:, :, GROUP_W:GROUP_W + SSD_CONV_CH]
    conv = cb_ref[...]
    for j in range(SSD_CONV):
        o = HIST - (SSD_CONV - 1) + j
        conv = conv + ext_ref[:, o:o + lt, :] * cw_ref[j:j + 1, :]
    tail = ext_ref[:, lt:lt + HIST, :]
    ext_ref[:, 0:HIST, :] = tail
    hout_ref[...] = tail

    xbc = _silu(conv).reshape(rows, SSD_CONV_CH)
    xs = xbc[:, 0:GROUP_W]
    bm = xbc[:, GROUP_W:GROUP_W + SSD_BC]
    cm = xbc[:, GROUP_W + SSD_BC:]
    z = p[:, :, 0:GROUP_W].reshape(rows, GROUP_W)
    dt = _softplus(p[:, :, GROUP_W + SSD_CONV_CH:].reshape(rows, GROUP_W) + dtb_ref[...])
    a = dt * (-jnp.exp(alog_ref[...]))
    unit = lt if nb > 1 else CHUNK
    units = rows // unit
    per_chunk = CHUNK // unit
    cum = _unit_cumsum(a, unit)
    last, last_b = _bcast_last(cum, units, unit)
    xdt = xs * dt
    xw = xdt * jnp.exp(last_b - cum)
    ecum = jnp.exp(cum)
    elast = jnp.exp(last)
    ones_row = jnp.full((CHUNK, hd), 1.0 / hd, F32)
    cmask, _ = _chunk_masks(CHUNK, unit)
    heads = range(SSD_HEADS)
    hs = lambda h: slice(h * hd, (h + 1) * hd)
    gs = lambda h: slice((h // 2) * SSD_STATE, (h // 2 + 1) * SSD_STATE)
    cs = lambda j: slice(j * CHUNK, (j + 1) * CHUNK)
    us = lambda u: slice(u * unit, (u + 1) * unit)

    yh = {}
    for j in range(rows // CHUNK):
        cb = [_dot(cm[cs(j), gs(2 * g)], bm[cs(j), gs(2 * g)], _NT) for g in range(SSD_NGROUPS)]
        for h in heads:
            ccol = cum[cs(j), hs(h)]
            crow = _dot_exact_lhs(ones_row, ccol, _NT)
            decay = jnp.exp(jnp.where(cmask, ccol - crow, -jnp.inf))
            yh[j, h] = _dot(cb[h // 2] * decay, xdt[cs(j), hs(h)])
    ds = {(u, h): _dot(xw[us(u), hs(h)], bm[us(u), gs(h)], _TN) for u in range(units) for h in heads}

    before = {}
    for h in heads:
        if nb > 1:
            for u in range(units):
                before[u, h] = sout_ref[u, h]
                sout_ref[u, h] = before[u, h] * elast[u][:, hs(h)] + ds[u, h]
        else:
            s_cur = sout_ref[0, h]
            for u in range(units):
                before[u, h] = s_cur
                s_cur = s_cur * elast[u][:, hs(h)] + ds[u, h]
            sout_ref[0, h] = s_cur

    for j in range(rows // CHUNK):
        for h in heads:
            inter = [_dot(cm[us(u), gs(h)], before[u, h], _NT)
                     for u in range(j * per_chunk, (j + 1) * per_chunk)]
            inter = inter[0] if per_chunk == 1 else jnp.concatenate(inter, axis=0)
            yscr_ref[cs(j), hs(h)] = yh[j, h] + inter * ecum[cs(j), hs(h)]

    y = yscr_ref[...] + dsk_ref[...] * xs
    y = y * _silu(z)
    y = y * lax.rsqrt(_segsum(y * y, hd) * (1.0 / hd) + EPS) * ng_ref[...]
    y_ref[...] = y.reshape(nb, lt, GROUP_W)


def _ssd(p, states, prm, l, prev, nb, lt):
    assert CHUNK == SSD_HEADDIM
    assert (nb > 1 and (nb * lt) % CHUNK == 0 and CHUNK % lt == 0) or (nb == 1 and lt % CHUNK == 0)
    kern = functools.partial(_ssd_kernel, nb=nb, lt=lt)
    return _mixer_call(kern, "ssd", l, nb, lt, p, states, prm, prev,
                       [pltpu.VMEM((nb, HIST + lt, SSD_CONV_CH), F32),
                        pltpu.VMEM((nb * lt, GROUP_W), F32)])


def _unit_lower_inverse(a2, n, levels, dot):
    ri = lax.broadcasted_iota(jnp.int32, (n, n), 0)
    ci = lax.broadcasted_iota(jnp.int32, (n, n), 1)
    x = [jnp.where(ri == ci, 1.0, 0.0) - jnp.where((ri % 2 == 1) & (ci == ri - 1), a, 0.0) for a in a2]
    s = 2
    while s < levels:
        off = ((ri // s) % 2 == 1) & ((ci // s) == (ri // s) - 1)
        t = [dot(xi, jnp.where(off, a, 0.0)) for xi, a in zip(x, a2)]
        x = [xi - dot(ti, xi) for xi, ti in zip(x, t)]
        s *= 2
    return x


def _rwkv_chunk_multi(ops, sout_ref, yscr_ref, nb, lt):
    hd = RWKV_HEADDIM
    per_chunk = CHUNK // lt
    idx = [(c, h) for c in range(nb // per_chunk) for h in range(RWKV_HEADS)]
    n = range(len(idx))
    incl, strict = _chunk_masks(CHUNK, lt)
    mask2 = jnp.concatenate([strict, incl], axis=0)
    top = lambda m: m[0:CHUNK]
    bot = lambda m: m[CHUNK:2 * CHUNK]
    bl = lambda name: [ops[name][c * CHUNK:(c + 1) * CHUNK, h * hd:(h + 1) * hd] for c, h in idx]
    rt, kt, k2t, bt, k2g, bg, v = (bl(name) for name in ('rt', 'kt', 'k2t', 'bt', 'k2g', 'bg', 'v'))
    kr = [jnp.concatenate([kt[j], rt[j]], axis=0) for j in n]
    m1 = [jnp.where(mask2, _dot_a(kr[j], k2t[j], _NT), 0.0) for j in n]
    m2 = [jnp.where(mask2, _dot_a(kr[j], bt[j], _NT), 0.0) for j in n]
    x = _unit_lower_inverse([top(m) for m in m2], CHUNK, lt, _dot_inv)
    m1v = [_dot_app(m1[j], v[j]) for j in n]
    seq = lambda i: slice(i * lt, (i + 1) * lt)
    seqs = range(per_chunk)
    s_old = [[sout_ref[c * per_chunk + i, h] for i in seqs] for c, h in idx]
    pq = [[_dot_st(jnp.concatenate([kt[j][seq(i)], rt[j][seq(i)]], axis=0), s_old[j][i], _NT)
           for i in seqs] for j in n]
    p0 = [jnp.concatenate([pq[j][i][0:lt] for i in seqs], axis=0) for j in n]
    q0 = [jnp.concatenate([pq[j][i][lt:2 * lt] for i in seqs], axis=0) for j in n]
    u = [_dot_app(x[j], p0[j] + top(m1v[j])) for j in n]
    l2u = [_dot_app(bot(m2[j]), u[j]) for j in n]
    for j, (c, h) in enumerate(idx):
        yscr_ref[c * CHUNK:(c + 1) * CHUNK, h * hd:(h + 1) * hd] = q0[j] + bot(m1v[j]) - l2u[j]
        for i in seqs:
            sq = c * per_chunk + i
            sout_ref[sq, h] = (s_old[j][i] * ops['elast'][sq][:, h * hd:(h + 1) * hd]
                               + _dot_st(v[j][seq(i)], k2g[j][seq(i)], _TN)
                               - _dot_st(u[j][seq(i)], bg[j][seq(i)], _TN))


def _rwkv_chunks_seq(ops, state, yscr_ref, nc):
    hd = RWKV_HEADDIM
    pw = 2 * hd
    pairs = RWKV_HEADS // 2
    idx = [(c, q) for c in range(nc) for q in range(pairs)]
    n = range(len(idx))
    top = lambda m: m[0:CHUNK]
    bot = lambda m: m[CHUNK:2 * CHUNK]
    ri = lax.broadcasted_iota(jnp.int32, (CHUNK, pw), 0)
    cl = lax.broadcasted_iota(jnp.int32, (CHUNK, pw), 1) % hd
    left = lax.broadcasted_iota(jnp.int32, (CHUNK, pw), 1) < hd
    strict, incl = cl < ri, cl <= ri
    mask2 = jnp.concatenate([strict, incl], axis=0)
    eye = jnp.where(cl == ri, 1.0, 0.0)
    bd = lambda p: jnp.concatenate([jnp.where(left, p, 0.0), jnp.where(left, 0.0, p)], axis=0)
    on_diag = jnp.concatenate([left, ~left], axis=0)
    r2 = lax.broadcasted_iota(jnp.int32, (pw, pw), 0)
    c2 = lax.broadcasted_iota(jnp.int32, (pw, pw), 1)
    eye2 = jnp.where(r2 == c2, 1.0, 0.0)
    diag_pack = lambda m: jnp.where(left, top(m), bot(m))

    bl = lambda name: [ops[name][c * CHUNK:(c + 1) * CHUNK, q * pw:(q + 1) * pw] for c, q in idx]
    rt, kt, k2t, bt, k2g, bg, v = (bl(name) for name in ('rt', 'kt', 'k2t', 'bt', 'k2g', 'bg', 'v'))
    kr = [jnp.concatenate([kt[j], rt[j]], axis=0) for j in n]
    m1 = [jnp.where(mask2, _dot_a(kr[j], bd(k2t[j]), _NT), 0.0) for j in n]
    m2 = [jnp.where(mask2, _dot_a(kr[j], bd(bt[j]), _NT), 0.0) for j in n]
    x = [eye - jnp.where((ri % 2 == 1) & (cl == ri - 1), top(m), 0.0) for m in m2]
    s = 2
    while s < CHUNK:
        off = ((ri // s) % 2 == 1) & ((cl // s) == (ri // s) - 1)
        t = [_dot_inv(x[j], bd(jnp.where(off, top(m2[j]), 0.0))) for j in n]
        x = [x[j] - _dot_inv(t[j], bd(x[j])) for j in n]
        s *= 2
    m1v = [_dot_app(m1[j], bd(v[j])) for j in n]
    kp = [_dot_app(x[j], bd(kt[j])) for j in n]
    ul = [_dot_app(x[j], bd(top(m1v[j]))) for j in n]
    rp = [rt[j] - _dot_app(bot(m2[j]), bd(kp[j])) for j in n]
    yl = [bot(m1v[j]) - _dot_app(bot(m2[j]), bd(ul[j])) for j in n]
    mc = [eye2 * ops['elast'][c][:, q * pw:(q + 1) * pw]
          - jnp.where(on_diag, _dot_st(kp[j], bg[j], _TN), 0.0) for j, (c, q) in enumerate(idx)]
    dc = [diag_pack(_dot_st(v[j], k2g[j], _TN) - _dot_st(ul[j], bg[j], _TN)) for j in n]
    for c in range(nc):
        js = [c * pairs + q for q in range(pairs)]
        ys = [_dot_st(rp[j], bd(state[q]), _NT) + yl[j] for q, j in enumerate(js)]
        state = [_dot_st(state[q], mc[j]) + dc[j] for q, j in enumerate(js)]
        for q in range(pairs):
            yscr_ref[c * CHUNK:(c + 1) * CHUNK, q * pw:(q + 1) * pw] = ys[q]
    return state


def _rwkv_kernel(p_ref, sh0_ref, s0_ref, mu_ref, wl_ref, w0_ref, a0_ref, kk_ref, ka_ref, rk_ref,
                 lnw_ref, lnb_ref, y_ref, shout_ref, sout_ref, ext_ref, yscr_ref, *, nb, lt):
    rows = nb * lt
    hd = RWKV_HEADDIM
    gw = GROUP_W

    @pl.when(pl.program_id(1) == 0)
    def _():
        ext_ref[:, HIST - 1:HIST, :] = sh0_ref[...]
        sout_ref[...] = s0_ref[...]

    p = p_ref[...]
    ext_ref[:, HIST:HIST + lt, :] = p
    prev = ext_ref[:, HIST - 1:HIST - 1 + lt, :].reshape(rows, RWKV_PROJ)
    last_p = p[:, lt - 1:lt, :]
    ext_ref[:, HIST - 1:HIST, :] = last_p
    shout_ref[...] = last_p
    p = p.reshape(rows, RWKV_PROJ)

    pm = p + (prev - p) * mu_ref[...]
    r = pm[:, 0:gw]
    k = pm[:, gw:2 * gw]
    v = pm[:, 2 * gw:3 * gw]
    lo = pm[:, 3 * gw:]
    lane = lax.broadcasted_iota(jnp.int32, (rows, RWKV_LORA), 1)
    t = jnp.where(lane < 32, jnp.tanh(lo), jnp.where(lane < 64, lo, _sigmoid(lo)))
    lora = _dot(t, wl_ref[...])
    wlog = -_softplus(-(w0_ref[...] + lora[:, 0:gw])) - 0.5
    lw = -jnp.exp(wlog)
    a = _sigmoid(a0_ref[...] + lora[:, gw:2 * gw])
    gate = lora[:, 2 * gw:]
    kk = k * kk_ref[...]
    kk = kk * jnp.minimum(lax.rsqrt(_segsum(kk * kk, hd)), 1e12)
    k2 = k * (1.0 + (a - 1.0) * ka_ref[...])
    beta = kk * a

    unit = lt if nb > 1 else CHUNK
    gcum = _unit_cumsum(lw, unit)
    glast, glast_b = _bcast_last(gcum, rows // unit, unit)
    eg = jnp.exp(gcum)
    einv = jnp.exp(-gcum)
    etail = jnp.exp(glast_b - gcum)
    ops = dict(rt=r * eg, kt=kk * jnp.exp(gcum - lw), k2t=k2 * einv, bt=beta * einv,
               k2g=k2 * etail, bg=beta * etail, v=v, elast=jnp.exp(glast))
    if nb > 1:
        _rwkv_chunk_multi(ops, sout_ref, yscr_ref, nb, lt)
    else:
        pairs = range(RWKV_HEADS // 2)
        state = [jnp.concatenate([sout_ref[0, 2 * q], sout_ref[0, 2 * q + 1]], axis=1) for q in pairs]
        state = _rwkv_chunks_seq(ops, state, yscr_ref, rows // CHUNK)
        for q in pairs:
            sout_ref[0, 2 * q] = state[q][:, 0:hd]
            sout_ref[0, 2 * q + 1] = state[q][:, hd:2 * hd]

    y = yscr_ref[...]
    mean = _segsum(y, hd) * (1.0 / hd)
    dlt = y - mean
    var = _segsum(dlt * dlt, hd) * (1.0 / hd)
    yn = dlt * lax.rsqrt(var + RWKV_LN_EPS) * lnw_ref[...] + lnb_ref[...]
    bonus = _segsum(r * k2 * rk_ref[...], hd) * v
    y_ref[...] = ((yn + bonus) * gate).reshape(nb, lt, gw)


def _rwkv(p, states, prm, l, prev, nb, lt):
    assert (nb > 1 and (nb * lt) % CHUNK == 0 and CHUNK % lt == 0) or (nb == 1 and lt % CHUNK == 0)
    kern = functools.partial(_rwkv_kernel, nb=nb, lt=lt)
    return _mixer_call(kern, "rwkv", l, nb, lt, p, states, prm, prev,
                       [pltpu.VMEM((nb, HIST + lt, RWKV_PROJ), F32),
                        pltpu.VMEM((nb * lt, GROUP_W), F32)])


def _gla_kernel(p_ref, s0_ref, wg2_ref, bg_ref, ng_ref, y_ref, sout_ref, yscr_ref, *, nb, lt, cs):
    rows = nb * lt
    dk, dv = GLA_DK, GLA_DV
    qk = GLA_QK
    units = rows // cs
    per_group = GLA_CS // cs

    @pl.when(pl.program_id(1) == 0)
    def _():
        sout_ref[...] = s0_ref[...]

    p = p_ref[...].reshape(rows, GLA_PW)
    q = p[:, 0:qk] * (dk ** -0.5)
    k = p[:, qk:2 * qk]
    v = p[:, 2 * qk:2 * qk + GROUP_W]
    dg = p[:, 2 * qk + GROUP_W:2 * qk + GROUP_W + 128]
    og = p[:, 2 * qk + GROUP_W + 128:]
    lg = -_softplus(-(_dot(dg, wg2_ref[...]) + bg_ref[...])) * (1.0 / GLA_TAU)
    b = _unit_cumsum(lg, cs)
    bl, bl_b = _bcast_last(b, units, cs)
    qd = q * jnp.exp(b)
    kd = k * jnp.exp(-b)
    kw = k * jnp.exp(bl_b - b)
    ebl = jnp.exp(bl)
    gmask, _ = _chunk_masks(GLA_CS, cs)

    heads = range(GLA_HEADS)
    ks = lambda h: slice(h * dk, (h + 1) * dk)
    vs = lambda h: slice(h * dv, (h + 1) * dv)
    gs = lambda g: slice(g * GLA_CS, (g + 1) * GLA_CS)
    us = lambda u: slice(u * cs, (u + 1) * cs)

    yh = {}
    for g in range(rows // GLA_CS):
        for h in heads:
            att = jnp.where(gmask, _dot(qd[gs(g), ks(h)], kd[gs(g), ks(h)], _NT), 0.0)
            yh[g, h] = _dot(att, v[gs(g), vs(h)])
    ds = {(u, h): _dot(kw[us(u), ks(h)], v[us(u), vs(h)], _TN) for u in range(units) for h in heads}
    ri = lax.broadcasted_iota(jnp.int32, (qk, qk), 0)
    ci = lax.broadcasted_iota(jnp.int32, (qk, qk), 1)
    ones_kv = jnp.ones((qk, dv), F32)
    ecol_all = [_dot_exact_rhs(jnp.where(ri == ci, ebl[u], 0.0), ones_kv) for u in range(units)]
    ecol = {(u, h): ecol_all[u][ks(h)] for u in range(units) for h in heads}

    before = {}
    for h in heads:
        if nb > 1:
            for u in range(units):
                before[u, h] = sout_ref[u, h]
                sout_ref[u, h] = before[u, h] * ecol[u, h] + ds[u, h]
        else:
            s_cur = sout_ref[0, h]
            for u in range(units):
                before[u, h] = s_cur
                s_cur = s_cur * ecol[u, h] + ds[u, h]
            sout_ref[0, h] = s_cur

    for g in range(rows // GLA_CS):
        for h in heads:
            inter = [_dot(qd[us(u), ks(h)], before[u, h])
                     for u in range(g * per_group, (g + 1) * per_group)]
            inter = inter[0] if per_group == 1 else jnp.concatenate(inter, axis=0)
            yscr_ref[gs(g), vs(h)] = yh[g, h] + inter

    y = yscr_ref[...]
    y = y * lax.rsqrt(_segsum(y * y, dv) * (1.0 / dv) + EPS) * ng_ref[...]
    y_ref[...] = (y * _silu(og)).reshape(nb, lt, GROUP_W)


def _gla(p, states, prm, l, prev, nb, lt, cs):
    assert (nb == 1 and cs == GLA_CS and lt % cs == 0) or (
        lt == cs and (nb * cs) % GLA_CS == 0 and GLA_CS % cs == 0)
    kern = functools.partial(_gla_kernel, nb=nb, lt=lt, cs=cs)
    return _mixer_call(kern, "gla", l, nb, lt, p, states, prm, prev,
                       [pltpu.VMEM((nb * lt, GROUP_W), F32)])


def _s5_pitch(lt):
    return lt if (lt // 8) % 2 == 1 else lt + 8


def _s5_kernel(u_ref, hr0_ref, hi0_ref, wb_ref, wc_ref, ar_ref, ai_ref, d_ref, wglu_ref, bglu_ref,
               y_ref, hr_ref, hi_ref, io_ref, h_ref, *, nb, lt):
    n = S5_CH
    wt = GROUP_W // LANES

    @pl.when(pl.program_id(1) == 0)
    def _():
        hr_ref[...] = hr0_ref[...]
        hi_ref[...] = hi0_ref[...]

    lanes = lambda x, j: x[:, j * LANES:(j + 1) * LANES]
    pitch = _s5_pitch(lt)
    seq_rows = lambda i: slice(i * pitch, i * pitch + lt)
    at_t = lambda t: pl.ds(t, nb, stride=pitch)
    step = lambda t: slice(t * nb, (t + 1) * nb)
    for j in range(wt):
        for i in range(nb):
            io_ref[j, seq_rows(i), :] = u_ref[i, :, j * LANES:(j + 1) * LANES]
    u = jnp.concatenate([jnp.concatenate([io_ref[j, at_t(t), :] for j in range(wt)], axis=1)
                         for t in range(lt)], axis=0)
    bu = _dot(u, wb_ref[...])
    ar = jnp.broadcast_to(ar_ref[...], (nb, n))
    ai = jnp.broadcast_to(ai_ref[...], (nb, n))
    hr = hr_ref[...]
    hi = hi_ref[...]
    for t in range(lt):
        hr, hi = (ar * hr - ai * hi + bu[step(t), 0:n], ar * hi + ai * hr + bu[step(t), n:2 * n])
        h_ref[step(t), 0:n] = hr
        h_ref[step(t), n:2 * n] = hi
    hr_ref[...] = hr
    hi_ref[...] = hi
    y = _dot(h_ref[:, 0:n], wc_ref[0]) - _dot(h_ref[:, n:2 * n], wc_ref[1])
    y = y + d_ref[...] * u
    y = 0.5 * y * (1.0 + jnp.tanh(math.sqrt(2.0 / math.pi) * (y + 0.044715 * (y * y * y))))
    y = y * _sigmoid(_dot(y, wglu_ref[...]) + bglu_ref[...])
    for t in range(lt):
        for j in range(wt):
            io_ref[j, at_t(t), :] = lanes(y[step(t)], j)
    for j in range(wt):
        for i in range(nb):
            y_ref[i, :, j * LANES:(j + 1) * LANES] = io_ref[j, seq_rows(i), :]


def _s5(u, states, prm, l, prev, nb, lt):
    kern = functools.partial(_s5_kernel, nb=nb, lt=lt)
    return _mixer_call(kern, "s5", l, nb, lt, u, states, prm, prev,
                       [pltpu.VMEM((GROUP_W // LANES, nb * _s5_pitch(lt), LANES), F32),
                        pltpu.VMEM((nb * lt, 2 * S5_CH), F32)])


def _s5_zoh_kernel(lr_ref, li_ref, ls_ref, bre_ref, bim_ref, ar_ref, ai_ref, bbre_ref, bbim_ref):
    lr = lr_ref[...]
    li = li_ref[...]
    dts = jnp.exp(ls_ref[...])
    mag = jnp.exp(lr * dts)
    ar = mag * jnp.cos(li * dts)
    ai = mag * jnp.sin(li * dts)
    den = lr * lr + li * li
    fr = ((ar - 1.0) * lr + ai * li) / den
    fi = (ai * lr - (ar - 1.0) * li) / den
    ar_ref[...] = ar
    ai_ref[...] = ai
    r, n = lr.shape
    per_input = lambda f: jnp.broadcast_to(f[:, None, :], (r, S5_GROUP, n)).reshape(r * S5_GROUP, n)
    fre, fie = per_input(fr), per_input(fi)
    bre = bre_ref[...]
    bim = bim_ref[...]
    bbre_ref[...] = fre * bre - fie * bim
    bbim_ref[...] = fre * bim + fie * bre


def _s5_zoh(lam_re, lam_im, log_step, b_re, b_im):
    dep, g, n = lam_re.shape
    rows = dep * g
    flat = lambda a: a.astype(F32).reshape(rows, n)
    flat_b = lambda b: jnp.swapaxes(b.astype(F32), -1, -2).reshape(rows * S5_GROUP, n)
    vm = pl.BlockSpec(memory_space=pltpu.VMEM)
    ar, ai, bbr, bbi = pl.pallas_call(
        _s5_zoh_kernel,
        out_shape=(jax.ShapeDtypeStruct((rows, n), F32),) * 2
        + (jax.ShapeDtypeStruct((rows * S5_GROUP, n), F32),) * 2,
        in_specs=[vm] * 5,
        out_specs=(vm,) * 4,
        name="s5_zoh",
    )(flat(lam_re), flat(lam_im), log_step.astype(F32).reshape(rows, 1), flat_b(b_re), flat_b(b_im))
    return (ar.reshape(dep, g, n), ai.reshape(dep, g, n),
            bbr.reshape(dep, g, S5_GROUP, n), bbi.reshape(dep, g, S5_GROUP, n))


def _block_diag_in(b):
    dep, g, i, n = b.shape
    eye = jnp.eye(g, dtype=b.dtype)
    return jnp.einsum('lgin,gh->lgihn', b, eye).reshape(dep, g * i, g * n)


def _block_diag_out(c):
    dep, g, o, n = c.shape
    eye = jnp.eye(g, dtype=c.dtype)
    return jnp.einsum('lgon,gh->lgnho', c, eye).reshape(dep, g * n, g * o)


def _prep_params(W):
    dep = W['w_in'].shape[0]
    row = lambda name: W[name].reshape(dep, 1, -1).astype(F32)
    rep = lambda name, n: jnp.repeat(W[name].reshape(dep, -1), n, axis=1).reshape(dep, 1, -1).astype(F32)
    w_in = W['w_in'].astype(BF16)
    o_r = SSD_PROJ
    o_g = o_r + RWKV_PROJ
    o_s = o_g + GLA_PROJ
    w_ssd = jnp.concatenate([w_in[..., :GROUP_W + SSD_CONV_CH],
                             jnp.repeat(w_in[..., GROUP_W + SSD_CONV_CH:o_r], SSD_HEADDIM, axis=2)], axis=2)
    w_rwkv = w_in[..., o_r:o_g]
    gq = o_g + 2 * GLA_QK + GROUP_W
    w_gla = jnp.concatenate([w_in[..., o_g:gq], w_in[..., gq:gq + GLA_GATE_LORA],
                             jnp.zeros((dep, D_MODEL, 128 - GLA_GATE_LORA), BF16),
                             w_in[..., gq + GLA_GATE_LORA:o_s]], axis=2)
    w_s5 = w_in[..., o_s:]
    in_w = (w_ssd, w_rwkv, w_gla, w_s5)

    ssd = (W['ssd_conv_w'].astype(F32), row('ssd_conv_b'), rep('ssd_a_log', SSD_HEADDIM),
           rep('ssd_dt_bias', SSD_HEADDIM), rep('ssd_d', SSD_HEADDIM), row('ssd_norm_g'))

    z = lambda r, c: jnp.zeros((dep, r, c), F32)
    w_lora = jnp.concatenate([
        jnp.concatenate([W['rwkv_w2'], z(32, 2 * GROUP_W)], axis=2),
        jnp.concatenate([z(32, GROUP_W), W['rwkv_a2'], z(32, GROUP_W)], axis=2),
        jnp.concatenate([z(64, 2 * GROUP_W), W['rwkv_g2']], axis=2)], axis=1).astype(BF16)
    rwkv = (row('rwkv_mu'), w_lora, row('rwkv_w0'), row('rwkv_a0'), row('rwkv_k_k'),
            row('rwkv_k_a'), row('rwkv_r_k'), row('rwkv_lnx_w'), row('rwkv_lnx_b'))

    wg2 = jnp.concatenate([W['gla_wg2'], z(128 - GLA_GATE_LORA, GLA_QK)], axis=1).astype(BF16)
    gla = (wg2, row('gla_bg'), row('gla_norm_g'))

    ar, ai, bb_re, bb_im = _s5_zoh(W['s5_lam_re'], W['s5_lam_im'], W['s5_log_step'],
                                   W['s5_b_re'], W['s5_b_im'])
    wb = jnp.concatenate([_block_diag_in(bb_re), _block_diag_in(bb_im)], axis=2).astype(BF16)
    wc = jnp.stack([_block_diag_out(W['s5_c_re']), _block_diag_out(W['s5_c_im'])], axis=1).astype(BF16)
    s5 = (wb, wc, ar.reshape(dep, 1, -1), ai.reshape(dep, 1, -1), row('s5_d'),
          W['s5_w_glu'].astype(BF16), row('s5_b_glu'))

    g4 = lambda name: W[name].reshape(dep, 1, 1, D_MODEL).astype(F32)
    return dict(
        in_w=in_w, ssd=ssd, rwkv=rwkv, gla=gla, s5=s5,
        g_mix_pre=g4('g_mix_pre'), g_mix_post=g4('g_mix_post'),
        g_ffn_pre=g4('g_ffn_pre'), g_ffn_post=g4('g_ffn_post'),
        w_out=W['w_out'].reshape(dep, 4, GROUP_W, D_MODEL).astype(BF16),
        w1=W['mlp_w1'].astype(BF16), w2=W['mlp_w2'].astype(BF16))


def _run_group(x, mod, first, st, P, cfg):
    dep, bsz = st['ssd'].shape[:2]
    hist = jnp.pad(st['ssd_conv'], ((0, 0), (0, 0), (HIST - (SSD_CONV - 1), 0), (0, 0)))
    ssd_st = (hist, st['ssd'])
    rwkv_st = (st['rwkv_shift'].reshape(dep, bsz, 1, RWKV_PROJ), st['rwkv'])
    gla_st = (st['gla'],)
    s5_st = (st['s5_re'].reshape(dep, bsz, S5_CH), st['s5_im'].reshape(dep, bsz, S5_CH))
    new = dict(ssd=None, rwkv=None, gla=None, s5=None)
    for l in range(DEPTH):
        p_ssd, p_rwkv, p_gla, p_s5 = _in_proj(x, mod, first, P['g_mix_pre'], P['in_w'], l, *cfg['tok'])
        y_ssd, *new['ssd'] = _ssd(p_ssd, ssd_st, P['ssd'], l, new['ssd'], *cfg['ssd'])
        y_rwkv, *new['rwkv'] = _rwkv(p_rwkv, rwkv_st, P['rwkv'], l, new['rwkv'], *cfg['rwkv'])
        y_gla, *new['gla'] = _gla(p_gla, gla_st, P['gla'], l, new['gla'], *cfg['gla'])
        y_s5, *new['s5'] = _s5(p_s5, s5_st, P['s5'], l, new['s5'], *cfg['s5'])
        x = _out_mlp(x, (y_ssd, y_rwkv, y_gla, y_s5), mod, first, P['g_mix_post'], P['g_ffn_pre'],
                     P['g_ffn_post'], P['w_out'], P['w1'], P['w2'], l, *cfg['tok'])
    hist_new, ssd_new = new['ssd']
    shift_new, rwkv_new = new['rwkv']
    s5r_new, s5i_new = new['s5']
    return x, dict(
        ssd=ssd_new, ssd_conv=hist_new[:, :, HIST - (SSD_CONV - 1):, :],
        rwkv=rwkv_new, rwkv_shift=shift_new.reshape(dep, bsz, RWKV_PROJ), gla=new['gla'][0],
        s5_re=s5r_new.reshape(dep, bsz, S5_GROUPS, S5_STATE),
        s5_im=s5i_new.reshape(dep, bsz, S5_GROUPS, S5_STATE))


_PROMPT_CFG = dict(tok=(1, 512), ssd=(1, 512), rwkv=(1, 512), gla=(1, 512, GLA_CS), s5=(8, 128))
_SAMPLE_CFG = dict(tok=(64, 8), ssd=(32, 8), rwkv=(32, 8), gla=(16, 8, 8), s5=(64, 8))


def kernel(x_prompt, x_sample, c_prompt, c_sample, state_ssd, state_ssd_conv, state_rwkv, state_rwkv_shift, state_gla, state_s5_re, state_s5_im, w_ada, b_ada, g_mix_pre, g_mix_post, g_ffn_pre, g_ffn_post, w_in, w_out, ssd_conv_w, ssd_conv_b, ssd_a_log, ssd_dt_bias, ssd_d, ssd_norm_g, rwkv_mu, rwkv_w0, rwkv_w2, rwkv_a0, rwkv_a2, rwkv_g2, rwkv_k_k, rwkv_k_a, rwkv_r_k, rwkv_lnx_w, rwkv_lnx_b, gla_wg2, gla_bg, gla_norm_g, s5_lam_re, s5_lam_im, s5_log_step, s5_b_re, s5_b_im, s5_c_re, s5_c_im, s5_d, s5_w_glu, s5_b_glu, mlp_w1, mlp_w2):
    W = dict(w_in=w_in, w_out=w_out, g_mix_pre=g_mix_pre, g_mix_post=g_mix_post,
             g_ffn_pre=g_ffn_pre, g_ffn_post=g_ffn_post,
             ssd_conv_w=ssd_conv_w, ssd_conv_b=ssd_conv_b, ssd_a_log=ssd_a_log,
             ssd_dt_bias=ssd_dt_bias, ssd_d=ssd_d, ssd_norm_g=ssd_norm_g,
             rwkv_mu=rwkv_mu, rwkv_w0=rwkv_w0, rwkv_w2=rwkv_w2, rwkv_a0=rwkv_a0, rwkv_a2=rwkv_a2,
             rwkv_g2=rwkv_g2, rwkv_k_k=rwkv_k_k, rwkv_k_a=rwkv_k_a, rwkv_r_k=rwkv_r_k,
             rwkv_lnx_w=rwkv_lnx_w, rwkv_lnx_b=rwkv_lnx_b,
             gla_wg2=gla_wg2, gla_bg=gla_bg, gla_norm_g=gla_norm_g,
             s5_lam_re=s5_lam_re, s5_lam_im=s5_lam_im, s5_log_step=s5_log_step,
             s5_b_re=s5_b_re, s5_b_im=s5_b_im, s5_c_re=s5_c_re, s5_c_im=s5_c_im,
             s5_d=s5_d, s5_w_glu=s5_w_glu, s5_b_glu=s5_b_glu, mlp_w1=mlp_w1, mlp_w2=mlp_w2)
    P = _prep_params(W)

    nbp, nbs = x_prompt.shape[0], x_sample.shape[0]
    mod = _ada(jnp.concatenate([c_sample, c_prompt], axis=0), w_ada, b_ada)
    mod = mod.reshape(DEPTH, nbs + nbp, N_MOD, D_MODEL)

    st_sample = dict(ssd=state_ssd, ssd_conv=state_ssd_conv, rwkv=state_rwkv,
                     rwkv_shift=state_rwkv_shift, gla=state_gla, s5_re=state_s5_re, s5_im=state_s5_im)
    st_prompt = {n: jnp.zeros((DEPTH, nbp) + v.shape[2:], v.dtype) for n, v in st_sample.items()}

    y_prompt, sp = _run_group(x_prompt, mod, nbs, st_prompt, P, _PROMPT_CFG)
    y_sample, ss = _run_group(x_sample, mod, 0, st_sample, P, _SAMPLE_CFG)
    return (y_prompt, y_sample,
            sp['ssd'], ss['ssd'], sp['ssd_conv'], ss['ssd_conv'],
            sp['rwkv'], ss['rwkv'], sp['rwkv_shift'], ss['rwkv_shift'],
            sp['gla'], ss['gla'], sp['s5_re'], ss['s5_re'], sp['s5_im'], ss['s5_im'])
```

```python
import functools
import math

import jax
import jax.numpy as jnp
from jax import lax
from jax.experimental import pallas as pl
from jax.experimental.pallas import tpu as pltpu

F32 = jnp.float32
BF16 = jnp.bfloat16

D_MODEL = 1024
DEPTH = 2
GROUP_W = 256
D_FF = 4096
N_MOD = 6
EPS = 1e-6

SSD_HEADS = 4
SSD_HEADDIM = 64
SSD_STATE = 64
SSD_NGROUPS = 2
SSD_BC = SSD_NGROUPS * SSD_STATE
SSD_CONV = 4
SSD_CONV_CH = GROUP_W + 2 * SSD_BC
SSD_PROJ = GROUP_W + SSD_CONV_CH + SSD_HEADS
SSD_PW = GROUP_W + SSD_CONV_CH + GROUP_W

RWKV_HEADS = 4
RWKV_HEADDIM = 64
RWKV_LORA = 128
RWKV_PROJ = 3 * GROUP_W + RWKV_LORA
RWKV_LN_EPS = 64e-5

GLA_HEADS = 4
GLA_DK = 32
GLA_DV = 64
GLA_GATE_LORA = 16
GLA_TAU = 16.0
GLA_QK = GLA_HEADS * GLA_DK
GLA_PROJ = 2 * GLA_QK + GROUP_W + GLA_GATE_LORA + GROUP_W
GLA_PW = 2 * GLA_QK + GROUP_W + 128 + GROUP_W

S5_GROUP = 16
S5_GROUPS = 16
S5_STATE = 64
S5_CH = S5_GROUPS * S5_STATE
S5_PROJ = GROUP_W

LANES = 128
HIST = 8
CHUNK = 64
GLA_CS = 32
VMEM_LIMIT = 56 * 1024 * 1024


def _dot(a, b, dims=(((1,), (0,)), ((), ()))):
    return lax.dot_general(a.astype(BF16), b.astype(BF16), dims, preferred_element_type=F32)


_NT = (((1,), (1,)), ((), ()))
_TN = (((0,), (0,)), ((), ()))
_NN = (((1,), (0,)), ((), ()))


def _split2(a):
    hi = a.astype(BF16)
    lo = (a - hi.astype(F32)).astype(BF16)
    return hi, lo


def _dot3(a, b, dims=_NN):
    ah, al = _split2(a)
    bh, bl = _split2(b)
    d = lambda x, y: lax.dot_general(x, y, dims, preferred_element_type=F32)
    return d(ah, bh) + (d(ah, bl) + d(al, bh))


_dot_a = _dot
_dot_inv = _dot
_dot_app = _dot
_dot_st = _dot


def _dot_exact_lhs(m, x, dims=_NN):
    mb = m.astype(BF16)
    x1 = x.astype(BF16)
    r1 = x - x1.astype(F32)
    x2 = r1.astype(BF16)
    x3 = (r1 - x2.astype(F32)).astype(BF16)
    d = lambda y: lax.dot_general(mb, y, dims, preferred_element_type=F32)
    return d(x1) + (d(x2) + d(x3))


def _dot_exact_rhs(x, m):
    mb = m.astype(BF16)
    x1 = x.astype(BF16)
    r1 = x - x1.astype(F32)
    x2 = r1.astype(BF16)
    x3 = (r1 - x2.astype(F32)).astype(BF16)
    d = lambda y: lax.dot_general(y, mb, _NN, preferred_element_type=F32)
    return d(x1) + (d(x2) + d(x3))


def _segsum(x, width):
    n = x.shape[-1]
    r = lax.broadcasted_iota(jnp.int32, (n, n), 0) // width
    c = lax.broadcasted_iota(jnp.int32, (n, n), 1) // width
    j = jnp.where(r == c, 1.0, 0.0).astype(BF16)
    hi, lo = _split2(x)
    d = lambda y: lax.dot_general(y, j, _NN, preferred_element_type=F32)
    return d(hi) + d(lo)


def _sigmoid(x):
    return 1.0 / (1.0 + jnp.exp(-x))


def _silu(x):
    return x * _sigmoid(x)


def _softplus(x):
    return jnp.maximum(x, 0.0) + jnp.log1p(jnp.exp(-jnp.abs(x)))


def _rms(x, g):
    return x * lax.rsqrt(jnp.mean(x * x, axis=-1, keepdims=True) + EPS) * g


def _chunk_masks(rows, lt):
    r = lax.broadcasted_iota(jnp.int32, (rows, rows), 0)
    c = lax.broadcasted_iota(jnp.int32, (rows, rows), 1)
    same = (r // lt) == (c // lt)
    return same & (c <= r), same & (c < r)


def _unit_cumsum(x, unit):
    incl, _ = _chunk_masks(CHUNK, unit)
    tri = jnp.where(incl, 1.0, 0.0)
    parts = [_dot_exact_lhs(tri, x[j * CHUNK:(j + 1) * CHUNK]) for j in range(x.shape[0] // CHUNK)]
    return parts[0] if len(parts) == 1 else jnp.concatenate(parts, axis=0)


def _lsel(a, l, **kw):
    shape = a.shape[1:]
    return pl.BlockSpec((None,) + shape, lambda i, c: (l,) + (0,) * len(shape), **kw)


def _lseq(a, l, nb):
    shape = a.shape[2:]
    return pl.BlockSpec((None, nb) + shape, lambda i, c: (l, i) + (0,) * len(shape))


def _mixer_call(kern, name, l, nb, lt, p, states, params, prev, scratch):
    bsz, seq, width = p.shape
    n_in = 1 + len(states) + len(params)
    n_st = len(states)
    alias = tuple(prev) if prev is not None else ()
    first = prev is None

    def body(*refs):
        ins, outs = refs[:n_in], refs[n_in + len(alias):]
        y_ref, st_refs, scratch_refs = outs[0], outs[1:1 + n_st], outs[1 + n_st:]
        if first:
            @pl.when(pl.program_id(1) == 0)
            def _():
                for r in st_refs:
                    r[1:] = jnp.zeros((r.shape[0] - 1,) + r.shape[1:], F32)
            st_refs = [r.at[0] for r in st_refs]
        return kern(*ins, y_ref, *st_refs, *scratch_refs)

    def st_out_spec(s):
        if not first:
            return _lseq(s, l, nb)
        shape = s.shape[2:]
        return pl.BlockSpec((s.shape[0], nb) + shape, lambda i, c: (0, i) + (0,) * len(shape))

    tok = lambda w: pl.BlockSpec((nb, lt, w), lambda i, c: (i, c, 0))
    return pl.pallas_call(
        body,
        out_shape=(jax.ShapeDtypeStruct((bsz, seq, GROUP_W), F32),)
        + tuple(jax.ShapeDtypeStruct(s.shape, F32) for s in states),
        grid=(bsz // nb, seq // lt),
        in_specs=[tok(width)] + [_lseq(s, l, nb) for s in states] + [_lsel(a, l) for a in params]
        + [pl.BlockSpec(memory_space=pl.ANY)] * len(alias),
        out_specs=(tok(GROUP_W),) + tuple(st_out_spec(s) for s in states),
        scratch_shapes=scratch,
        input_output_aliases={n_in + k: 1 + k for k in range(len(alias))},
        compiler_params=pltpu.CompilerParams(
            dimension_semantics=("parallel", "arbitrary"), vmem_limit_bytes=VMEM_LIMIT),
        name=name,
    )(p, *states, *params, *alias)


def _bcast_last(x, nb, lt):
    c = x.shape[-1]
    last = x.reshape(nb, lt, c)[:, lt - 1:lt, :]
    return last, jnp.broadcast_to(last, (nb, lt, c)).reshape(nb * lt, c)


def _ada_kernel(c_ref, w_ref, b_ref, o_ref):
    c = c_ref[...]
    o_ref[0] = _dot(_silu(c), w_ref[0]) + b_ref[0]


def _ada(c_all, w_ada, b_ada):
    rows = c_all.shape[0]
    n = N_MOD * D_MODEL
    tn = 1536
    return pl.pallas_call(
        _ada_kernel,
        out_shape=jax.ShapeDtypeStruct((DEPTH, rows, n), F32),
        grid=(DEPTH, n // tn),
        in_specs=[
            pl.BlockSpec((rows, D_MODEL), lambda l, j: (0, 0)),
            pl.BlockSpec((1, D_MODEL, tn), lambda l, j: (l, 0, j)),
            pl.BlockSpec((1, 1, tn), lambda l, j: (l, 0, j)),
        ],
        out_specs=pl.BlockSpec((1, rows, tn), lambda l, j: (l, 0, j)),
        compiler_params=pltpu.CompilerParams(
            dimension_semantics=("parallel", "parallel"), vmem_limit_bytes=VMEM_LIMIT),
        name="ada",
    )(c_all, w_ada, b_ada.reshape(DEPTH, 1, n))


def _in_kernel(x_ref, mod_ref, g_ref, wssd_ref, wrwkv_ref, wgla_ref, ws5_ref,
               ossd_ref, orwkv_ref, ogla_ref, os5_ref):
    nb, lt, d = x_ref.shape
    for s in range(2):
        if nb == 1:
            sel = (slice(None), slice(s * (lt // 2), (s + 1) * (lt // 2)))
            m = mod_ref[...]
        else:
            sel = (slice(s * (nb // 2), (s + 1) * (nb // 2)),)
            m = mod_ref[sel]
        x = x_ref[sel]
        h = _rms(x, g_ref[...]) * (1.0 + m[:, 1:2, :]) + m[:, 0:1, :]
        hb = h.reshape(x.shape[0] * x.shape[1], d).astype(BF16)
        for w_ref, o_ref in ((wssd_ref, ossd_ref), (wrwkv_ref, orwkv_ref),
                             (wgla_ref, ogla_ref), (ws5_ref, os5_ref)):
            o_ref[sel] = jnp.dot(hb, w_ref[...], preferred_element_type=F32).reshape(
                x.shape[:2] + (o_ref.shape[-1],))


def _mod_spec(mod, l, first, nb):
    return pl.BlockSpec((None, nb, N_MOD, mod.shape[-1]), lambda i, j: (l, first // nb + i, 0, 0))


def _in_proj(x, mod, first, g, wts, l, nb, lt):
    bsz, seq, d = x.shape
    widths = (SSD_PW, RWKV_PROJ, GLA_PW, S5_PROJ)
    return pl.pallas_call(
        _in_kernel,
        out_shape=tuple(jax.ShapeDtypeStruct((bsz, seq, w), F32) for w in widths),
        grid=(bsz // nb, seq // lt),
        in_specs=[pl.BlockSpec((nb, lt, d), lambda i, j: (i, j, 0)), _mod_spec(mod, l, first, nb),
                  _lsel(g, l)] + [_lsel(w, l) for w in wts],
        out_specs=tuple(pl.BlockSpec((nb, lt, w), lambda i, j: (i, j, 0)) for w in widths),
        compiler_params=pltpu.CompilerParams(
            dimension_semantics=("parallel", "parallel"), vmem_limit_bytes=VMEM_LIMIT),
        name="in_proj",
    )(x, mod, g, *wts)


def _out_mlp_kernel(x_ref, y0_ref, y1_ref, y2_ref, y3_ref, mod_ref, gpost_ref, gpre_ref, gfpost_ref,
                    wout_ref, w1_ref, w2_ref, o_ref):
    nb, lt, d = x_ref.shape
    if nb == 1:
        hshape = (1, lt // 2, d)
        half = lambda ref, s: ref[:, s * (lt // 2):(s + 1) * (lt // 2), :]
        mods = [mod_ref[...]] * 2
    else:
        hshape = (nb // 2, lt, d)
        half = lambda ref, s: ref[s * (nb // 2):(s + 1) * (nb // 2)]
        mods = [half(mod_ref, s) for s in range(2)]
    hrows = hshape[0] * hshape[1]
    x1s, hbs = [], []
    for s in range(2):
        m = mods[s]
        mix = None
        for i, y_ref in enumerate((y0_ref, y1_ref, y2_ref, y3_ref)):
            t = jnp.dot(half(y_ref, s).reshape(hrows, GROUP_W).astype(BF16), wout_ref[i],
                        preferred_element_type=F32)
            mix = t if mix is None else mix + t
        x1 = half(x_ref, s) + m[:, 2:3, :] * _rms(mix, gpost_ref[0]).reshape(hshape)
        h = _rms(x1, gpre_ref[...]) * (1.0 + m[:, 4:5, :]) + m[:, 3:4, :]
        x1s.append(x1)
        hbs.append(h.reshape(hrows, d).astype(BF16))
    accs = [None, None]
    step = 1024
    for j in range(D_FF // step):
        for s in range(2):
            a = jnp.dot(hbs[s], w1_ref[:, j * step:(j + 1) * step], preferred_element_type=F32)
            a = jnp.square(jnp.maximum(a, 0.0)).astype(BF16)
            t = jnp.dot(a, w2_ref[j * step:(j + 1) * step, :], preferred_element_type=F32)
            accs[s] = t if accs[s] is None else accs[s] + t
    for s in range(2):
        out = x1s[s] + mods[s][:, 5:6, :] * _rms(accs[s], gfpost_ref[0]).reshape(hshape)
        if nb == 1:
            o_ref[:, s * (lt // 2):(s + 1) * (lt // 2), :] = out
        else:
            o_ref[s * (nb // 2):(s + 1) * (nb // 2)] = out


def _out_mlp(x, ys, mod, first, gpost, gpre, gfpost, wout, w1, w2, l, nb, lt):
    bsz, seq, d = x.shape
    tok = lambda i, j: (i, j, 0)
    one = pl.Buffered(1)
    return pl.pallas_call(
        _out_mlp_kernel,
        out_shape=jax.ShapeDtypeStruct((bsz, seq, d), F32),
        grid=(bsz // nb, seq // lt),
        in_specs=[pl.BlockSpec((nb, lt, d), tok)]
        + [pl.BlockSpec((nb, lt, GROUP_W), tok)] * 4
        + [_mod_spec(mod, l, first, nb)]
        + [_lsel(g, l) for g in (gpost, gpre, gfpost)]
        + [_lsel(w, l, pipeline_mode=one) for w in (wout, w1, w2)],
        out_specs=pl.BlockSpec((nb, lt, d), tok),
        compiler_params=pltpu.CompilerParams(
            dimension_semantics=("parallel", "parallel"), vmem_limit_bytes=VMEM_LIMIT),
        name="out_mlp",
    )(x, *ys, mod, gpost, gpre, gfpost, wout, w1, w2)


def _ssd_kernel(p_ref, hist_ref, s0_ref, cw_ref, cb_ref, alog_ref, dtb_ref, dsk_ref, ng_ref,
                y_ref, hout_ref, sout_ref, ext_ref, yscr_ref, *, nb, lt):
    rows = nb * lt
    hd = SSD_HEADDIM

    @pl.when(pl.program_id(1) == 0)
    def _():
        ext_ref[:, 0:HIST, :] = hist_ref[...]
        sout_ref[...] = s0_ref[...]

    p = p_ref[...]
    ext_ref[:, HIST:HIST + lt, :] = p[:, :, GROUP_W:GROUP_W + SSD_CONV_CH]
    conv = cb_ref[...]
    for j in range(SSD_CONV):
        o = HIST - (SSD_CONV - 1) + j
        conv = conv + ext_ref[:, o:o + lt, :] * cw_ref[j:j + 1, :]
    tail = ext_ref[:, lt:lt + HIST, :]
    ext_ref[:, 0:HIST, :] = tail
    hout_ref[...] = tail

    xbc = _silu(conv).reshape(rows, SSD_CONV_CH)
    xs = xbc[:, 0:GROUP_W]
    bm = xbc[:, GROUP_W:GROUP_W + SSD_BC]
    cm = xbc[:, GROUP_W + SSD_BC:]
    z = p[:, :, 0:GROUP_W].reshape(rows, GROUP_W)
    dt = _softplus(p[:, :, GROUP_W + SSD_CONV_CH:].reshape(rows, GROUP_W) + dtb_ref[...])
    a = dt * (-jnp.exp(alog_ref[...]))
    unit = lt if nb > 1 else CHUNK
    units = rows // unit
    per_chunk = CHUNK // unit
    cum = _unit_cumsum(a, unit)
    last, last_b = _bcast_last(cum, units, unit)
    xdt = xs * dt
    xw = xdt * jnp.exp(last_b - cum)
    ecum = jnp.exp(cum)
    elast = jnp.exp(last)
    ones_row = jnp.full((CHUNK, hd), 1.0 / hd, F32)
    cmask, _ = _chunk_masks(CHUNK, unit)
    heads = range(SSD_HEADS)
    hs = lambda h: slice(h * hd, (h + 1) * hd)
    gs = lambda h: slice((h // 2) * SSD_STATE, (h // 2 + 1) * SSD_STATE)
    cs = lambda j: slice(j * CHUNK, (j + 1) * CHUNK)
    us = lambda u: slice(u * unit, (u + 1) * unit)

    yh = {}
    for j in range(rows // CHUNK):
        cb = [_dot(cm[cs(j), gs(2 * g)], bm[cs(j), gs(2 * g)], _NT) for g in range(SSD_NGROUPS)]
        for h in heads:
            ccol = cum[cs(j), hs(h)]
            crow = _dot_exact_lhs(ones_row, ccol, _NT)
            decay = jnp.exp(jnp.where(cmask, ccol - crow, -jnp.inf))
            yh[j, h] = _dot(cb[h // 2] * decay, xdt[cs(j), hs(h)])
    ds = {(u, h): _dot(xw[us(u), hs(h)], bm[us(u), gs(h)], _TN) for u in range(units) for h in heads}

    before = {}
    for h in heads:
        if nb > 1:
            for u in range(units):
                before[u, h] = sout_ref[u, h]
                sout_ref[u, h] = before[u, h] * elast[u][:, hs(h)] + ds[u, h]
        else:
            s_cur = sout_ref[0, h]
            for u in range(units):
                before[u, h] = s_cur
                s_cur = s_cur * elast[u][:, hs(h)] + ds[u, h]
            sout_ref[0, h] = s_cur

    for j in range(rows // CHUNK):
        for h in heads:
            inter = [_dot(cm[us(u), gs(h)], before[u, h], _NT)
                     for u in range(j * per_chunk, (j + 1) * per_chunk)]
            inter = inter[0] if per_chunk == 1 else jnp.concatenate(inter, axis=0)
            yscr_ref[cs(j), hs(h)] = yh[j, h] + inter * ecum[cs(j), hs(h)]

    y = yscr_ref[...] + dsk_ref[...] * xs
    y = y * _silu(z)
    y = y * lax.rsqrt(_segsum(y * y, hd) * (1.0 / hd) + EPS) * ng_ref[...]
    y_ref[...] = y.reshape(nb, lt, GROUP_W)


def _ssd(p, states, prm, l, prev, nb, lt):
    assert CHUNK == SSD_HEADDIM
    assert (nb > 1 and (nb * lt) % CHUNK == 0 and CHUNK % lt == 0) or (nb == 1 and lt % CHUNK == 0)
    kern = functools.partial(_ssd_kernel, nb=nb, lt=lt)
    return _mixer_call(kern, "ssd", l, nb, lt, p, states, prm, prev,
                       [pltpu.VMEM((nb, HIST + lt, SSD_CONV_CH), F32),
                        pltpu.VMEM((nb * lt, GROUP_W), F32)])


def _unit_lower_inverse(a2, n, levels, dot):
    ri = lax.broadcasted_iota(jnp.int32, (n, n), 0)
    ci = lax.broadcasted_iota(jnp.int32, (n, n), 1)
    x = [jnp.where(ri == ci, 1.0, 0.0) - jnp.where((ri % 2 == 1) & (ci == ri - 1), a, 0.0) for a in a2]
    s = 2
    while s < levels:
        off = ((ri // s) % 2 == 1) & ((ci // s) == (ri // s) - 1)
        t = [dot(xi, jnp.where(off, a, 0.0)) for xi, a in zip(x, a2)]
        x = [xi - dot(ti, xi) for xi, ti in zip(x, t)]
        s *= 2
    return x


def _rwkv_chunk_multi(ops, sout_ref, yscr_ref, nb, lt):
    hd = RWKV_HEADDIM
    per_chunk = CHUNK // lt
    idx = [(c, h) for c in range(nb // per_chunk) for h in range(RWKV_HEADS)]
    n = range(len(idx))
    incl, strict = _chunk_masks(CHUNK, lt)
    mask2 = jnp.concatenate([strict, incl], axis=0)
    top = lambda m: m[0:CHUNK]
    bot = lambda m: m[CHUNK:2 * CHUNK]
    bl = lambda name: [ops[name][c * CHUNK:(c + 1) * CHUNK, h * hd:(h + 1) * hd] for c, h in idx]
    rt, kt, k2t, bt, k2g, bg, v = (bl(name) for name in ('rt', 'kt', 'k2t', 'bt', 'k2g', 'bg', 'v'))
    kr = [jnp.concatenate([kt[j], rt[j]], axis=0) for j in n]
    m1 = [jnp.where(mask2, _dot_a(kr[j], k2t[j], _NT), 0.0) for j in n]
    m2 = [jnp.where(mask2, _dot_a(kr[j], bt[j], _NT), 0.0) for j in n]
    x = _unit_lower_inverse([top(m) for m in m2], CHUNK, lt, _dot_inv)
    m1v = [_dot_app(m1[j], v[j]) for j in n]
    seq = lambda i: slice(i * lt, (i + 1) * lt)
    seqs = range(per_chunk)
    s_old = [[sout_ref[c * per_chunk + i, h] for i in seqs] for c, h in idx]
    pq = [[_dot_st(jnp.concatenate([kt[j][seq(i)], rt[j][seq(i)]], axis=0), s_old[j][i], _NT)
           for i in seqs] for j in n]
    p0 = [jnp.concatenate([pq[j][i][0:lt] for i in seqs], axis=0) for j in n]
    q0 = [jnp.concatenate([pq[j][i][lt:2 * lt] for i in seqs], axis=0) for j in n]
    u = [_dot_app(x[j], p0[j] + top(m1v[j])) for j in n]
    l2u = [_dot_app(bot(m2[j]), u[j]) for j in n]
    for j, (c, h) in enumerate(idx):
        yscr_ref[c * CHUNK:(c + 1) * CHUNK, h * hd:(h + 1) * hd] = q0[j] + bot(m1v[j]) - l2u[j]
        for i in seqs:
            sq = c * per_chunk + i
            sout_ref[sq, h] = (s_old[j][i] * ops['elast'][sq][:, h * hd:(h + 1) * hd]
                               + _dot_st(v[j][seq(i)], k2g[j][seq(i)], _TN)
                               - _dot_st(u[j][seq(i)], bg[j][seq(i)], _TN))


def _rwkv_chunks_seq(ops, state, yscr_ref, nc):
    hd = RWKV_HEADDIM
    pw = 2 * hd
    pairs = RWKV_HEADS // 2
    idx = [(c, q) for c in range(nc) for q in range(pairs)]
    n = range(len(idx))
    top = lambda m: m[0:CHUNK]
    bot = lambda m: m[CHUNK:2 * CHUNK]
    ri = lax.broadcasted_iota(jnp.int32, (CHUNK, pw), 0)
    cl = lax.broadcasted_iota(jnp.int32, (CHUNK, pw), 1) % hd
    left = lax.broadcasted_iota(jnp.int32, (CHUNK, pw), 1) < hd
    strict, incl = cl < ri, cl <= ri
    mask2 = jnp.concatenate([strict, incl], axis=0)
    eye = jnp.where(cl == ri, 1.0, 0.0)
    bd = lambda p: jnp.concatenate([jnp.where(left, p, 0.0), jnp.where(left, 0.0, p)], axis=0)
    on_diag = jnp.concatenate([left, ~left], axis=0)
    r2 = lax.broadcasted_iota(jnp.int32, (pw, pw), 0)
    c2 = lax.broadcasted_iota(jnp.int32, (pw, pw), 1)
    eye2 = jnp.where(r2 == c2, 1.0, 0.0)
    diag_pack = lambda m: jnp.where(left, top(m), bot(m))

    bl = lambda name: [ops[name][c * CHUNK:(c + 1) * CHUNK, q * pw:(q + 1) * pw] for c, q in idx]
    rt, kt, k2t, bt, k2g, bg, v = (bl(name) for name in ('rt', 'kt', 'k2t', 'bt', 'k2g', 'bg', 'v'))
    kr = [jnp.concatenate([kt[j], rt[j]], axis=0) for j in n]
    m1 = [jnp.where(mask2, _dot_a(kr[j], bd(k2t[j]), _NT), 0.0) for j in n]
    m2 = [jnp.where(mask2, _dot_a(kr[j], bd(bt[j]), _NT), 0.0) for j in n]
    x = [eye - jnp.where((ri % 2 == 1) & (cl == ri - 1), top(m), 0.0) for m in m2]
    s = 2
    while s < CHUNK:
        off = ((ri // s) % 2 == 1) & ((cl // s) == (ri // s) - 1)
        t = [_dot_inv(x[j], bd(jnp.where(off, top(m2[j]), 0.0))) for j in n]
        x = [x[j] - _dot_inv(t[j], bd(x[j])) for j in n]
        s *= 2
    m1v = [_dot_app(m1[j], bd(v[j])) for j in n]
    kp = [_dot_app(x[j], bd(kt[j])) for j in n]
    ul = [_dot_app(x[j], bd(top(m1v[j]))) for j in n]
    rp = [rt[j] - _dot_app(bot(m2[j]), bd(kp[j])) for j in n]
    yl = [bot(m1v[j]) - _dot_app(bot(m2[j]), bd(ul[j])) for j in n]
    mc = [eye2 * ops['elast'][c][:, q * pw:(q + 1) * pw]
          - jnp.where(on_diag, _dot_st(kp[j], bg[j], _TN), 0.0) for j, (c, q) in enumerate(idx)]
    dc = [diag_pack(_dot_st(v[j], k2g[j], _TN) - _dot_st(ul[j], bg[j], _TN)) for j in n]
    for c in range(nc):
        js = [c * pairs + q for q in range(pairs)]
        ys = [_dot_st(rp[j], bd(state[q]), _NT) + yl[j] for q, j in enumerate(js)]
        state = [_dot_st(state[q], mc[j]) + dc[j] for q, j in enumerate(js)]
        for q in range(pairs):
            yscr_ref[c * CHUNK:(c + 1) * CHUNK, q * pw:(q + 1) * pw] = ys[q]
    return state


def _rwkv_kernel(p_ref, sh0_ref, s0_ref, mu_ref, wl_ref, w0_ref, a0_ref, kk_ref, ka_ref, rk_ref,
                 lnw_ref, lnb_ref, y_ref, shout_ref, sout_ref, ext_ref, yscr_ref, *, nb, lt):
    rows = nb * lt
    hd = RWKV_HEADDIM
    gw = GROUP_W

    @pl.when(pl.program_id(1) == 0)
    def _():
        ext_ref[:, HIST - 1:HIST, :] = sh0_ref[...]
        sout_ref[...] = s0_ref[...]

    p = p_ref[...]
    ext_ref[:, HIST:HIST + lt, :] = p
    prev = ext_ref[:, HIST - 1:HIST - 1 + lt, :].reshape(rows, RWKV_PROJ)
    last_p = p[:, lt - 1:lt, :]
    ext_ref[:, HIST - 1:HIST, :] = last_p
    shout_ref[...] = last_p
    p = p.reshape(rows, RWKV_PROJ)

    pm = p + (prev - p) * mu_ref[...]
    r = pm[:, 0:gw]
    k = pm[:, gw:2 * gw]
    v = pm[:, 2 * gw:3 * gw]
    lo = pm[:, 3 * gw:]
    lane = lax.broadcasted_iota(jnp.int32, (rows, RWKV_LORA), 1)
    t = jnp.where(lane < 32, jnp.tanh(lo), jnp.where(lane < 64, lo, _sigmoid(lo)))
    lora = _dot(t, wl_ref[...])
    wlog = -_softplus(-(w0_ref[...] + lora[:, 0:gw])) - 0.5
    lw = -jnp.exp(wlog)
    a = _sigmoid(a0_ref[...] + lora[:, gw:2 * gw])
    gate = lora[:, 2 * gw:]
    kk = k * kk_ref[...]
    kk = kk * jnp.minimum(lax.rsqrt(_segsum(kk * kk, hd)), 1e12)
    k2 = k * (1.0 + (a - 1.0) * ka_ref[...])
    beta = kk * a

    unit = lt if nb > 1 else CHUNK
    gcum = _unit_cumsum(lw, unit)
    glast, glast_b = _bcast_last(gcum, rows // unit, unit)
    eg = jnp.exp(gcum)
    einv = jnp.exp(-gcum)
    etail = jnp.exp(glast_b - gcum)
    ops = dict(rt=r * eg, kt=kk * jnp.exp(gcum - lw), k2t=k2 * einv, bt=beta * einv,
               k2g=k2 * etail, bg=beta * etail, v=v, elast=jnp.exp(glast))
    if nb > 1:
        _rwkv_chunk_multi(ops, sout_ref, yscr_ref, nb, lt)
    else:
        pairs = range(RWKV_HEADS // 2)
        state = [jnp.concatenate([sout_ref[0, 2 * q], sout_ref[0, 2 * q + 1]], axis=1) for q in pairs]
        state = _rwkv_chunks_seq(ops, state, yscr_ref, rows // CHUNK)
        for q in pairs:
            sout_ref[0, 2 * q] = state[q][:, 0:hd]
            sout_ref[0, 2 * q + 1] = state[q][:, hd:2 * hd]

    y = yscr_ref[...]
    mean = _segsum(y, hd) * (1.0 / hd)
    dlt = y - mean
    var = _segsum(dlt * dlt, hd) * (1.0 / hd)
    yn = dlt * lax.rsqrt(var + RWKV_LN_EPS) * lnw_ref[...] + lnb_ref[...]
    bonus = _segsum(r * k2 * rk_ref[...], hd) * v
    y_ref[...] = ((yn + bonus) * gate).reshape(nb, lt, gw)


def _rwkv(p, states, prm, l, prev, nb, lt):
    assert (nb > 1 and (nb * lt) % CHUNK == 0 and CHUNK % lt == 0) or (nb == 1 and lt % CHUNK == 0)
    kern = functools.partial(_rwkv_kernel, nb=nb, lt=lt)
    return _mixer_call(kern, "rwkv", l, nb, lt, p, states, prm, prev,
                       [pltpu.VMEM((nb, HIST + lt, RWKV_PROJ), F32),
                        pltpu.VMEM((nb * lt, GROUP_W), F32)])


def _gla_kernel(p_ref, s0_ref, wg2_ref, bg_ref, ng_ref, y_ref, sout_ref, yscr_ref, *, nb, lt, cs):
    rows = nb * lt
    dk, dv = GLA_DK, GLA_DV
    qk = GLA_QK
    units = rows // cs
    per_group = GLA_CS // cs

    @pl.when(pl.program_id(1) == 0)
    def _():
        sout_ref[...] = s0_ref[...]

    p = p_ref[...].reshape(rows, GLA_PW)
    q = p[:, 0:qk] * (dk ** -0.5)
    k = p[:, qk:2 * qk]
    v = p[:, 2 * qk:2 * qk + GROUP_W]
    dg = p[:, 2 * qk + GROUP_W:2 * qk + GROUP_W + 128]
    og = p[:, 2 * qk + GROUP_W + 128:]
    lg = -_softplus(-(_dot(dg, wg2_ref[...]) + bg_ref[...])) * (1.0 / GLA_TAU)
    b = _unit_cumsum(lg, cs)
    bl, bl_b = _bcast_last(b, units, cs)
    qd = q * jnp.exp(b)
    kd = k * jnp.exp(-b)
    kw = k * jnp.exp(bl_b - b)
    ebl = jnp.exp(bl)
    gmask, _ = _chunk_masks(GLA_CS, cs)

    heads = range(GLA_HEADS)
    ks = lambda h: slice(h * dk, (h + 1) * dk)
    vs = lambda h: slice(h * dv, (h + 1) * dv)
    gs = lambda g: slice(g * GLA_CS, (g + 1) * GLA_CS)
    us = lambda u: slice(u * cs, (u + 1) * cs)

    yh = {}
    for g in range(rows // GLA_CS):
        for h in heads:
            att = jnp.where(gmask, _dot(qd[gs(g), ks(h)], kd[gs(g), ks(h)], _NT), 0.0)
            yh[g, h] = _dot(att, v[gs(g), vs(h)])
    ds = {(u, h): _dot(kw[us(u), ks(h)], v[us(u), vs(h)], _TN) for u in range(units) for h in heads}
    ri = lax.broadcasted_iota(jnp.int32, (qk, qk), 0)
    ci = lax.broadcasted_iota(jnp.int32, (qk, qk), 1)
    ones_kv = jnp.ones((qk, dv), F32)
    ecol_all = [_dot_exact_rhs(jnp.where(ri == ci, ebl[u], 0.0), ones_kv) for u in range(units)]
    ecol = {(u, h): ecol_all[u][ks(h)] for u in range(units) for h in heads}

    before = {}
    for h in heads:
        if nb > 1:
            for u in range(units):
                before[u, h] = sout_ref[u, h]
                sout_ref[u, h] = before[u, h] * ecol[u, h] + ds[u, h]
        else:
            s_cur = sout_ref[0, h]
            for u in range(units):
                before[u, h] = s_cur
                s_cur = s_cur * ecol[u, h] + ds[u, h]
            sout_ref[0, h] = s_cur

    for g in range(rows // GLA_CS):
        for h in heads:
            inter = [_dot(qd[us(u), ks(h)], before[u, h])
                     for u in range(g * per_group, (g + 1) * per_group)]
            inter = inter[0] if per_group == 1 else jnp.concatenate(inter, axis=0)
            yscr_ref[gs(g), vs(h)] = yh[g, h] + inter

    y = yscr_ref[...]
    y = y * lax.rsqrt(_segsum(y * y, dv) * (1.0 / dv) + EPS) * ng_ref[...]
    y_ref[...] = (y * _silu(og)).reshape(nb, lt, GROUP_W)


def _gla(p, states, prm, l, prev, nb, lt, cs):
    assert (nb == 1 and cs == GLA_CS and lt % cs == 0) or (
        lt == cs and (nb * cs) % GLA_CS == 0 and GLA_CS % cs == 0)
    kern = functools.partial(_gla_kernel, nb=nb, lt=lt, cs=cs)
    return _mixer_call(kern, "gla", l, nb, lt, p, states, prm, prev,
                       [pltpu.VMEM((nb * lt, GROUP_W), F32)])


def _s5_pitch(lt):
    return lt if (lt // 8) % 2 == 1 else lt + 8


def _s5_kernel(u_ref, hr0_ref, hi0_ref, wb_ref, wc_ref, ar_ref, ai_ref, d_ref, wglu_ref, bglu_ref,
               y_ref, hr_ref, hi_ref, io_ref, h_ref, *, nb, lt):
    n = S5_CH
    wt = GROUP_W // LANES

    @pl.when(pl.program_id(1) == 0)
    def _():
        hr_ref[...] = hr0_ref[...]
        hi_ref[...] = hi0_ref[...]

    lanes = lambda x, j: x[:, j * LANES:(j + 1) * LANES]
    pitch = _s5_pitch(lt)
    seq_rows = lambda i: slice(i * pitch, i * pitch + lt)
    at_t = lambda t: pl.ds(t, nb, stride=pitch)
    step = lambda t: slice(t * nb, (t + 1) * nb)
    for j in range(wt):
        for i in range(nb):
            io_ref[j, seq_rows(i), :] = u_ref[i, :, j * LANES:(j + 1) * LANES]
    u = jnp.concatenate([jnp.concatenate([io_ref[j, at_t(t), :] for j in range(wt)], axis=1)
                         for t in range(lt)], axis=0)
    bu = _dot(u, wb_ref[...])
    ar = jnp.broadcast_to(ar_ref[...], (nb, n))
    ai = jnp.broadcast_to(ai_ref[...], (nb, n))
    hr = hr_ref[...]
    hi = hi_ref[...]
    for t in range(lt):
        hr, hi = (ar * hr - ai * hi + bu[step(t), 0:n], ar * hi + ai * hr + bu[step(t), n:2 * n])
        h_ref[step(t), 0:n] = hr
        h_ref[step(t), n:2 * n] = hi
    hr_ref[...] = hr
    hi_ref[...] = hi
    y = _dot(h_ref[:, 0:n], wc_ref[0]) - _dot(h_ref[:, n:2 * n], wc_ref[1])
    y = y + d_ref[...] * u
    y = 0.5 * y * (1.0 + jnp.tanh(math.sqrt(2.0 / math.pi) * (y + 0.044715 * (y * y * y))))
    y = y * _sigmoid(_dot(y, wglu_ref[...]) + bglu_ref[...])
    for t in range(lt):
        for j in range(wt):
            io_ref[j, at_t(t), :] = lanes(y[step(t)], j)
    for j in range(wt):
        for i in range(nb):
            y_ref[i, :, j * LANES:(j + 1) * LANES] = io_ref[j, seq_rows(i), :]


def _s5(u, states, prm, l, prev, nb, lt):
    kern = functools.partial(_s5_kernel, nb=nb, lt=lt)
    return _mixer_call(kern, "s5", l, nb, lt, u, states, prm, prev,
                       [pltpu.VMEM((GROUP_W // LANES, nb * _s5_pitch(lt), LANES), F32),
                        pltpu.VMEM((nb * lt, 2 * S5_CH), F32)])


def _s5_zoh_kernel(lr_ref, li_ref, ls_ref, bre_ref, bim_ref, ar_ref, ai_ref, bbre_ref, bbim_ref):
    lr = lr_ref[...]
    li = li_ref[...]
    dts = jnp.exp(ls_ref[...])
    mag = jnp.exp(lr * dts)
    ar = mag * jnp.cos(li * dts)
    ai = mag * jnp.sin(li * dts)
    den = lr * lr + li * li
    fr = ((ar - 1.0) * lr + ai * li) / den
    fi = (ai * lr - (ar - 1.0) * li) / den
    ar_ref[...] = ar
    ai_ref[...] = ai
    r, n = lr.shape
    per_input = lambda f: jnp.broadcast_to(f[:, None, :], (r, S5_GROUP, n)).reshape(r * S5_GROUP, n)
    fre, fie = per_input(fr), per_input(fi)
    bre = bre_ref[...]
    bim = bim_ref[...]
    bbre_ref[...] = fre * bre - fie * bim
    bbim_ref[...] = fre * bim + fie * bre


def _s5_zoh(lam_re, lam_im, log_step, b_re, b_im):
    dep, g, n = lam_re.shape
    rows = dep * g
    flat = lambda a: a.astype(F32).reshape(rows, n)
    flat_b = lambda b: jnp.swapaxes(b.astype(F32), -1, -2).reshape(rows * S5_GROUP, n)
    vm = pl.BlockSpec(memory_space=pltpu.VMEM)
    ar, ai, bbr, bbi = pl.pallas_call(
        _s5_zoh_kernel,
        out_shape=(jax.ShapeDtypeStruct((rows, n), F32),) * 2
        + (jax.ShapeDtypeStruct((rows * S5_GROUP, n), F32),) * 2,
        in_specs=[vm] * 5,
        out_specs=(vm,) * 4,
        name="s5_zoh",
    )(flat(lam_re), flat(lam_im), log_step.astype(F32).reshape(rows, 1), flat_b(b_re), flat_b(b_im))
    return (ar.reshape(dep, g, n), ai.reshape(dep, g, n),
            bbr.reshape(dep, g, S5_GROUP, n), bbi.reshape(dep, g, S5_GROUP, n))


def _block_diag_in(b):
    dep, g, i, n = b.shape
    eye = jnp.eye(g, dtype=b.dtype)
    return jnp.einsum('lgin,gh->lgihn', b, eye).reshape(dep, g * i, g * n)


def _block_diag_out(c):
    dep, g, o, n = c.shape
    eye = jnp.eye(g, dtype=c.dtype)
    return jnp.einsum('lgon,gh->lgnho', c, eye).reshape(dep, g * n, g * o)


def _prep_params(W):
    dep = W['w_in'].shape[0]
    row = lambda name: W[name].reshape(dep, 1, -1).astype(F32)
    rep = lambda name, n: jnp.repeat(W[name].reshape(dep, -1), n, axis=1).reshape(dep, 1, -1).astype(F32)
    w_in = W['w_in'].astype(BF16)
    o_r = SSD_PROJ
    o_g = o_r + RWKV_PROJ
    o_s = o_g + GLA_PROJ
    w_ssd = jnp.concatenate([w_in[..., :GROUP_W + SSD_CONV_CH],
                             jnp.repeat(w_in[..., GROUP_W + SSD_CONV_CH:o_r], SSD_HEADDIM, axis=2)], axis=2)
    w_rwkv = w_in[..., o_r:o_g]
    gq = o_g + 2 * GLA_QK + GROUP_W
    w_gla = jnp.concatenate([w_in[..., o_g:gq], w_in[..., gq:gq + GLA_GATE_LORA],
                             jnp.zeros((dep, D_MODEL, 128 - GLA_GATE_LORA), BF16),
                             w_in[..., gq + GLA_GATE_LORA:o_s]], axis=2)
    w_s5 = w_in[..., o_s:]
    in_w = (w_ssd, w_rwkv, w_gla, w_s5)

    ssd = (W['ssd_conv_w'].astype(F32), row('ssd_conv_b'), rep('ssd_a_log', SSD_HEADDIM),
           rep('ssd_dt_bias', SSD_HEADDIM), rep('ssd_d', SSD_HEADDIM), row('ssd_norm_g'))

    z = lambda r, c: jnp.zeros((dep, r, c), F32)
    w_lora = jnp.concatenate([
        jnp.concatenate([W['rwkv_w2'], z(32, 2 * GROUP_W)], axis=2),
        jnp.concatenate([z(32, GROUP_W), W['rwkv_a2'], z(32, GROUP_W)], axis=2),
        jnp.concatenate([z(64, 2 * GROUP_W), W['rwkv_g2']], axis=2)], axis=1).astype(BF16)
    rwkv = (row('rwkv_mu'), w_lora, row('rwkv_w0'), row('rwkv_a0'), row('rwkv_k_k'),
            row('rwkv_k_a'), row('rwkv_r_k'), row('rwkv_lnx_w'), row('rwkv_lnx_b'))

    wg2 = jnp.concatenate([W['gla_wg2'], z(128 - GLA_GATE_LORA, GLA_QK)], axis=1).astype(BF16)
    gla = (wg2, row('gla_bg'), row('gla_norm_g'))

    ar, ai, bb_re, bb_im = _s5_zoh(W['s5_lam_re'], W['s5_lam_im'], W['s5_log_step'],
                                   W['s5_b_re'], W['s5_b_im'])
    wb = jnp.concatenate([_block_diag_in(bb_re), _block_diag_in(bb_im)], axis=2).astype(BF16)
    wc = jnp.stack([_block_diag_out(W['s5_c_re']), _block_diag_out(W['s5_c_im'])], axis=1).astype(BF16)
    s5 = (wb, wc, ar.reshape(dep, 1, -1), ai.reshape(dep, 1, -1), row('s5_d'),
          W['s5_w_glu'].astype(BF16), row('s5_b_glu'))

    g4 = lambda name: W[name].reshape(dep, 1, 1, D_MODEL).astype(F32)
    return dict(
        in_w=in_w, ssd=ssd, rwkv=rwkv, gla=gla, s5=s5,
        g_mix_pre=g4('g_mix_pre'), g_mix_post=g4('g_mix_post'),
        g_ffn_pre=g4('g_ffn_pre'), g_ffn_post=g4('g_ffn_post'),
        w_out=W['w_out'].reshape(dep, 4, GROUP_W, D_MODEL).astype(BF16),
        w1=W['mlp_w1'].astype(BF16), w2=W['mlp_w2'].astype(BF16))


def _run_group(x, mod, first, st, P, cfg):
    dep, bsz = st['ssd'].shape[:2]
    hist = jnp.pad(st['ssd_conv'], ((0, 0), (0, 0), (HIST - (SSD_CONV - 1), 0), (0, 0)))
    ssd_st = (hist, st['ssd'])
    rwkv_st = (st['rwkv_shift'].reshape(dep, bsz, 1, RWKV_PROJ), st['rwkv'])
    gla_st = (st['gla'],)
    s5_st = (st['s5_re'].reshape(dep, bsz, S5_CH), st['s5_im'].reshape(dep, bsz, S5_CH))
    new = dict(ssd=None, rwkv=None, gla=None, s5=None)
    for l in range(DEPTH):
        p_ssd, p_rwkv, p_gla, p_s5 = _in_proj(x, mod, first, P['g_mix_pre'], P['in_w'], l, *cfg['tok'])
        y_ssd, *new['ssd'] = _ssd(p_ssd, ssd_st, P['ssd'], l, new['ssd'], *cfg['ssd'])
        y_rwkv, *new['rwkv'] = _rwkv(p_rwkv, rwkv_st, P['rwkv'], l, new['rwkv'], *cfg['rwkv'])
        y_gla, *new['gla'] = _gla(p_gla, gla_st, P['gla'], l, new['gla'], *cfg['gla'])
        y_s5, *new['s5'] = _s5(p_s5, s5_st, P['s5'], l, new['s5'], *cfg['s5'])
        x = _out_mlp(x, (y_ssd, y_rwkv, y_gla, y_s5), mod, first, P['g_mix_post'], P['g_ffn_pre'],
                     P['g_ffn_post'], P['w_out'], P['w1'], P['w2'], l, *cfg['tok'])
    hist_new, ssd_new = new['ssd']
    shift_new, rwkv_new = new['rwkv']
    s5r_new, s5i_new = new['s5']
    return x, dict(
        ssd=ssd_new, ssd_conv=hist_new[:, :, HIST - (SSD_CONV - 1):, :],
        rwkv=rwkv_new, rwkv_shift=shift_new.reshape(dep, bsz, RWKV_PROJ), gla=new['gla'][0],
        s5_re=s5r_new.reshape(dep, bsz, S5_GROUPS, S5_STATE),
        s5_im=s5i_new.reshape(dep, bsz, S5_GROUPS, S5_STATE))


_PROMPT_CFG = dict(tok=(1, 512), ssd=(1, 512), rwkv=(1, 512), gla=(1, 512, GLA_CS), s5=(8, 128))
_SAMPLE_CFG = dict(tok=(64, 8), ssd=(32, 8), rwkv=(32, 8), gla=(16, 8, 8), s5=(64, 8))


def kernel(x_prompt, x_sample, c_prompt, c_sample, state_ssd, state_ssd_conv, state_rwkv, state_rwkv_shift, state_gla, state_s5_re, state_s5_im, w_ada, b_ada, g_mix_pre, g_mix_post, g_ffn_pre, g_ffn_post, w_in, w_out, ssd_conv_w, ssd_conv_b, ssd_a_log, ssd_dt_bias, ssd_d, ssd_norm_g, rwkv_mu, rwkv_w0, rwkv_w2, rwkv_a0, rwkv_a2, rwkv_g2, rwkv_k_k, rwkv_k_a, rwkv_r_k, rwkv_lnx_w, rwkv_lnx_b, gla_wg2, gla_bg, gla_norm_g, s5_lam_re, s5_lam_im, s5_log_step, s5_b_re, s5_b_im, s5_c_re, s5_c_im, s5_d, s5_w_glu, s5_b_glu, mlp_w1, mlp_w2):
    W = dict(w_in=w_in, w_out=w_out, g_mix_pre=g_mix_pre, g_mix_post=g_mix_post,
             g_ffn_pre=g_ffn_pre, g_ffn_post=g_ffn_post,
             ssd_conv_w=ssd_conv_w, ssd_conv_b=ssd_conv_b, ssd_a_log=ssd_a_log,
             ssd_dt_bias=ssd_dt_bias, ssd_d=ssd_d, ssd_norm_g=ssd_norm_g,
             rwkv_mu=rwkv_mu, rwkv_w0=rwkv_w0, rwkv_w2=rwkv_w2, rwkv_a0=rwkv_a0, rwkv_a2=rwkv_a2,
             rwkv_g2=rwkv_g2, rwkv_k_k=rwkv_k_k, rwkv_k_a=rwkv_k_a, rwkv_r_k=rwkv_r_k,
             rwkv_lnx_w=rwkv_lnx_w, rwkv_lnx_b=rwkv_lnx_b,
             gla_wg2=gla_wg2, gla_bg=gla_bg, gla_norm_g=gla_norm_g,
             s5_lam_re=s5_lam_re, s5_lam_im=s5_lam_im, s5_log_step=s5_log_step,
             s5_b_re=s5_b_re, s5_b_im=s5_b_im, s5_c_re=s5_c_re, s5_c_im=s5_c_im,
             s5_d=s5_d, s5_w_glu=s5_w_glu, s5_b_glu=s5_b_glu, mlp_w1=mlp_w1, mlp_w2=mlp_w2)
    P = _prep_params(W)

    nbp, nbs = x_prompt.shape[0], x_sample.shape[0]
    mod = _ada(jnp.concatenate([c_sample, c_prompt], axis=0), w_ada, b_ada)
    mod = mod.reshape(DEPTH, nbs + nbp, N_MOD, D_MODEL)

    st_sample = dict(ssd=state_ssd, ssd_conv=state_ssd_conv, rwkv=state_rwkv,
                     rwkv_shift=state_rwkv_shift, gla=state_gla, s5_re=state_s5_re, s5_im=state_s5_im)
    st_prompt = {n: jnp.zeros((DEPTH, nbp) + v.shape[2:], v.dtype) for n, v in st_sample.items()}

    y_prompt, sp = _run_group(x_prompt, mod, nbs, st_prompt, P, _PROMPT_CFG)
    y_sample, ss = _run_group(x_sample, mod, 0, st_sample, P, _SAMPLE_CFG)
    return (y_prompt, y_sample,
            sp['ssd'], ss['ssd'], sp['ssd_conv'], ss['ssd_conv'],
            sp['rwkv'], ss['rwkv'], sp['rwkv_shift'], ss['rwkv_shift'],
            sp['gla'], ss['gla'], sp['s5_re'], ss['s5_re'], sp['s5_im'], ss['s5_im'])
```

```python
import functools
import math

import jax
import jax.numpy as jnp
from jax import lax
from jax.experimental import pallas as pl
from jax.experimental.pallas import tpu as pltpu

F32 = jnp.float32
BF16 = jnp.bfloat16

D_MODEL = 1024
DEPTH = 2
GROUP_W = 256
D_FF = 4096
N_MOD = 6
EPS = 1e-6

SSD_HEADS = 4
SSD_HEADDIM = 64
SSD_STATE = 64
SSD_NGROUPS = 2
SSD_BC = SSD_NGROUPS * SSD_STATE
SSD_CONV = 4
SSD_CONV_CH = GROUP_W + 2 * SSD_BC
SSD_PROJ = GROUP_W + SSD_CONV_CH + SSD_HEADS
SSD_PW = GROUP_W + SSD_CONV_CH + GROUP_W

RWKV_HEADS = 4
RWKV_HEADDIM = 64
RWKV_LORA = 128
RWKV_PROJ = 3 * GROUP_W + RWKV_LORA
RWKV_LN_EPS = 64e-5

GLA_HEADS = 4
GLA_DK = 32
GLA_DV = 64
GLA_GATE_LORA = 16
GLA_TAU = 16.0
GLA_QK = GLA_HEADS * GLA_DK
GLA_PROJ = 2 * GLA_QK + GROUP_W + GLA_GATE_LORA + GROUP_W
GLA_PW = 2 * GLA_QK + GROUP_W + 128 + GROUP_W

S5_GROUP = 16
S5_GROUPS = 16
S5_STATE = 64
S5_CH = S5_GROUPS * S5_STATE
S5_PROJ = GROUP_W

LANES = 128
HIST = 8
CHUNK = 64
GLA_CS = 32
VMEM_LIMIT = 58 * 1024 * 1024


def _dot(a, b, dims=(((1,), (0,)), ((), ()))):
    return lax.dot_general(a.astype(BF16), b.astype(BF16), dims, preferred_element_type=F32)


_NT = (((1,), (1,)), ((), ()))
_TN = (((0,), (0,)), ((), ()))
_NN = (((1,), (0,)), ((), ()))


def _split2(a):
    hi = a.astype(BF16)
    lo = (a - hi.astype(F32)).astype(BF16)
    return hi, lo


def _dot3(a, b, dims=_NN):
    ah, al = _split2(a)
    bh, bl = _split2(b)
    d = lambda x, y: lax.dot_general(x, y, dims, preferred_element_type=F32)
    return d(ah, bh) + (d(ah, bl) + d(al, bh))


_dot_a = _dot
_dot_inv = _dot
_dot_app = _dot
_dot_st = _dot


def _dot_exact_lhs(m, x, dims=_NN):
    mb = m.astype(BF16)
    x1 = x.astype(BF16)
    r1 = x - x1.astype(F32)
    x2 = r1.astype(BF16)
    x3 = (r1 - x2.astype(F32)).astype(BF16)
    d = lambda y: lax.dot_general(mb, y, dims, preferred_element_type=F32)
    return d(x1) + (d(x2) + d(x3))


def _dot_exact_rhs(x, m):
    mb = m.astype(BF16)
    x1 = x.astype(BF16)
    r1 = x - x1.astype(F32)
    x2 = r1.astype(BF16)
    x3 = (r1 - x2.astype(F32)).astype(BF16)
    d = lambda y: lax.dot_general(y, mb, _NN, preferred_element_type=F32)
    return d(x1) + (d(x2) + d(x3))


def _segsum(x, width):
    n = x.shape[-1]
    r = lax.broadcasted_iota(jnp.int32, (n, n), 0) // width
    c = lax.broadcasted_iota(jnp.int32, (n, n), 1) // width
    j = jnp.where(r == c, 1.0, 0.0).astype(BF16)
    hi, lo = _split2(x)
    d = lambda y: lax.dot_general(y, j, _NN, preferred_element_type=F32)
    return d(hi) + d(lo)


def _sigmoid(x):
    return 1.0 / (1.0 + jnp.exp(-x))


def _silu(x):
    return x * _sigmoid(x)


def _softplus(x):
    return jnp.maximum(x, 0.0) + jnp.log1p(jnp.exp(-jnp.abs(x)))


def _rms(x, g):
    return x * lax.rsqrt(jnp.mean(x * x, axis=-1, keepdims=True) + EPS) * g


def _chunk_masks(rows, lt):
    r = lax.broadcasted_iota(jnp.int32, (rows, rows), 0)
    c = lax.broadcasted_iota(jnp.int32, (rows, rows), 1)
    same = (r // lt) == (c // lt)
    return same & (c <= r), same & (c < r)


def _unit_cumsum(x, unit):
    incl, _ = _chunk_masks(CHUNK, unit)
    tri = jnp.where(incl, 1.0, 0.0)
    parts = [_dot_exact_lhs(tri, x[j * CHUNK:(j + 1) * CHUNK]) for j in range(x.shape[0] // CHUNK)]
    return parts[0] if len(parts) == 1 else jnp.concatenate(parts, axis=0)


def _lsel(a, l, **kw):
    shape = a.shape[1:]
    return pl.BlockSpec((None,) + shape, lambda i, c: (l,) + (0,) * len(shape), **kw)


def _lseq(a, l, nb):
    shape = a.shape[2:]
    return pl.BlockSpec((None, nb) + shape, lambda i, c: (l, i) + (0,) * len(shape))


def _mixer_call(kern, name, l, nb, lt, p, states, params, prev, scratch):
    bsz, seq, width = p.shape
    n_in = 1 + len(states) + len(params)
    n_st = len(states)
    alias = tuple(prev) if prev is not None else ()
    first = prev is None

    def body(*refs):
        ins, outs = refs[:n_in], refs[n_in + len(alias):]
        y_ref, st_refs, scratch_refs = outs[0], outs[1:1 + n_st], outs[1 + n_st:]
        if first:
            @pl.when(pl.program_id(1) == 0)
            def _():
                for r in st_refs:
                    r[1:] = jnp.zeros((r.shape[0] - 1,) + r.shape[1:], F32)
            st_refs = [r.at[0] for r in st_refs]
        return kern(*ins, y_ref, *st_refs, *scratch_refs)

    def st_out_spec(s):
        if not first:
            return _lseq(s, l, nb)
        shape = s.shape[2:]
        return pl.BlockSpec((s.shape[0], nb) + shape, lambda i, c: (0, i) + (0,) * len(shape))

    tok = lambda w: pl.BlockSpec((nb, lt, w), lambda i, c: (i, c, 0))
    return pl.pallas_call(
        body,
        out_shape=(jax.ShapeDtypeStruct((bsz, seq, GROUP_W), F32),)
        + tuple(jax.ShapeDtypeStruct(s.shape, F32) for s in states),
        grid=(bsz // nb, seq // lt),
        in_specs=[tok(width)] + [_lseq(s, l, nb) for s in states] + [_lsel(a, l) for a in params]
        + [pl.BlockSpec(memory_space=pl.ANY)] * len(alias),
        out_specs=(tok(GROUP_W),) + tuple(st_out_spec(s) for s in states),
        scratch_shapes=scratch,
        input_output_aliases={n_in + k: 1 + k for k in range(len(alias))},
        compiler_params=pltpu.CompilerParams(
            dimension_semantics=("parallel", "arbitrary"), vmem_limit_bytes=VMEM_LIMIT),
        name=name,
    )(p, *states, *params, *alias)


def _bcast_last(x, nb, lt):
    c = x.shape[-1]
    last = x.reshape(nb, lt, c)[:, lt - 1:lt, :]
    return last, jnp.broadcast_to(last, (nb, lt, c)).reshape(nb * lt, c)


def _ada_kernel(c_ref, w_ref, b_ref, o_ref):
    c = c_ref[...]
    o_ref[0] = _dot(_silu(c), w_ref[0]) + b_ref[0]


def _ada(c_all, w_ada, b_ada):
    rows = c_all.shape[0]
    n = N_MOD * D_MODEL
    tn = 1536
    return pl.pallas_call(
        _ada_kernel,
        out_shape=jax.ShapeDtypeStruct((DEPTH, rows, n), F32),
        grid=(DEPTH, n // tn),
        in_specs=[
            pl.BlockSpec((rows, D_MODEL), lambda l, j: (0, 0)),
            pl.BlockSpec((1, D_MODEL, tn), lambda l, j: (l, 0, j)),
            pl.BlockSpec((1, 1, tn), lambda l, j: (l, 0, j)),
        ],
        out_specs=pl.BlockSpec((1, rows, tn), lambda l, j: (l, 0, j)),
        compiler_params=pltpu.CompilerParams(
            dimension_semantics=("parallel", "parallel"), vmem_limit_bytes=VMEM_LIMIT),
        name="ada",
    )(c_all, w_ada, b_ada.reshape(DEPTH, 1, n))


def _in_kernel(x_ref, mod_ref, g_ref, wssd_ref, wrwkv_ref, wgla_ref, ws5_ref,
               ossd_ref, orwkv_ref, ogla_ref, os5_ref):
    nb, lt, d = x_ref.shape
    for s in range(2):
        if nb == 1:
            sel = (slice(None), slice(s * (lt // 2), (s + 1) * (lt // 2)))
            m = mod_ref[...]
        else:
            sel = (slice(s * (nb // 2), (s + 1) * (nb // 2)),)
            m = mod_ref[sel]
        x = x_ref[sel]
        h = _rms(x, g_ref[...]) * (1.0 + m[:, 1:2, :]) + m[:, 0:1, :]
        hb = h.reshape(x.shape[0] * x.shape[1], d).astype(BF16)
        for w_ref, o_ref in ((wssd_ref, ossd_ref), (wrwkv_ref, orwkv_ref),
                             (wgla_ref, ogla_ref), (ws5_ref, os5_ref)):
            o_ref[sel] = jnp.dot(hb, w_ref[...], preferred_element_type=F32).reshape(
                x.shape[:2] + (o_ref.shape[-1],))


def _mod_spec(mod, l, first, nb):
    return pl.BlockSpec((None, nb, N_MOD, mod.shape[-1]), lambda i, j: (l, first // nb + i, 0, 0))


def _in_proj(x, mod, first, g, wts, l, nb, lt):
    bsz, seq, d = x.shape
    widths = (SSD_PW, RWKV_PROJ, GLA_PW, S5_PROJ)
    return pl.pallas_call(
        _in_kernel,
        out_shape=tuple(jax.ShapeDtypeStruct((bsz, seq, w), F32) for w in widths),
        grid=(bsz // nb, seq // lt),
        in_specs=[pl.BlockSpec((nb, lt, d), lambda i, j: (i, j, 0)), _mod_spec(mod, l, first, nb),
                  _lsel(g, l)] + [_lsel(w, l) for w in wts],
        out_specs=tuple(pl.BlockSpec((nb, lt, w), lambda i, j: (i, j, 0)) for w in widths),
        compiler_params=pltpu.CompilerParams(
            dimension_semantics=("parallel", "parallel"), vmem_limit_bytes=VMEM_LIMIT),
        name="in_proj",
    )(x, mod, g, *wts)


def _out_mlp_kernel(x_ref, y0_ref, y1_ref, y2_ref, y3_ref, mod_ref, gpost_ref, gpre_ref, gfpost_ref,
                    wout_ref, w1_ref, w2_ref, o_ref):
    nb, lt, d = x_ref.shape
    if nb == 1:
        hshape = (1, lt // 2, d)
        half = lambda ref, s: ref[:, s * (lt // 2):(s + 1) * (lt // 2), :]
        mods = [mod_ref[...]] * 2
    else:
        hshape = (nb // 2, lt, d)
        half = lambda ref, s: ref[s * (nb // 2):(s + 1) * (nb // 2)]
        mods = [half(mod_ref, s) for s in range(2)]
    hrows = hshape[0] * hshape[1]
    x1s, hbs = [], []
    for s in range(2):
        m = mods[s]
        mix = None
        for i, y_ref in enumerate((y0_ref, y1_ref, y2_ref, y3_ref)):
            t = jnp.dot(half(y_ref, s).reshape(hrows, GROUP_W).astype(BF16), wout_ref[i],
                        preferred_element_type=F32)
            mix = t if mix is None else mix + t
        x1 = half(x_ref, s) + m[:, 2:3, :] * _rms(mix, gpost_ref[0]).reshape(hshape)
        h = _rms(x1, gpre_ref[...]) * (1.0 + m[:, 4:5, :]) + m[:, 3:4, :]
        x1s.append(x1)
        hbs.append(h.reshape(hrows, d).astype(BF16))
    accs = [None, None]
    step = 1024
    for j in range(D_FF // step):
        for s in range(2):
            a = jnp.dot(hbs[s], w1_ref[:, j * step:(j + 1) * step], preferred_element_type=F32)
            a = jnp.square(jnp.maximum(a, 0.0)).astype(BF16)
            t = jnp.dot(a, w2_ref[j * step:(j + 1) * step, :], preferred_element_type=F32)
            accs[s] = t if accs[s] is None else accs[s] + t
    for s in range(2):
        out = x1s[s] + mods[s][:, 5:6, :] * _rms(accs[s], gfpost_ref[0]).reshape(hshape)
        if nb == 1:
            o_ref[:, s * (lt // 2):(s + 1) * (lt // 2), :] = out
        else:
            o_ref[s * (nb // 2):(s + 1) * (nb // 2)] = out


def _out_mlp(x, ys, mod, first, gpost, gpre, gfpost, wout, w1, w2, l, nb, lt):
    bsz, seq, d = x.shape
    tok = lambda i, j: (i, j, 0)
    one = pl.Buffered(1)
    return pl.pallas_call(
        _out_mlp_kernel,
        out_shape=jax.ShapeDtypeStruct((bsz, seq, d), F32),
        grid=(bsz // nb, seq // lt),
        in_specs=[pl.BlockSpec((nb, lt, d), tok)]
        + [pl.BlockSpec((nb, lt, GROUP_W), tok)] * 4
        + [_mod_spec(mod, l, first, nb)]
        + [_lsel(g, l) for g in (gpost, gpre, gfpost)]
        + [_lsel(w, l, pipeline_mode=one) for w in (wout, w1, w2)],
        out_specs=pl.BlockSpec((nb, lt, d), tok),
        compiler_params=pltpu.CompilerParams(
            dimension_semantics=("parallel", "parallel"), vmem_limit_bytes=VMEM_LIMIT),
        name="out_mlp",
    )(x, *ys, mod, gpost, gpre, gfpost, wout, w1, w2)


def _ssd_kernel(p_ref, hist_ref, s0_ref, cw_ref, cb_ref, alog_ref, dtb_ref, dsk_ref, ng_ref,
                y_ref, hout_ref, sout_ref, ext_ref, yscr_ref, *, nb, lt):
    rows = nb * lt
    hd = SSD_HEADDIM

    @pl.when(pl.program_id(1) == 0)
    def _():
        ext_ref[:, 0:HIST, :] = hist_ref[...]
        sout_ref[...] = s0_ref[...]

    p = p_ref[...]
    ext_ref[:, HIST:HIST + lt, :] = p[:, :, GROUP_W:GROUP_W + SSD_CONV_CH]
    conv = cb_ref[...]
    for j in range(SSD_CONV):
        o = HIST - (SSD_CONV - 1) + j
        conv = conv + ext_ref[:, o:o + lt, :] * cw_ref[j:j + 1, :]
    tail = ext_ref[:, lt:lt + HIST, :]
    ext_ref[:, 0:HIST, :] = tail
    hout_ref[...] = tail

    xbc = _silu(conv).reshape(rows, SSD_CONV_CH)
    xs = xbc[:, 0:GROUP_W]
    bm = xbc[:, GROUP_W:GROUP_W + SSD_BC]
    cm = xbc[:, GROUP_W + SSD_BC:]
    z = p[:, :, 0:GROUP_W].reshape(rows, GROUP_W)
    dt = _softplus(p[:, :, GROUP_W + SSD_CONV_CH:].reshape(rows, GROUP_W) + dtb_ref[...])
    a = dt * (-jnp.exp(alog_ref[...]))
    unit = lt if nb > 1 else CHUNK
    units = rows // unit
    per_chunk = CHUNK // unit
    cum = _unit_cumsum(a, unit)
    last, last_b = _bcast_last(cum, units, unit)
    xdt = xs * dt
    xw = xdt * jnp.exp(last_b - cum)
    ecum = jnp.exp(cum)
    elast = jnp.exp(last)
    ones_row = jnp.full((CHUNK, hd), 1.0 / hd, F32)
    cmask, _ = _chunk_masks(CHUNK, unit)
    heads = range(SSD_HEADS)
    hs = lambda h: slice(h * hd, (h + 1) * hd)
    gs = lambda h: slice((h // 2) * SSD_STATE, (h // 2 + 1) * SSD_STATE)
    cs = lambda j: slice(j * CHUNK, (j + 1) * CHUNK)
    us = lambda u: slice(u * unit, (u + 1) * unit)

    yh = {}
    for j in range(rows // CHUNK):
        cb = [_dot(cm[cs(j), gs(2 * g)], bm[cs(j), gs(2 * g)], _NT) for g in range(SSD_NGROUPS)]
        for h in heads:
            ccol = cum[cs(j), hs(h)]
            crow = _dot_exact_lhs(ones_row, ccol, _NT)
            decay = jnp.exp(jnp.where(cmask, ccol - crow, -jnp.inf))
            yh[j, h] = _dot(cb[h // 2] * decay, xdt[cs(j), hs(h)])
    ds = {(u, h): _dot(xw[us(u), hs(h)], bm[us(u), gs(h)], _TN) for u in range(units) for h in heads}

    before = {}
    for h in heads:
        if nb > 1:
            for u in range(units):
                before[u, h] = sout_ref[u, h]
                sout_ref[u, h] = before[u, h] * elast[u][:, hs(h)] + ds[u, h]
        else:
            s_cur = sout_ref[0, h]
            for u in range(units):
                before[u, h] = s_cur
                s_cur = s_cur * elast[u][:, hs(h)] + ds[u, h]
            sout_ref[0, h] = s_cur

    for j in range(rows // CHUNK):
        for h in heads:
            inter = [_dot(cm[us(u), gs(h)], before[u, h], _NT)
                     for u in range(j * per_chunk, (j + 1) * per_chunk)]
            inter = inter[0] if per_chunk == 1 else jnp.concatenate(inter, axis=0)
            yscr_ref[cs(j), hs(h)] = yh[j, h] + inter * ecum[cs(j), hs(h)]

    y = yscr_ref[...] + dsk_ref[...] * xs
    y = y * _silu(z)
    y = y * lax.rsqrt(_segsum(y * y, hd) * (1.0 / hd) + EPS) * ng_ref[...]
    y_ref[...] = y.reshape(nb, lt, GROUP_W)


def _ssd(p, states, prm, l, prev, nb, lt):
    assert CHUNK == SSD_HEADDIM
    assert (nb > 1 and (nb * lt) % CHUNK == 0 and CHUNK % lt == 0) or (nb == 1 and lt % CHUNK == 0)
    kern = functools.partial(_ssd_kernel, nb=nb, lt=lt)
    return _mixer_call(kern, "ssd", l, nb, lt, p, states, prm, prev,
                       [pltpu.VMEM((nb, HIST + lt, SSD_CONV_CH), F32),
                        pltpu.VMEM((nb * lt, GROUP_W), F32)])


def _unit_lower_inverse(a2, n, levels, dot):
    ri = lax.broadcasted_iota(jnp.int32, (n, n), 0)
    ci = lax.broadcasted_iota(jnp.int32, (n, n), 1)
    x = [jnp.where(ri == ci, 1.0, 0.0) - jnp.where((ri % 2 == 1) & (ci == ri - 1), a, 0.0) for a in a2]
    s = 2
    while s < levels:
        off = ((ri // s) % 2 == 1) & ((ci // s) == (ri // s) - 1)
        t = [dot(xi, jnp.where(off, a, 0.0)) for xi, a in zip(x, a2)]
        x = [xi - dot(ti, xi) for xi, ti in zip(x, t)]
        s *= 2
    return x


def _rwkv_chunk_multi(ops, sout_ref, yscr_ref, nb, lt):
    hd = RWKV_HEADDIM
    per_chunk = CHUNK // lt
    idx = [(c, h) for c in range(nb // per_chunk) for h in range(RWKV_HEADS)]
    n = range(len(idx))
    incl, strict = _chunk_masks(CHUNK, lt)
    mask2 = jnp.concatenate([strict, incl], axis=0)
    top = lambda m: m[0:CHUNK]
    bot = lambda m: m[CHUNK:2 * CHUNK]
    bl = lambda name: [ops[name][c * CHUNK:(c + 1) * CHUNK, h * hd:(h + 1) * hd] for c, h in idx]
    rt, kt, k2t, bt, k2g, bg, v = (bl(name) for name in ('rt', 'kt', 'k2t', 'bt', 'k2g', 'bg', 'v'))
    kr = [jnp.concatenate([kt[j], rt[j]], axis=0) for j in n]
    m1 = [jnp.where(mask2, _dot_a(kr[j], k2t[j], _NT), 0.0) for j in n]
    m2 = [jnp.where(mask2, _dot_a(kr[j], bt[j], _NT), 0.0) for j in n]
    x = _unit_lower_inverse([top(m) for m in m2], CHUNK, lt, _dot_inv)
    m1v = [_dot_app(m1[j], v[j]) for j in n]
    seq = lambda i: slice(i * lt, (i + 1) * lt)
    seqs = range(per_chunk)
    s_old = [[sout_ref[c * per_chunk + i, h] for i in seqs] for c, h in idx]
    pq = [[_dot_st(jnp.concatenate([kt[j][seq(i)], rt[j][seq(i)]], axis=0), s_old[j][i], _NT)
           for i in seqs] for j in n]
    p0 = [jnp.concatenate([pq[j][i][0:lt] for i in seqs], axis=0) for j in n]
    q0 = [jnp.concatenate([pq[j][i][lt:2 * lt] for i in seqs], axis=0) for j in n]
    u = [_dot_app(x[j], p0[j] + top(m1v[j])) for j in n]
    l2u = [_dot_app(bot(m2[j]), u[j]) for j in n]
    for j, (c, h) in enumerate(idx):
        yscr_ref[c * CHUNK:(c + 1) * CHUNK, h * hd:(h + 1) * hd] = q0[j] + bot(m1v[j]) - l2u[j]
        for i in seqs:
            sq = c * per_chunk + i
            sout_ref[sq, h] = (s_old[j][i] * ops['elast'][sq][:, h * hd:(h + 1) * hd]
                               + _dot_st(v[j][seq(i)], k2g[j][seq(i)], _TN)
                               - _dot_st(u[j][seq(i)], bg[j][seq(i)], _TN))


def _rwkv_chunks_seq(ops, state, yscr_ref, nc):
    hd = RWKV_HEADDIM
    pw = 2 * hd
    pairs = RWKV_HEADS // 2
    idx = [(c, q) for c in range(nc) for q in range(pairs)]
    n = range(len(idx))
    top = lambda m: m[0:CHUNK]
    bot = lambda m: m[CHUNK:2 * CHUNK]
    ri = lax.broadcasted_iota(jnp.int32, (CHUNK, pw), 0)
    cl = lax.broadcasted_iota(jnp.int32, (CHUNK, pw), 1) % hd
    left = lax.broadcasted_iota(jnp.int32, (CHUNK, pw), 1) < hd
    strict, incl = cl < ri, cl <= ri
    mask2 = jnp.concatenate([strict, incl], axis=0)
    eye = jnp.where(cl == ri, 1.0, 0.0)
    bd = lambda p: jnp.concatenate([jnp.where(left, p, 0.0), jnp.where(left, 0.0, p)], axis=0)
    on_diag = jnp.concatenate([left, ~left], axis=0)
    r2 = lax.broadcasted_iota(jnp.int32, (pw, pw), 0)
    c2 = lax.broadcasted_iota(jnp.int32, (pw, pw), 1)
    eye2 = jnp.where(r2 == c2, 1.0, 0.0)
    diag_pack = lambda m: jnp.where(left, top(m), bot(m))

    bl = lambda name: [ops[name][c * CHUNK:(c + 1) * CHUNK, q * pw:(q + 1) * pw] for c, q in idx]
    rt, kt, k2t, bt, k2g, bg, v = (bl(name) for name in ('rt', 'kt', 'k2t', 'bt', 'k2g', 'bg', 'v'))
    kr = [jnp.concatenate([kt[j], rt[j]], axis=0) for j in n]
    m1 = [jnp.where(mask2, _dot_a(kr[j], bd(k2t[j]), _NT), 0.0) for j in n]
    m2 = [jnp.where(mask2, _dot_a(kr[j], bd(bt[j]), _NT), 0.0) for j in n]
    x = [eye - jnp.where((ri % 2 == 1) & (cl == ri - 1), top(m), 0.0) for m in m2]
    s = 2
    while s < CHUNK:
        off = ((ri // s) % 2 == 1) & ((cl // s) == (ri // s) - 1)
        t = [_dot_inv(x[j], bd(jnp.where(off, top(m2[j]), 0.0))) for j in n]
        x = [x[j] - _dot_inv(t[j], bd(x[j])) for j in n]
        s *= 2
    m1v = [_dot_app(m1[j], bd(v[j])) for j in n]
    kp = [_dot_app(x[j], bd(kt[j])) for j in n]
    ul = [_dot_app(x[j], bd(top(m1v[j]))) for j in n]
    rp = [rt[j] - _dot_app(bot(m2[j]), bd(kp[j])) for j in n]
    yl = [bot(m1v[j]) - _dot_app(bot(m2[j]), bd(ul[j])) for j in n]
    mc = [eye2 * ops['elast'][c][:, q * pw:(q + 1) * pw]
          - jnp.where(on_diag, _dot_st(kp[j], bg[j], _TN), 0.0) for j, (c, q) in enumerate(idx)]
    dc = [diag_pack(_dot_st(v[j], k2g[j], _TN) - _dot_st(ul[j], bg[j], _TN)) for j in n]
    for c in range(nc):
        js = [c * pairs + q for q in range(pairs)]
        ys = [_dot_st(rp[j], bd(state[q]), _NT) + yl[j] for q, j in enumerate(js)]
        state = [_dot_st(state[q], mc[j]) + dc[j] for q, j in enumerate(js)]
        for q in range(pairs):
            yscr_ref[c * CHUNK:(c + 1) * CHUNK, q * pw:(q + 1) * pw] = ys[q]
    return state


def _rwkv_kernel(p_ref, sh0_ref, s0_ref, mu_ref, wl_ref, w0_ref, a0_ref, kk_ref, ka_ref, rk_ref,
                 lnw_ref, lnb_ref, y_ref, shout_ref, sout_ref, ext_ref, yscr_ref, *, nb, lt):
    rows = nb * lt
    hd = RWKV_HEADDIM
    gw = GROUP_W

    @pl.when(pl.program_id(1) == 0)
    def _():
        ext_ref[:, HIST - 1:HIST, :] = sh0_ref[...]
        sout_ref[...] = s0_ref[...]

    p = p_ref[...]
    ext_ref[:, HIST:HIST + lt, :] = p
    prev = ext_ref[:, HIST - 1:HIST - 1 + lt, :].reshape(rows, RWKV_PROJ)
    last_p = p[:, lt - 1:lt, :]
    ext_ref[:, HIST - 1:HIST, :] = last_p
    shout_ref[...] = last_p
    p = p.reshape(rows, RWKV_PROJ)

    pm = p + (prev - p) * mu_ref[...]
    r = pm[:, 0:gw]
    k = pm[:, gw:2 * gw]
    v = pm[:, 2 * gw:3 * gw]
    lo = pm[:, 3 * gw:]
    lane = lax.broadcasted_iota(jnp.int32, (rows, RWKV_LORA), 1)
    t = jnp.where(lane < 32, jnp.tanh(lo), jnp.where(lane < 64, lo, _sigmoid(lo)))
    lora = _dot(t, wl_ref[...])
    wlog = -_softplus(-(w0_ref[...] + lora[:, 0:gw])) - 0.5
    lw = -jnp.exp(wlog)
    a = _sigmoid(a0_ref[...] + lora[:, gw:2 * gw])
    gate = lora[:, 2 * gw:]
    kk = k * kk_ref[...]
    kk = kk * jnp.minimum(lax.rsqrt(_segsum(kk * kk, hd)), 1e12)
    k2 = k * (1.0 + (a - 1.0) * ka_ref[...])
    beta = kk * a

    unit = lt if nb > 1 else CHUNK
    gcum = _unit_cumsum(lw, unit)
    glast, glast_b = _bcast_last(gcum, rows // unit, unit)
    eg = jnp.exp(gcum)
    einv = jnp.exp(-gcum)
    etail = jnp.exp(glast_b - gcum)
    ops = dict(rt=r * eg, kt=kk * jnp.exp(gcum - lw), k2t=k2 * einv, bt=beta * einv,
               k2g=k2 * etail, bg=beta * etail, v=v, elast=jnp.exp(glast))
    if nb > 1:
        _rwkv_chunk_multi(ops, sout_ref, yscr_ref, nb, lt)
    else:
        pairs = range(RWKV_HEADS // 2)
        state = [jnp.concatenate([sout_ref[0, 2 * q], sout_ref[0, 2 * q + 1]], axis=1) for q in pairs]
        state = _rwkv_chunks_seq(ops, state, yscr_ref, rows // CHUNK)
        for q in pairs:
            sout_ref[0, 2 * q] = state[q][:, 0:hd]
            sout_ref[0, 2 * q + 1] = state[q][:, hd:2 * hd]

    y = yscr_ref[...]
    mean = _segsum(y, hd) * (1.0 / hd)
    dlt = y - mean
    var = _segsum(dlt * dlt, hd) * (1.0 / hd)
    yn = dlt * lax.rsqrt(var + RWKV_LN_EPS) * lnw_ref[...] + lnb_ref[...]
    bonus = _segsum(r * k2 * rk_ref[...], hd) * v
    y_ref[...] = ((yn + bonus) * gate).reshape(nb, lt, gw)


def _rwkv(p, states, prm, l, prev, nb, lt):
    assert (nb > 1 and (nb * lt) % CHUNK == 0 and CHUNK % lt == 0) or (nb == 1 and lt % CHUNK == 0)
    kern = functools.partial(_rwkv_kernel, nb=nb, lt=lt)
    return _mixer_call(kern, "rwkv", l, nb, lt, p, states, prm, prev,
                       [pltpu.VMEM((nb, HIST + lt, RWKV_PROJ), F32),
                        pltpu.VMEM((nb * lt, GROUP_W), F32)])


def _gla_kernel(p_ref, s0_ref, wg2_ref, bg_ref, ng_ref, y_ref, sout_ref, yscr_ref, *, nb, lt, cs):
    rows = nb * lt
    dk, dv = GLA_DK, GLA_DV
    qk = GLA_QK
    units = rows // cs
    per_group = GLA_CS // cs

    @pl.when(pl.program_id(1) == 0)
    def _():
        sout_ref[...] = s0_ref[...]

    p = p_ref[...].reshape(rows, GLA_PW)
    q = p[:, 0:qk] * (dk ** -0.5)
    k = p[:, qk:2 * qk]
    v = p[:, 2 * qk:2 * qk + GROUP_W]
    dg = p[:, 2 * qk + GROUP_W:2 * qk + GROUP_W + 128]
    og = p[:, 2 * qk + GROUP_W + 128:]
    lg = -_softplus(-(_dot(dg, wg2_ref[...]) + bg_ref[...])) * (1.0 / GLA_TAU)
    b = _unit_cumsum(lg, cs)
    bl, bl_b = _bcast_last(b, units, cs)
    qd = q * jnp.exp(b)
    kd = k * jnp.exp(-b)
    kw = k * jnp.exp(bl_b - b)
    ebl = jnp.exp(bl)
    gmask, _ = _chunk_masks(GLA_CS, cs)

    heads = range(GLA_HEADS)
    ks = lambda h: slice(h * dk, (h + 1) * dk)
    vs = lambda h: slice(h * dv, (h + 1) * dv)
    gs = lambda g: slice(g * GLA_CS, (g + 1) * GLA_CS)
    us = lambda u: slice(u * cs, (u + 1) * cs)

    yh = {}
    for g in range(rows // GLA_CS):
        for h in heads:
            att = jnp.where(gmask, _dot(qd[gs(g), ks(h)], kd[gs(g), ks(h)], _NT), 0.0)
            yh[g, h] = _dot(att, v[gs(g), vs(h)])
    ds = {(u, h): _dot(kw[us(u), ks(h)], v[us(u), vs(h)], _TN) for u in range(units) for h in heads}
    ri = lax.broadcasted_iota(jnp.int32, (qk, qk), 0)
    ci = lax.broadcasted_iota(jnp.int32, (qk, qk), 1)
    ones_kv = jnp.ones((qk, dv), F32)
    ecol_all = [_dot_exact_rhs(jnp.where(ri == ci, ebl[u], 0.0), ones_kv) for u in range(units)]
    ecol = {(u, h): ecol_all[u][ks(h)] for u in range(units) for h in heads}

    before = {}
    for h in heads:
        if nb > 1:
            for u in range(units):
                before[u, h] = sout_ref[u, h]
                sout_ref[u, h] = before[u, h] * ecol[u, h] + ds[u, h]
        else:
            s_cur = sout_ref[0, h]
            for u in range(units):
                before[u, h] = s_cur
                s_cur = s_cur * ecol[u, h] + ds[u, h]
            sout_ref[0, h] = s_cur

    for g in range(rows // GLA_CS):
        for h in heads:
            inter = [_dot(qd[us(u), ks(h)], before[u, h])
                     for u in range(g * per_group, (g + 1) * per_group)]
            inter = inter[0] if per_group == 1 else jnp.concatenate(inter, axis=0)
            yscr_ref[gs(g), vs(h)] = yh[g, h] + inter

    y = yscr_ref[...]
    y = y * lax.rsqrt(_segsum(y * y, dv) * (1.0 / dv) + EPS) * ng_ref[...]
    y_ref[...] = (y * _silu(og)).reshape(nb, lt, GROUP_W)


def _gla(p, states, prm, l, prev, nb, lt, cs):
    assert (nb == 1 and cs == GLA_CS and lt % cs == 0) or (
        lt == cs and (nb * cs) % GLA_CS == 0 and GLA_CS % cs == 0)
    kern = functools.partial(_gla_kernel, nb=nb, lt=lt, cs=cs)
    return _mixer_call(kern, "gla", l, nb, lt, p, states, prm, prev,
                       [pltpu.VMEM((nb * lt, GROUP_W), F32)])


def _s5_pitch(lt):
    return lt if (lt // 8) % 2 == 1 else lt + 8


def _s5_kernel(u_ref, hr0_ref, hi0_ref, wb_ref, wc_ref, ar_ref, ai_ref, d_ref, wglu_ref, bglu_ref,
               y_ref, hr_ref, hi_ref, io_ref, h_ref, *, nb, lt):
    n = S5_CH
    wt = GROUP_W // LANES

    @pl.when(pl.program_id(1) == 0)
    def _():
        hr_ref[...] = hr0_ref[...]
        hi_ref[...] = hi0_ref[...]

    lanes = lambda x, j: x[:, j * LANES:(j + 1) * LANES]
    pitch = _s5_pitch(lt)
    seq_rows = lambda i: slice(i * pitch, i * pitch + lt)
    at_t = lambda t: pl.ds(t, nb, stride=pitch)
    step = lambda t: slice(t * nb, (t + 1) * nb)
    for j in range(wt):
        for i in range(nb):
            io_ref[j, seq_rows(i), :] = u_ref[i, :, j * LANES:(j + 1) * LANES]
    u = jnp.concatenate([jnp.concatenate([io_ref[j, at_t(t), :] for j in range(wt)], axis=1)
                         for t in range(lt)], axis=0)
    bu = _dot(u, wb_ref[...])
    ar = jnp.broadcast_to(ar_ref[...], (nb, n))
    ai = jnp.broadcast_to(ai_ref[...], (nb, n))
    hr = hr_ref[...]
    hi = hi_ref[...]
    for t in range(lt):
        hr, hi = (ar * hr - ai * hi + bu[step(t), 0:n], ar * hi + ai * hr + bu[step(t), n:2 * n])
        h_ref[step(t), 0:n] = hr
        h_ref[step(t), n:2 * n] = hi
    hr_ref[...] = hr
    hi_ref[...] = hi
    y = _dot(h_ref[:, 0:n], wc_ref[0]) - _dot(h_ref[:, n:2 * n], wc_ref[1])
    y = y + d_ref[...] * u
    y = 0.5 * y * (1.0 + jnp.tanh(math.sqrt(2.0 / math.pi) * (y + 0.044715 * (y * y * y))))
    y = y * _sigmoid(_dot(y, wglu_ref[...]) + bglu_ref[...])
    for t in range(lt):
        for j in range(wt):
            io_ref[j, at_t(t), :] = lanes(y[step(t)], j)
    for j in range(wt):
        for i in range(nb):
            y_ref[i, :, j * LANES:(j + 1) * LANES] = io_ref[j, seq_rows(i), :]


def _s5(u, states, prm, l, prev, nb, lt):
    kern = functools.partial(_s5_kernel, nb=nb, lt=lt)
    return _mixer_call(kern, "s5", l, nb, lt, u, states, prm, prev,
                       [pltpu.VMEM((GROUP_W // LANES, nb * _s5_pitch(lt), LANES), F32),
                        pltpu.VMEM((nb * lt, 2 * S5_CH), F32)])


def _s5_zoh_kernel(lr_ref, li_ref, ls_ref, bre_ref, bim_ref, ar_ref, ai_ref, bbre_ref, bbim_ref):
    lr = lr_ref[...]
    li = li_ref[...]
    dts = jnp.exp(ls_ref[...])
    mag = jnp.exp(lr * dts)
    ar = mag * jnp.cos(li * dts)
    ai = mag * jnp.sin(li * dts)
    den = lr * lr + li * li
    fr = ((ar - 1.0) * lr + ai * li) / den
    fi = (ai * lr - (ar - 1.0) * li) / den
    ar_ref[...] = ar
    ai_ref[...] = ai
    r, n = lr.shape
    per_input = lambda f: jnp.broadcast_to(f[:, None, :], (r, S5_GROUP, n)).reshape(r * S5_GROUP, n)
    fre, fie = per_input(fr), per_input(fi)
    bre = bre_ref[...]
    bim = bim_ref[...]
    bbre_ref[...] = fre * bre - fie * bim
    bbim_ref[...] = fre * bim + fie * bre


def _s5_zoh(lam_re, lam_im, log_step, b_re, b_im):
    dep, g, n = lam_re.shape
    rows = dep * g
    flat = lambda a: a.astype(F32).reshape(rows, n)
    flat_b = lambda b: jnp.swapaxes(b.astype(F32), -1, -2).reshape(rows * S5_GROUP, n)
    vm = pl.BlockSpec(memory_space=pltpu.VMEM)
    ar, ai, bbr, bbi = pl.pallas_call(
        _s5_zoh_kernel,
        out_shape=(jax.ShapeDtypeStruct((rows, n), F32),) * 2
        + (jax.ShapeDtypeStruct((rows * S5_GROUP, n), F32),) * 2,
        in_specs=[vm] * 5,
        out_specs=(vm,) * 4,
        name="s5_zoh",
    )(flat(lam_re), flat(lam_im), log_step.astype(F32).reshape(rows, 1), flat_b(b_re), flat_b(b_im))
    return (ar.reshape(dep, g, n), ai.reshape(dep, g, n),
            bbr.reshape(dep, g, S5_GROUP, n), bbi.reshape(dep, g, S5_GROUP, n))


def _block_diag_in(b):
    dep, g, i, n = b.shape
    eye = jnp.eye(g, dtype=b.dtype)
    return jnp.einsum('lgin,gh->lgihn', b, eye).reshape(dep, g * i, g * n)


def _block_diag_out(c):
    dep, g, o, n = c.shape
    eye = jnp.eye(g, dtype=c.dtype)
    return jnp.einsum('lgon,gh->lgnho', c, eye).reshape(dep, g * n, g * o)


def _prep_params(W):
    dep = W['w_in'].shape[0]
    row = lambda name: W[name].reshape(dep, 1, -1).astype(F32)
    rep = lambda name, n: jnp.repeat(W[name].reshape(dep, -1), n, axis=1).reshape(dep, 1, -1).astype(F32)
    w_in = W['w_in'].astype(BF16)
    o_r = SSD_PROJ
    o_g = o_r + RWKV_PROJ
    o_s = o_g + GLA_PROJ
    w_ssd = jnp.concatenate([w_in[..., :GROUP_W + SSD_CONV_CH],
                             jnp.repeat(w_in[..., GROUP_W + SSD_CONV_CH:o_r], SSD_HEADDIM, axis=2)], axis=2)
    w_rwkv = w_in[..., o_r:o_g]
    gq = o_g + 2 * GLA_QK + GROUP_W
    w_gla = jnp.concatenate([w_in[..., o_g:gq], w_in[..., gq:gq + GLA_GATE_LORA],
                             jnp.zeros((dep, D_MODEL, 128 - GLA_GATE_LORA), BF16),
                             w_in[..., gq + GLA_GATE_LORA:o_s]], axis=2)
    w_s5 = w_in[..., o_s:]
    in_w = (w_ssd, w_rwkv, w_gla, w_s5)

    ssd = (W['ssd_conv_w'].astype(F32), row('ssd_conv_b'), rep('ssd_a_log', SSD_HEADDIM),
           rep('ssd_dt_bias', SSD_HEADDIM), rep('ssd_d', SSD_HEADDIM), row('ssd_norm_g'))

    z = lambda r, c: jnp.zeros((dep, r, c), F32)
    w_lora = jnp.concatenate([
        jnp.concatenate([W['rwkv_w2'], z(32, 2 * GROUP_W)], axis=2),
        jnp.concatenate([z(32, GROUP_W), W['rwkv_a2'], z(32, GROUP_W)], axis=2),
        jnp.concatenate([z(64, 2 * GROUP_W), W['rwkv_g2']], axis=2)], axis=1).astype(BF16)
    rwkv = (row('rwkv_mu'), w_lora, row('rwkv_w0'), row('rwkv_a0'), row('rwkv_k_k'),
            row('rwkv_k_a'), row('rwkv_r_k'), row('rwkv_lnx_w'), row('rwkv_lnx_b'))

    wg2 = jnp.concatenate([W['gla_wg2'], z(128 - GLA_GATE_LORA, GLA_QK)], axis=1).astype(BF16)
    gla = (wg2, row('gla_bg'), row('gla_norm_g'))

    ar, ai, bb_re, bb_im = _s5_zoh(W['s5_lam_re'], W['s5_lam_im'], W['s5_log_step'],
                                   W['s5_b_re'], W['s5_b_im'])
    wb = jnp.concatenate([_block_diag_in(bb_re), _block_diag_in(bb_im)], axis=2).astype(BF16)
    wc = jnp.stack([_block_diag_out(W['s5_c_re']), _block_diag_out(W['s5_c_im'])], axis=1).astype(BF16)
    s5 = (wb, wc, ar.reshape(dep, 1, -1), ai.reshape(dep, 1, -1), row('s5_d'),
          W['s5_w_glu'].astype(BF16), row('s5_b_glu'))

    g4 = lambda name: W[name].reshape(dep, 1, 1, D_MODEL).astype(F32)
    return dict(
        in_w=in_w, ssd=ssd, rwkv=rwkv, gla=gla, s5=s5,
        g_mix_pre=g4('g_mix_pre'), g_mix_post=g4('g_mix_post'),
        g_ffn_pre=g4('g_ffn_pre'), g_ffn_post=g4('g_ffn_post'),
        w_out=W['w_out'].reshape(dep, 4, GROUP_W, D_MODEL).astype(BF16),
        w1=W['mlp_w1'].astype(BF16), w2=W['mlp_w2'].astype(BF16))


def _run_group(x, mod, first, st, P, cfg):
    dep, bsz = st['ssd'].shape[:2]
    hist = jnp.pad(st['ssd_conv'], ((0, 0), (0, 0), (HIST - (SSD_CONV - 1), 0), (0, 0)))
    ssd_st = (hist, st['ssd'])
    rwkv_st = (st['rwkv_shift'].reshape(dep, bsz, 1, RWKV_PROJ), st['rwkv'])
    gla_st = (st['gla'],)
    s5_st = (st['s5_re'].reshape(dep, bsz, S5_CH), st['s5_im'].reshape(dep, bsz, S5_CH))
    new = dict(ssd=None, rwkv=None, gla=None, s5=None)
    for l in range(DEPTH):
        p_ssd, p_rwkv, p_gla, p_s5 = _in_proj(x, mod, first, P['g_mix_pre'], P['in_w'], l, *cfg['tok'])
        y_ssd, *new['ssd'] = _ssd(p_ssd, ssd_st, P['ssd'], l, new['ssd'], *cfg['ssd'])
        y_rwkv, *new['rwkv'] = _rwkv(p_rwkv, rwkv_st, P['rwkv'], l, new['rwkv'], *cfg['rwkv'])
        y_gla, *new['gla'] = _gla(p_gla, gla_st, P['gla'], l, new['gla'], *cfg['gla'])
        y_s5, *new['s5'] = _s5(p_s5, s5_st, P['s5'], l, new['s5'], *cfg['s5'])
        x = _out_mlp(x, (y_ssd, y_rwkv, y_gla, y_s5), mod, first, P['g_mix_post'], P['g_ffn_pre'],
                     P['g_ffn_post'], P['w_out'], P['w1'], P['w2'], l, *cfg['mlp'])
    hist_new, ssd_new = new['ssd']
    shift_new, rwkv_new = new['rwkv']
    s5r_new, s5i_new = new['s5']
    return x, dict(
        ssd=ssd_new, ssd_conv=hist_new[:, :, HIST - (SSD_CONV - 1):, :],
        rwkv=rwkv_new, rwkv_shift=shift_new.reshape(dep, bsz, RWKV_PROJ), gla=new['gla'][0],
        s5_re=s5r_new.reshape(dep, bsz, S5_GROUPS, S5_STATE),
        s5_im=s5i_new.reshape(dep, bsz, S5_GROUPS, S5_STATE))


_PROMPT_CFG = dict(tok=(1, 512), mlp=(1, 1024), ssd=(1, 512), rwkv=(1, 512), gla=(1, 512, GLA_CS),
                   s5=(8, 128))
_SAMPLE_CFG = dict(tok=(64, 8), mlp=(64, 8), ssd=(32, 8), rwkv=(32, 8), gla=(16, 8, 8), s5=(64, 8))


def kernel(x_prompt, x_sample, c_prompt, c_sample, state_ssd, state_ssd_conv, state_rwkv, state_rwkv_shift, state_gla, state_s5_re, state_s5_im, w_ada, b_ada, g_mix_pre, g_mix_post, g_ffn_pre, g_ffn_post, w_in, w_out, ssd_conv_w, ssd_conv_b, ssd_a_log, ssd_dt_bias, ssd_d, ssd_norm_g, rwkv_mu, rwkv_w0, rwkv_w2, rwkv_a0, rwkv_a2, rwkv_g2, rwkv_k_k, rwkv_k_a, rwkv_r_k, rwkv_lnx_w, rwkv_lnx_b, gla_wg2, gla_bg, gla_norm_g, s5_lam_re, s5_lam_im, s5_log_step, s5_b_re, s5_b_im, s5_c_re, s5_c_im, s5_d, s5_w_glu, s5_b_glu, mlp_w1, mlp_w2):
    W = dict(w_in=w_in, w_out=w_out, g_mix_pre=g_mix_pre, g_mix_post=g_mix_post,
             g_ffn_pre=g_ffn_pre, g_ffn_post=g_ffn_post,
             ssd_conv_w=ssd_conv_w, ssd_conv_b=ssd_conv_b, ssd_a_log=ssd_a_log,
             ssd_dt_bias=ssd_dt_bias, ssd_d=ssd_d, ssd_norm_g=ssd_norm_g,
             rwkv_mu=rwkv_mu, rwkv_w0=rwkv_w0, rwkv_w2=rwkv_w2, rwkv_a0=rwkv_a0, rwkv_a2=rwkv_a2,
             rwkv_g2=rwkv_g2, rwkv_k_k=rwkv_k_k, rwkv_k_a=rwkv_k_a, rwkv_r_k=rwkv_r_k,
             rwkv_lnx_w=rwkv_lnx_w, rwkv_lnx_b=rwkv_lnx_b,
             gla_wg2=gla_wg2, gla_bg=gla_bg, gla_norm_g=gla_norm_g,
             s5_lam_re=s5_lam_re, s5_lam_im=s5_lam_im, s5_log_step=s5_log_step,
             s5_b_re=s5_b_re, s5_b_im=s5_b_im, s5_c_re=s5_c_re, s5_c_im=s5_c_im,
             s5_d=s5_d, s5_w_glu=s5_w_glu, s5_b_glu=s5_b_glu, mlp_w1=mlp_w1, mlp_w2=mlp_w2)
    P = _prep_params(W)

    nbp, nbs = x_prompt.shape[0], x_sample.shape[0]
    mod = _ada(jnp.concatenate([c_sample, c_prompt], axis=0), w_ada, b_ada)
    mod = mod.reshape(DEPTH, nbs + nbp, N_MOD, D_MODEL)

    st_sample = dict(ssd=state_ssd, ssd_conv=state_ssd_conv, rwkv=state_rwkv,
                     rwkv_shift=state_rwkv_shift, gla=state_gla, s5_re=state_s5_re, s5_im=state_s5_im)
    st_prompt = {n: jnp.zeros((DEPTH, nbp) + v.shape[2:], v.dtype) for n, v in st_sample.items()}

    y_prompt, sp = _run_group(x_prompt, mod, nbs, st_prompt, P, _PROMPT_CFG)
    y_sample, ss = _run_group(x_sample, mod, 0, st_sample, P, _SAMPLE_CFG)
    return (y_prompt, y_sample,
            sp['ssd'], ss['ssd'], sp['ssd_conv'], ss['ssd_conv'],
            sp['rwkv'], ss['rwkv'], sp['rwkv_shift'], ss['rwkv_shift'],
            sp['gla'], ss['gla'], sp['s5_re'], ss['s5_re'], sp['s5_im'], ss['s5_im'])
```

```python
import functools
import math

import jax
import jax.numpy as jnp
from jax import lax
from jax.experimental import pallas as pl
from jax.experimental.pallas import tpu as pltpu

F32 = jnp.float32
BF16 = jnp.bfloat16

D_MODEL = 1024
DEPTH = 2
GROUP_W = 256
D_FF = 4096
N_MOD = 6
EPS = 1e-6

SSD_HEADS = 4
SSD_HEADDIM = 64
SSD_STATE = 64
SSD_NGROUPS = 2
SSD_BC = SSD_NGROUPS * SSD_STATE
SSD_CONV = 4
SSD_CONV_CH = GROUP_W + 2 * SSD_BC
SSD_PROJ = GROUP_W + SSD_CONV_CH + SSD_HEADS
SSD_PW = GROUP_W + SSD_CONV_CH + GROUP_W

RWKV_HEADS = 4
RWKV_HEADDIM = 64
RWKV_LORA = 128
RWKV_PROJ = 3 * GROUP_W + RWKV_LORA
RWKV_LN_EPS = 64e-5

GLA_HEADS = 4
GLA_DK = 32
GLA_DV = 64
GLA_GATE_LORA = 16
GLA_TAU = 16.0
GLA_QK = GLA_HEADS * GLA_DK
GLA_PROJ = 2 * GLA_QK + GROUP_W + GLA_GATE_LORA + GROUP_W
GLA_PW = 2 * GLA_QK + GROUP_W + 128 + GROUP_W

S5_GROUP = 16
S5_GROUPS = 16
S5_STATE = 64
S5_CH = S5_GROUPS * S5_STATE
S5_PROJ = GROUP_W

LANES = 128
HIST = 8
CHUNK = 64
GLA_CS = 32
VMEM_LIMIT = 58 * 1024 * 1024


def _dot(a, b, dims=(((1,), (0,)), ((), ()))):
    return lax.dot_general(a.astype(BF16), b.astype(BF16), dims, preferred_element_type=F32)


_NT = (((1,), (1,)), ((), ()))
_TN = (((0,), (0,)), ((), ()))
_NN = (((1,), (0,)), ((), ()))


def _split2(a):
    hi = a.astype(BF16)
    lo = (a - hi.astype(F32)).astype(BF16)
    return hi, lo


def _dot3(a, b, dims=_NN):
    ah, al = _split2(a)
    bh, bl = _split2(b)
    d = lambda x, y: lax.dot_general(x, y, dims, preferred_element_type=F32)
    return d(ah, bh) + (d(ah, bl) + d(al, bh))


_dot_a = _dot
_dot_inv = _dot
_dot_app = _dot
_dot_st = _dot


def _dot_exact_lhs(m, x, dims=_NN):
    mb = m.astype(BF16)
    x1 = x.astype(BF16)
    r1 = x - x1.astype(F32)
    x2 = r1.astype(BF16)
    x3 = (r1 - x2.astype(F32)).astype(BF16)
    d = lambda y: lax.dot_general(mb, y, dims, preferred_element_type=F32)
    return d(x1) + (d(x2) + d(x3))


def _dot_exact_rhs(x, m):
    mb = m.astype(BF16)
    x1 = x.astype(BF16)
    r1 = x - x1.astype(F32)
    x2 = r1.astype(BF16)
    x3 = (r1 - x2.astype(F32)).astype(BF16)
    d = lambda y: lax.dot_general(y, mb, _NN, preferred_element_type=F32)
    return d(x1) + (d(x2) + d(x3))


def _segsum(x, width):
    n = x.shape[-1]
    r = lax.broadcasted_iota(jnp.int32, (n, n), 0) // width
    c = lax.broadcasted_iota(jnp.int32, (n, n), 1) // width
    j = jnp.where(r == c, 1.0, 0.0).astype(BF16)
    hi, lo = _split2(x)
    d = lambda y: lax.dot_general(y, j, _NN, preferred_element_type=F32)
    return d(hi) + d(lo)


def _sigmoid(x):
    return 1.0 / (1.0 + jnp.exp(-x))


def _silu(x):
    return x * _sigmoid(x)


def _softplus(x):
    return jnp.maximum(x, 0.0) + jnp.log1p(jnp.exp(-jnp.abs(x)))


def _rms(x, g):
    return x * lax.rsqrt(jnp.mean(x * x, axis=-1, keepdims=True) + EPS) * g


def _chunk_masks(rows, lt):
    r = lax.broadcasted_iota(jnp.int32, (rows, rows), 0)
    c = lax.broadcasted_iota(jnp.int32, (rows, rows), 1)
    same = (r // lt) == (c // lt)
    return same & (c <= r), same & (c < r)


def _unit_cumsum(x, unit):
    incl, _ = _chunk_masks(CHUNK, unit)
    tri = jnp.where(incl, 1.0, 0.0)
    parts = [_dot_exact_lhs(tri, x[j * CHUNK:(j + 1) * CHUNK]) for j in range(x.shape[0] // CHUNK)]
    return parts[0] if len(parts) == 1 else jnp.concatenate(parts, axis=0)


def _lsel(a, l, **kw):
    shape = a.shape[1:]
    return pl.BlockSpec((None,) + shape, lambda i, c: (l,) + (0,) * len(shape), **kw)


def _lseq(a, l, nb):
    shape = a.shape[2:]
    return pl.BlockSpec((None, nb) + shape, lambda i, c: (l, i) + (0,) * len(shape))


def _mixer_call(kern, name, l, nb, lt, p, states, params, prev, scratch):
    bsz, seq, width = p.shape
    n_in = 1 + len(states) + len(params)
    n_st = len(states)
    alias = tuple(prev) if prev is not None else ()
    first = prev is None

    def body(*refs):
        ins, outs = refs[:n_in], refs[n_in + len(alias):]
        y_ref, st_refs, scratch_refs = outs[0], outs[1:1 + n_st], outs[1 + n_st:]
        if first:
            @pl.when(pl.program_id(1) == 0)
            def _():
                for r in st_refs:
                    r[1:] = jnp.zeros((r.shape[0] - 1,) + r.shape[1:], F32)
            st_refs = [r.at[0] for r in st_refs]
        return kern(*ins, y_ref, *st_refs, *scratch_refs)

    def st_out_spec(s):
        if not first:
            return _lseq(s, l, nb)
        shape = s.shape[2:]
        return pl.BlockSpec((s.shape[0], nb) + shape, lambda i, c: (0, i) + (0,) * len(shape))

    tok = lambda w: pl.BlockSpec((nb, lt, w), lambda i, c: (i, c, 0))
    return pl.pallas_call(
        body,
        out_shape=(jax.ShapeDtypeStruct((bsz, seq, GROUP_W), F32),)
        + tuple(jax.ShapeDtypeStruct(s.shape, F32) for s in states),
        grid=(bsz // nb, seq // lt),
        in_specs=[tok(width)] + [_lseq(s, l, nb) for s in states] + [_lsel(a, l) for a in params]
        + [pl.BlockSpec(memory_space=pl.ANY)] * len(alias),
        out_specs=(tok(GROUP_W),) + tuple(st_out_spec(s) for s in states),
        scratch_shapes=scratch,
        input_output_aliases={n_in + k: 1 + k for k in range(len(alias))},
        compiler_params=pltpu.CompilerParams(
            dimension_semantics=("parallel", "arbitrary"), vmem_limit_bytes=VMEM_LIMIT),
        name=name,
    )(p, *states, *params, *alias)


def _bcast_last(x, nb, lt):
    c = x.shape[-1]
    last = x.reshape(nb, lt, c)[:, lt - 1:lt, :]
    return last, jnp.broadcast_to(last, (nb, lt, c)).reshape(nb * lt, c)


def _ada_kernel(c_ref, w_ref, b_ref, o_ref):
    c = c_ref[...]
    o_ref[0] = _dot(_silu(c), w_ref[0]) + b_ref[0]


def _ada(c_all, w_ada, b_ada):
    rows = c_all.shape[0]
    n = N_MOD * D_MODEL
    tn = 1536
    return pl.pallas_call(
        _ada_kernel,
        out_shape=jax.ShapeDtypeStruct((DEPTH, rows, n), F32),
        grid=(DEPTH, n // tn),
        in_specs=[
            pl.BlockSpec((rows, D_MODEL), lambda l, j: (0, 0)),
            pl.BlockSpec((1, D_MODEL, tn), lambda l, j: (l, 0, j)),
            pl.BlockSpec((1, 1, tn), lambda l, j: (l, 0, j)),
        ],
        out_specs=pl.BlockSpec((1, rows, tn), lambda l, j: (l, 0, j)),
        compiler_params=pltpu.CompilerParams(
            dimension_semantics=("parallel", "parallel"), vmem_limit_bytes=VMEM_LIMIT),
        name="ada",
    )(c_all, w_ada, b_ada.reshape(DEPTH, 1, n))


def _in_kernel(x_ref, mod_ref, g_ref, wssd_ref, wrwkv_ref, wgla_ref, ws5_ref,
               ossd_ref, orwkv_ref, ogla_ref, os5_ref):
    nb, lt, d = x_ref.shape
    for s in range(2):
        if nb == 1:
            sel = (slice(None), slice(s * (lt // 2), (s + 1) * (lt // 2)))
            m = mod_ref[...]
        else:
            sel = (slice(s * (nb // 2), (s + 1) * (nb // 2)),)
            m = mod_ref[sel]
        x = x_ref[sel]
        h = _rms(x, g_ref[...]) * (1.0 + m[:, 1:2, :]) + m[:, 0:1, :]
        hb = h.reshape(x.shape[0] * x.shape[1], d).astype(BF16)
        for w_ref, o_ref in ((wssd_ref, ossd_ref), (wrwkv_ref, orwkv_ref),
                             (wgla_ref, ogla_ref), (ws5_ref, os5_ref)):
            o_ref[sel] = jnp.dot(hb, w_ref[...], preferred_element_type=F32).reshape(
                x.shape[:2] + (o_ref.shape[-1],))


def _mod_spec(mod, l, first, nb):
    return pl.BlockSpec((None, nb, N_MOD, mod.shape[-1]), lambda i, j: (l, first // nb + i, 0, 0))


def _in_proj(x, mod, first, g, wts, l, nb, lt):
    bsz, seq, d = x.shape
    widths = (SSD_PW, RWKV_PROJ, GLA_PW, S5_PROJ)
    return pl.pallas_call(
        _in_kernel,
        out_shape=tuple(jax.ShapeDtypeStruct((bsz, seq, w), F32) for w in widths),
        grid=(bsz // nb, seq // lt),
        in_specs=[pl.BlockSpec((nb, lt, d), lambda i, j: (i, j, 0)), _mod_spec(mod, l, first, nb),
                  _lsel(g, l)] + [_lsel(w, l) for w in wts],
        out_specs=tuple(pl.BlockSpec((nb, lt, w), lambda i, j: (i, j, 0)) for w in widths),
        compiler_params=pltpu.CompilerParams(
            dimension_semantics=("parallel", "parallel"), vmem_limit_bytes=VMEM_LIMIT),
        name="in_proj",
    )(x, mod, g, *wts)


def _out_mlp_kernel(x_ref, y0_ref, y1_ref, y2_ref, y3_ref, mod_ref, gpost_ref, gpre_ref, gfpost_ref,
                    wout_ref, w1_ref, w2_ref, o_ref):
    nb, lt, d = x_ref.shape
    if nb == 1:
        hshape = (1, lt // 2, d)
        half = lambda ref, s: ref[:, s * (lt // 2):(s + 1) * (lt // 2), :]
        mods = [mod_ref[...]] * 2
    else:
        hshape = (nb // 2, lt, d)
        half = lambda ref, s: ref[s * (nb // 2):(s + 1) * (nb // 2)]
        mods = [half(mod_ref, s) for s in range(2)]
    hrows = hshape[0] * hshape[1]
    x1s, hbs = [], []
    for s in range(2):
        m = mods[s]
        mix = None
        for i, y_ref in enumerate((y0_ref, y1_ref, y2_ref, y3_ref)):
            t = jnp.dot(half(y_ref, s).reshape(hrows, GROUP_W).astype(BF16), wout_ref[i],
                        preferred_element_type=F32)
            mix = t if mix is None else mix + t
        x1 = half(x_ref, s) + m[:, 2:3, :] * _rms(mix, gpost_ref[0]).reshape(hshape)
        h = _rms(x1, gpre_ref[...]) * (1.0 + m[:, 4:5, :]) + m[:, 3:4, :]
        x1s.append(x1)
        hbs.append(h.reshape(hrows, d).astype(BF16))
    accs = [None, None]
    step = 1024
    for j in range(D_FF // step):
        for s in range(2):
            a = jnp.dot(hbs[s], w1_ref[:, j * step:(j + 1) * step], preferred_element_type=F32)
            a = jnp.square(jnp.maximum(a, 0.0)).astype(BF16)
            t = jnp.dot(a, w2_ref[j * step:(j + 1) * step, :], preferred_element_type=F32)
            accs[s] = t if accs[s] is None else accs[s] + t
    for s in range(2):
        out = x1s[s] + mods[s][:, 5:6, :] * _rms(accs[s], gfpost_ref[0]).reshape(hshape)
        if nb == 1:
            o_ref[:, s * (lt // 2):(s + 1) * (lt // 2), :] = out
        else:
            o_ref[s * (nb // 2):(s + 1) * (nb // 2)] = out


def _out_mlp(x, ys, mod, first, gpost, gpre, gfpost, wout, w1, w2, l, nb, lt):
    bsz, seq, d = x.shape
    tok = lambda i, j: (i, j, 0)
    one = pl.Buffered(1)
    return pl.pallas_call(
        _out_mlp_kernel,
        out_shape=jax.ShapeDtypeStruct((bsz, seq, d), F32),
        grid=(bsz // nb, seq // lt),
        in_specs=[pl.BlockSpec((nb, lt, d), tok)]
        + [pl.BlockSpec((nb, lt, GROUP_W), tok)] * 4
        + [_mod_spec(mod, l, first, nb)]
        + [_lsel(g, l) for g in (gpost, gpre, gfpost)]
        + [_lsel(w, l, pipeline_mode=one) for w in (wout, w1, w2)],
        out_specs=pl.BlockSpec((nb, lt, d), tok),
        compiler_params=pltpu.CompilerParams(
            dimension_semantics=("parallel", "parallel"), vmem_limit_bytes=VMEM_LIMIT),
        name="out_mlp",
    )(x, *ys, mod, gpost, gpre, gfpost, wout, w1, w2)


def _ssd_kernel(p_ref, hist_ref, s0_ref, cw_ref, cb_ref, alog_ref, dtb_ref, dsk_ref, ng_ref,
                y_ref, hout_ref, sout_ref, ext_ref, yscr_ref, *, nb, lt):
    rows = nb * lt
    hd = SSD_HEADDIM

    @pl.when(pl.program_id(1) == 0)
    def _():
        ext_ref[:, 0:HIST, :] = hist_ref[...]
        sout_ref[...] = s0_ref[...]

    p = p_ref[...]
    ext_ref[:, HIST:HIST + lt, :] = p[:, :, GROUP_W:GROUP_W + SSD_CONV_CH]
    conv = cb_ref[...]
    for j in range(SSD_CONV):
        o = HIST - (SSD_CONV - 1) + j
        conv = conv + ext_ref[:, o:o + lt, :] * cw_ref[j:j + 1, :]
    tail = ext_ref[:, lt:lt + HIST, :]
    ext_ref[:, 0:HIST, :] = tail
    hout_ref[...] = tail

    xbc = _silu(conv).reshape(rows, SSD_CONV_CH)
    xs = xbc[:, 0:GROUP_W]
    bm = xbc[:, GROUP_W:GROUP_W + SSD_BC]
    cm = xbc[:, GROUP_W + SSD_BC:]
    z = p[:, :, 0:GROUP_W].reshape(rows, GROUP_W)
    dt = _softplus(p[:, :, GROUP_W + SSD_CONV_CH:].reshape(rows, GROUP_W) + dtb_ref[...])
    a = dt * (-jnp.exp(alog_ref[...]))
    unit = lt if nb > 1 else CHUNK
    units = rows // unit
    per_chunk = CHUNK // unit
    cum = _unit_cumsum(a, unit)
    last, last_b = _bcast_last(cum, units, unit)
    xdt = xs * dt
    xw = xdt * jnp.exp(last_b - cum)
    ecum = jnp.exp(cum)
    elast = jnp.exp(last)
    ones_row = jnp.full((CHUNK, hd), 1.0 / hd, F32)
    cmask, _ = _chunk_masks(CHUNK, unit)
    heads = range(SSD_HEADS)
    hs = lambda h: slice(h * hd, (h + 1) * hd)
    gs = lambda h: slice((h // 2) * SSD_STATE, (h // 2 + 1) * SSD_STATE)
    cs = lambda j: slice(j * CHUNK, (j + 1) * CHUNK)
    us = lambda u: slice(u * unit, (u + 1) * unit)

    yh = {}
    for j in range(rows // CHUNK):
        cb = [_dot(cm[cs(j), gs(2 * g)], bm[cs(j), gs(2 * g)], _NT) for g in range(SSD_NGROUPS)]
        for h in heads:
            ccol = cum[cs(j), hs(h)]
            crow = _dot_exact_lhs(ones_row, ccol, _NT)
            decay = jnp.exp(jnp.where(cmask, ccol - crow, -jnp.inf))
            yh[j, h] = _dot(cb[h // 2] * decay, xdt[cs(j), hs(h)])
    ds = {(u, h): _dot(xw[us(u), hs(h)], bm[us(u), gs(h)], _TN) for u in range(units) for h in heads}

    before = {}
    for h in heads:
        if nb > 1:
            for u in range(units):
                before[u, h] = sout_ref[u, h]
                sout_ref[u, h] = before[u, h] * elast[u][:, hs(h)] + ds[u, h]
        else:
            s_cur = sout_ref[0, h]
            for u in range(units):
                before[u, h] = s_cur
                s_cur = s_cur * elast[u][:, hs(h)] + ds[u, h]
            sout_ref[0, h] = s_cur

    for j in range(rows // CHUNK):
        for h in heads:
            inter = [_dot(cm[us(u), gs(h)], before[u, h], _NT)
                     for u in range(j * per_chunk, (j + 1) * per_chunk)]
            inter = inter[0] if per_chunk == 1 else jnp.concatenate(inter, axis=0)
            yscr_ref[cs(j), hs(h)] = yh[j, h] + inter * ecum[cs(j), hs(h)]

    y = yscr_ref[...] + dsk_ref[...] * xs
    y = y * _silu(z)
    y = y * lax.rsqrt(_segsum(y * y, hd) * (1.0 / hd) + EPS) * ng_ref[...]
    y_ref[...] = y.reshape(nb, lt, GROUP_W)


def _ssd(p, states, prm, l, prev, nb, lt):
    assert CHUNK == SSD_HEADDIM
    assert (nb > 1 and (nb * lt) % CHUNK == 0 and CHUNK % lt == 0) or (nb == 1 and lt % CHUNK == 0)
    kern = functools.partial(_ssd_kernel, nb=nb, lt=lt)
    return _mixer_call(kern, "ssd", l, nb, lt, p, states, prm, prev,
                       [pltpu.VMEM((nb, HIST + lt, SSD_CONV_CH), F32),
                        pltpu.VMEM((nb * lt, GROUP_W), F32)])


def _unit_lower_inverse(a2, n, levels, dot):
    ri = lax.broadcasted_iota(jnp.int32, (n, n), 0)
    ci = lax.broadcasted_iota(jnp.int32, (n, n), 1)
    x = [jnp.where(ri == ci, 1.0, 0.0) - jnp.where((ri % 2 == 1) & (ci == ri - 1), a, 0.0) for a in a2]
    s = 2
    while s < levels:
        off = ((ri // s) % 2 == 1) & ((ci // s) == (ri // s) - 1)
        t = [dot(xi, jnp.where(off, a, 0.0)) for xi, a in zip(x, a2)]
        x = [xi - dot(ti, xi) for xi, ti in zip(x, t)]
        s *= 2
    return x


def _rwkv_chunk_multi(ops, sout_ref, yscr_ref, nb, lt):
    hd = RWKV_HEADDIM
    per_chunk = CHUNK // lt
    idx = [(c, h) for c in range(nb // per_chunk) for h in range(RWKV_HEADS)]
    n = range(len(idx))
    incl, strict = _chunk_masks(CHUNK, lt)
    mask2 = jnp.concatenate([strict, incl], axis=0)
    top = lambda m: m[0:CHUNK]
    bot = lambda m: m[CHUNK:2 * CHUNK]
    bl = lambda name: [ops[name][c * CHUNK:(c + 1) * CHUNK, h * hd:(h + 1) * hd] for c, h in idx]
    rt, kt, k2t, bt, k2g, bg, v = (bl(name) for name in ('rt', 'kt', 'k2t', 'bt', 'k2g', 'bg', 'v'))
    kr = [jnp.concatenate([kt[j], rt[j]], axis=0) for j in n]
    m1 = [jnp.where(mask2, _dot_a(kr[j], k2t[j], _NT), 0.0) for j in n]
    m2 = [jnp.where(mask2, _dot_a(kr[j], bt[j], _NT), 0.0) for j in n]
    x = _unit_lower_inverse([top(m) for m in m2], CHUNK, lt, _dot_inv)
    m1v = [_dot_app(m1[j], v[j]) for j in n]
    seq = lambda i: slice(i * lt, (i + 1) * lt)
    seqs = range(per_chunk)
    s_old = [[sout_ref[c * per_chunk + i, h] for i in seqs] for c, h in idx]
    pq = [[_dot_st(jnp.concatenate([kt[j][seq(i)], rt[j][seq(i)]], axis=0), s_old[j][i], _NT)
           for i in seqs] for j in n]
    p0 = [jnp.concatenate([pq[j][i][0:lt] for i in seqs], axis=0) for j in n]
    q0 = [jnp.concatenate([pq[j][i][lt:2 * lt] for i in seqs], axis=0) for j in n]
    u = [_dot_app(x[j], p0[j] + top(m1v[j])) for j in n]
    l2u = [_dot_app(bot(m2[j]), u[j]) for j in n]
    for j, (c, h) in enumerate(idx):
        yscr_ref[c * CHUNK:(c + 1) * CHUNK, h * hd:(h + 1) * hd] = q0[j] + bot(m1v[j]) - l2u[j]
        for i in seqs:
            sq = c * per_chunk + i
            sout_ref[sq, h] = (s_old[j][i] * ops['elast'][sq][:, h * hd:(h + 1) * hd]
                               + _dot_st(v[j][seq(i)], k2g[j][seq(i)], _TN)
                               - _dot_st(u[j][seq(i)], bg[j][seq(i)], _TN))


def _rwkv_chunks_seq(ops, state, yscr_ref, nc):
    hd = RWKV_HEADDIM
    pw = 2 * hd
    pairs = RWKV_HEADS // 2
    idx = [(c, q) for c in range(nc) for q in range(pairs)]
    n = range(len(idx))
    top = lambda m: m[0:CHUNK]
    bot = lambda m: m[CHUNK:2 * CHUNK]
    ri = lax.broadcasted_iota(jnp.int32, (CHUNK, pw), 0)
    cl = lax.broadcasted_iota(jnp.int32, (CHUNK, pw), 1) % hd
    left = lax.broadcasted_iota(jnp.int32, (CHUNK, pw), 1) < hd
    strict, incl = cl < ri, cl <= ri
    mask2 = jnp.concatenate([strict, incl], axis=0)
    eye = jnp.where(cl == ri, 1.0, 0.0)
    bd = lambda p: jnp.concatenate([jnp.where(left, p, 0.0), jnp.where(left, 0.0, p)], axis=0)
    on_diag = jnp.concatenate([left, ~left], axis=0)
    r2 = lax.broadcasted_iota(jnp.int32, (pw, pw), 0)
    c2 = lax.broadcasted_iota(jnp.int32, (pw, pw), 1)
    eye2 = jnp.where(r2 == c2, 1.0, 0.0)
    diag_pack = lambda m: jnp.where(left, top(m), bot(m))

    bl = lambda name: [ops[name][c * CHUNK:(c + 1) * CHUNK, q * pw:(q + 1) * pw] for c, q in idx]
    rt, kt, k2t, bt, k2g, bg, v = (bl(name) for name in ('rt', 'kt', 'k2t', 'bt', 'k2g', 'bg', 'v'))
    kr = [jnp.concatenate([kt[j], rt[j]], axis=0) for j in n]
    m1 = [jnp.where(mask2, _dot_a(kr[j], bd(k2t[j]), _NT), 0.0) for j in n]
    m2 = [jnp.where(mask2, _dot_a(kr[j], bd(bt[j]), _NT), 0.0) for j in n]
    x = [eye - jnp.where((ri % 2 == 1) & (cl == ri - 1), top(m), 0.0) for m in m2]
    s = 2
    while s < CHUNK:
        off = ((ri // s) % 2 == 1) & ((cl // s) == (ri // s) - 1)
        t = [_dot_inv(x[j], bd(jnp.where(off, top(m2[j]), 0.0))) for j in n]
        x = [x[j] - _dot_inv(t[j], bd(x[j])) for j in n]
        s *= 2
    m1v = [_dot_app(m1[j], bd(v[j])) for j in n]
    kp = [_dot_app(x[j], bd(kt[j])) for j in n]
    ul = [_dot_app(x[j], bd(top(m1v[j]))) for j in n]
    rp = [rt[j] - _dot_app(bot(m2[j]), bd(kp[j])) for j in n]
    yl = [bot(m1v[j]) - _dot_app(bot(m2[j]), bd(ul[j])) for j in n]
    mc = [eye2 * ops['elast'][c][:, q * pw:(q + 1) * pw]
          - jnp.where(on_diag, _dot_st(kp[j], bg[j], _TN), 0.0) for j, (c, q) in enumerate(idx)]
    dc = [diag_pack(_dot_st(v[j], k2g[j], _TN) - _dot_st(ul[j], bg[j], _TN)) for j in n]
    for c in range(nc):
        js = [c * pairs + q for q in range(pairs)]
        ys = [_dot_st(rp[j], bd(state[q]), _NT) + yl[j] for q, j in enumerate(js)]
        state = [_dot_st(state[q], mc[j]) + dc[j] for q, j in enumerate(js)]
        for q in range(pairs):
            yscr_ref[c * CHUNK:(c + 1) * CHUNK, q * pw:(q + 1) * pw] = ys[q]
    return state


def _rwkv_kernel(p_ref, sh0_ref, s0_ref, mu_ref, wl_ref, w0_ref, a0_ref, kk_ref, ka_ref, rk_ref,
                 lnw_ref, lnb_ref, y_ref, shout_ref, sout_ref, ext_ref, yscr_ref, *, nb, lt):
    rows = nb * lt
    hd = RWKV_HEADDIM
    gw = GROUP_W

    @pl.when(pl.program_id(1) == 0)
    def _():
        ext_ref[:, HIST - 1:HIST, :] = sh0_ref[...]
        sout_ref[...] = s0_ref[...]

    p = p_ref[...]
    ext_ref[:, HIST:HIST + lt, :] = p
    prev = ext_ref[:, HIST - 1:HIST - 1 + lt, :].reshape(rows, RWKV_PROJ)
    last_p = p[:, lt - 1:lt, :]
    ext_ref[:, HIST - 1:HIST, :] = last_p
    shout_ref[...] = last_p
    p = p.reshape(rows, RWKV_PROJ)

    pm = p + (prev - p) * mu_ref[...]
    r = pm[:, 0:gw]
    k = pm[:, gw:2 * gw]
    v = pm[:, 2 * gw:3 * gw]
    lo = pm[:, 3 * gw:]
    lane = lax.broadcasted_iota(jnp.int32, (rows, RWKV_LORA), 1)
    t = jnp.where(lane < 32, jnp.tanh(lo), jnp.where(lane < 64, lo, _sigmoid(lo)))
    lora = _dot(t, wl_ref[...])
    wlog = -_softplus(-(w0_ref[...] + lora[:, 0:gw])) - 0.5
    lw = -jnp.exp(wlog)
    a = _sigmoid(a0_ref[...] + lora[:, gw:2 * gw])
    gate = lora[:, 2 * gw:]
    kk = k * kk_ref[...]
    kk = kk * jnp.minimum(lax.rsqrt(_segsum(kk * kk, hd)), 1e12)
    k2 = k * (1.0 + (a - 1.0) * ka_ref[...])
    beta = kk * a

    unit = lt if nb > 1 else CHUNK
    gcum = _unit_cumsum(lw, unit)
    glast, glast_b = _bcast_last(gcum, rows // unit, unit)
    eg = jnp.exp(gcum)
    einv = jnp.exp(-gcum)
    etail = jnp.exp(glast_b - gcum)
    ops = dict(rt=r * eg, kt=kk * jnp.exp(gcum - lw), k2t=k2 * einv, bt=beta * einv,
               k2g=k2 * etail, bg=beta * etail, v=v, elast=jnp.exp(glast))
    if nb > 1:
        _rwkv_chunk_multi(ops, sout_ref, yscr_ref, nb, lt)
    else:
        pairs = range(RWKV_HEADS // 2)
        state = [jnp.concatenate([sout_ref[0, 2 * q], sout_ref[0, 2 * q + 1]], axis=1) for q in pairs]
        state = _rwkv_chunks_seq(ops, state, yscr_ref, rows // CHUNK)
        for q in pairs:
            sout_ref[0, 2 * q] = state[q][:, 0:hd]
            sout_ref[0, 2 * q + 1] = state[q][:, hd:2 * hd]

    y = yscr_ref[...]
    mean = _segsum(y, hd) * (1.0 / hd)
    dlt = y - mean
    var = _segsum(dlt * dlt, hd) * (1.0 / hd)
    yn = dlt * lax.rsqrt(var + RWKV_LN_EPS) * lnw_ref[...] + lnb_ref[...]
    bonus = _segsum(r * k2 * rk_ref[...], hd) * v
    y_ref[...] = ((yn + bonus) * gate).reshape(nb, lt, gw)


def _rwkv(p, states, prm, l, prev, nb, lt):
    assert (nb > 1 and (nb * lt) % CHUNK == 0 and CHUNK % lt == 0) or (nb == 1 and lt % CHUNK == 0)
    kern = functools.partial(_rwkv_kernel, nb=nb, lt=lt)
    return _mixer_call(kern, "rwkv", l, nb, lt, p, states, prm, prev,
                       [pltpu.VMEM((nb, HIST + lt, RWKV_PROJ), F32),
                        pltpu.VMEM((nb * lt, GROUP_W), F32)])


def _gla_kernel(p_ref, s0_ref, wg2_ref, bg_ref, ng_ref, y_ref, sout_ref, yscr_ref, *, nb, lt, cs):
    rows = nb * lt
    dk, dv = GLA_DK, GLA_DV
    qk = GLA_QK
    units = rows // cs
    per_group = GLA_CS // cs

    @pl.when(pl.program_id(1) == 0)
    def _():
        sout_ref[...] = s0_ref[...]

    p = p_ref[...].reshape(rows, GLA_PW)
    q = p[:, 0:qk] * (dk ** -0.5)
    k = p[:, qk:2 * qk]
    v = p[:, 2 * qk:2 * qk + GROUP_W]
    dg = p[:, 2 * qk + GROUP_W:2 * qk + GROUP_W + 128]
    og = p[:, 2 * qk + GROUP_W + 128:]
    lg = -_softplus(-(_dot(dg, wg2_ref[...]) + bg_ref[...])) * (1.0 / GLA_TAU)
    b = _unit_cumsum(lg, cs)
    bl, bl_b = _bcast_last(b, units, cs)
    qd = q * jnp.exp(b)
    kd = k * jnp.exp(-b)
    kw = k * jnp.exp(bl_b - b)
    ebl = jnp.exp(bl)
    gmask, _ = _chunk_masks(GLA_CS, cs)

    heads = range(GLA_HEADS)
    ks = lambda h: slice(h * dk, (h + 1) * dk)
    vs = lambda h: slice(h * dv, (h + 1) * dv)
    gs = lambda g: slice(g * GLA_CS, (g + 1) * GLA_CS)
    us = lambda u: slice(u * cs, (u + 1) * cs)

    yh = {}
    for g in range(rows // GLA_CS):
        for h in heads:
            att = jnp.where(gmask, _dot(qd[gs(g), ks(h)], kd[gs(g), ks(h)], _NT), 0.0)
            yh[g, h] = _dot(att, v[gs(g), vs(h)])
    ds = {(u, h): _dot(kw[us(u), ks(h)], v[us(u), vs(h)], _TN) for u in range(units) for h in heads}
    ri = lax.broadcasted_iota(jnp.int32, (qk, qk), 0)
    ci = lax.broadcasted_iota(jnp.int32, (qk, qk), 1)
    ones_kv = jnp.ones((qk, dv), F32)
    ecol_all = [_dot_exact_rhs(jnp.where(ri == ci, ebl[u], 0.0), ones_kv) for u in range(units)]
    ecol = {(u, h): ecol_all[u][ks(h)] for u in range(units) for h in heads}

    before = {}
    for h in heads:
        if nb > 1:
            for u in range(units):
                before[u, h] = sout_ref[u, h]
                sout_ref[u, h] = before[u, h] * ecol[u, h] + ds[u, h]
        else:
            s_cur = sout_ref[0, h]
            for u in range(units):
                before[u, h] = s_cur
                s_cur = s_cur * ecol[u, h] + ds[u, h]
            sout_ref[0, h] = s_cur

    for g in range(rows // GLA_CS):
        for h in heads:
            inter = [_dot(qd[us(u), ks(h)], before[u, h])
                     for u in range(g * per_group, (g + 1) * per_group)]
            inter = inter[0] if per_group == 1 else jnp.concatenate(inter, axis=0)
            yscr_ref[gs(g), vs(h)] = yh[g, h] + inter

    y = yscr_ref[...]
    y = y * lax.rsqrt(_segsum(y * y, dv) * (1.0 / dv) + EPS) * ng_ref[...]
    y_ref[...] = (y * _silu(og)).reshape(nb, lt, GROUP_W)


def _gla(p, states, prm, l, prev, nb, lt, cs):
    assert (nb == 1 and cs == GLA_CS and lt % cs == 0) or (
        lt == cs and (nb * cs) % GLA_CS == 0 and GLA_CS % cs == 0)
    kern = functools.partial(_gla_kernel, nb=nb, lt=lt, cs=cs)
    return _mixer_call(kern, "gla", l, nb, lt, p, states, prm, prev,
                       [pltpu.VMEM((nb * lt, GROUP_W), F32)])


def _s5_pitch(lt):
    return lt if (lt // 8) % 2 == 1 else lt + 8


def _s5_kernel(u_ref, hr0_ref, hi0_ref, wb_ref, wc_ref, ar_ref, ai_ref, d_ref, wglu_ref, bglu_ref,
               y_ref, hr_ref, hi_ref, io_ref, h_ref, *, nb, lt):
    n = S5_CH
    wt = GROUP_W // LANES

    @pl.when(pl.program_id(1) == 0)
    def _():
        hr_ref[...] = hr0_ref[...]
        hi_ref[...] = hi0_ref[...]

    lanes = lambda x, j: x[:, j * LANES:(j + 1) * LANES]
    pitch = _s5_pitch(lt)
    seq_rows = lambda i: slice(i * pitch, i * pitch + lt)
    at_t = lambda t: pl.ds(t, nb, stride=pitch)
    step = lambda t: slice(t * nb, (t + 1) * nb)
    for j in range(wt):
        for i in range(nb):
            io_ref[j, seq_rows(i), :] = u_ref[i, :, j * LANES:(j + 1) * LANES]
    u = jnp.concatenate([jnp.concatenate([io_ref[j, at_t(t), :] for j in range(wt)], axis=1)
                         for t in range(lt)], axis=0)
    bu = _dot(u, wb_ref[...])
    ar = jnp.broadcast_to(ar_ref[...], (nb, n))
    ai = jnp.broadcast_to(ai_ref[...], (nb, n))
    hr = hr_ref[...]
    hi = hi_ref[...]
    for t in range(lt):
        hr, hi = (ar * hr - ai * hi + bu[step(t), 0:n], ar * hi + ai * hr + bu[step(t), n:2 * n])
        h_ref[step(t), 0:n] = hr
        h_ref[step(t), n:2 * n] = hi
    hr_ref[...] = hr
    hi_ref[...] = hi
    y = _dot(h_ref[:, 0:n], wc_ref[0]) - _dot(h_ref[:, n:2 * n], wc_ref[1])
    y = y + d_ref[...] * u
    y = 0.5 * y * (1.0 + jnp.tanh(math.sqrt(2.0 / math.pi) * (y + 0.044715 * (y * y * y))))
    y = y * _sigmoid(_dot(y, wglu_ref[...]) + bglu_ref[...])
    for t in range(lt):
        for j in range(wt):
            io_ref[j, at_t(t), :] = lanes(y[step(t)], j)
    for j in range(wt):
        for i in range(nb):
            y_ref[i, :, j * LANES:(j + 1) * LANES] = io_ref[j, seq_rows(i), :]


def _s5(u, states, prm, l, prev, nb, lt):
    kern = functools.partial(_s5_kernel, nb=nb, lt=lt)
    return _mixer_call(kern, "s5", l, nb, lt, u, states, prm, prev,
                       [pltpu.VMEM((GROUP_W // LANES, nb * _s5_pitch(lt), LANES), F32),
                        pltpu.VMEM((nb * lt, 2 * S5_CH), F32)])


def _s5_zoh_kernel(lr_ref, li_ref, ls_ref, bre_ref, bim_ref, ar_ref, ai_ref, bbre_ref, bbim_ref):
    lr = lr_ref[...]
    li = li_ref[...]
    dts = jnp.exp(ls_ref[...])
    mag = jnp.exp(lr * dts)
    ar = mag * jnp.cos(li * dts)
    ai = mag * jnp.sin(li * dts)
    den = lr * lr + li * li
    fr = ((ar - 1.0) * lr + ai * li) / den
    fi = (ai * lr - (ar - 1.0) * li) / den
    ar_ref[...] = ar
    ai_ref[...] = ai
    r, n = lr.shape
    per_input = lambda f: jnp.broadcast_to(f[:, None, :], (r, S5_GROUP, n)).reshape(r * S5_GROUP, n)
    fre, fie = per_input(fr), per_input(fi)
    bre = bre_ref[...]
    bim = bim_ref[...]
    bbre_ref[...] = fre * bre - fie * bim
    bbim_ref[...] = fre * bim + fie * bre


def _s5_zoh(lam_re, lam_im, log_step, b_re, b_im):
    dep, g, n = lam_re.shape
    rows = dep * g
    flat = lambda a: a.astype(F32).reshape(rows, n)
    flat_b = lambda b: jnp.swapaxes(b.astype(F32), -1, -2).reshape(rows * S5_GROUP, n)
    vm = pl.BlockSpec(memory_space=pltpu.VMEM)
    ar, ai, bbr, bbi = pl.pallas_call(
        _s5_zoh_kernel,
        out_shape=(jax.ShapeDtypeStruct((rows, n), F32),) * 2
        + (jax.ShapeDtypeStruct((rows * S5_GROUP, n), F32),) * 2,
        in_specs=[vm] * 5,
        out_specs=(vm,) * 4,
        name="s5_zoh",
    )(flat(lam_re), flat(lam_im), log_step.astype(F32).reshape(rows, 1), flat_b(b_re), flat_b(b_im))
    return (ar.reshape(dep, g, n), ai.reshape(dep, g, n),
            bbr.reshape(dep, g, S5_GROUP, n), bbi.reshape(dep, g, S5_GROUP, n))


def _block_diag_in(b):
    dep, g, i, n = b.shape
    eye = jnp.eye(g, dtype=b.dtype)
    return jnp.einsum('lgin,gh->lgihn', b, eye).reshape(dep, g * i, g * n)


def _block_diag_out(c):
    dep, g, o, n = c.shape
    eye = jnp.eye(g, dtype=c.dtype)
    return jnp.einsum('lgon,gh->lgnho', c, eye).reshape(dep, g * n, g * o)


def _prep_params(W):
    dep = W['w_in'].shape[0]
    row = lambda name: W[name].reshape(dep, 1, -1).astype(F32)
    rep = lambda name, n: jnp.repeat(W[name].reshape(dep, -1), n, axis=1).reshape(dep, 1, -1).astype(F32)
    w_in = W['w_in'].astype(BF16)
    o_r = SSD_PROJ
    o_g = o_r + RWKV_PROJ
    o_s = o_g + GLA_PROJ
    w_ssd = jnp.concatenate([w_in[..., :GROUP_W + SSD_CONV_CH],
                             jnp.repeat(w_in[..., GROUP_W + SSD_CONV_CH:o_r], SSD_HEADDIM, axis=2)], axis=2)
    w_rwkv = w_in[..., o_r:o_g]
    gq = o_g + 2 * GLA_QK + GROUP_W
    w_gla = jnp.concatenate([w_in[..., o_g:gq], w_in[..., gq:gq + GLA_GATE_LORA],
                             jnp.zeros((dep, D_MODEL, 128 - GLA_GATE_LORA), BF16),
                             w_in[..., gq + GLA_GATE_LORA:o_s]], axis=2)
    w_s5 = w_in[..., o_s:]
    in_w = (w_ssd, w_rwkv, w_gla, w_s5)

    ssd = (W['ssd_conv_w'].astype(F32), row('ssd_conv_b'), rep('ssd_a_log', SSD_HEADDIM),
           rep('ssd_dt_bias', SSD_HEADDIM), rep('ssd_d', SSD_HEADDIM), row('ssd_norm_g'))

    z = lambda r, c: jnp.zeros((dep, r, c), F32)
    w_lora = jnp.concatenate([
        jnp.concatenate([W['rwkv_w2'], z(32, 2 * GROUP_W)], axis=2),
        jnp.concatenate([z(32, GROUP_W), W['rwkv_a2'], z(32, GROUP_W)], axis=2),
        jnp.concatenate([z(64, 2 * GROUP_W), W['rwkv_g2']], axis=2)], axis=1).astype(BF16)
    rwkv = (row('rwkv_mu'), w_lora, row('rwkv_w0'), row('rwkv_a0'), row('rwkv_k_k'),
            row('rwkv_k_a'), row('rwkv_r_k'), row('rwkv_lnx_w'), row('rwkv_lnx_b'))

    wg2 = jnp.concatenate([W['gla_wg2'], z(128 - GLA_GATE_LORA, GLA_QK)], axis=1).astype(BF16)
    gla = (wg2, row('gla_bg'), row('gla_norm_g'))

    ar, ai, bb_re, bb_im = _s5_zoh(W['s5_lam_re'], W['s5_lam_im'], W['s5_log_step'],
                                   W['s5_b_re'], W['s5_b_im'])
    wb = jnp.concatenate([_block_diag_in(bb_re), _block_diag_in(bb_im)], axis=2).astype(BF16)
    wc = jnp.stack([_block_diag_out(W['s5_c_re']), _block_diag_out(W['s5_c_im'])], axis=1).astype(BF16)
    s5 = (wb, wc, ar.reshape(dep, 1, -1), ai.reshape(dep, 1, -1), row('s5_d'),
          W['s5_w_glu'].astype(BF16), row('s5_b_glu'))

    g4 = lambda name: W[name].reshape(dep, 1, 1, D_MODEL).astype(F32)
    return dict(
        in_w=in_w, ssd=ssd, rwkv=rwkv, gla=gla, s5=s5,
        g_mix_pre=g4('g_mix_pre'), g_mix_post=g4('g_mix_post'),
        g_ffn_pre=g4('g_ffn_pre'), g_ffn_post=g4('g_ffn_post'),
        w_out=W['w_out'].reshape(dep, 4, GROUP_W, D_MODEL).astype(BF16),
        w1=W['mlp_w1'].astype(BF16), w2=W['mlp_w2'].astype(BF16))


def _run_group(x, mod, first, st, P, cfg):
    dep, bsz = st['ssd'].shape[:2]
    hist = jnp.pad(st['ssd_conv'], ((0, 0), (0, 0), (HIST - (SSD_CONV - 1), 0), (0, 0)))
    ssd_st = (hist, st['ssd'])
    rwkv_st = (st['rwkv_shift'].reshape(dep, bsz, 1, RWKV_PROJ), st['rwkv'])
    gla_st = (st['gla'],)
    s5_st = (st['s5_re'].reshape(dep, bsz, S5_CH), st['s5_im'].reshape(dep, bsz, S5_CH))
    new = dict(ssd=None, rwkv=None, gla=None, s5=None)
    for l in range(DEPTH):
        p_ssd, p_rwkv, p_gla, p_s5 = _in_proj(x, mod, first, P['g_mix_pre'], P['in_w'], l, *cfg['tok'])
        y_ssd, *new['ssd'] = _ssd(p_ssd, ssd_st, P['ssd'], l, new['ssd'], *cfg['ssd'])
        y_rwkv, *new['rwkv'] = _rwkv(p_rwkv, rwkv_st, P['rwkv'], l, new['rwkv'], *cfg['rwkv'])
        y_gla, *new['gla'] = _gla(p_gla, gla_st, P['gla'], l, new['gla'], *cfg['gla'])
        y_s5, *new['s5'] = _s5(p_s5, s5_st, P['s5'], l, new['s5'], *cfg['s5'])
        x = _out_mlp(x, (y_ssd, y_rwkv, y_gla, y_s5), mod, first, P['g_mix_post'], P['g_ffn_pre'],
                     P['g_ffn_post'], P['w_out'], P['w1'], P['w2'], l, *cfg['mlp'])
    hist_new, ssd_new = new['ssd']
    shift_new, rwkv_new = new['rwkv']
    s5r_new, s5i_new = new['s5']
    return x, dict(
        ssd=ssd_new, ssd_conv=hist_new[:, :, HIST - (SSD_CONV - 1):, :],
        rwkv=rwkv_new, rwkv_shift=shift_new.reshape(dep, bsz, RWKV_PROJ), gla=new['gla'][0],
        s5_re=s5r_new.reshape(dep, bsz, S5_GROUPS, S5_STATE),
        s5_im=s5i_new.reshape(dep, bsz, S5_GROUPS, S5_STATE))


_PROMPT_CFG = dict(tok=(1, 1024), mlp=(1, 1024), ssd=(1, 1024), rwkv=(1, 512), gla=(1, 1024, GLA_CS),
                   s5=(8, 128))
_SAMPLE_CFG = dict(tok=(64, 8), mlp=(64, 8), ssd=(32, 8), rwkv=(32, 8), gla=(16, 8, 8), s5=(64, 8))


def kernel(x_prompt, x_sample, c_prompt, c_sample, state_ssd, state_ssd_conv, state_rwkv, state_rwkv_shift, state_gla, state_s5_re, state_s5_im, w_ada, b_ada, g_mix_pre, g_mix_post, g_ffn_pre, g_ffn_post, w_in, w_out, ssd_conv_w, ssd_conv_b, ssd_a_log, ssd_dt_bias, ssd_d, ssd_norm_g, rwkv_mu, rwkv_w0, rwkv_w2, rwkv_a0, rwkv_a2, rwkv_g2, rwkv_k_k, rwkv_k_a, rwkv_r_k, rwkv_lnx_w, rwkv_lnx_b, gla_wg2, gla_bg, gla_norm_g, s5_lam_re, s5_lam_im, s5_log_step, s5_b_re, s5_b_im, s5_c_re, s5_c_im, s5_d, s5_w_glu, s5_b_glu, mlp_w1, mlp_w2):
    W = dict(w_in=w_in, w_out=w_out, g_mix_pre=g_mix_pre, g_mix_post=g_mix_post,
             g_ffn_pre=g_ffn_pre, g_ffn_post=g_ffn_post,
             ssd_conv_w=ssd_conv_w, ssd_conv_b=ssd_conv_b, ssd_a_log=ssd_a_log,
             ssd_dt_bias=ssd_dt_bias, ssd_d=ssd_d, ssd_norm_g=ssd_norm_g,
             rwkv_mu=rwkv_mu, rwkv_w0=rwkv_w0, rwkv_w2=rwkv_w2, rwkv_a0=rwkv_a0, rwkv_a2=rwkv_a2,
             rwkv_g2=rwkv_g2, rwkv_k_k=rwkv_k_k, rwkv_k_a=rwkv_k_a, rwkv_r_k=rwkv_r_k,
             rwkv_lnx_w=rwkv_lnx_w, rwkv_lnx_b=rwkv_lnx_b,
             gla_wg2=gla_wg2, gla_bg=gla_bg, gla_norm_g=gla_norm_g,
             s5_lam_re=s5_lam_re, s5_lam_im=s5_lam_im, s5_log_step=s5_log_step,
             s5_b_re=s5_b_re, s5_b_im=s5_b_im, s5_c_re=s5_c_re, s5_c_im=s5_c_im,
             s5_d=s5_d, s5_w_glu=s5_w_glu, s5_b_glu=s5_b_glu, mlp_w1=mlp_w1, mlp_w2=mlp_w2)
    P = _prep_params(W)

    nbp, nbs = x_prompt.shape[0], x_sample.shape[0]
    mod = _ada(jnp.concatenate([c_sample, c_prompt], axis=0), w_ada, b_ada)
    mod = mod.reshape(DEPTH, nbs + nbp, N_MOD, D_MODEL)

    st_sample = dict(ssd=state_ssd, ssd_conv=state_ssd_conv, rwkv=state_rwkv,
                     rwkv_shift=state_rwkv_shift, gla=state_gla, s5_re=state_s5_re, s5_im=state_s5_im)
    st_prompt = {n: jnp.zeros((DEPTH, nbp) + v.shape[2:], v.dtype) for n, v in st_sample.items()}

    y_prompt, sp = _run_group(x_prompt, mod, nbs, st_prompt, P, _PROMPT_CFG)
    y_sample, ss = _run_group(x_sample, mod, 0, st_sample, P, _SAMPLE_CFG)
    return (y_prompt, y_sample,
            sp['ssd'], ss['ssd'], sp['ssd_conv'], ss['ssd_conv'],
            sp['rwkv'], ss['rwkv'], sp['rwkv_shift'], ss['rwkv_shift'],
            sp['gla'], ss['gla'], sp['s5_re'], ss['s5_re'], sp['s5_im'], ss['s5_im'])
```
